```python
import math
import jax, jax.numpy as jnp
from jax import lax
import numpy as np

D_MODEL = 2048
BATCH = 8
SEQ = 2048
DEPTH = 2

N_MIXERS = 2
N_SSD_LAYERS = (DEPTH + 1) // 2
N_DSA_LAYERS = DEPTH // 2
DN_ALPHA = (2.0 * DEPTH) ** 0.25
DN_BETA = (8.0 * DEPTH) ** -0.25
LN_EPS = 1e-5
RMS_EPS = 1e-6

SSD_EXPAND = 2
SSD_D_INNER = SSD_EXPAND * D_MODEL
SSD_HEAD_DIM = 64
SSD_N_HEADS = SSD_D_INNER // SSD_HEAD_DIM
SSD_N_GROUPS = 8
SSD_HEADS_PER_GROUP = SSD_N_HEADS // SSD_N_GROUPS
SSD_D_STATE = 128
SSD_D_CONV = 4
SSD_CHUNK = 256
SSD_CONV_DIM = SSD_D_INNER + 2 * SSD_N_GROUPS * SSD_D_STATE
SSD_IN_DIM = SSD_D_INNER + SSD_CONV_DIM + SSD_N_HEADS
SSD_NORM_GROUP = SSD_D_INNER // SSD_N_GROUPS

DSA_N_HEADS = 32
DSA_HEAD_DIM = 128
DSA_WIDTH = DSA_N_HEADS * DSA_HEAD_DIM
DSA_Q_RANK = 512
DSA_KV_RANK = 256
IDX_N_HEADS = 16
IDX_HEAD_DIM = 64
IDX_TOPK = 256
Q_BLOCK = 128
DSA_IN_DIM = DSA_Q_RANK + DSA_KV_RANK + IDX_HEAD_DIM + IDX_N_HEADS + DSA_WIDTH

kernel_name = 'hybrid_ssd_dsa_deepnorm'


def layer_norm(x, g, b):
    xf = x.astype(jnp.float32)
    mu = jnp.mean(xf, -1, keepdims=True)
    var = jnp.mean(jnp.square(xf - mu), -1, keepdims=True)
    return ((xf - mu) * lax.rsqrt(var + LN_EPS) * g + b).astype(x.dtype)


def rms_norm(x, g):
    xf = x.astype(jnp.float32)
    y = xf * lax.rsqrt(jnp.mean(xf * xf, -1, keepdims=True) + RMS_EPS) * g
    return y.astype(x.dtype)


def causal_depthwise_conv(u, w, b):
    k, c = w.shape
    out = lax.conv_general_dilated(u, w[:, None, :], window_strides=(1,), padding=[(k - 1, 0)],
                                   dimension_numbers=('NWC', 'WIO', 'NWC'), feature_group_count=c)
    return out + b


def segsum(a):
    cs = jnp.cumsum(a, -1)
    t = a.shape[-1]
    diff = cs[..., :, None] - cs[..., None, :]
    mask = jnp.tril(jnp.ones((t, t), dtype=bool))
    return jnp.where(mask, diff, -jnp.inf)


def ssd_chunked(xs, dt, a, bm, cm, chunk):
    b, L, g, r, p = xs.shape
    n = bm.shape[-1]
    c = L // chunk
    xdt = (xs * dt[..., None]).reshape(b, c, chunk, g, r, p)
    bc = bm.reshape(b, c, chunk, g, n)
    cc = cm.reshape(b, c, chunk, g, n)
    adt = (dt * a).reshape(b, c, chunk, g, r).transpose(0, 3, 4, 1, 2)
    a_cs = jnp.cumsum(adt, -1)
    decay = jnp.exp(segsum(adt))
    cb = jnp.einsum('bclgn,bcsgn->bgcls', cc, bc)
    y_diag = jnp.einsum('bgcls,bgrcls,bcsgrp->bclgrp', cb, decay, xdt)
    decay_states = jnp.exp(a_cs[..., -1:] - a_cs)
    states = jnp.einsum('bclgn,bgrcl,bclgrp->bcgrpn', bc, decay_states, xdt)
    states = jnp.concatenate([jnp.zeros_like(states[:, :1]), states], 1)
    chunk_a = jnp.pad(a_cs[..., -1], ((0, 0), (0, 0), (0, 0), (1, 0)))
    chunk_decay = jnp.exp(segsum(chunk_a))
    new_states = jnp.einsum('bgrzc,bcgrpn->bzgrpn', chunk_decay, states)
    prev_states = new_states[:, :-1]
    y_off = jnp.einsum('bclgn,bcgrpn,bgrcl->bclgrp', cc, prev_states, jnp.exp(a_cs))
    return (y_diag + y_off).reshape(b, L, g, r, p)


def ssd_mixer(x, w_in, conv_w, conv_b, dt_bias, a_log, d_skip, norm_g, w_out):
    b, L, _ = x.shape
    proj = x @ w_in
    z, xbc, dt = jnp.split(proj, [SSD_D_INNER, SSD_D_INNER + SSD_CONV_DIM], -1)
    xbc = jax.nn.silu(causal_depthwise_conv(xbc, conv_w, conv_b))
    xs, bm, cm = jnp.split(xbc, [SSD_D_INNER, SSD_D_INNER + SSD_N_GROUPS * SSD_D_STATE], -1)
    g, r, p = SSD_N_GROUPS, SSD_HEADS_PER_GROUP, SSD_HEAD_DIM
    xs = xs.reshape(b, L, g, r, p)
    bm = bm.reshape(b, L, g, SSD_D_STATE)
    cm = cm.reshape(b, L, g, SSD_D_STATE)
    dt = jax.nn.softplus(dt.astype(jnp.float32) + dt_bias).reshape(b, L, g, r)
    a = -jnp.exp(a_log.astype(jnp.float32)).reshape(g, r)
    chunk = math.gcd(SSD_CHUNK, L)
    y = ssd_chunked(xs, dt, a, bm, cm, chunk)
    y = y + d_skip.reshape(g, r)[..., None] * xs
    y = y.reshape(b, L, SSD_D_INNER).astype(jnp.float32)
    yg = (y * jax.nn.silu(z.astype(jnp.float32))).reshape(b, L, g, SSD_NORM_GROUP)
    yg = yg * lax.rsqrt(jnp.mean(yg * yg, -1, keepdims=True) + RMS_EPS)
    yg = yg.reshape(b, L, SSD_D_INNER) * norm_g
    return yg.astype(x.dtype) @ w_out


def dsa_mixer(x, w_in, q_norm_g, kv_norm_g, w_uq, w_uk, w_uv, w_idx_q, w_out):
    b, L, _ = x.shape
    proj = x @ w_in
    s1 = DSA_Q_RANK
    s2 = s1 + DSA_KV_RANK
    s3 = s2 + IDX_HEAD_DIM
    s4 = s3 + IDX_N_HEADS
    c_q, c_kv, k_idx, w_idx, gate = jnp.split(proj, [s1, s2, s3, s4], -1)
    c_q = rms_norm(c_q, q_norm_g)
    c_kv = rms_norm(c_kv, kv_norm_g)
    q = (c_q @ w_uq).reshape(b, L, DSA_N_HEADS, DSA_HEAD_DIM)
    q_lat = jnp.einsum('bthd,hcd->bthc', q, w_uk)
    q_idx = (c_q @ w_idx_q).reshape(b, L, IDX_N_HEADS, IDX_HEAD_DIM)
    w_idx = w_idx.astype(jnp.float32) * (IDX_N_HEADS ** -0.5)
    k_idx_f = k_idx.astype(jnp.float32)
    k_top = min(IDX_TOPK, L // 4)
    nblk = L // Q_BLOCK
    key_pos = jnp.arange(L)
    idx_scale = IDX_HEAD_DIM ** -0.5
    attn_scale = DSA_HEAD_DIM ** -0.5

    def to_blocks(t):
        return t.reshape((b, nblk, Q_BLOCK) + t.shape[2:]).swapaxes(0, 1)

    def block(args):
        qi, wi, ql, tpos = args
        s = jnp.einsum('bthd,bsd->bths', qi.astype(jnp.float32), k_idx_f)
        score = jnp.einsum('bths,bth->bts', jax.nn.relu(s), wi) * idx_scale
        causal = key_pos[None, :] <= tpos[:, None]
        score = jnp.where(causal[None], score, -jnp.inf)
        _, sel = lax.top_k(score, k_top)
        kv_sel = jax.vmap(lambda c, i: c[i])(c_kv, sel)
        logits = jnp.einsum('bthc,btkc->bthk', ql, kv_sel).astype(jnp.float32) * attn_scale
        valid = sel <= tpos[None, :, None]
        logits = jnp.where(valid[:, :, None, :], logits, -jnp.inf)
        pr = jax.nn.softmax(logits, -1).astype(kv_sel.dtype)
        return jnp.einsum('bthk,btkc->bthc', pr, kv_sel)

    pos = jnp.arange(L).reshape(nblk, Q_BLOCK)
    o_lat = lax.map(block, (to_blocks(q_idx), to_blocks(w_idx), to_blocks(q_lat), pos))
    o_lat = o_lat.swapaxes(0, 1).reshape(b, L, DSA_N_HEADS, DSA_KV_RANK)
    o = jnp.einsum('bthc,hcv->bthv', o_lat, w_uv).reshape(b, L, DSA_WIDTH)
    return (o * jax.nn.silu(gate)) @ w_out


def setup_inputs(seed: int = 0) -> dict:
    key = jax.random.key(seed)
    ks = jax.random.split(key, 24)
    f32 = jnp.float32

    def nrm(k, shape, fan_in, scale=1.0):
        return jax.random.normal(k, shape, f32) * (scale * fan_in ** -0.5)

    x = jax.random.normal(ks[0], (BATCH, SEQ, D_MODEL), f32)
    ns, nd = N_SSD_LAYERS, N_DSA_LAYERS
    ssd_w_in = nrm(ks[1], (ns, D_MODEL, SSD_IN_DIM), D_MODEL)
    ssd_conv_w = nrm(ks[2], (ns, SSD_D_CONV, SSD_CONV_DIM), SSD_D_CONV)
    ssd_conv_b = 0.02 * jax.random.normal(ks[3], (ns, SSD_CONV_DIM), f32)
    u = jax.random.uniform(ks[4], (ns, SSD_N_HEADS), f32)
    dt0 = jnp.exp(u * (math.log(0.1) - math.log(0.001)) + math.log(0.001))
    ssd_dt_bias = dt0 + jnp.log(-jnp.expm1(-dt0))
    ssd_a_log = jnp.log(jax.random.uniform(ks[5], (ns, SSD_N_HEADS), f32, 1.0, 16.0))
    ssd_d_skip = 1.0 + 0.02 * jax.random.normal(ks[6], (ns, SSD_N_HEADS), f32)
    ssd_norm_g = 1.0 + 0.02 * jax.random.normal(ks[7], (ns, SSD_D_INNER), f32)
    ssd_w_out = nrm(ks[8], (ns, SSD_D_INNER, D_MODEL), SSD_D_INNER, DN_BETA)
    dsa_w_in = nrm(ks[9], (nd, D_MODEL, DSA_IN_DIM), D_MODEL)
    dsa_q_norm_g = 1.0 + 0.02 * jax.random.normal(ks[10], (nd, DSA_Q_RANK), f32)
    dsa_kv_norm_g = 1.0 + 0.02 * jax.random.normal(ks[11], (nd, DSA_KV_RANK), f32)
    dsa_w_uq = nrm(ks[12], (nd, DSA_Q_RANK, DSA_WIDTH), DSA_Q_RANK)
    dsa_w_uk = nrm(ks[13], (nd, DSA_N_HEADS, DSA_KV_RANK, DSA_HEAD_DIM), DSA_KV_RANK)
    dsa_w_uv = nrm(ks[14], (nd, DSA_N_HEADS, DSA_KV_RANK, DSA_HEAD_DIM), DSA_KV_RANK)
    dsa_w_idx_q = nrm(ks[15], (nd, DSA_Q_RANK, IDX_N_HEADS * IDX_HEAD_DIM), DSA_Q_RANK)
    dsa_w_out = nrm(ks[16], (nd, DSA_WIDTH, D_MODEL), DSA_WIDTH, DN_BETA)
    ln_g = 1.0 + 0.02 * jax.random.normal(ks[17], (DEPTH, D_MODEL), f32)
    ln_b = 0.02 * jax.random.normal(ks[18], (DEPTH, D_MODEL), f32)
    return {'x': x, 'ssd_w_in': ssd_w_in, 'ssd_conv_w': ssd_conv_w, 'ssd_conv_b': ssd_conv_b,
            'ssd_dt_bias': ssd_dt_bias, 'ssd_a_log': ssd_a_log, 'ssd_d_skip': ssd_d_skip,
            'ssd_norm_g': ssd_norm_g, 'ssd_w_out': ssd_w_out, 'dsa_w_in': dsa_w_in,
            'dsa_q_norm_g': dsa_q_norm_g, 'dsa_kv_norm_g': dsa_kv_norm_g, 'dsa_w_uq': dsa_w_uq,
            'dsa_w_uk': dsa_w_uk, 'dsa_w_uv': dsa_w_uv, 'dsa_w_idx_q': dsa_w_idx_q,
            'dsa_w_out': dsa_w_out, 'ln_g': ln_g, 'ln_b': ln_b}


def reference(x, ssd_w_in, ssd_conv_w, ssd_conv_b, ssd_dt_bias, ssd_a_log, ssd_d_skip,
              ssd_norm_g, ssd_w_out, dsa_w_in, dsa_q_norm_g, dsa_kv_norm_g, dsa_w_uq,
              dsa_w_uk, dsa_w_uv, dsa_w_idx_q, dsa_w_out, ln_g, ln_b):
    for i in range(DEPTH):
        j = i // N_MIXERS
        if i % N_MIXERS == 0:
            h = ssd_mixer(x, ssd_w_in[j], ssd_conv_w[j], ssd_conv_b[j], ssd_dt_bias[j],
                          ssd_a_log[j], ssd_d_skip[j], ssd_norm_g[j], ssd_w_out[j])
        else:
            h = dsa_mixer(x, dsa_w_in[j], dsa_q_norm_g[j], dsa_kv_norm_g[j], dsa_w_uq[j],
                          dsa_w_uk[j], dsa_w_uv[j], dsa_w_idx_q[j], dsa_w_out[j])
        x = layer_norm(DN_ALPHA * x + h, ln_g[i], ln_b[i])
    return x
```

```python
import functools
import math

import jax
import jax.numpy as jnp
from jax import lax
from jax.experimental import pallas as pl
from jax.experimental.pallas import tpu as pltpu

F32 = jnp.float32
BF16 = jnp.bfloat16

V7X_VMEM_BYTES = 64 * 1024 * 1024
LANES = 128
VMEM_LIMIT_BYTES = V7X_VMEM_BYTES - 8 * 1024 * 1024

LN_EPS = 1e-5
RMS_EPS = 1e-6

SSD_HEAD_DIM = 64
SSD_N_GROUPS = 8
SSD_D_STATE = 128
SSD_CHUNK = 256
DSA_N_HEADS = 32
DSA_HEAD_DIM = 128
DSA_Q_RANK = 512
DSA_KV_RANK = 256
IDX_N_HEADS = 16
IDX_HEAD_DIM = 64
IDX_TOPK = 256

ATT_TQ = 256
IDX_TQ = 512
ATT_HEADS_PER_STEP = 4
BISECT_ITERS = 32


def _cparams(n_axes):
    return pltpu.CompilerParams(dimension_semantics=("arbitrary",) * n_axes,
                                vmem_limit_bytes=VMEM_LIMIT_BYTES)


def _sigmoid(v):
    return 1.0 / (1.0 + jnp.exp(-v))


def _proj_kernel(x_ref, w_ref, *rest, epilogue):
    y = jnp.dot(x_ref[...], w_ref[...], preferred_element_type=F32)
    if epilogue == "none":
        (o_ref,) = rest
    elif epilogue == "silu":
        (o_ref,) = rest
        y = y * _sigmoid(y)
    elif epilogue == "softplus":
        b_ref, o_ref = rest
        y = y + b_ref[...]
        y = jnp.maximum(y, 0.0) + jnp.log1p(jnp.exp(-jnp.abs(y)))
    elif epilogue == "conv_silu":
        cw_ref, cb_ref, o_ref = rest
        k_taps = cw_ref.shape[0]
        row = lax.broadcasted_iota(jnp.int32, y.shape, 0)
        acc = cb_ref[...] + cw_ref[k_taps - 1:k_taps, :] * y
        for back in range(1, k_taps):
            shifted = jnp.where(row >= back, pltpu.roll(y, back, 0), 0.0)
            acc = acc + cw_ref[k_taps - 1 - back:k_taps - back, :] * shifted
        y = acc * _sigmoid(acc)
    else:
        raise ValueError(epilogue)
    o_ref[...] = y.astype(o_ref.dtype)


def _proj(x, w, *, batch, seq, tn, epilogue, extra=(), out_dtype=BF16):
    k_dim, n_dim = w.shape
    assert n_dim % tn == 0 and tn % LANES == 0
    extra_specs = [pl.BlockSpec((e.shape[0], tn), lambda b, j: (0, j)) for e in extra]
    return pl.pallas_call(
        functools.partial(_proj_kernel, epilogue=epilogue),
        grid=(batch, n_dim // tn),
        in_specs=[pl.BlockSpec((seq, k_dim), lambda b, j: (b, 0)),
                  pl.BlockSpec((k_dim, tn), lambda b, j: (0, j))] + extra_specs,
        out_specs=pl.BlockSpec((seq, tn), lambda b, j: (b, j)),
        out_shape=jax.ShapeDtypeStruct((batch * seq, n_dim), out_dtype),
        compiler_params=_cparams(2),
        name="proj_" + epilogue,
    )(x, w, *extra)


def _cumsum_rows(v):
    n = v.shape[0]
    row = lax.broadcasted_iota(jnp.int32, v.shape, 0)
    shift = 1
    while shift < n:
        v = v + jnp.where(row >= shift, pltpu.roll(v, shift, 0), 0.0)
        shift *= 2
    return v


def _ssd_kernel(xs_ref, b_ref, c_ref, z_ref, dt_ref, alog_ref, dskip_ref, ng_ref, e_ref, o_ref,
                state_ref, *, heads_per_group, head_dim):
    g = pl.program_id(1)
    chunk = pl.program_id(2)
    q = xs_ref.shape[0]

    @pl.when(chunk == 0)
    def _():
        state_ref[...] = jnp.zeros_like(state_ref)

    xs = xs_ref[...].astype(F32)
    bm = b_ref[...]
    cm = c_ref[...]
    dt = dt_ref[...]
    a = -jnp.exp(alog_ref[pl.ds(g, 1), :])
    acs = _cumsum_rows(dt * a)
    acs_t = acs.T
    last = acs[q - 1:q, :]
    expand_mat = e_ref[...]

    def expand(v):
        return jnp.dot(v.astype(BF16), expand_mat, preferred_element_type=F32)

    def expand_hi_lo(v):
        hi = v.astype(BF16)
        lo = (v - hi.astype(F32)).astype(BF16)
        return (jnp.dot(hi, expand_mat, preferred_element_type=F32)
                + jnp.dot(lo, expand_mat, preferred_element_type=F32))

    dt_x = expand(dt)
    decay_in_x = expand_hi_lo(jnp.exp(acs))
    decay_out_dt_x = expand(jnp.exp(last - acs) * dt)
    xdt = xs * dt_x

    cb = lax.dot_general(cm, bm, (((1,), (1,)), ((), ())), preferred_element_type=F32)
    li = lax.broadcasted_iota(jnp.int32, (q, q), 0)
    si = lax.broadcasted_iota(jnp.int32, (q, q), 1)
    causal = li >= si
    heads_per_slab = LANES // head_dim
    lane_head = lax.broadcasted_iota(jnp.int32, (q, LANES), 1) // head_dim
    slabs = []
    for j in range(heads_per_group // heads_per_slab):
        x_slab = xdt[:, j * LANES:(j + 1) * LANES]
        acc = None
        for sub in range(heads_per_slab):
            r = j * heads_per_slab + sub
            diff = acs[:, r:r + 1] - acs_t[r:r + 1, :]
            decay = jnp.exp(jnp.where(causal, diff, -jnp.inf))
            m = (cb * decay).astype(BF16)
            x_head = jnp.where(lane_head == sub, x_slab, 0.0).astype(BF16)
            part = jnp.dot(m, x_head, preferred_element_type=F32)
            acc = part if acc is None else acc + part
        slabs.append(acc)
    y_diag = jnp.concatenate(slabs, axis=1)

    s_prev = state_ref[...]
    y_off = jnp.dot(cm, s_prev.astype(BF16), preferred_element_type=F32) * decay_in_x
    y = y_diag + y_off + dskip_ref[pl.ds(g, 1), :] * xs
    b_t = bm.astype(F32).T.astype(BF16)
    s_new = jnp.dot(b_t, (xs * decay_out_dt_x).astype(BF16), preferred_element_type=F32)
    state_ref[...] = s_prev * decay_in_x[q - 1:q, :] + s_new

    z = z_ref[...].astype(F32)
    yg = y * (z * _sigmoid(z))
    ms = jnp.mean(yg * yg, axis=-1, keepdims=True)
    o_ref[...] = (yg * lax.rsqrt(ms + RMS_EPS) * ng_ref[...]).astype(o_ref.dtype)


def _ssd_scan(xbc, z, dt, alog, dskip, norm_g, expand_mat, *, batch, seq, d_inner):
    n_groups, d_state, chunk = SSD_N_GROUPS, SSD_D_STATE, math.gcd(SSD_CHUNK, seq)
    gw = d_inner // n_groups
    heads_per_group = gw // SSD_HEAD_DIM
    n_chunks = seq // chunk
    b_col0 = d_inner // d_state
    c_col0 = b_col0 + n_groups
    row = lambda b, g, c: b * n_chunks + c
    return pl.pallas_call(
        functools.partial(_ssd_kernel, heads_per_group=heads_per_group, head_dim=SSD_HEAD_DIM),
        grid=(batch, n_groups, n_chunks),
        in_specs=[
            pl.BlockSpec((chunk, gw), lambda b, g, c: (row(b, g, c), g)),
            pl.BlockSpec((chunk, d_state), lambda b, g, c: (row(b, g, c), b_col0 + g)),
            pl.BlockSpec((chunk, d_state), lambda b, g, c: (row(b, g, c), c_col0 + g)),
            pl.BlockSpec((chunk, gw), lambda b, g, c: (row(b, g, c), g)),
            pl.BlockSpec((chunk, LANES), lambda b, g, c: (row(b, g, c), g)),
            pl.BlockSpec((n_groups, LANES), lambda b, g, c: (0, 0)),
            pl.BlockSpec((n_groups, gw), lambda b, g, c: (0, 0)),
            pl.BlockSpec((1, gw), lambda b, g, c: (0, g)),
            pl.BlockSpec((LANES, gw), lambda b, g, c: (0, 0)),
        ],
        out_specs=pl.BlockSpec((chunk, gw), lambda b, g, c: (row(b, g, c), g)),
        out_shape=jax.ShapeDtypeStruct((batch * seq, d_inner), BF16),
        scratch_shapes=[pltpu.VMEM((d_state, gw), F32)],
        compiler_params=_cparams(3),
        name="ssd_scan",
    )(xbc, xbc, xbc, z, dt, alog, dskip, norm_g, expand_mat)


def _outproj_ln_kernel(y_ref, w_ref, x_ref, g_ref, b_ref, o_ref, obf_ref, *, alpha):
    h = jnp.dot(y_ref[...], w_ref[...], preferred_element_type=F32)
    v = alpha * x_ref[...] + h
    mu = jnp.mean(v, axis=-1, keepdims=True)
    d = v - mu
    var = jnp.mean(d * d, axis=-1, keepdims=True)
    out = d * lax.rsqrt(var + LN_EPS) * g_ref[...] + b_ref[...]
    o_ref[...] = out
    obf_ref[...] = out.astype(BF16)


def _outproj_ln(y, w, x, ln_g, ln_b, *, alpha, tm):
    m, k_dim = y.shape
    d = w.shape[1]
    return pl.pallas_call(
        functools.partial(_outproj_ln_kernel, alpha=alpha),
        grid=(m // tm,),
        in_specs=[pl.BlockSpec((tm, k_dim), lambda i: (i, 0)),
                  pl.BlockSpec((k_dim, d), lambda i: (0, 0), pipeline_mode=pl.Buffered(1)),
                  pl.BlockSpec((tm, d), lambda i: (i, 0)),
                  pl.BlockSpec((1, d), lambda i: (0, 0)),
                  pl.BlockSpec((1, d), lambda i: (0, 0))],
        out_specs=[pl.BlockSpec((tm, d), lambda i: (i, 0)),
                   pl.BlockSpec((tm, d), lambda i: (i, 0))],
        out_shape=[jax.ShapeDtypeStruct((m, d), F32), jax.ShapeDtypeStruct((m, d), BF16)],
        compiler_params=_cparams(1),
        name="outproj_ln",
    )(y, w, x, ln_g, ln_b)


def _dsa_latent_kernel(x_ref, w_ref, qg_ref, kvg_ref, cq_ref, ckv_ref, ckvt_ref, kit_ref, slab_ref):
    p = jnp.dot(x_ref[...], w_ref[...], preferred_element_type=F32)
    q_rank = cq_ref.shape[1]
    kv_rank = ckv_ref.shape[1]
    tm = p.shape[0]
    cq = p[:, :q_rank]
    ckv = p[:, q_rank:q_rank + kv_rank]
    slab = p[:, q_rank + kv_rank:]
    cq = cq * lax.rsqrt(jnp.mean(cq * cq, axis=-1, keepdims=True) + RMS_EPS) * qg_ref[...]
    ckv = ckv * lax.rsqrt(jnp.mean(ckv * ckv, axis=-1, keepdims=True) + RMS_EPS) * kvg_ref[...]
    cq_ref[...] = cq.astype(BF16)
    ckv_ref[...] = ckv.astype(BF16)
    ckv_t = ckv.T
    tile = ckvt_ref.shape[3]
    for j in range(tm // tile):
        ckvt_ref[0, j] = ckv_t[:, j * tile:(j + 1) * tile].astype(BF16)
    slab_t = slab.T
    row = lax.broadcasted_iota(jnp.int32, slab_t.shape, 0)
    kit_ref[0, 0] = jnp.where(row < IDX_HEAD_DIM, slab_t, 0.0).astype(BF16)
    slab_ref[...] = slab


def _dsa_latent(x, w, q_gain, kv_gain, *, batch, seq):
    tm = IDX_TQ
    n_t = seq // tm
    k_dim, n_dim = w.shape
    per = tm // ATT_TQ
    return pl.pallas_call(
        _dsa_latent_kernel,
        grid=(batch, n_t),
        in_specs=[pl.BlockSpec((tm, k_dim), lambda b, t: (b * n_t + t, 0)),
                  pl.BlockSpec((k_dim, n_dim), lambda b, t: (0, 0)),
                  pl.BlockSpec((1, DSA_Q_RANK), lambda b, t: (0, 0)),
                  pl.BlockSpec((1, DSA_KV_RANK), lambda b, t: (0, 0))],
        out_specs=[pl.BlockSpec((tm, DSA_Q_RANK), lambda b, t: (b * n_t + t, 0)),
                   pl.BlockSpec((tm, DSA_KV_RANK), lambda b, t: (b * n_t + t, 0)),
                   pl.BlockSpec((1, per, DSA_KV_RANK, ATT_TQ), lambda b, t: (b, t, 0, 0)),
                   pl.BlockSpec((1, 1, LANES, tm), lambda b, t: (b, t, 0, 0)),
                   pl.BlockSpec((tm, LANES), lambda b, t: (b * n_t + t, 0))],
        out_shape=[jax.ShapeDtypeStruct((batch * seq, DSA_Q_RANK), BF16),
                   jax.ShapeDtypeStruct((batch * seq, DSA_KV_RANK), BF16),
                   jax.ShapeDtypeStruct((batch, seq // ATT_TQ, DSA_KV_RANK, ATT_TQ), BF16),
                   jax.ShapeDtypeStruct((batch, n_t, LANES, tm), BF16),
                   jax.ShapeDtypeStruct((batch * seq, LANES), F32)],
        compiler_params=_cparams(2),
        name="dsa_latent",
    )(x, w, q_gain, kv_gain)


def _dsa_query_kernel(cq_ref, wuq_ref, wuk_ref, widx_ref, ql_ref, qi_ref, *, scale):
    cq = cq_ref[...]

    @pl.when(pl.program_id(2) == 0)
    def _():
        qi_ref[...] = jnp.dot(cq, widx_ref[...], preferred_element_type=F32).astype(BF16)

    for j in range(wuq_ref.shape[0]):
        qh = jnp.dot(cq, wuq_ref[j], preferred_element_type=F32).astype(BF16)
        ql = lax.dot_general(qh, wuk_ref[j], (((1,), (1,)), ((), ())), preferred_element_type=F32)
        ql_ref[0, j] = (ql * scale).astype(BF16)


def _dsa_query(cq, wuq, wuk, widx, *, batch, seq, scale):
    tm = min(seq, 1024)
    hb = 8
    n_t = seq // tm
    n_heads = wuq.shape[0]
    return pl.pallas_call(
        functools.partial(_dsa_query_kernel, scale=scale),
        grid=(batch, n_t, n_heads // hb),
        in_specs=[pl.BlockSpec((tm, DSA_Q_RANK), lambda b, t, h: (b * n_t + t, 0)),
                  pl.BlockSpec((hb, DSA_Q_RANK, DSA_HEAD_DIM), lambda b, t, h: (h, 0, 0)),
                  pl.BlockSpec((hb, DSA_KV_RANK, DSA_HEAD_DIM), lambda b, t, h: (h, 0, 0)),
                  pl.BlockSpec(widx.shape, lambda b, t, h: (0, 0))],
        out_specs=[pl.BlockSpec((1, hb, tm, DSA_KV_RANK), lambda b, t, h: (b, h, t, 0)),
                   pl.BlockSpec((tm, widx.shape[1]), lambda b, t, h: (b * n_t + t, 0))],
        out_shape=[jax.ShapeDtypeStruct((batch, n_heads, seq, DSA_KV_RANK), BF16),
                   jax.ShapeDtypeStruct((batch * seq, widx.shape[1]), BF16)],
        compiler_params=_cparams(3),
        name="dsa_query",
    )(cq, wuq, wuk, widx)


def _indexer_kernel(qi_ref, slab_ref, kit_ref, bias_ref, sc_ref, *, k_top, w_scale, n_iter):
    i = pl.program_id(1)
    n_t, tq, kt_w = sc_ref.shape
    w = slab_ref[...] * w_scale
    t_pos = i * tq + lax.broadcasted_iota(jnp.int32, (tq, kt_w), 0)
    neg_inf = jnp.full((tq, kt_w), -jnp.inf, F32)

    for kt in range(n_t):
        @pl.when(kt <= i)
        def _(kt=kt):
            keys = kit_ref[0, kt]
            acc = jnp.zeros((tq, kt_w), F32)
            for h in range(IDX_N_HEADS):
                s = jnp.dot(qi_ref[:, h * LANES:(h + 1) * LANES], keys, preferred_element_type=F32)
                acc = acc + jnp.maximum(s, 0.0) * w[:, IDX_HEAD_DIM + h:IDX_HEAD_DIM + h + 1]
            s_pos = kt * kt_w + lax.broadcasted_iota(jnp.int32, (tq, kt_w), 1)
            sc_ref[kt] = jnp.where(s_pos <= t_pos, acc, neg_inf)

        @pl.when(kt > i)
        def _(kt=kt):
            sc_ref[kt] = neg_inf

    def bounds(kt, carry):
        lo, hi = carry
        s = sc_ref[kt]
        hi = jnp.maximum(hi, jnp.max(s, axis=-1, keepdims=True))
        lo = jnp.minimum(lo, jnp.min(jnp.where(s == -jnp.inf, jnp.inf, s), axis=-1, keepdims=True))
        return lo, hi

    lo0 = jnp.full((tq, 1), jnp.inf, F32)
    hi0 = jnp.full((tq, 1), -jnp.inf, F32)
    lo, hi = lax.fori_loop(0, i + 1, bounds, (lo0, hi0))

    def halve(_, carry):
        lo, hi = carry
        mid = 0.5 * lo + 0.5 * hi
        mid_b = jnp.broadcast_to(mid, (tq, LANES))

        def count(kt, cnt):
            s = sc_ref[kt]
            for j in range(kt_w // LANES):
                cnt = cnt + jnp.where(s[:, j * LANES:(j + 1) * LANES] >= mid_b, 1.0, 0.0)
            return cnt

        cnt = lax.fori_loop(0, i + 1, count, jnp.zeros((tq, LANES), F32))
        enough = jnp.sum(cnt, axis=-1, keepdims=True) >= k_top
        return jnp.where(enough, mid, lo), jnp.where(enough, hi, mid)

    lo, hi = lax.fori_loop(0, n_iter, halve, (lo, hi))

    sub = bias_ref.shape[3]
    for kt in range(n_t):
        bias = jnp.where(sc_ref[kt] >= lo, 0.0, -jnp.inf)
        for rq in range(tq // sub):
            for cq in range(kt_w // sub):
                bias_ref[0, rq, kt * (kt_w // sub) + cq] = bias[rq * sub:(rq + 1) * sub, cq * sub:(cq + 1) * sub]


def _dsa_indexer(qi, slab, kit, *, batch, seq, k_top, w_scale):
    tq = IDX_TQ
    n_t = seq // tq
    n_att = seq // ATT_TQ
    return pl.pallas_call(
        functools.partial(_indexer_kernel, k_top=k_top, w_scale=w_scale, n_iter=BISECT_ITERS),
        grid=(batch, n_t),
        in_specs=[pl.BlockSpec((tq, qi.shape[1]), lambda b, i: (b * n_t + i, 0)),
                  pl.BlockSpec((tq, LANES), lambda b, i: (b * n_t + i, 0)),
                  pl.BlockSpec((1, n_t, LANES, tq), lambda b, i: (b, 0, 0, 0))],
        out_specs=pl.BlockSpec((1, tq // ATT_TQ, n_att, ATT_TQ, ATT_TQ), lambda b, i: (b, i, 0, 0, 0)),
        out_shape=jax.ShapeDtypeStruct((batch, n_att, n_att, ATT_TQ, ATT_TQ), F32),
        scratch_shapes=[pltpu.VMEM((n_t, tq, tq), F32)],
        compiler_params=_cparams(2),
        name="dsa_indexer",
    )(qi, slab, kit)


def _attn_kernel(ql_ref, bias_ref, kt_ref, v_ref, wuv_ref, gate_ref, o_ref,
                 s_scr, m_scr, l_scr, acc_scr, og_scr, *, hb):
    i = pl.program_id(1)
    n_heads, tq = ql_ref.shape[1], ql_ref.shape[2]
    kt_w = kt_ref.shape[3]
    rows = hb * tq

    def head_batch(g, carry):
        q = ql_ref[0, pl.ds(g * hb, hb)].reshape(rows, ql_ref.shape[3])
        m_scr[...] = jnp.full(m_scr.shape, -jnp.inf, F32)

        def scores(kt, c):
            s = jnp.dot(q, kt_ref[0, kt], preferred_element_type=F32)
            s = (s.reshape(hb, tq, kt_w) + bias_ref[0, 0, kt][None]).reshape(rows, kt_w)
            s_scr[kt] = s
            mm = m_scr[...]
            for j in range(kt_w // LANES):
                mm = jnp.maximum(mm, s[:, j * LANES:(j + 1) * LANES])
            m_scr[...] = mm
            return c

        lax.fori_loop(0, i + 1, scores, 0)
        m = jnp.max(m_scr[...], axis=-1, keepdims=True)
        l_scr[...] = jnp.zeros(l_scr.shape, F32)
        acc_scr[...] = jnp.zeros(acc_scr.shape, F32)

        def weighted(kt, c):
            p = jnp.exp2(s_scr[kt] - m)
            ll = l_scr[...]
            for j in range(kt_w // LANES):
                ll = ll + p[:, j * LANES:(j + 1) * LANES]
            l_scr[...] = ll
            vals = v_ref[pl.ds(pl.multiple_of(kt * kt_w, kt_w), kt_w), :]
            acc_scr[...] += jnp.dot(p.astype(BF16), vals, preferred_element_type=F32)
            return c

        lax.fori_loop(0, i + 1, weighted, 0)
        denom = jnp.sum(l_scr[...], axis=-1, keepdims=True)
        o_lat = (acc_scr[...] / denom).astype(BF16)
        for j in range(hb):
            og_scr[g * hb + j] = jnp.dot(o_lat[j * tq:(j + 1) * tq], wuv_ref[g * hb + j],
                                         preferred_element_type=F32)
        return carry

    lax.fori_loop(0, n_heads // hb, head_batch, 0)
    hd = og_scr.shape[2]
    for h in range(n_heads):
        gate = gate_ref[:, h * hd:(h + 1) * hd].astype(F32)
        o_ref[:, h * hd:(h + 1) * hd] = (og_scr[h] * gate).astype(o_ref.dtype)


def _dsa_attention(ql, bias, ckv_t, ckv, wuv, gate, *, batch, seq):
    tq = ATT_TQ
    n_q = seq // tq
    hb = ATT_HEADS_PER_STEP
    n_heads, kv_rank, hd = wuv.shape
    width = n_heads * hd
    return pl.pallas_call(
        functools.partial(_attn_kernel, hb=hb),
        grid=(batch, n_q),
        in_specs=[pl.BlockSpec((1, n_heads, tq, kv_rank), lambda b, i: (b, 0, i, 0)),
                  pl.BlockSpec((1, 1, n_q, tq, tq), lambda b, i: (b, i, 0, 0, 0)),
                  pl.BlockSpec((1, n_q, kv_rank, tq), lambda b, i: (b, 0, 0, 0)),
                  pl.BlockSpec((seq, kv_rank), lambda b, i: (b, 0)),
                  pl.BlockSpec((n_heads, kv_rank, hd), lambda b, i: (0, 0, 0)),
                  pl.BlockSpec((tq, width), lambda b, i: (b * n_q + i, 0))],
        out_specs=pl.BlockSpec((tq, width), lambda b, i: (b * n_q + i, 0)),
        out_shape=jax.ShapeDtypeStruct((batch * seq, width), BF16),
        scratch_shapes=[pltpu.VMEM((n_q, hb * tq, tq), F32),
                        pltpu.VMEM((hb * tq, LANES), F32),
                        pltpu.VMEM((hb * tq, LANES), F32),
                        pltpu.VMEM((hb * tq, kv_rank), F32),
                        pltpu.VMEM((n_heads, tq, hd), F32)],
        compiler_params=_cparams(2),
        name="dsa_attention",
    )(ql, bias, ckv_t, ckv, wuv, gate)


def _ssd_layer(x, x_bf, w_in, conv_w, conv_b, dt_bias, a_log, d_skip, norm_g, w_out, ln_g, ln_b,
               *, batch, seq, alpha):
    d_inner = w_out.shape[0]
    n_heads = a_log.shape[0]
    n_groups = SSD_N_GROUPS
    heads_per_group = n_heads // n_groups
    conv_dim = conv_w.shape[1]
    gw = d_inner // n_groups
    assert gw % LANES == 0 and heads_per_group <= LANES and SSD_D_STATE == LANES
    assert LANES % SSD_HEAD_DIM == 0 and heads_per_group % (LANES // SSD_HEAD_DIM) == 0

    def per_group_lanes(v):
        lead = v.shape[:-1]
        v = v.reshape(lead + (n_groups, heads_per_group))
        v = jnp.pad(v, [(0, 0)] * len(lead) + [(0, 0), (0, LANES - heads_per_group)])
        return v.reshape(lead + (n_groups * LANES,))

    w_z = w_in[:, :d_inner].astype(BF16)
    w_xbc = w_in[:, d_inner:d_inner + conv_dim].astype(BF16)
    w_dt = per_group_lanes(w_in[:, d_inner + conv_dim:]).astype(BF16)
    dt_b = per_group_lanes(dt_bias)[None, :]
    alog = per_group_lanes(a_log).reshape(n_groups, LANES)
    dskip = jnp.repeat(d_skip, SSD_HEAD_DIM).reshape(n_groups, gw)
    lane_of = jnp.arange(gw) // SSD_HEAD_DIM
    expand_mat = (jnp.arange(LANES)[:, None] == lane_of[None, :]).astype(BF16)

    z = _proj(x_bf, w_z, batch=batch, seq=seq, tn=512, epilogue="none")
    xbc = _proj(x_bf, w_xbc, batch=batch, seq=seq, tn=512, epilogue="conv_silu",
                extra=(conv_w, conv_b[None, :]))
    dt = _proj(x_bf, w_dt, batch=batch, seq=seq, tn=512, epilogue="softplus", extra=(dt_b,), out_dtype=F32)
    yn = _ssd_scan(xbc, z, dt, alog, dskip, norm_g[None, :], expand_mat, batch=batch, seq=seq, d_inner=d_inner)
    return _outproj_ln(yn, w_out.astype(BF16), x, ln_g[None, :], ln_b[None, :], alpha=alpha, tm=512)


def _dsa_layer(x, x_bf, w_in, q_norm_g, kv_norm_g, w_uq, w_uk, w_uv, w_idx_q, w_out, ln_g, ln_b,
               *, batch, seq, alpha):
    assert seq % IDX_TQ == 0 and IDX_TQ % ATT_TQ == 0 and IDX_HEAD_DIM + IDX_N_HEADS <= LANES
    small = DSA_Q_RANK + DSA_KV_RANK + IDX_HEAD_DIM + IDX_N_HEADS
    w_small = jnp.pad(w_in[:, :small], ((0, 0), (0, DSA_Q_RANK + DSA_KV_RANK + LANES - small))).astype(BF16)
    w_gate = w_in[:, small:].astype(BF16)
    wuq = w_uq.reshape(DSA_Q_RANK, DSA_N_HEADS, DSA_HEAD_DIM).transpose(1, 0, 2).astype(BF16)
    widx = w_idx_q.reshape(DSA_Q_RANK, IDX_N_HEADS, IDX_HEAD_DIM)
    widx = jnp.pad(widx, ((0, 0), (0, 0), (0, LANES - IDX_HEAD_DIM))).reshape(DSA_Q_RANK, IDX_N_HEADS * LANES)
    k_top = min(IDX_TOPK, seq // 4)

    cq, ckv, ckv_t, kit, slab = _dsa_latent(x_bf, w_small, q_norm_g[None, :], kv_norm_g[None, :],
                                            batch=batch, seq=seq)
    gate = _proj(x_bf, w_gate, batch=batch, seq=seq, tn=512, epilogue="silu")
    ql, qi = _dsa_query(cq, wuq, w_uk.astype(BF16), widx.astype(BF16), batch=batch, seq=seq,
                        scale=DSA_HEAD_DIM ** -0.5 * math.log2(math.e))
    bias = _dsa_indexer(qi, slab, kit, batch=batch, seq=seq, k_top=k_top,
                        w_scale=IDX_N_HEADS ** -0.5 * IDX_HEAD_DIM ** -0.5)
    og = _dsa_attention(ql, bias, ckv_t, ckv, w_uv.astype(BF16), gate, batch=batch, seq=seq)
    return _outproj_ln(og, w_out.astype(BF16), x, ln_g[None, :], ln_b[None, :], alpha=alpha, tm=512)


def kernel(x, ssd_w_in, ssd_conv_w, ssd_conv_b, ssd_dt_bias, ssd_a_log, ssd_d_skip, ssd_norm_g, ssd_w_out,
           dsa_w_in, dsa_q_norm_g, dsa_kv_norm_g, dsa_w_uq, dsa_w_uk, dsa_w_uv, dsa_w_idx_q, dsa_w_out,
           ln_g, ln_b):
    batch, seq, d_model = x.shape
    depth = ln_g.shape[0]
    alpha = (2.0 * depth) ** 0.25
    xf = x.reshape(batch * seq, d_model)
    x_bf = xf.astype(BF16)
    for i in range(depth):
        j = i // 2
        if i % 2 == 0:
            xf, x_bf = _ssd_layer(xf, x_bf, ssd_w_in[j], ssd_conv_w[j], ssd_conv_b[j], ssd_dt_bias[j],
                                  ssd_a_log[j], ssd_d_skip[j], ssd_norm_g[j], ssd_w_out[j], ln_g[i], ln_b[i],
                                  batch=batch, seq=seq, alpha=alpha)
        else:
            xf, x_bf = _dsa_layer(xf, x_bf, dsa_w_in[j], dsa_q_norm_g[j], dsa_kv_norm_g[j], dsa_w_uq[j],
                                  dsa_w_uk[j], dsa_w_uv[j], dsa_w_idx_q[j], dsa_w_out[j], ln_g[i], ln_b[i],
                                  batch=batch, seq=seq, alpha=alpha)
    return xf.reshape(batch, seq, d_model)
```

```python
import functools
import math

import jax
import jax.numpy as jnp
from jax import lax
from jax.experimental import pallas as pl
from jax.experimental.pallas import tpu as pltpu

F32 = jnp.float32
BF16 = jnp.bfloat16

V7X_VMEM_BYTES = 64 * 1024 * 1024
LANES = 128
SUBLANES = 8
VMEM_LIMIT_BYTES = V7X_VMEM_BYTES - 8 * 1024 * 1024

LN_EPS = 1e-5
RMS_EPS = 1e-6

SSD_HEAD_DIM = 64
SSD_N_GROUPS = 8
SSD_D_STATE = 128
SSD_CHUNK = 256
DSA_N_HEADS = 32
DSA_HEAD_DIM = 128
DSA_Q_RANK = 512
DSA_KV_RANK = 256
IDX_N_HEADS = 16
IDX_HEAD_DIM = 64
IDX_TOPK = 256

ATT_TQ = 256
ATT_STEP_ELEMS = 4 * 1024 * 1024
ATT_CHUNK_ELEMS = 1024 * 1024
IDX_TQ = 512
BISECT_ITERS = 32


def _cparams(n_axes, flags=None):
    return pltpu.CompilerParams(dimension_semantics=("arbitrary",) * n_axes,
                                vmem_limit_bytes=VMEM_LIMIT_BYTES, flags=flags)


def _sigmoid(v):
    return 1.0 / (1.0 + jnp.exp(-v))


def _proj_kernel(x_ref, w_ref, *rest, epilogue):
    y = jnp.dot(x_ref[...], w_ref[...], preferred_element_type=F32)
    if epilogue == "none":
        (o_ref,) = rest
    elif epilogue == "silu":
        (o_ref,) = rest
        y = y * _sigmoid(y)
    elif epilogue == "softplus":
        b_ref, o_ref = rest
        y = y + b_ref[...]
        y = jnp.maximum(y, 0.0) + jnp.log1p(jnp.exp(-jnp.abs(y)))
    elif epilogue == "conv_silu":
        cw_ref, cb_ref, o_ref = rest
        k_taps = cw_ref.shape[0]
        row = lax.broadcasted_iota(jnp.int32, y.shape, 0)
        acc = cb_ref[...] + cw_ref[k_taps - 1:k_taps, :] * y
        for back in range(1, k_taps):
            shifted = jnp.where(row >= back, pltpu.roll(y, back, 0), 0.0)
            acc = acc + cw_ref[k_taps - 1 - back:k_taps - back, :] * shifted
        y = acc * _sigmoid(acc)
    else:
        raise ValueError(epilogue)
    o_ref[...] = y.astype(o_ref.dtype)


def _proj(x, w, *, batch, seq, tn, epilogue, extra=(), out_dtype=BF16):
    k_dim, n_dim = w.shape
    assert n_dim % tn == 0 and tn % LANES == 0
    extra_specs = [pl.BlockSpec((e.shape[0], tn), lambda b, j: (0, j)) for e in extra]
    return pl.pallas_call(
        functools.partial(_proj_kernel, epilogue=epilogue),
        grid=(batch, n_dim // tn),
        in_specs=[pl.BlockSpec((seq, k_dim), lambda b, j: (b, 0)),
                  pl.BlockSpec((k_dim, tn), lambda b, j: (0, j))] + extra_specs,
        out_specs=pl.BlockSpec((seq, tn), lambda b, j: (b, j)),
        out_shape=jax.ShapeDtypeStruct((batch * seq, n_dim), out_dtype),
        compiler_params=_cparams(2),
        name="proj_" + epilogue,
    )(x, w, *extra)


def _cumsum_rows(v):
    n = v.shape[0]
    row = lax.broadcasted_iota(jnp.int32, v.shape, 0)
    shift = 1
    while shift < n:
        v = v + jnp.where(row >= shift, pltpu.roll(v, shift, 0), 0.0)
        shift *= 2
    return v


def _ssd_kernel(xs_ref, b_ref, c_ref, z_ref, dt_ref, alog_ref, dskip_ref, ng_ref, e_ref, o_ref,
                state_ref, *, heads_per_group, head_dim):
    g = pl.program_id(1)
    chunk = pl.program_id(2)
    q = xs_ref.shape[0]

    @pl.when(chunk == 0)
    def _():
        state_ref[...] = jnp.zeros_like(state_ref)

    xs = xs_ref[...].astype(F32)
    bm = b_ref[...]
    cm = c_ref[...]
    dt = dt_ref[...]
    a = -jnp.exp(alog_ref[pl.ds(g, 1), :])
    acs = _cumsum_rows(dt * a)
    acs_t = acs.T
    last = acs[q - 1:q, :]
    expand_mat = e_ref[...]

    def expand(v):
        return jnp.dot(v.astype(BF16), expand_mat, preferred_element_type=F32)

    def expand_hi_lo(v):
        hi = v.astype(BF16)
        lo = (v - hi.astype(F32)).astype(BF16)
        return (jnp.dot(hi, expand_mat, preferred_element_type=F32)
                + jnp.dot(lo, expand_mat, preferred_element_type=F32))

    dt_x = expand(dt)
    decay_in_x = expand_hi_lo(jnp.exp(acs))
    decay_out_dt_x = expand(jnp.exp(last - acs) * dt)
    xdt = xs * dt_x

    cb = lax.dot_general(cm, bm, (((1,), (1,)), ((), ())), preferred_element_type=F32)
    li = lax.broadcasted_iota(jnp.int32, (q, q), 0)
    si = lax.broadcasted_iota(jnp.int32, (q, q), 1)
    causal = li >= si
    heads_per_slab = LANES // head_dim
    lane_head = lax.broadcasted_iota(jnp.int32, (q, LANES), 1) // head_dim
    slabs = []
    for j in range(heads_per_group // heads_per_slab):
        x_slab = xdt[:, j * LANES:(j + 1) * LANES]
        acc = None
        for sub in range(heads_per_slab):
            r = j * heads_per_slab + sub
            diff = acs[:, r:r + 1] - acs_t[r:r + 1, :]
            decay = jnp.exp(jnp.where(causal, diff, -jnp.inf))
            m = (cb * decay).astype(BF16)
            x_head = jnp.where(lane_head == sub, x_slab, 0.0).astype(BF16)
            part = jnp.dot(m, x_head, preferred_element_type=F32)
            acc = part if acc is None else acc + part
        slabs.append(acc)
    y_diag = jnp.concatenate(slabs, axis=1)

    s_prev = state_ref[...]
    y_off = jnp.dot(cm, s_prev.astype(BF16), preferred_element_type=F32) * decay_in_x
    y = y_diag + y_off + dskip_ref[pl.ds(g, 1), :] * xs
    b_t = bm.astype(F32).T.astype(BF16)
    s_new = jnp.dot(b_t, (xs * decay_out_dt_x).astype(BF16), preferred_element_type=F32)
    state_ref[...] = s_prev * decay_in_x[q - 1:q, :] + s_new

    z = z_ref[...].astype(F32)
    yg = y * (z * _sigmoid(z))
    ms = jnp.mean(yg * yg, axis=-1, keepdims=True)
    o_ref[...] = (yg * lax.rsqrt(ms + RMS_EPS) * ng_ref[...]).astype(o_ref.dtype)


def _ssd_scan(xbc, z, dt, alog, dskip, norm_g, expand_mat, *, batch, seq, d_inner):
    n_groups, d_state, chunk = SSD_N_GROUPS, SSD_D_STATE, math.gcd(SSD_CHUNK, seq)
    gw = d_inner // n_groups
    heads_per_group = gw // SSD_HEAD_DIM
    n_chunks = seq // chunk
    b_col0 = d_inner // d_state
    c_col0 = b_col0 + n_groups
    row = lambda b, g, c: b * n_chunks + c
    return pl.pallas_call(
        functools.partial(_ssd_kernel, heads_per_group=heads_per_group, head_dim=SSD_HEAD_DIM),
        grid=(batch, n_groups, n_chunks),
        in_specs=[
            pl.BlockSpec((chunk, gw), lambda b, g, c: (row(b, g, c), g)),
            pl.BlockSpec((chunk, d_state), lambda b, g, c: (row(b, g, c), b_col0 + g)),
            pl.BlockSpec((chunk, d_state), lambda b, g, c: (row(b, g, c), c_col0 + g)),
            pl.BlockSpec((chunk, gw), lambda b, g, c: (row(b, g, c), g)),
            pl.BlockSpec((chunk, LANES), lambda b, g, c: (row(b, g, c), g)),
            pl.BlockSpec((n_groups, LANES), lambda b, g, c: (0, 0)),
            pl.BlockSpec((n_groups, gw), lambda b, g, c: (0, 0)),
            pl.BlockSpec((1, gw), lambda b, g, c: (0, g)),
            pl.BlockSpec((LANES, gw), lambda b, g, c: (0, 0)),
        ],
        out_specs=pl.BlockSpec((chunk, gw), lambda b, g, c: (row(b, g, c), g)),
        out_shape=jax.ShapeDtypeStruct((batch * seq, d_inner), BF16),
        scratch_shapes=[pltpu.VMEM((d_state, gw), F32)],
        compiler_params=_cparams(3),
        name="ssd_scan",
    )(xbc, xbc, xbc, z, dt, alog, dskip, norm_g, expand_mat)


def _outproj_ln_kernel(y_ref, w_ref, x_ref, g_ref, b_ref, o_ref, obf_ref, *, alpha):
    h = jnp.dot(y_ref[...], w_ref[...], preferred_element_type=F32)
    v = alpha * x_ref[...] + h
    mu = jnp.mean(v, axis=-1, keepdims=True)
    d = v - mu
    var = jnp.mean(d * d, axis=-1, keepdims=True)
    out = d * lax.rsqrt(var + LN_EPS) * g_ref[...] + b_ref[...]
    o_ref[...] = out
    obf_ref[...] = out.astype(BF16)


def _outproj_ln(y, w, x, ln_g, ln_b, *, alpha, tm):
    m, k_dim = y.shape
    d = w.shape[1]
    return pl.pallas_call(
        functools.partial(_outproj_ln_kernel, alpha=alpha),
        grid=(m // tm,),
        in_specs=[pl.BlockSpec((tm, k_dim), lambda i: (i, 0)),
                  pl.BlockSpec((k_dim, d), lambda i: (0, 0), pipeline_mode=pl.Buffered(1)),
                  pl.BlockSpec((tm, d), lambda i: (i, 0)),
                  pl.BlockSpec((1, d), lambda i: (0, 0)),
                  pl.BlockSpec((1, d), lambda i: (0, 0))],
        out_specs=[pl.BlockSpec((tm, d), lambda i: (i, 0)),
                   pl.BlockSpec((tm, d), lambda i: (i, 0))],
        out_shape=[jax.ShapeDtypeStruct((m, d), F32), jax.ShapeDtypeStruct((m, d), BF16)],
        compiler_params=_cparams(1),
        name="outproj_ln",
    )(y, w, x, ln_g, ln_b)


def _dsa_latent_kernel(x_ref, w_ref, qg_ref, kvg_ref, cq_ref, cqt_ref, ckv_ref, ckvt_ref, slab_ref, slabt_ref):
    p = jnp.dot(x_ref[...], w_ref[...], preferred_element_type=F32)
    q_rank = cq_ref.shape[1]
    kv_rank = ckv_ref.shape[1]
    cq = p[:, :q_rank]
    ckv = p[:, q_rank:q_rank + kv_rank]
    slab = p[:, q_rank + kv_rank:]
    cq = cq * lax.rsqrt(jnp.mean(cq * cq, axis=-1, keepdims=True) + RMS_EPS) * qg_ref[...]
    ckv = ckv * lax.rsqrt(jnp.mean(ckv * ckv, axis=-1, keepdims=True) + RMS_EPS) * kvg_ref[...]
    cq_ref[...] = cq.astype(BF16)
    cqt_ref[0] = cq.T.astype(BF16)
    ckv_ref[...] = ckv.astype(BF16)
    ckvt_ref[0] = ckv.T.astype(BF16)
    slab_ref[...] = slab.astype(BF16)
    slabt_ref[0] = slab.T


def _dsa_latent(x, w, q_gain, kv_gain, *, batch, seq):
    tm = IDX_TQ
    n_t = seq // tm
    k_dim, n_dim = w.shape
    return pl.pallas_call(
        _dsa_latent_kernel,
        grid=(batch, n_t),
        in_specs=[pl.BlockSpec((tm, k_dim), lambda b, t: (b * n_t + t, 0)),
                  pl.BlockSpec((k_dim, n_dim), lambda b, t: (0, 0)),
                  pl.BlockSpec((1, DSA_Q_RANK), lambda b, t: (0, 0)),
                  pl.BlockSpec((1, DSA_KV_RANK), lambda b, t: (0, 0))],
        out_specs=[pl.BlockSpec((tm, DSA_Q_RANK), lambda b, t: (b * n_t + t, 0)),
                   pl.BlockSpec((1, DSA_Q_RANK, tm), lambda b, t: (b, 0, t)),
                   pl.BlockSpec((tm, DSA_KV_RANK), lambda b, t: (b * n_t + t, 0)),
                   pl.BlockSpec((1, DSA_KV_RANK, tm), lambda b, t: (b, 0, t)),
                   pl.BlockSpec((tm, LANES), lambda b, t: (b * n_t + t, 0)),
                   pl.BlockSpec((1, LANES, tm), lambda b, t: (b, 0, t))],
        out_shape=[jax.ShapeDtypeStruct((batch * seq, DSA_Q_RANK), BF16),
                   jax.ShapeDtypeStruct((batch, DSA_Q_RANK, seq), BF16),
                   jax.ShapeDtypeStruct((batch * seq, DSA_KV_RANK), BF16),
                   jax.ShapeDtypeStruct((batch, DSA_KV_RANK, seq), BF16),
                   jax.ShapeDtypeStruct((batch * seq, LANES), BF16),
                   jax.ShapeDtypeStruct((batch, LANES, seq), F32)],
        compiler_params=_cparams(2),
        name="dsa_latent",
    )(x, w, q_gain, kv_gain)


def _dsa_query_kernel(cq_ref, wuq_ref, wuk_ref, ql_ref, *, scale):
    cq = cq_ref[...]
    for j in range(wuq_ref.shape[0]):
        qh = jnp.dot(cq, wuq_ref[j], preferred_element_type=F32).astype(BF16)
        ql = lax.dot_general(qh, wuk_ref[j], (((1,), (1,)), ((), ())), preferred_element_type=F32)
        ql_ref[0, j] = (ql * scale).astype(BF16)


def _dsa_query(cq, wuq, wuk, *, batch, seq, scale):
    tm = min(seq, 1024)
    hb = 8
    n_t = seq // tm
    n_heads = wuq.shape[0]
    return pl.pallas_call(
        functools.partial(_dsa_query_kernel, scale=scale),
        grid=(batch, n_t, n_heads // hb),
        in_specs=[pl.BlockSpec((tm, DSA_Q_RANK), lambda b, t, h: (b * n_t + t, 0)),
                  pl.BlockSpec((hb, DSA_Q_RANK, DSA_HEAD_DIM), lambda b, t, h: (h, 0, 0)),
                  pl.BlockSpec((hb, DSA_KV_RANK, DSA_HEAD_DIM), lambda b, t, h: (h, 0, 0))],
        out_specs=pl.BlockSpec((1, hb, tm, DSA_KV_RANK), lambda b, t, h: (b, h, t, 0)),
        out_shape=jax.ShapeDtypeStruct((batch, n_heads, seq, DSA_KV_RANK), BF16),
        compiler_params=_cparams(3),
        name="dsa_query",
    )(cq, wuq, wuk)


def _indexer_kernel(keys_ref, slabt_ref, cqt_ref, widx_ref, bias_ref, qt_scr, sc_scr, *, k_top, w_scale, n_iter):
    i = pl.program_id(1)
    n_t, kt_w, tq = sc_scr.shape
    cq_t = cqt_ref[0]
    for h in range(IDX_N_HEADS):
        qt_scr[h] = jnp.dot(widx_ref[h], cq_t, preferred_element_type=F32).astype(BF16)
    w_t = slabt_ref[0][IDX_HEAD_DIM:IDX_HEAD_DIM + IDX_N_HEADS, :] * w_scale
    q_pos = i * tq + lax.broadcasted_iota(jnp.int32, (kt_w, tq), 1)

    for kt in range(n_t):
        @pl.when(kt <= i)
        def _(kt=kt):
            keys = keys_ref[kt * kt_w:(kt + 1) * kt_w, :]
            acc = jnp.zeros((kt_w, tq), F32)
            for h in range(IDX_N_HEADS):
                s = jnp.dot(keys, qt_scr[h], preferred_element_type=F32)
                acc = acc + jnp.maximum(s, 0.0) * w_t[h:h + 1, :]
            k_pos = kt * kt_w + lax.broadcasted_iota(jnp.int32, (kt_w, tq), 0)
            sc_scr[kt] = jnp.where(k_pos <= q_pos, acc, -jnp.inf)

    def bounds(kt, carry):
        lo, hi = carry
        s = sc_scr[kt]
        hi = jnp.maximum(hi, jnp.max(s, axis=0, keepdims=True))
        lo = jnp.minimum(lo, jnp.min(jnp.where(s == -jnp.inf, jnp.inf, s), axis=0, keepdims=True))
        return lo, hi

    lo, hi = lax.fori_loop(0, i + 1, bounds,
                           (jnp.full((1, tq), jnp.inf, F32), jnp.full((1, tq), -jnp.inf, F32)))

    def halve(_, carry):
        lo, hi = carry
        mid = 0.5 * lo + 0.5 * hi

        def count(kt, cnt):
            return cnt + jnp.sum(jnp.where(sc_scr[kt] >= mid, 1.0, 0.0), axis=0, keepdims=True)

        cnt = lax.fori_loop(0, i + 1, count, jnp.zeros((1, tq), F32))
        enough = cnt >= k_top
        return jnp.where(enough, mid, lo), jnp.where(enough, hi, mid)

    lo, hi = lax.fori_loop(0, n_iter, halve, (lo, hi))

    for kt in range(n_t):
        @pl.when(kt <= i)
        def _(kt=kt):
            bias_t = jnp.where(sc_scr[kt] >= lo, 0.0, -jnp.inf)
            bias_ref[:, kt * kt_w:(kt + 1) * kt_w] = bias_t.T

        @pl.when(kt > i)
        def _(kt=kt):
            bias_ref[:, kt * kt_w:(kt + 1) * kt_w] = jnp.full((tq, kt_w), -jnp.inf, F32)


def _dsa_indexer(keys, slab_t, cq_t, widx_t, *, batch, seq, k_top, w_scale):
    tq = IDX_TQ
    n_t = seq // tq
    return pl.pallas_call(
        functools.partial(_indexer_kernel, k_top=k_top, w_scale=w_scale, n_iter=BISECT_ITERS),
        grid=(batch, n_t),
        in_specs=[pl.BlockSpec((seq, LANES), lambda b, i: (b, 0)),
                  pl.BlockSpec((1, LANES, tq), lambda b, i: (b, 0, i)),
                  pl.BlockSpec((1, DSA_Q_RANK, tq), lambda b, i: (b, 0, i)),
                  pl.BlockSpec(widx_t.shape, lambda b, i: (0, 0, 0))],
        out_specs=pl.BlockSpec((tq, seq), lambda b, i: (b * n_t + i, 0)),
        out_shape=jax.ShapeDtypeStruct((batch * seq, seq), F32),
        scratch_shapes=[pltpu.VMEM((IDX_N_HEADS, LANES, tq), BF16),
                        pltpu.VMEM((n_t, tq, tq), F32)],
        compiler_params=_cparams(2),
        name="dsa_indexer",
    )(keys, slab_t, cq_t, widx_t)


def _attn_kernel(ql_ref, bias_ref, kt_ref, v_ref, wuv_ref, gate_ref, *rest, n_keys, hc):
    o_ref = rest[-1]
    hb, tq, kv_rank = ql_ref.shape[1:]
    hd = wuv_ref.shape[2]
    rows = hc * tq
    n_chunks = hb // hc

    def scores(c):
        q = ql_ref[0, c * hc:(c + 1) * hc].reshape(rows, kv_rank)
        s = jnp.dot(q, kt_ref[0], preferred_element_type=F32)
        return (s.reshape(hc, tq, n_keys) + bias_ref[...][None]).reshape(rows, n_keys)

    def finish(c, s):
        m = jnp.max(s, axis=-1, keepdims=True)
        p = jnp.exp2(s - m)
        denom = jnp.sum(p, axis=-1, keepdims=True)
        o_lat = jnp.dot(p.astype(BF16), v_ref[:n_keys, :], preferred_element_type=F32)
        o_lat = (o_lat / denom).astype(BF16)
        for j in range(hc):
            h = c * hc + j
            o_h = jnp.dot(o_lat[j * tq:(j + 1) * tq], wuv_ref[h], preferred_element_type=F32)
            gate = gate_ref[:, h * hd:(h + 1) * hd].astype(F32)
            o_ref[:, h * hd:(h + 1) * hd] = (o_h * gate).astype(o_ref.dtype)

    s_next = scores(0)
    for c in range(n_chunks):
        s_cur = s_next
        if c + 1 < n_chunks:
            s_next = scores(c + 1)
        finish(c, s_cur)


def _dsa_attention(ql, bias, ckv_t, ckv, wuv, gate, *, batch, seq):
    tq = ATT_TQ
    n_q = seq // tq
    n_heads, kv_rank, hd = wuv.shape
    width = n_heads * hd
    og = None
    for i in range(n_q):
        n_keys = (i + 1) * tq
        hb = n_heads
        while hb > 1 and hb * tq * n_keys > ATT_STEP_ELEMS:
            hb //= 2
        hc = hb
        while hc > 1 and hc * tq * n_keys > ATT_CHUNK_ELEMS:
            hc //= 2
        in_specs = [pl.BlockSpec((1, hb, tq, kv_rank), lambda b, g, i=i: (b, g, i, 0)),
                    pl.BlockSpec((tq, n_keys), lambda b, g, i=i: (b * n_q + i, 0)),
                    pl.BlockSpec((1, kv_rank, n_keys), lambda b, g: (b, 0, 0)),
                    pl.BlockSpec((seq, kv_rank), lambda b, g: (b, 0)),
                    pl.BlockSpec((hb, kv_rank, hd), lambda b, g: (g, 0, 0)),
                    pl.BlockSpec((tq, hb * hd), lambda b, g, i=i: (b * n_q + i, g))]
        args = [ql, bias, ckv_t, ckv, wuv, gate]
        aliases = {}
        if og is not None:
            in_specs.append(pl.BlockSpec(memory_space=pl.ANY))
            args.append(og)
            aliases = {len(args) - 1: 0}
        og = pl.pallas_call(
            functools.partial(_attn_kernel, n_keys=n_keys, hc=hc),
            grid=(batch, n_heads // hb),
            in_specs=in_specs,
            out_specs=pl.BlockSpec((tq, hb * hd), lambda b, g, i=i: (b * n_q + i, g)),
            out_shape=jax.ShapeDtypeStruct((batch * seq, width), BF16),
            input_output_aliases=aliases,
            compiler_params=_cparams(2),
            name="dsa_attention_q%d" % i,
        )(*args)
    return og


def _ssd_layer(x, x_bf, w_in, conv_w, conv_b, dt_bias, a_log, d_skip, norm_g, w_out, ln_g, ln_b,
               *, batch, seq, alpha):
    d_inner = w_out.shape[0]
    n_heads = a_log.shape[0]
    n_groups = SSD_N_GROUPS
    heads_per_group = n_heads // n_groups
    conv_dim = conv_w.shape[1]
    gw = d_inner // n_groups
    assert gw % LANES == 0 and heads_per_group <= LANES and SSD_D_STATE == LANES
    assert LANES % SSD_HEAD_DIM == 0 and heads_per_group % (LANES // SSD_HEAD_DIM) == 0

    def per_group_lanes(v):
        lead = v.shape[:-1]
        v = v.reshape(lead + (n_groups, heads_per_group))
        v = jnp.pad(v, [(0, 0)] * len(lead) + [(0, 0), (0, LANES - heads_per_group)])
        return v.reshape(lead + (n_groups * LANES,))

    w_z = w_in[:, :d_inner].astype(BF16)
    w_xbc = w_in[:, d_inner:d_inner + conv_dim].astype(BF16)
    w_dt = per_group_lanes(w_in[:, d_inner + conv_dim:]).astype(BF16)
    dt_b = per_group_lanes(dt_bias)[None, :]
    alog = per_group_lanes(a_log).reshape(n_groups, LANES)
    dskip = jnp.repeat(d_skip, SSD_HEAD_DIM).reshape(n_groups, gw)
    lane_of = jnp.arange(gw) // SSD_HEAD_DIM
    expand_mat = (jnp.arange(LANES)[:, None] == lane_of[None, :]).astype(BF16)

    z = _proj(x_bf, w_z, batch=batch, seq=seq, tn=512, epilogue="none")
    xbc = _proj(x_bf, w_xbc, batch=batch, seq=seq, tn=512, epilogue="conv_silu",
                extra=(conv_w, conv_b[None, :]))
    dt = _proj(x_bf, w_dt, batch=batch, seq=seq, tn=512, epilogue="softplus", extra=(dt_b,), out_dtype=F32)
    yn = _ssd_scan(xbc, z, dt, alog, dskip, norm_g[None, :], expand_mat, batch=batch, seq=seq, d_inner=d_inner)
    return _outproj_ln(yn, w_out.astype(BF16), x, ln_g[None, :], ln_b[None, :], alpha=alpha, tm=512)


def _dsa_layer(x, x_bf, w_in, q_norm_g, kv_norm_g, w_uq, w_uk, w_uv, w_idx_q, w_out, ln_g, ln_b,
               *, batch, seq, alpha):
    assert seq % IDX_TQ == 0 and IDX_TQ % ATT_TQ == 0 and IDX_HEAD_DIM + IDX_N_HEADS <= LANES
    small = DSA_Q_RANK + DSA_KV_RANK + IDX_HEAD_DIM + IDX_N_HEADS
    w_small = jnp.pad(w_in[:, :small], ((0, 0), (0, DSA_Q_RANK + DSA_KV_RANK + LANES - small))).astype(BF16)
    w_gate = w_in[:, small:].astype(BF16)
    wuq = w_uq.reshape(DSA_Q_RANK, DSA_N_HEADS, DSA_HEAD_DIM).transpose(1, 0, 2).astype(BF16)
    widx_t = w_idx_q.reshape(DSA_Q_RANK, IDX_N_HEADS, IDX_HEAD_DIM).transpose(1, 2, 0)
    widx_t = jnp.pad(widx_t, ((0, 0), (0, LANES - IDX_HEAD_DIM), (0, 0))).astype(BF16)
    k_top = min(IDX_TOPK, seq // 4)

    cq, cq_t, ckv, ckv_t, slab, slab_t = _dsa_latent(x_bf, w_small, q_norm_g[None, :], kv_norm_g[None, :],
                                                     batch=batch, seq=seq)
    gate = _proj(x_bf, w_gate, batch=batch, seq=seq, tn=512, epilogue="silu")
    ql = _dsa_query(cq, wuq, w_uk.astype(BF16), batch=batch, seq=seq,
                    scale=DSA_HEAD_DIM ** -0.5 * math.log2(math.e))
    bias = _dsa_indexer(slab, slab_t, cq_t, widx_t, batch=batch, seq=seq, k_top=k_top,
                        w_scale=IDX_N_HEADS ** -0.5 * IDX_HEAD_DIM ** -0.5)
    og = _dsa_attention(ql, bias, ckv_t, ckv, w_uv.astype(BF16), gate, batch=batch, seq=seq)
    return _outproj_ln(og, w_out.astype(BF16), x, ln_g[None, :], ln_b[None, :], alpha=alpha, tm=512)


def kernel(x, ssd_w_in, ssd_conv_w, ssd_conv_b, ssd_dt_bias, ssd_a_log, ssd_d_skip, ssd_norm_g, ssd_w_out,
           dsa_w_in, dsa_q_norm_g, dsa_kv_norm_g, dsa_w_uq, dsa_w_uk, dsa_w_uv, dsa_w_idx_q, dsa_w_out,
           ln_g, ln_b):
    batch, seq, d_model = x.shape
    depth = ln_g.shape[0]
    alpha = (2.0 * depth) ** 0.25
    xf = x.reshape(batch * seq, d_model)
    x_bf = xf.astype(BF16)
    for i in range(depth):
        j = i // 2
        if i % 2 == 0:
            xf, x_bf = _ssd_layer(xf, x_bf, ssd_w_in[j], ssd_conv_w[j], ssd_conv_b[j], ssd_dt_bias[j],
                                  ssd_a_log[j], ssd_d_skip[j], ssd_norm_g[j], ssd_w_out[j], ln_g[i], ln_b[i],
                                  batch=batch, seq=seq, alpha=alpha)
        else:
            xf, x_bf = _dsa_layer(xf, x_bf, dsa_w_in[j], dsa_q_norm_g[j], dsa_kv_norm_g[j], dsa_w_uq[j],
                                  dsa_w_uk[j], dsa_w_uv[j], dsa_w_idx_q[j], dsa_w_out[j], ln_g[i], ln_b[i],
                                  batch=batch, seq=seq, alpha=alpha)
    return xf.reshape(batch, seq, d_model)
```

```python
import functools
import math

import jax
import jax.numpy as jnp
from jax import lax
from jax.experimental import pallas as pl
from jax.experimental.pallas import tpu as pltpu

F32 = jnp.float32
BF16 = jnp.bfloat16

V7X_VMEM_BYTES = 64 * 1024 * 1024
LANES = 128
SUBLANES = 8
VMEM_LIMIT_BYTES = V7X_VMEM_BYTES - 8 * 1024 * 1024

LN_EPS = 1e-5
RMS_EPS = 1e-6

SSD_HEAD_DIM = 64
SSD_N_GROUPS = 8
SSD_D_STATE = 128
SSD_CHUNK = 256
DSA_N_HEADS = 32
DSA_HEAD_DIM = 128
DSA_Q_RANK = 512
DSA_KV_RANK = 256
IDX_N_HEADS = 16
IDX_HEAD_DIM = 64
IDX_TOPK = 256

ATT_TQ = 256
ATT_STEP_ELEMS = 4 * 1024 * 1024
ATT_CHUNK_ELEMS = 1024 * 1024
IDX_TQ = 512
BISECT_ITERS = 32


def _cparams(n_axes, flags=None):
    return pltpu.CompilerParams(dimension_semantics=("arbitrary",) * n_axes,
                                vmem_limit_bytes=VMEM_LIMIT_BYTES, flags=flags)


def _sigmoid(v):
    return 1.0 / (1.0 + jnp.exp(-v))


def _proj_kernel(x_ref, w_ref, *rest, epilogue):
    y = jnp.dot(x_ref[...], w_ref[...], preferred_element_type=F32)
    if epilogue == "none":
        (o_ref,) = rest
    elif epilogue == "silu":
        (o_ref,) = rest
        y = y * _sigmoid(y)
    elif epilogue == "softplus":
        b_ref, o_ref = rest
        y = y + b_ref[...]
        y = jnp.maximum(y, 0.0) + jnp.log1p(jnp.exp(-jnp.abs(y)))
    elif epilogue == "conv_silu":
        cw_ref, cb_ref, o_ref = rest
        k_taps = cw_ref.shape[0]
        row = lax.broadcasted_iota(jnp.int32, y.shape, 0)
        acc = cb_ref[...] + cw_ref[k_taps - 1:k_taps, :] * y
        for back in range(1, k_taps):
            shifted = jnp.where(row >= back, pltpu.roll(y, back, 0), 0.0)
            acc = acc + cw_ref[k_taps - 1 - back:k_taps - back, :] * shifted
        y = acc * _sigmoid(acc)
    else:
        raise ValueError(epilogue)
    o_ref[...] = y.astype(o_ref.dtype)


def _proj(x, w, *, batch, seq, tn, epilogue, extra=(), out_dtype=BF16):
    k_dim, n_dim = w.shape
    assert n_dim % tn == 0 and tn % LANES == 0
    extra_specs = [pl.BlockSpec((e.shape[0], tn), lambda b, j: (0, j)) for e in extra]
    return pl.pallas_call(
        functools.partial(_proj_kernel, epilogue=epilogue),
        grid=(batch, n_dim // tn),
        in_specs=[pl.BlockSpec((seq, k_dim), lambda b, j: (b, 0)),
                  pl.BlockSpec((k_dim, tn), lambda b, j: (0, j))] + extra_specs,
        out_specs=pl.BlockSpec((seq, tn), lambda b, j: (b, j)),
        out_shape=jax.ShapeDtypeStruct((batch * seq, n_dim), out_dtype),
        compiler_params=_cparams(2),
        name="proj_" + epilogue,
    )(x, w, *extra)


def _cumsum_rows(v):
    n = v.shape[0]
    row = lax.broadcasted_iota(jnp.int32, v.shape, 0)
    shift = 1
    while shift < n:
        v = v + jnp.where(row >= shift, pltpu.roll(v, shift, 0), 0.0)
        shift *= 2
    return v


def _ssd_kernel(xs_ref, b_ref, c_ref, z_ref, dt_ref, alog_ref, dskip_ref, ng_ref, e_ref, o_ref,
                state_ref, *, heads_per_group, head_dim):
    g = pl.program_id(1)
    chunk = pl.program_id(2)
    q = xs_ref.shape[0]

    @pl.when(chunk == 0)
    def _():
        state_ref[...] = jnp.zeros_like(state_ref)

    xs = xs_ref[...].astype(F32)
    bm = b_ref[...]
    cm = c_ref[...]
    dt = dt_ref[...]
    a = -jnp.exp(alog_ref[pl.ds(g, 1), :])
    acs = _cumsum_rows(dt * a)
    acs_t = acs.T
    last = acs[q - 1:q, :]
    expand_mat = e_ref[...]

    def expand(v):
        return jnp.dot(v.astype(BF16), expand_mat, preferred_element_type=F32)

    def expand_hi_lo(v):
        hi = v.astype(BF16)
        lo = (v - hi.astype(F32)).astype(BF16)
        return (jnp.dot(hi, expand_mat, preferred_element_type=F32)
                + jnp.dot(lo, expand_mat, preferred_element_type=F32))

    dt_x = expand(dt)
    decay_in_x = expand_hi_lo(jnp.exp(acs))
    decay_out_dt_x = expand(jnp.exp(last - acs) * dt)
    xdt = xs * dt_x

    cb = lax.dot_general(cm, bm, (((1,), (1,)), ((), ())), preferred_element_type=F32)
    li = lax.broadcasted_iota(jnp.int32, (q, q), 0)
    si = lax.broadcasted_iota(jnp.int32, (q, q), 1)
    causal = li >= si
    heads_per_slab = LANES // head_dim
    lane_head = lax.broadcasted_iota(jnp.int32, (q, LANES), 1) // head_dim
    slabs = []
    for j in range(heads_per_group // heads_per_slab):
        x_slab = xdt[:, j * LANES:(j + 1) * LANES]
        acc = None
        for sub in range(heads_per_slab):
            r = j * heads_per_slab + sub
            diff = acs[:, r:r + 1] - acs_t[r:r + 1, :]
            decay = jnp.exp(jnp.where(causal, diff, -jnp.inf))
            m = (cb * decay).astype(BF16)
            x_head = jnp.where(lane_head == sub, x_slab, 0.0).astype(BF16)
            part = jnp.dot(m, x_head, preferred_element_type=F32)
            acc = part if acc is None else acc + part
        slabs.append(acc)
    y_diag = jnp.concatenate(slabs, axis=1)

    s_prev = state_ref[...]
    y_off = jnp.dot(cm, s_prev.astype(BF16), preferred_element_type=F32) * decay_in_x
    y = y_diag + y_off + dskip_ref[pl.ds(g, 1), :] * xs
    b_t = bm.astype(F32).T.astype(BF16)
    s_new = jnp.dot(b_t, (xs * decay_out_dt_x).astype(BF16), preferred_element_type=F32)
    state_ref[...] = s_prev * decay_in_x[q - 1:q, :] + s_new

    z = z_ref[...].astype(F32)
    yg = y * (z * _sigmoid(z))
    ms = jnp.mean(yg * yg, axis=-1, keepdims=True)
    o_ref[...] = (yg * lax.rsqrt(ms + RMS_EPS) * ng_ref[...]).astype(o_ref.dtype)


def _ssd_scan(xbc, z, dt, alog, dskip, norm_g, expand_mat, *, batch, seq, d_inner):
    n_groups, d_state, chunk = SSD_N_GROUPS, SSD_D_STATE, math.gcd(SSD_CHUNK, seq)
    gw = d_inner // n_groups
    heads_per_group = gw // SSD_HEAD_DIM
    n_chunks = seq // chunk
    b_col0 = d_inner // d_state
    c_col0 = b_col0 + n_groups
    row = lambda b, g, c: b * n_chunks + c
    return pl.pallas_call(
        functools.partial(_ssd_kernel, heads_per_group=heads_per_group, head_dim=SSD_HEAD_DIM),
        grid=(batch, n_groups, n_chunks),
        in_specs=[
            pl.BlockSpec((chunk, gw), lambda b, g, c: (row(b, g, c), g)),
            pl.BlockSpec((chunk, d_state), lambda b, g, c: (row(b, g, c), b_col0 + g)),
            pl.BlockSpec((chunk, d_state), lambda b, g, c: (row(b, g, c), c_col0 + g)),
            pl.BlockSpec((chunk, gw), lambda b, g, c: (row(b, g, c), g)),
            pl.BlockSpec((chunk, LANES), lambda b, g, c: (row(b, g, c), g)),
            pl.BlockSpec((n_groups, LANES), lambda b, g, c: (0, 0)),
            pl.BlockSpec((n_groups, gw), lambda b, g, c: (0, 0)),
            pl.BlockSpec((1, gw), lambda b, g, c: (0, g)),
            pl.BlockSpec((LANES, gw), lambda b, g, c: (0, 0)),
        ],
        out_specs=pl.BlockSpec((chunk, gw), lambda b, g, c: (row(b, g, c), g)),
        out_shape=jax.ShapeDtypeStruct((batch * seq, d_inner), BF16),
        scratch_shapes=[pltpu.VMEM((d_state, gw), F32)],
        compiler_params=_cparams(3),
        name="ssd_scan",
    )(xbc, xbc, xbc, z, dt, alog, dskip, norm_g, expand_mat)


def _outproj_ln_kernel(y_ref, w_ref, x_ref, g_ref, b_ref, o_ref, obf_ref, *, alpha):
    h = jnp.dot(y_ref[...], w_ref[...], preferred_element_type=F32)
    v = alpha * x_ref[...] + h
    mu = jnp.mean(v, axis=-1, keepdims=True)
    d = v - mu
    var = jnp.mean(d * d, axis=-1, keepdims=True)
    out = d * lax.rsqrt(var + LN_EPS) * g_ref[...] + b_ref[...]
    o_ref[...] = out
    obf_ref[...] = out.astype(BF16)


def _outproj_ln(y, w, x, ln_g, ln_b, *, alpha, tm):
    m, k_dim = y.shape
    d = w.shape[1]
    return pl.pallas_call(
        functools.partial(_outproj_ln_kernel, alpha=alpha),
        grid=(m // tm,),
        in_specs=[pl.BlockSpec((tm, k_dim), lambda i: (i, 0)),
                  pl.BlockSpec((k_dim, d), lambda i: (0, 0), pipeline_mode=pl.Buffered(1)),
                  pl.BlockSpec((tm, d), lambda i: (i, 0)),
                  pl.BlockSpec((1, d), lambda i: (0, 0)),
                  pl.BlockSpec((1, d), lambda i: (0, 0))],
        out_specs=[pl.BlockSpec((tm, d), lambda i: (i, 0)),
                   pl.BlockSpec((tm, d), lambda i: (i, 0))],
        out_shape=[jax.ShapeDtypeStruct((m, d), F32), jax.ShapeDtypeStruct((m, d), BF16)],
        compiler_params=_cparams(1),
        name="outproj_ln",
    )(y, w, x, ln_g, ln_b)


def _dsa_latent_kernel(x_ref, w_ref, qg_ref, kvg_ref, cqt_ref, ckv_ref, ckvt_ref, slab_ref, slabt_ref):
    p = jnp.dot(x_ref[...], w_ref[...], preferred_element_type=F32)
    q_rank = cqt_ref.shape[1]
    kv_rank = ckv_ref.shape[1]
    cq = p[:, :q_rank]
    ckv = p[:, q_rank:q_rank + kv_rank]
    slab = p[:, q_rank + kv_rank:]
    cq = cq * lax.rsqrt(jnp.mean(cq * cq, axis=-1, keepdims=True) + RMS_EPS) * qg_ref[...]
    ckv = ckv * lax.rsqrt(jnp.mean(ckv * ckv, axis=-1, keepdims=True) + RMS_EPS) * kvg_ref[...]
    cqt_ref[0] = cq.T.astype(BF16)
    ckv_ref[...] = ckv.astype(BF16)
    ckvt_ref[0] = ckv.T.astype(BF16)
    slab_ref[...] = slab.astype(BF16)
    slabt_ref[0] = slab.T


def _dsa_latent(x, w, q_gain, kv_gain, *, batch, seq):
    tm = IDX_TQ
    n_t = seq // tm
    k_dim, n_dim = w.shape
    return pl.pallas_call(
        _dsa_latent_kernel,
        grid=(batch, n_t),
        in_specs=[pl.BlockSpec((tm, k_dim), lambda b, t: (b * n_t + t, 0)),
                  pl.BlockSpec((k_dim, n_dim), lambda b, t: (0, 0)),
                  pl.BlockSpec((1, DSA_Q_RANK), lambda b, t: (0, 0)),
                  pl.BlockSpec((1, DSA_KV_RANK), lambda b, t: (0, 0))],
        out_specs=[pl.BlockSpec((1, DSA_Q_RANK, tm), lambda b, t: (b, 0, t)),
                   pl.BlockSpec((tm, DSA_KV_RANK), lambda b, t: (b * n_t + t, 0)),
                   pl.BlockSpec((1, DSA_KV_RANK, tm), lambda b, t: (b, 0, t)),
                   pl.BlockSpec((tm, LANES), lambda b, t: (b * n_t + t, 0)),
                   pl.BlockSpec((1, LANES, tm), lambda b, t: (b, 0, t))],
        out_shape=[jax.ShapeDtypeStruct((batch, DSA_Q_RANK, seq), BF16),
                   jax.ShapeDtypeStruct((batch * seq, DSA_KV_RANK), BF16),
                   jax.ShapeDtypeStruct((batch, DSA_KV_RANK, seq), BF16),
                   jax.ShapeDtypeStruct((batch * seq, LANES), BF16),
                   jax.ShapeDtypeStruct((batch, LANES, seq), F32)],
        compiler_params=_cparams(2),
        name="dsa_latent",
    )(x, w, q_gain, kv_gain)


def _dsa_query_kernel(cqt_ref, wuqt_ref, wuk_ref, ql_ref, *, scale):
    hb, _, hd = wuk_ref.shape
    tq = ql_ref.shape[3] // hb
    q_t = jnp.dot(wuqt_ref[...], cqt_ref[0], preferred_element_type=F32).astype(BF16)
    for j in range(hb):
        ql_t = jnp.dot(wuk_ref[j], q_t[j * hd:(j + 1) * hd, :], preferred_element_type=F32)
        ql_t = (ql_t * scale).astype(BF16)
        for blk in range(ql_ref.shape[1]):
            ql_ref[0, blk, :, j * tq:(j + 1) * tq] = ql_t[:, blk * tq:(blk + 1) * tq]


def _dsa_query(cq_t, wuq_t, wuk, *, batch, seq, scale):
    tm = min(seq, 1024)
    hb = 8
    tq = ATT_TQ
    n_t = seq // tm
    n_heads, kv_rank, hd = wuk.shape
    return pl.pallas_call(
        functools.partial(_dsa_query_kernel, scale=scale),
        grid=(batch, n_t, n_heads // hb),
        in_specs=[pl.BlockSpec((1, DSA_Q_RANK, tm), lambda b, t, h: (b, 0, t)),
                  pl.BlockSpec((hb * hd, DSA_Q_RANK), lambda b, t, h: (h, 0)),
                  pl.BlockSpec((hb, kv_rank, hd), lambda b, t, h: (h, 0, 0))],
        out_specs=pl.BlockSpec((1, tm // tq, kv_rank, hb * tq), lambda b, t, h: (b, t, 0, h)),
        out_shape=jax.ShapeDtypeStruct((batch, seq // tq, kv_rank, n_heads * tq), BF16),
        compiler_params=_cparams(3),
        name="dsa_query",
    )(cq_t, wuq_t, wuk)


def _indexer_kernel(keys_ref, slabt_ref, cqt_ref, widx_ref, bias_ref, qt_scr, sc_scr, *, k_top, w_scale, n_iter):
    i = pl.program_id(1)
    n_t, kt_w, tq = sc_scr.shape
    cq_t = cqt_ref[0]
    for h in range(IDX_N_HEADS):
        qt_scr[h] = jnp.dot(widx_ref[h], cq_t, preferred_element_type=F32).astype(BF16)
    w_t = slabt_ref[0][IDX_HEAD_DIM:IDX_HEAD_DIM + IDX_N_HEADS, :] * w_scale
    q_pos = i * tq + lax.broadcasted_iota(jnp.int32, (kt_w, tq), 1)

    for kt in range(n_t):
        @pl.when(kt <= i)
        def _(kt=kt):
            keys = keys_ref[kt * kt_w:(kt + 1) * kt_w, :]
            acc = jnp.zeros((kt_w, tq), F32)
            for h in range(IDX_N_HEADS):
                s = jnp.dot(keys, qt_scr[h], preferred_element_type=F32)
                acc = acc + jnp.maximum(s, 0.0) * w_t[h:h + 1, :]
            k_pos = kt * kt_w + lax.broadcasted_iota(jnp.int32, (kt_w, tq), 0)
            sc_scr[kt] = jnp.where(k_pos <= q_pos, acc, -jnp.inf)

    def bounds(kt, carry):
        lo, hi = carry
        s = sc_scr[kt]
        hi = jnp.maximum(hi, jnp.max(s, axis=0, keepdims=True))
        lo = jnp.minimum(lo, jnp.min(jnp.where(s == -jnp.inf, jnp.inf, s), axis=0, keepdims=True))
        return lo, hi

    lo, hi = lax.fori_loop(0, i + 1, bounds,
                           (jnp.full((1, tq), jnp.inf, F32), jnp.full((1, tq), -jnp.inf, F32)))

    def halve(_, carry):
        lo, hi = carry
        mid = 0.5 * lo + 0.5 * hi

        def count(kt, cnt):
            return cnt + jnp.sum(jnp.where(sc_scr[kt] >= mid, 1.0, 0.0), axis=0, keepdims=True)

        cnt = lax.fori_loop(0, i + 1, count, jnp.zeros((1, tq), F32))
        enough = cnt >= k_top
        return jnp.where(enough, mid, lo), jnp.where(enough, hi, mid)

    lo, hi = lax.fori_loop(0, n_iter, halve, (lo, hi))

    for kt in range(n_t):
        @pl.when(kt <= i)
        def _(kt=kt):
            bias_ref[0, kt * kt_w:(kt + 1) * kt_w, :] = jnp.where(sc_scr[kt] >= lo, 0.0, -jnp.inf)

        @pl.when(kt > i)
        def _(kt=kt):
            bias_ref[0, kt * kt_w:(kt + 1) * kt_w, :] = jnp.full((kt_w, tq), -jnp.inf, F32)


def _dsa_indexer(keys, slab_t, cq_t, widx_t, *, batch, seq, k_top, w_scale):
    tq = IDX_TQ
    n_t = seq // tq
    return pl.pallas_call(
        functools.partial(_indexer_kernel, k_top=k_top, w_scale=w_scale, n_iter=BISECT_ITERS),
        grid=(batch, n_t),
        in_specs=[pl.BlockSpec((seq, LANES), lambda b, i: (b, 0)),
                  pl.BlockSpec((1, LANES, tq), lambda b, i: (b, 0, i)),
                  pl.BlockSpec((1, DSA_Q_RANK, tq), lambda b, i: (b, 0, i)),
                  pl.BlockSpec(widx_t.shape, lambda b, i: (0, 0, 0))],
        out_specs=pl.BlockSpec((1, seq, tq), lambda b, i: (b, 0, i)),
        out_shape=jax.ShapeDtypeStruct((batch, seq, seq), F32),
        scratch_shapes=[pltpu.VMEM((IDX_N_HEADS, LANES, tq), BF16),
                        pltpu.VMEM((n_t, tq, tq), F32)],
        compiler_params=_cparams(2),
        name="dsa_indexer",
    )(keys, slab_t, cq_t, widx_t)


def _attn_kernel(ql_ref, bias_ref, k_ref, vt_ref, wuvt_ref, gate_ref, *rest, n_keys, hc):
    o_ref = rest[-1]
    tq = bias_ref.shape[2]
    hb, hd, kv_rank = wuvt_ref.shape
    cols = hc * tq
    n_chunks = hb // hc
    keys = k_ref[:n_keys, :]
    bias_t = bias_ref[0]
    bias_rep = jnp.concatenate([bias_t] * hc, axis=1) if hc > 1 else bias_t

    def scores(c):
        q_t = ql_ref[0, 0, :, c * cols:(c + 1) * cols]
        return jnp.dot(keys, q_t, preferred_element_type=F32) + bias_rep

    def finish(c, s):
        m = jnp.max(s, axis=0, keepdims=True)
        p = jnp.exp2(s - m)
        denom = jnp.sum(p, axis=0, keepdims=True)
        o_t = jnp.dot(vt_ref[0], p.astype(BF16), preferred_element_type=F32)
        o_t = (o_t / denom).astype(BF16)
        for j in range(hc):
            h = c * hc + j
            oh_t = jnp.dot(wuvt_ref[h], o_t[:, j * tq:(j + 1) * tq], preferred_element_type=F32)
            gate = gate_ref[:, h * hd:(h + 1) * hd].astype(F32)
            o_ref[:, h * hd:(h + 1) * hd] = (oh_t.T * gate).astype(o_ref.dtype)

    s_next = scores(0)
    for c in range(n_chunks):
        s_cur = s_next
        if c + 1 < n_chunks:
            s_next = scores(c + 1)
        finish(c, s_cur)


def _dsa_attention(ql, bias, ckv_t, ckv, wuv_t, gate, *, batch, seq):
    tq = ATT_TQ
    n_q = seq // tq
    n_heads, hd, kv_rank = wuv_t.shape
    width = n_heads * hd
    og = None
    for i in range(n_q):
        n_keys = (i + 1) * tq
        hb = n_heads
        while hb > 1 and hb * tq * n_keys > ATT_STEP_ELEMS:
            hb //= 2
        hc = hb
        while hc > 1 and hc * tq * n_keys > ATT_CHUNK_ELEMS:
            hc //= 2
        in_specs = [pl.BlockSpec((1, 1, kv_rank, hb * tq), lambda b, g, i=i: (b, i, 0, g)),
                    pl.BlockSpec((1, n_keys, tq), lambda b, g, i=i: (b, 0, i)),
                    pl.BlockSpec((seq, kv_rank), lambda b, g: (b, 0)),
                    pl.BlockSpec((1, kv_rank, n_keys), lambda b, g: (b, 0, 0)),
                    pl.BlockSpec((hb, hd, kv_rank), lambda b, g: (g, 0, 0)),
                    pl.BlockSpec((tq, hb * hd), lambda b, g, i=i: (b * n_q + i, g))]
        args = [ql, bias, ckv, ckv_t, wuv_t, gate]
        aliases = {}
        if og is not None:
            in_specs.append(pl.BlockSpec(memory_space=pl.ANY))
            args.append(og)
            aliases = {len(args) - 1: 0}
        og = pl.pallas_call(
            functools.partial(_attn_kernel, n_keys=n_keys, hc=hc),
            grid=(batch, n_heads // hb),
            in_specs=in_specs,
            out_specs=pl.BlockSpec((tq, hb * hd), lambda b, g, i=i: (b * n_q + i, g)),
            out_shape=jax.ShapeDtypeStruct((batch * seq, width), BF16),
            input_output_aliases=aliases,
            compiler_params=_cparams(2),
            name="dsa_attention_q%d" % i,
        )(*args)
    return og


def _ssd_layer(x, x_bf, w_in, conv_w, conv_b, dt_bias, a_log, d_skip, norm_g, w_out, ln_g, ln_b,
               *, batch, seq, alpha):
    d_inner = w_out.shape[0]
    n_heads = a_log.shape[0]
    n_groups = SSD_N_GROUPS
    heads_per_group = n_heads // n_groups
    conv_dim = conv_w.shape[1]
    gw = d_inner // n_groups
    assert gw % LANES == 0 and heads_per_group <= LANES and SSD_D_STATE == LANES
    assert LANES % SSD_HEAD_DIM == 0 and heads_per_group % (LANES // SSD_HEAD_DIM) == 0

    def per_group_lanes(v):
        lead = v.shape[:-1]
        v = v.reshape(lead + (n_groups, heads_per_group))
        v = jnp.pad(v, [(0, 0)] * len(lead) + [(0, 0), (0, LANES - heads_per_group)])
        return v.reshape(lead + (n_groups * LANES,))

    w_z = w_in[:, :d_inner].astype(BF16)
    w_xbc = w_in[:, d_inner:d_inner + conv_dim].astype(BF16)
    w_dt = per_group_lanes(w_in[:, d_inner + conv_dim:]).astype(BF16)
    dt_b = per_group_lanes(dt_bias)[None, :]
    alog = per_group_lanes(a_log).reshape(n_groups, LANES)
    dskip = jnp.repeat(d_skip, SSD_HEAD_DIM).reshape(n_groups, gw)
    lane_of = jnp.arange(gw) // SSD_HEAD_DIM
    expand_mat = (jnp.arange(LANES)[:, None] == lane_of[None, :]).astype(BF16)

    z = _proj(x_bf, w_z, batch=batch, seq=seq, tn=512, epilogue="none")
    xbc = _proj(x_bf, w_xbc, batch=batch, seq=seq, tn=512, epilogue="conv_silu",
                extra=(conv_w, conv_b[None, :]))
    dt = _proj(x_bf, w_dt, batch=batch, seq=seq, tn=512, epilogue="softplus", extra=(dt_b,), out_dtype=F32)
    yn = _ssd_scan(xbc, z, dt, alog, dskip, norm_g[None, :], expand_mat, batch=batch, seq=seq, d_inner=d_inner)
    return _outproj_ln(yn, w_out.astype(BF16), x, ln_g[None, :], ln_b[None, :], alpha=alpha, tm=512)


def _dsa_layer(x, x_bf, w_in, q_norm_g, kv_norm_g, w_uq, w_uk, w_uv, w_idx_q, w_out, ln_g, ln_b,
               *, batch, seq, alpha):
    assert seq % IDX_TQ == 0 and IDX_TQ % ATT_TQ == 0 and IDX_HEAD_DIM + IDX_N_HEADS <= LANES
    small = DSA_Q_RANK + DSA_KV_RANK + IDX_HEAD_DIM + IDX_N_HEADS
    w_small = jnp.pad(w_in[:, :small], ((0, 0), (0, DSA_Q_RANK + DSA_KV_RANK + LANES - small))).astype(BF16)
    w_gate = w_in[:, small:].astype(BF16)
    wuq_t = w_uq.T.astype(BF16)
    wuv_t = w_uv.transpose(0, 2, 1).astype(BF16)
    widx_t = w_idx_q.reshape(DSA_Q_RANK, IDX_N_HEADS, IDX_HEAD_DIM).transpose(1, 2, 0)
    widx_t = jnp.pad(widx_t, ((0, 0), (0, LANES - IDX_HEAD_DIM), (0, 0))).astype(BF16)
    k_top = min(IDX_TOPK, seq // 4)

    cq_t, ckv, ckv_t, slab, slab_t = _dsa_latent(x_bf, w_small, q_norm_g[None, :], kv_norm_g[None, :],
                                                     batch=batch, seq=seq)
    gate = _proj(x_bf, w_gate, batch=batch, seq=seq, tn=512, epilogue="silu")
    ql = _dsa_query(cq_t, wuq_t, w_uk.astype(BF16), batch=batch, seq=seq,
                    scale=DSA_HEAD_DIM ** -0.5 * math.log2(math.e))
    bias = _dsa_indexer(slab, slab_t, cq_t, widx_t, batch=batch, seq=seq, k_top=k_top,
                        w_scale=IDX_N_HEADS ** -0.5 * IDX_HEAD_DIM ** -0.5)
    og = _dsa_attention(ql, bias, ckv_t, ckv, wuv_t, gate, batch=batch, seq=seq)
    return _outproj_ln(og, w_out.astype(BF16), x, ln_g[None, :], ln_b[None, :], alpha=alpha, tm=512)


def kernel(x, ssd_w_in, ssd_conv_w, ssd_conv_b, ssd_dt_bias, ssd_a_log, ssd_d_skip, ssd_norm_g, ssd_w_out,
           dsa_w_in, dsa_q_norm_g, dsa_kv_norm_g, dsa_w_uq, dsa_w_uk, dsa_w_uv, dsa_w_idx_q, dsa_w_out,
           ln_g, ln_b):
    batch, seq, d_model = x.shape
    depth = ln_g.shape[0]
    alpha = (2.0 * depth) ** 0.25
    xf = x.reshape(batch * seq, d_model)
    x_bf = xf.astype(BF16)
    for i in range(depth):
        j = i // 2
        if i % 2 == 0:
            xf, x_bf = _ssd_layer(xf, x_bf, ssd_w_in[j], ssd_conv_w[j], ssd_conv_b[j], ssd_dt_bias[j],
                                  ssd_a_log[j], ssd_d_skip[j], ssd_norm_g[j], ssd_w_out[j], ln_g[i], ln_b[i],
                                  batch=batch, seq=seq, alpha=alpha)
        else:
            xf, x_bf = _dsa_layer(xf, x_bf, dsa_w_in[j], dsa_q_norm_g[j], dsa_kv_norm_g[j], dsa_w_uq[j],
                                  dsa_w_uk[j], dsa_w_uv[j], dsa_w_idx_q[j], dsa_w_out[j], ln_g[i], ln_b[i],
                                  batch=batch, seq=seq, alpha=alpha)
    return xf.reshape(batch, seq, d_model)
```

```python
import functools
import math

import jax
import jax.numpy as jnp
from jax import lax
from jax.experimental import pallas as pl
from jax.experimental.pallas import tpu as pltpu

F32 = jnp.float32
BF16 = jnp.bfloat16

V7X_VMEM_BYTES = 64 * 1024 * 1024
LANES = 128
SUBLANES = 8
VMEM_LIMIT_BYTES = V7X_VMEM_BYTES - 8 * 1024 * 1024

LN_EPS = 1e-5
RMS_EPS = 1e-6

SSD_HEAD_DIM = 64
SSD_N_GROUPS = 8
SSD_D_STATE = 128
SSD_CHUNK = 256
DSA_N_HEADS = 32
DSA_HEAD_DIM = 128
DSA_Q_RANK = 512
DSA_KV_RANK = 256
IDX_N_HEADS = 16
IDX_HEAD_DIM = 64
IDX_TOPK = 256

ATT_TQ = 256
ATT_STEP_ELEMS = 4 * 1024 * 1024
ATT_CHUNK_ELEMS = 1024 * 1024
IDX_TQ = 512
BISECT_ITERS = 32


def _cparams(n_axes, flags=None):
    return pltpu.CompilerParams(dimension_semantics=("arbitrary",) * n_axes,
                                vmem_limit_bytes=VMEM_LIMIT_BYTES, flags=flags)


def _sigmoid(v):
    return 1.0 / (1.0 + jnp.exp(-v))


def _proj_kernel(x_ref, w_ref, *rest, epilogue, n_split):
    o_ref = rest[-1]
    x = x_ref[...]
    cw = w_ref.shape[1] // n_split

    def matmul(c):
        return jnp.dot(x, w_ref[:, c * cw:(c + 1) * cw], preferred_element_type=F32)

    def finish(c, y):
        cols = slice(c * cw, (c + 1) * cw)
        if epilogue == "silu":
            y = y * _sigmoid(y)
        elif epilogue == "conv_silu":
            conv_w_ref, conv_b_ref = rest[:2]
            k_taps = conv_w_ref.shape[0]
            row = lax.broadcasted_iota(jnp.int32, y.shape, 0)
            acc = conv_b_ref[:, cols] + conv_w_ref[k_taps - 1:k_taps, cols] * y
            for back in range(1, k_taps):
                shifted = jnp.where(row >= back, pltpu.roll(y, back, 0), 0.0)
                acc = acc + conv_w_ref[k_taps - 1 - back:k_taps - back, cols] * shifted
            y = acc * _sigmoid(acc)
        else:
            raise ValueError(epilogue)
        o_ref[:, cols] = y.astype(o_ref.dtype)

    y_next = matmul(0)
    for c in range(n_split):
        y = y_next
        if c + 1 < n_split:
            y_next = matmul(c + 1)
        finish(c, y)


def _proj(x, w, *, batch, seq, tn, epilogue, n_split=1, extra=(), out_dtype=BF16):
    k_dim, n_dim = w.shape
    assert n_dim % tn == 0 and tn % (n_split * LANES) == 0
    extra_specs = [pl.BlockSpec((e.shape[0], tn), lambda b, j: (0, j)) for e in extra]
    return pl.pallas_call(
        functools.partial(_proj_kernel, epilogue=epilogue, n_split=n_split),
        grid=(batch, n_dim // tn),
        in_specs=[pl.BlockSpec((seq, k_dim), lambda b, j: (b, 0)),
                  pl.BlockSpec((k_dim, tn), lambda b, j: (0, j))] + extra_specs,
        out_specs=pl.BlockSpec((seq, tn), lambda b, j: (b, j)),
        out_shape=jax.ShapeDtypeStruct((batch * seq, n_dim), out_dtype),
        compiler_params=_cparams(2),
        name="proj_" + epilogue,
    )(x, w, *extra)


def _dt_kernel(x_ref, w_ref, b_ref, spread_ref, o_ref):
    y = jnp.dot(x_ref[...], w_ref[...], preferred_element_type=F32) + b_ref[...]
    y = jnp.maximum(y, 0.0) + jnp.log1p(jnp.exp(-jnp.abs(y)))
    spread = spread_ref[...]
    hi = y.astype(BF16)
    rest = y - hi.astype(F32)
    mid = rest.astype(BF16)
    lo = (rest - mid.astype(F32)).astype(BF16)
    o_ref[...] = (jnp.dot(hi, spread, preferred_element_type=F32)
                  + jnp.dot(mid, spread, preferred_element_type=F32)
                  + jnp.dot(lo, spread, preferred_element_type=F32))


def _dt_proj(x, w, bias, spread, *, batch, seq):
    k_dim = w.shape[0]
    width = spread.shape[1]
    return pl.pallas_call(
        _dt_kernel,
        grid=(batch,),
        in_specs=[pl.BlockSpec((seq, k_dim), lambda b: (b, 0)),
                  pl.BlockSpec((k_dim, LANES), lambda b: (0, 0)),
                  pl.BlockSpec((1, LANES), lambda b: (0, 0)),
                  pl.BlockSpec((LANES, width), lambda b: (0, 0))],
        out_specs=pl.BlockSpec((seq, width), lambda b: (b, 0)),
        out_shape=jax.ShapeDtypeStruct((batch * seq, width), F32),
        compiler_params=_cparams(1),
        name="dt_proj",
    )(x, w, bias, spread)


def _cumsum_rows(v):
    n = v.shape[0]
    row = lax.broadcasted_iota(jnp.int32, v.shape, 0)
    shift = 1
    while shift < n:
        v = v + jnp.where(row >= shift, pltpu.roll(v, shift, 0), 0.0)
        shift *= 2
    return v


def _ssd_kernel(xs_ref, b_ref, c_ref, z_ref, dt_ref, alog_ref, dskip_ref, ng_ref, e_ref, o_ref,
                state_ref, *, heads_per_group, head_dim):
    g = pl.program_id(1)
    chunk = pl.program_id(2)
    q = xs_ref.shape[0]

    @pl.when(chunk == 0)
    def _():
        state_ref[...] = jnp.zeros_like(state_ref)

    xs = xs_ref[...].astype(F32)
    bm = b_ref[...]
    cm = c_ref[...]
    dt = dt_ref[...]
    a = -jnp.exp(alog_ref[pl.ds(g, 1), :]) * math.log2(math.e)
    acs = _cumsum_rows(dt * a)
    acs_t = acs.T
    last = acs[q - 1:q, :]
    expand_mat = e_ref[...]

    def expand(v):
        return jnp.dot(v.astype(BF16), expand_mat, preferred_element_type=F32)

    def expand_hi_lo(v):
        hi = v.astype(BF16)
        lo = (v - hi.astype(F32)).astype(BF16)
        return (jnp.dot(hi, expand_mat, preferred_element_type=F32)
                + jnp.dot(lo, expand_mat, preferred_element_type=F32))

    dt_x = expand(dt)
    decay_in_x = expand_hi_lo(jnp.exp2(acs))
    decay_out_dt_x = expand(jnp.exp2(last - acs) * dt)
    xdt = xs * dt_x

    cb = lax.dot_general(cm, bm, (((1,), (1,)), ((), ())), preferred_element_type=F32)
    blk = LANES
    tri = (lax.broadcasted_iota(jnp.int32, (blk, blk), 0) >= lax.broadcasted_iota(jnp.int32, (blk, blk), 1))

    def masked_cb(r):
        cols = []
        for s0 in range(0, q, blk):
            parts = []
            if s0 > 0:
                parts.append(jnp.zeros((s0, blk), BF16))
            diff = acs[s0:s0 + blk, r:r + 1] - acs_t[r:r + 1, s0:s0 + blk]
            decay = jnp.exp2(jnp.where(tri, diff, -jnp.inf))
            parts.append((cb[s0:s0 + blk, s0:s0 + blk] * decay).astype(BF16))
            if s0 + blk < q:
                diff = acs[s0 + blk:, r:r + 1] - acs_t[r:r + 1, s0:s0 + blk]
                parts.append((cb[s0 + blk:, s0:s0 + blk] * jnp.exp2(diff)).astype(BF16))
            cols.append(jnp.concatenate(parts, axis=0) if len(parts) > 1 else parts[0])
        return jnp.concatenate(cols, axis=1) if len(cols) > 1 else cols[0]

    heads_per_slab = LANES // head_dim
    lane_head = lax.broadcasted_iota(jnp.int32, (q, LANES), 1) // head_dim
    slabs = []
    for j in range(heads_per_group // heads_per_slab):
        x_slab = xdt[:, j * LANES:(j + 1) * LANES]
        acc = None
        for sub in range(heads_per_slab):
            r = j * heads_per_slab + sub
            m = masked_cb(r)
            x_head = jnp.where(lane_head == sub, x_slab, 0.0).astype(BF16)
            part = jnp.dot(m, x_head, preferred_element_type=F32)
            acc = part if acc is None else acc + part
        slabs.append(acc)
    y_diag = jnp.concatenate(slabs, axis=1)

    s_prev = state_ref[...]
    y_off = jnp.dot(cm, s_prev.astype(BF16), preferred_element_type=F32) * decay_in_x
    y = y_diag + y_off + dskip_ref[pl.ds(g, 1), :] * xs
    b_t = bm.astype(F32).T.astype(BF16)
    s_new = jnp.dot(b_t, (xs * decay_out_dt_x).astype(BF16), preferred_element_type=F32)
    state_ref[...] = s_prev * decay_in_x[q - 1:q, :] + s_new

    yg = y * z_ref[...].astype(F32)
    ms = jnp.mean(yg * yg, axis=-1, keepdims=True)
    o_ref[...] = (yg * lax.rsqrt(ms + RMS_EPS) * ng_ref[...]).astype(o_ref.dtype)


def _ssd_scan(xbc, z, dt, alog, dskip, norm_g, expand_mat, *, batch, seq, d_inner):
    n_groups, d_state, chunk = SSD_N_GROUPS, SSD_D_STATE, math.gcd(SSD_CHUNK, seq)
    gw = d_inner // n_groups
    heads_per_group = gw // SSD_HEAD_DIM
    n_chunks = seq // chunk
    b_col0 = d_inner // d_state
    c_col0 = b_col0 + n_groups
    row = lambda b, g, c: b * n_chunks + c
    return pl.pallas_call(
        functools.partial(_ssd_kernel, heads_per_group=heads_per_group, head_dim=SSD_HEAD_DIM),
        grid=(batch, n_groups, n_chunks),
        in_specs=[
            pl.BlockSpec((chunk, gw), lambda b, g, c: (row(b, g, c), g)),
            pl.BlockSpec((chunk, d_state), lambda b, g, c: (row(b, g, c), b_col0 + g)),
            pl.BlockSpec((chunk, d_state), lambda b, g, c: (row(b, g, c), c_col0 + g)),
            pl.BlockSpec((chunk, gw), lambda b, g, c: (row(b, g, c), g)),
            pl.BlockSpec((chunk, LANES), lambda b, g, c: (row(b, g, c), g)),
            pl.BlockSpec((n_groups, LANES), lambda b, g, c: (0, 0)),
            pl.BlockSpec((n_groups, gw), lambda b, g, c: (0, 0)),
            pl.BlockSpec((1, gw), lambda b, g, c: (0, g)),
            pl.BlockSpec((LANES, gw), lambda b, g, c: (0, 0)),
        ],
        out_specs=pl.BlockSpec((chunk, gw), lambda b, g, c: (row(b, g, c), g)),
        out_shape=jax.ShapeDtypeStruct((batch * seq, d_inner), BF16),
        scratch_shapes=[pltpu.VMEM((d_state, gw), F32)],
        compiler_params=_cparams(3),
        name="ssd_scan",
    )(xbc, xbc, xbc, z, dt, alog, dskip, norm_g, expand_mat)


def _outproj_ln_kernel(y_ref, w_ref, x_ref, g_ref, b_ref, o_ref, obf_ref, *, alpha):
    h = jnp.dot(y_ref[...], w_ref[...], preferred_element_type=F32)
    v = alpha * x_ref[...] + h
    mu = jnp.mean(v, axis=-1, keepdims=True)
    d = v - mu
    var = jnp.mean(d * d, axis=-1, keepdims=True)
    out = d * lax.rsqrt(var + LN_EPS) * g_ref[...] + b_ref[...]
    o_ref[...] = out
    obf_ref[...] = out.astype(BF16)


def _outproj_ln(y, w, x, ln_g, ln_b, *, alpha, tm):
    m, k_dim = y.shape
    d = w.shape[1]
    return pl.pallas_call(
        functools.partial(_outproj_ln_kernel, alpha=alpha),
        grid=(m // tm,),
        in_specs=[pl.BlockSpec((tm, k_dim), lambda i: (i, 0)),
                  pl.BlockSpec((k_dim, d), lambda i: (0, 0), pipeline_mode=pl.Buffered(1)),
                  pl.BlockSpec((tm, d), lambda i: (i, 0)),
                  pl.BlockSpec((1, d), lambda i: (0, 0)),
                  pl.BlockSpec((1, d), lambda i: (0, 0))],
        out_specs=[pl.BlockSpec((tm, d), lambda i: (i, 0)),
                   pl.BlockSpec((tm, d), lambda i: (i, 0))],
        out_shape=[jax.ShapeDtypeStruct((m, d), F32), jax.ShapeDtypeStruct((m, d), BF16)],
        compiler_params=_cparams(1),
        name="outproj_ln",
    )(y, w, x, ln_g, ln_b)


def _dsa_latent_kernel(x_ref, w_ref, qg_ref, kvg_ref, cqt_ref, ckv_ref, ckvt_ref, slab_ref, slabt_ref):
    p = jnp.dot(x_ref[...], w_ref[...], preferred_element_type=F32)
    q_rank = cqt_ref.shape[1]
    kv_rank = ckv_ref.shape[1]
    cq = p[:, :q_rank]
    ckv = p[:, q_rank:q_rank + kv_rank]
    slab = p[:, q_rank + kv_rank:]
    cq = cq * lax.rsqrt(jnp.mean(cq * cq, axis=-1, keepdims=True) + RMS_EPS) * qg_ref[...]
    ckv = ckv * lax.rsqrt(jnp.mean(ckv * ckv, axis=-1, keepdims=True) + RMS_EPS) * kvg_ref[...]
    cqt_ref[0] = cq.T.astype(BF16)
    ckv_ref[...] = ckv.astype(BF16)
    ckvt_ref[0] = ckv.T.astype(BF16)
    slab_ref[...] = slab.astype(BF16)
    slabt_ref[0] = slab.T


def _dsa_latent(x, w, q_gain, kv_gain, *, batch, seq):
    tm = IDX_TQ
    n_t = seq // tm
    k_dim, n_dim = w.shape
    return pl.pallas_call(
        _dsa_latent_kernel,
        grid=(batch, n_t),
        in_specs=[pl.BlockSpec((tm, k_dim), lambda b, t: (b * n_t + t, 0)),
                  pl.BlockSpec((k_dim, n_dim), lambda b, t: (0, 0)),
                  pl.BlockSpec((1, DSA_Q_RANK), lambda b, t: (0, 0)),
                  pl.BlockSpec((1, DSA_KV_RANK), lambda b, t: (0, 0))],
        out_specs=[pl.BlockSpec((1, DSA_Q_RANK, tm), lambda b, t: (b, 0, t)),
                   pl.BlockSpec((tm, DSA_KV_RANK), lambda b, t: (b * n_t + t, 0)),
                   pl.BlockSpec((1, DSA_KV_RANK, tm), lambda b, t: (b, 0, t)),
                   pl.BlockSpec((tm, LANES), lambda b, t: (b * n_t + t, 0)),
                   pl.BlockSpec((1, LANES, tm), lambda b, t: (b, 0, t))],
        out_shape=[jax.ShapeDtypeStruct((batch, DSA_Q_RANK, seq), BF16),
                   jax.ShapeDtypeStruct((batch * seq, DSA_KV_RANK), BF16),
                   jax.ShapeDtypeStruct((batch, DSA_KV_RANK, seq), BF16),
                   jax.ShapeDtypeStruct((batch * seq, LANES), BF16),
                   jax.ShapeDtypeStruct((batch, LANES, seq), F32)],
        compiler_params=_cparams(2),
        name="dsa_latent",
    )(x, w, q_gain, kv_gain)


def _dsa_query_kernel(cqt_ref, wuqt_ref, wuk_ref, ql_ref, *, scale):
    hb, _, hd = wuk_ref.shape
    tq = ql_ref.shape[3] // hb
    q_t = jnp.dot(wuqt_ref[...], cqt_ref[0], preferred_element_type=F32).astype(BF16)
    for j in range(hb):
        ql_t = jnp.dot(wuk_ref[j], q_t[j * hd:(j + 1) * hd, :], preferred_element_type=F32)
        ql_t = (ql_t * scale).astype(BF16)
        for blk in range(ql_ref.shape[1]):
            ql_ref[0, blk, :, j * tq:(j + 1) * tq] = ql_t[:, blk * tq:(blk + 1) * tq]


def _dsa_query(cq_t, wuq_t, wuk, *, batch, seq, scale):
    tm = min(seq, 1024)
    hb = 8
    tq = ATT_TQ
    n_t = seq // tm
    n_heads, kv_rank, hd = wuk.shape
    return pl.pallas_call(
        functools.partial(_dsa_query_kernel, scale=scale),
        grid=(batch, n_t, n_heads // hb),
        in_specs=[pl.BlockSpec((1, DSA_Q_RANK, tm), lambda b, t, h: (b, 0, t)),
                  pl.BlockSpec((hb * hd, DSA_Q_RANK), lambda b, t, h: (h, 0)),
                  pl.BlockSpec((hb, kv_rank, hd), lambda b, t, h: (h, 0, 0))],
        out_specs=pl.BlockSpec((1, tm // tq, kv_rank, hb * tq), lambda b, t, h: (b, t, 0, h)),
        out_shape=jax.ShapeDtypeStruct((batch, seq // tq, kv_rank, n_heads * tq), BF16),
        compiler_params=_cparams(3),
        name="dsa_query",
    )(cq_t, wuq_t, wuk)


def _indexer_kernel(keys_ref, slabt_ref, cqt_ref, widx_ref, bias_ref, qt_scr, sc_scr, *, k_top, w_scale, n_iter):
    i = pl.program_id(1)
    n_t, kt_w, tq = sc_scr.shape
    cq_t = cqt_ref[0]
    for h in range(IDX_N_HEADS):
        qt_scr[h] = jnp.dot(widx_ref[h], cq_t, preferred_element_type=F32).astype(BF16)
    w_t = slabt_ref[0][IDX_HEAD_DIM:IDX_HEAD_DIM + IDX_N_HEADS, :] * w_scale
    q_pos = i * tq + lax.broadcasted_iota(jnp.int32, (kt_w, tq), 1)

    for kt in range(n_t):
        @pl.when(kt <= i)
        def _(kt=kt):
            keys = keys_ref[kt * kt_w:(kt + 1) * kt_w, :]
            acc = jnp.zeros((kt_w, tq), F32)
            for h in range(IDX_N_HEADS):
                s = jnp.dot(keys, qt_scr[h], preferred_element_type=F32)
                acc = acc + jnp.maximum(s, 0.0) * w_t[h:h + 1, :]
            k_pos = kt * kt_w + lax.broadcasted_iota(jnp.int32, (kt_w, tq), 0)
            sc_scr[kt] = jnp.where(k_pos <= q_pos, acc, -jnp.inf)

    def bounds(kt, carry):
        lo, hi = carry
        s = sc_scr[kt]
        hi = jnp.maximum(hi, jnp.max(s, axis=0, keepdims=True))
        lo = jnp.minimum(lo, jnp.min(jnp.where(s == -jnp.inf, jnp.inf, s), axis=0, keepdims=True))
        return lo, hi

    lo, hi = lax.fori_loop(0, i + 1, bounds,
                           (jnp.full((1, tq), jnp.inf, F32), jnp.full((1, tq), -jnp.inf, F32)))

    def halve(_, carry):
        lo, hi = carry
        mid = 0.5 * lo + 0.5 * hi

        def count(kt, cnt):
            return cnt + jnp.sum(jnp.where(sc_scr[kt] >= mid, 1.0, 0.0), axis=0, keepdims=True)

        cnt = lax.fori_loop(0, i + 1, count, jnp.zeros((1, tq), F32))
        enough = cnt >= k_top
        return jnp.where(enough, mid, lo), jnp.where(enough, hi, mid)

    lo, hi = lax.fori_loop(0, n_iter, halve, (lo, hi))

    for kt in range(n_t):
        @pl.when(kt <= i)
        def _(kt=kt):
            bias_ref[0, kt * kt_w:(kt + 1) * kt_w, :] = jnp.where(sc_scr[kt] >= lo, 0.0, -jnp.inf)

        @pl.when(kt > i)
        def _(kt=kt):
            bias_ref[0, kt * kt_w:(kt + 1) * kt_w, :] = jnp.full((kt_w, tq), -jnp.inf, F32)


def _dsa_indexer(keys, slab_t, cq_t, widx_t, *, batch, seq, k_top, w_scale):
    tq = IDX_TQ
    n_t = seq // tq
    return pl.pallas_call(
        functools.partial(_indexer_kernel, k_top=k_top, w_scale=w_scale, n_iter=BISECT_ITERS),
        grid=(batch, n_t),
        in_specs=[pl.BlockSpec((seq, LANES), lambda b, i: (b, 0)),
                  pl.BlockSpec((1, LANES, tq), lambda b, i: (b, 0, i)),
                  pl.BlockSpec((1, DSA_Q_RANK, tq), lambda b, i: (b, 0, i)),
                  pl.BlockSpec(widx_t.shape, lambda b, i: (0, 0, 0))],
        out_specs=pl.BlockSpec((1, seq, tq), lambda b, i: (b, 0, i)),
        out_shape=jax.ShapeDtypeStruct((batch, seq, seq), F32),
        scratch_shapes=[pltpu.VMEM((IDX_N_HEADS, LANES, tq), BF16),
                        pltpu.VMEM((n_t, tq, tq), F32)],
        compiler_params=_cparams(2),
        name="dsa_indexer",
    )(keys, slab_t, cq_t, widx_t)


def _attn_kernel(ql_ref, bias_ref, k_ref, vt_ref, wuvt_ref, gate_ref, *rest, n_keys, hc):
    o_ref = rest[-1]
    tq = bias_ref.shape[2]
    hb, hd, kv_rank = wuvt_ref.shape
    cols = hc * tq
    n_chunks = hb // hc
    keys = k_ref[:n_keys, :]
    bias_t = bias_ref[0]
    bias_rep = jnp.concatenate([bias_t] * hc, axis=1) if hc > 1 else bias_t

    def scores(c):
        q_t = ql_ref[0, 0, :, c * cols:(c + 1) * cols]
        return jnp.dot(keys, q_t, preferred_element_type=F32) + bias_rep

    def finish(c, s):
        m = jnp.max(s, axis=0, keepdims=True)
        p = jnp.exp2(s - m)
        denom = jnp.sum(p, axis=0, keepdims=True)
        o_t = jnp.dot(vt_ref[0], p.astype(BF16), preferred_element_type=F32)
        o_t = (o_t / denom).astype(BF16)
        for j in range(hc):
            h = c * hc + j
            oh_t = jnp.dot(wuvt_ref[h], o_t[:, j * tq:(j + 1) * tq], preferred_element_type=F32)
            gate = gate_ref[:, h * hd:(h + 1) * hd].astype(F32)
            o_ref[:, h * hd:(h + 1) * hd] = (oh_t.T * gate).astype(o_ref.dtype)

    s_next = scores(0)
    for c in range(n_chunks):
        s_cur = s_next
        if c + 1 < n_chunks:
            s_next = scores(c + 1)
        finish(c, s_cur)


def _dsa_attention(ql, bias, ckv_t, ckv, wuv_t, gate, *, batch, seq):
    tq = ATT_TQ
    n_q = seq // tq
    n_heads, hd, kv_rank = wuv_t.shape
    width = n_heads * hd
    og = None
    for i in range(n_q):
        n_keys = (i + 1) * tq
        hb = n_heads
        while hb > 1 and hb * tq * n_keys > ATT_STEP_ELEMS:
            hb //= 2
        hc = hb
        while hc > 1 and hc * tq * n_keys > ATT_CHUNK_ELEMS:
            hc //= 2
        in_specs = [pl.BlockSpec((1, 1, kv_rank, hb * tq), lambda b, g, i=i: (b, i, 0, g)),
                    pl.BlockSpec((1, n_keys, tq), lambda b, g, i=i: (b, 0, i)),
                    pl.BlockSpec((seq, kv_rank), lambda b, g: (b, 0)),
                    pl.BlockSpec((1, kv_rank, n_keys), lambda b, g: (b, 0, 0)),
                    pl.BlockSpec((hb, hd, kv_rank), lambda b, g: (g, 0, 0)),
                    pl.BlockSpec((tq, hb * hd), lambda b, g, i=i: (b * n_q + i, g))]
        args = [ql, bias, ckv, ckv_t, wuv_t, gate]
        aliases = {}
        if og is not None:
            in_specs.append(pl.BlockSpec(memory_space=pl.ANY))
            args.append(og)
            aliases = {len(args) - 1: 0}
        og = pl.pallas_call(
            functools.partial(_attn_kernel, n_keys=n_keys, hc=hc),
            grid=(batch, n_heads // hb),
            in_specs=in_specs,
            out_specs=pl.BlockSpec((tq, hb * hd), lambda b, g, i=i: (b * n_q + i, g)),
            out_shape=jax.ShapeDtypeStruct((batch * seq, width), BF16),
            input_output_aliases=aliases,
            compiler_params=_cparams(2),
            name="dsa_attention_q%d" % i,
        )(*args)
    return og


def _ssd_layer(x, x_bf, w_in, conv_w, conv_b, dt_bias, a_log, d_skip, norm_g, w_out, ln_g, ln_b,
               *, batch, seq, alpha):
    d_inner = w_out.shape[0]
    n_heads = a_log.shape[0]
    n_groups = SSD_N_GROUPS
    heads_per_group = n_heads // n_groups
    conv_dim = conv_w.shape[1]
    gw = d_inner // n_groups
    assert gw % LANES == 0 and heads_per_group <= LANES and SSD_D_STATE == LANES
    assert LANES % SSD_HEAD_DIM == 0 and heads_per_group % (LANES // SSD_HEAD_DIM) == 0

    def per_group_lanes(v):
        lead = v.shape[:-1]
        v = v.reshape(lead + (n_groups, heads_per_group))
        v = jnp.pad(v, [(0, 0)] * len(lead) + [(0, 0), (0, LANES - heads_per_group)])
        return v.reshape(lead + (n_groups * LANES,))

    w_z = w_in[:, :d_inner].astype(BF16)
    w_xbc = w_in[:, d_inner:d_inner + conv_dim].astype(BF16)
    w_dt = jnp.pad(w_in[:, d_inner + conv_dim:], ((0, 0), (0, LANES - n_heads))).astype(BF16)
    dt_b = jnp.pad(dt_bias, (0, LANES - n_heads))[None, :]
    head_of_lane = jnp.arange(n_groups * LANES)
    head_of_lane = jnp.where(head_of_lane % LANES < heads_per_group,
                             (head_of_lane // LANES) * heads_per_group + head_of_lane % LANES, -1)
    spread = (jnp.arange(LANES)[:, None] == head_of_lane[None, :]).astype(BF16)
    alog = per_group_lanes(a_log).reshape(n_groups, LANES)
    dskip = jnp.repeat(d_skip, SSD_HEAD_DIM).reshape(n_groups, gw)
    lane_of = jnp.arange(gw) // SSD_HEAD_DIM
    expand_mat = (jnp.arange(LANES)[:, None] == lane_of[None, :]).astype(BF16)

    z = _proj(x_bf, w_z, batch=batch, seq=seq, tn=512, epilogue="silu")
    xbc = _proj(x_bf, w_xbc, batch=batch, seq=seq, tn=1024, n_split=4, epilogue="conv_silu",
                extra=(conv_w, conv_b[None, :]))
    dt = _dt_proj(x_bf, w_dt, dt_b, spread, batch=batch, seq=seq)
    yn = _ssd_scan(xbc, z, dt, alog, dskip, norm_g[None, :], expand_mat, batch=batch, seq=seq, d_inner=d_inner)
    return _outproj_ln(yn, w_out.astype(BF16), x, ln_g[None, :], ln_b[None, :], alpha=alpha, tm=512)


def _dsa_layer(x, x_bf, w_in, q_norm_g, kv_norm_g, w_uq, w_uk, w_uv, w_idx_q, w_out, ln_g, ln_b,
               *, batch, seq, alpha):
    assert seq % IDX_TQ == 0 and IDX_TQ % ATT_TQ == 0 and IDX_HEAD_DIM + IDX_N_HEADS <= LANES
    small = DSA_Q_RANK + DSA_KV_RANK + IDX_HEAD_DIM + IDX_N_HEADS
    w_small = jnp.pad(w_in[:, :small], ((0, 0), (0, DSA_Q_RANK + DSA_KV_RANK + LANES - small))).astype(BF16)
    w_gate = w_in[:, small:].astype(BF16)
    wuq_t = w_uq.T.astype(BF16)
    wuv_t = w_uv.transpose(0, 2, 1).astype(BF16)
    widx_t = w_idx_q.reshape(DSA_Q_RANK, IDX_N_HEADS, IDX_HEAD_DIM).transpose(1, 2, 0)
    widx_t = jnp.pad(widx_t, ((0, 0), (0, LANES - IDX_HEAD_DIM), (0, 0))).astype(BF16)
    k_top = min(IDX_TOPK, seq // 4)

    cq_t, ckv, ckv_t, slab, slab_t = _dsa_latent(x_bf, w_small, q_norm_g[None, :], kv_norm_g[None, :],
                                                     batch=batch, seq=seq)
    gate = _proj(x_bf, w_gate, batch=batch, seq=seq, tn=512, epilogue="silu")
    ql = _dsa_query(cq_t, wuq_t, w_uk.astype(BF16), batch=batch, seq=seq,
                    scale=DSA_HEAD_DIM ** -0.5 * math.log2(math.e))
    bias = _dsa_indexer(slab, slab_t, cq_t, widx_t, batch=batch, seq=seq, k_top=k_top,
                        w_scale=IDX_N_HEADS ** -0.5 * IDX_HEAD_DIM ** -0.5)
    og = _dsa_attention(ql, bias, ckv_t, ckv, wuv_t, gate, batch=batch, seq=seq)
    return _outproj_ln(og, w_out.astype(BF16), x, ln_g[None, :], ln_b[None, :], alpha=alpha, tm=512)


def kernel(x, ssd_w_in, ssd_conv_w, ssd_conv_b, ssd_dt_bias, ssd_a_log, ssd_d_skip, ssd_norm_g, ssd_w_out,
           dsa_w_in, dsa_q_norm_g, dsa_kv_norm_g, dsa_w_uq, dsa_w_uk, dsa_w_uv, dsa_w_idx_q, dsa_w_out,
           ln_g, ln_b):
    batch, seq, d_model = x.shape
    depth = ln_g.shape[0]
    alpha = (2.0 * depth) ** 0.25
    xf = x.reshape(batch * seq, d_model)
    x_bf = xf.astype(BF16)
    for i in range(depth):
        j = i // 2
        if i % 2 == 0:
            xf, x_bf = _ssd_layer(xf, x_bf, ssd_w_in[j], ssd_conv_w[j], ssd_conv_b[j], ssd_dt_bias[j],
                                  ssd_a_log[j], ssd_d_skip[j], ssd_norm_g[j], ssd_w_out[j], ln_g[i], ln_b[i],
                                  batch=batch, seq=seq, alpha=alpha)
        else:
            xf, x_bf = _dsa_layer(xf, x_bf, dsa_w_in[j], dsa_q_norm_g[j], dsa_kv_norm_g[j], dsa_w_uq[j],
                                  dsa_w_uk[j], dsa_w_uv[j], dsa_w_idx_q[j], dsa_w_out[j], ln_g[i], ln_b[i],
                                  batch=batch, seq=seq, alpha=alpha)
    return xf.reshape(batch, seq, d_model)
```

```python
import functools
import math

import jax
import jax.numpy as jnp
from jax import lax
from jax.experimental import pallas as pl
from jax.experimental.pallas import tpu as pltpu

F32 = jnp.float32
BF16 = jnp.bfloat16

V7X_VMEM_BYTES = 64 * 1024 * 1024
LANES = 128
SUBLANES = 8
VMEM_LIMIT_BYTES = V7X_VMEM_BYTES - 8 * 1024 * 1024

LN_EPS = 1e-5
RMS_EPS = 1e-6

SSD_HEAD_DIM = 64
SSD_N_GROUPS = 8
SSD_D_STATE = 128
SSD_CHUNK = 256
DSA_N_HEADS = 32
DSA_HEAD_DIM = 128
DSA_Q_RANK = 512
DSA_KV_RANK = 256
IDX_N_HEADS = 16
IDX_HEAD_DIM = 64
IDX_TOPK = 256

ATT_TQ = 256
ATT_STEP_ELEMS = 4 * 1024 * 1024
ATT_CHUNK_ELEMS = 1024 * 1024
IDX_TQ = 512
BISECT_ITERS = 40
ONES_ROWS = 16


def _cparams(n_axes, flags=None):
    return pltpu.CompilerParams(dimension_semantics=("arbitrary",) * n_axes,
                                vmem_limit_bytes=VMEM_LIMIT_BYTES, flags=flags)


LOG2_E = math.log2(math.e)


def _sigmoid(v):
    return 1.0 / (1.0 + jnp.exp2(v * (-LOG2_E)))


def _proj_silu_kernel(x_ref, w_ref, o_ref):
    y = jnp.dot(x_ref[...], w_ref[...], preferred_element_type=F32)
    o_ref[...] = (y * _sigmoid(y)).astype(o_ref.dtype)


def _proj_conv_silu_kernel(x_ref, w_ref, conv_w_ref, conv_b_ref, o_ref, y_scr, *, chunk):
    seq, tn = o_ref.shape
    k_taps = conv_w_ref.shape[0]
    head = y_scr.shape[0] - seq
    y_scr[:head, :] = jnp.zeros((head, tn), F32)
    y_scr[head:, :] = jnp.dot(x_ref[...], w_ref[...], preferred_element_type=F32)
    for c0 in range(0, tn, chunk):
        cols = slice(c0, c0 + chunk)
        acc = conv_b_ref[:, cols] + conv_w_ref[k_taps - 1:k_taps, cols] * y_scr[head:, cols]
        for back in range(1, k_taps):
            tap = conv_w_ref[k_taps - 1 - back:k_taps - back, cols]
            acc = acc + tap * y_scr[head - back:head - back + seq, cols]
        o_ref[:, cols] = (acc * _sigmoid(acc)).astype(o_ref.dtype)


def _proj(x, w, *, batch, seq, tn, conv=None, out_dtype=BF16):
    k_dim, n_dim = w.shape
    assert n_dim % tn == 0 and tn % LANES == 0
    in_specs = [pl.BlockSpec((seq, k_dim), lambda b, j: (b, 0)),
                pl.BlockSpec((k_dim, tn), lambda b, j: (0, j))]
    if conv is None:
        body, args, scratch, name = _proj_silu_kernel, (x, w), [], "proj_silu"
    else:
        body = functools.partial(_proj_conv_silu_kernel, chunk=2 * LANES)
        args = (x, w) + tuple(conv)
        in_specs += [pl.BlockSpec((c.shape[0], tn), lambda b, j: (0, j)) for c in conv]
        assert conv[0].shape[0] - 1 <= SUBLANES
        scratch, name = [pltpu.VMEM((SUBLANES + seq, tn), F32)], "proj_conv_silu"
    return pl.pallas_call(
        body,
        grid=(batch, n_dim // tn),
        in_specs=in_specs,
        out_specs=pl.BlockSpec((seq, tn), lambda b, j: (b, j)),
        out_shape=jax.ShapeDtypeStruct((batch * seq, n_dim), out_dtype),
        scratch_shapes=scratch,
        compiler_params=_cparams(2),
        name=name,
    )(*args)


def _dt_kernel(x_ref, w_ref, b_ref, spread_ref, o_ref):
    y = jnp.dot(x_ref[...], w_ref[...], preferred_element_type=F32) + b_ref[...]
    y = jnp.maximum(y, 0.0) + jnp.log1p(jnp.exp(-jnp.abs(y)))
    spread = spread_ref[...]
    hi = y.astype(BF16)
    rest = y - hi.astype(F32)
    mid = rest.astype(BF16)
    lo = (rest - mid.astype(F32)).astype(BF16)
    o_ref[...] = (jnp.dot(hi, spread, preferred_element_type=F32)
                  + jnp.dot(mid, spread, preferred_element_type=F32)
                  + jnp.dot(lo, spread, preferred_element_type=F32))


def _dt_proj(x, w, bias, spread, *, batch, seq):
    k_dim = w.shape[0]
    width = spread.shape[1]
    return pl.pallas_call(
        _dt_kernel,
        grid=(batch,),
        in_specs=[pl.BlockSpec((seq, k_dim), lambda b: (b, 0)),
                  pl.BlockSpec((k_dim, LANES), lambda b: (0, 0)),
                  pl.BlockSpec((1, LANES), lambda b: (0, 0)),
                  pl.BlockSpec((LANES, width), lambda b: (0, 0))],
        out_specs=pl.BlockSpec((seq, width), lambda b: (b, 0)),
        out_shape=jax.ShapeDtypeStruct((batch * seq, width), F32),
        compiler_params=_cparams(1),
        name="dt_proj",
    )(x, w, bias, spread)


def _cumsum_rows(v):
    n = v.shape[0]
    row = lax.broadcasted_iota(jnp.int32, v.shape, 0)
    shift = 1
    while shift < n:
        v = v + jnp.where(row >= shift, pltpu.roll(v, shift, 0), 0.0)
        shift *= 2
    return v


def _ssd_kernel(xs_ref, b_ref, c_ref, z_ref, dt_ref, alog_ref, dskip_ref, ng_ref, e_ref, o_ref,
                state_ref, *, heads_per_group, head_dim):
    g = pl.program_id(1)
    chunk = pl.program_id(2)
    q = xs_ref.shape[0]

    @pl.when(chunk == 0)
    def _():
        state_ref[...] = jnp.zeros_like(state_ref)

    xs = xs_ref[...].astype(F32)
    bm = b_ref[...]
    cm = c_ref[...]
    dt = dt_ref[...]
    a = -jnp.exp(alog_ref[pl.ds(g, 1), :]) * math.log2(math.e)
    acs = _cumsum_rows(dt * a)
    acs_t = acs.T
    last = acs[q - 1:q, :]
    expand_mat = e_ref[...]

    def expand(v):
        return jnp.dot(v.astype(BF16), expand_mat, preferred_element_type=F32)

    def expand_hi_lo(v):
        hi = v.astype(BF16)
        lo = (v - hi.astype(F32)).astype(BF16)
        return (jnp.dot(hi, expand_mat, preferred_element_type=F32)
                + jnp.dot(lo, expand_mat, preferred_element_type=F32))

    dt_x = expand(dt)
    decay_in_x = expand_hi_lo(jnp.exp2(acs))
    decay_out_dt_x = expand(jnp.exp2(last - acs) * dt)
    xdt = xs * dt_x

    cb = lax.dot_general(cm, bm, (((1,), (1,)), ((), ())), preferred_element_type=F32)
    blk = LANES
    tri = (lax.broadcasted_iota(jnp.int32, (blk, blk), 0) >= lax.broadcasted_iota(jnp.int32, (blk, blk), 1))

    def masked_cb(r):
        cols = []
        for s0 in range(0, q, blk):
            parts = []
            if s0 > 0:
                parts.append(jnp.zeros((s0, blk), BF16))
            diff = acs[s0:s0 + blk, r:r + 1] - acs_t[r:r + 1, s0:s0 + blk]
            decay = jnp.exp2(jnp.where(tri, diff, -jnp.inf))
            parts.append((cb[s0:s0 + blk, s0:s0 + blk] * decay).astype(BF16))
            if s0 + blk < q:
                diff = acs[s0 + blk:, r:r + 1] - acs_t[r:r + 1, s0:s0 + blk]
                parts.append((cb[s0 + blk:, s0:s0 + blk] * jnp.exp2(diff)).astype(BF16))
            cols.append(jnp.concatenate(parts, axis=0) if len(parts) > 1 else parts[0])
        return jnp.concatenate(cols, axis=1) if len(cols) > 1 else cols[0]

    heads_per_slab = LANES // head_dim
    lane_head = lax.broadcasted_iota(jnp.int32, (q, LANES), 1) // head_dim
    slabs = []
    for j in range(heads_per_group // heads_per_slab):
        x_slab = xdt[:, j * LANES:(j + 1) * LANES]
        acc = None
        for sub in range(heads_per_slab):
            r = j * heads_per_slab + sub
            m = masked_cb(r)
            x_head = jnp.where(lane_head == sub, x_slab, 0.0).astype(BF16)
            part = jnp.dot(m, x_head, preferred_element_type=F32)
            acc = part if acc is None else acc + part
        slabs.append(acc)
    y_diag = jnp.concatenate(slabs, axis=1)

    s_prev = state_ref[...]
    y_off = jnp.dot(cm, s_prev.astype(BF16), preferred_element_type=F32) * decay_in_x
    y = y_diag + y_off + dskip_ref[pl.ds(g, 1), :] * xs
    b_t = bm.astype(F32).T.astype(BF16)
    s_new = jnp.dot(b_t, (xs * decay_out_dt_x).astype(BF16), preferred_element_type=F32)
    state_ref[...] = s_prev * decay_in_x[q - 1:q, :] + s_new

    yg = y * z_ref[...].astype(F32)
    ms = jnp.mean(yg * yg, axis=-1, keepdims=True)
    o_ref[...] = (yg * lax.rsqrt(ms + RMS_EPS) * ng_ref[...]).astype(o_ref.dtype)


def _ssd_scan(xbc, z, dt, alog, dskip, norm_g, expand_mat, *, batch, seq, d_inner):
    n_groups, d_state, chunk = SSD_N_GROUPS, SSD_D_STATE, math.gcd(SSD_CHUNK, seq)
    gw = d_inner // n_groups
    heads_per_group = gw // SSD_HEAD_DIM
    n_chunks = seq // chunk
    b_col0 = d_inner // d_state
    c_col0 = b_col0 + n_groups
    row = lambda b, g, c: b * n_chunks + c
    return pl.pallas_call(
        functools.partial(_ssd_kernel, heads_per_group=heads_per_group, head_dim=SSD_HEAD_DIM),
        grid=(batch, n_groups, n_chunks),
        in_specs=[
            pl.BlockSpec((chunk, gw), lambda b, g, c: (row(b, g, c), g)),
            pl.BlockSpec((chunk, d_state), lambda b, g, c: (row(b, g, c), b_col0 + g)),
            pl.BlockSpec((chunk, d_state), lambda b, g, c: (row(b, g, c), c_col0 + g)),
            pl.BlockSpec((chunk, gw), lambda b, g, c: (row(b, g, c), g)),
            pl.BlockSpec((chunk, LANES), lambda b, g, c: (row(b, g, c), g)),
            pl.BlockSpec((n_groups, LANES), lambda b, g, c: (0, 0)),
            pl.BlockSpec((n_groups, gw), lambda b, g, c: (0, 0)),
            pl.BlockSpec((1, gw), lambda b, g, c: (0, g)),
            pl.BlockSpec((LANES, gw), lambda b, g, c: (0, 0)),
        ],
        out_specs=pl.BlockSpec((chunk, gw), lambda b, g, c: (row(b, g, c), g)),
        out_shape=jax.ShapeDtypeStruct((batch * seq, d_inner), BF16),
        scratch_shapes=[pltpu.VMEM((d_state, gw), F32)],
        compiler_params=_cparams(3),
        name="ssd_scan",
    )(xbc, xbc, xbc, z, dt, alog, dskip, norm_g, expand_mat)


def _outproj_ln_kernel(y_ref, w_ref, x_ref, g_ref, b_ref, o_ref, obf_ref, *, alpha):
    h = jnp.dot(y_ref[...], w_ref[...], preferred_element_type=F32)
    v = alpha * x_ref[...] + h
    mu = jnp.mean(v, axis=-1, keepdims=True)
    d = v - mu
    var = jnp.mean(d * d, axis=-1, keepdims=True)
    out = d * lax.rsqrt(var + LN_EPS) * g_ref[...] + b_ref[...]
    o_ref[...] = out
    obf_ref[...] = out.astype(BF16)


def _outproj_ln(y, w, x, ln_g, ln_b, *, alpha, tm):
    m, k_dim = y.shape
    d = w.shape[1]
    return pl.pallas_call(
        functools.partial(_outproj_ln_kernel, alpha=alpha),
        grid=(m // tm,),
        in_specs=[pl.BlockSpec((tm, k_dim), lambda i: (i, 0)),
                  pl.BlockSpec((k_dim, d), lambda i: (0, 0), pipeline_mode=pl.Buffered(1)),
                  pl.BlockSpec((tm, d), lambda i: (i, 0)),
                  pl.BlockSpec((1, d), lambda i: (0, 0)),
                  pl.BlockSpec((1, d), lambda i: (0, 0))],
        out_specs=[pl.BlockSpec((tm, d), lambda i: (i, 0)),
                   pl.BlockSpec((tm, d), lambda i: (i, 0))],
        out_shape=[jax.ShapeDtypeStruct((m, d), F32), jax.ShapeDtypeStruct((m, d), BF16)],
        compiler_params=_cparams(1),
        name="outproj_ln",
    )(y, w, x, ln_g, ln_b)


def _dsa_latent_kernel(x_ref, w_ref, qg_ref, kvg_ref, cqt_ref, ckv_ref, ckvt_ref, slab_ref, slabt_ref):
    p = jnp.dot(x_ref[...], w_ref[...], preferred_element_type=F32)
    q_rank = cqt_ref.shape[1]
    kv_rank = ckv_ref.shape[1]
    cq = p[:, :q_rank]
    ckv = p[:, q_rank:q_rank + kv_rank]
    slab = p[:, q_rank + kv_rank:]
    cq = cq * lax.rsqrt(jnp.mean(cq * cq, axis=-1, keepdims=True) + RMS_EPS) * qg_ref[...]
    ckv = ckv * lax.rsqrt(jnp.mean(ckv * ckv, axis=-1, keepdims=True) + RMS_EPS) * kvg_ref[...]
    cqt_ref[0] = cq.T.astype(BF16)
    ckv_ref[...] = ckv.astype(BF16)
    ckvt_ref[0, :kv_rank, :] = ckv.T.astype(BF16)
    ckvt_ref[0, kv_rank:, :] = jnp.ones((ckvt_ref.shape[1] - kv_rank, ckvt_ref.shape[2]), BF16)
    slab_ref[...] = slab.astype(BF16)
    slabt_ref[0] = slab.T


def _dsa_latent(x, w, q_gain, kv_gain, *, batch, seq):
    tm = IDX_TQ
    n_t = seq // tm
    k_dim, n_dim = w.shape
    return pl.pallas_call(
        _dsa_latent_kernel,
        grid=(batch, n_t),
        in_specs=[pl.BlockSpec((tm, k_dim), lambda b, t: (b * n_t + t, 0)),
                  pl.BlockSpec((k_dim, n_dim), lambda b, t: (0, 0)),
                  pl.BlockSpec((1, DSA_Q_RANK), lambda b, t: (0, 0)),
                  pl.BlockSpec((1, DSA_KV_RANK), lambda b, t: (0, 0))],
        out_specs=[pl.BlockSpec((1, DSA_Q_RANK, tm), lambda b, t: (b, 0, t)),
                   pl.BlockSpec((tm, DSA_KV_RANK), lambda b, t: (b * n_t + t, 0)),
                   pl.BlockSpec((1, DSA_KV_RANK + ONES_ROWS, tm), lambda b, t: (b, 0, t)),
                   pl.BlockSpec((tm, LANES), lambda b, t: (b * n_t + t, 0)),
                   pl.BlockSpec((1, LANES, tm), lambda b, t: (b, 0, t))],
        out_shape=[jax.ShapeDtypeStruct((batch, DSA_Q_RANK, seq), BF16),
                   jax.ShapeDtypeStruct((batch * seq, DSA_KV_RANK), BF16),
                   jax.ShapeDtypeStruct((batch, DSA_KV_RANK + ONES_ROWS, seq), BF16),
                   jax.ShapeDtypeStruct((batch * seq, LANES), BF16),
                   jax.ShapeDtypeStruct((batch, LANES, seq), F32)],
        compiler_params=_cparams(2),
        name="dsa_latent",
    )(x, w, q_gain, kv_gain)


def _dsa_query_kernel(cqt_ref, wuqt_ref, wuk_ref, ql_ref, *, scale):
    hb, _, hd = wuk_ref.shape
    tq = ql_ref.shape[3] // hb
    q_t = jnp.dot(wuqt_ref[...], cqt_ref[0], preferred_element_type=F32).astype(BF16)
    for j in range(hb):
        ql_t = jnp.dot(wuk_ref[j], q_t[j * hd:(j + 1) * hd, :], preferred_element_type=F32)
        ql_t = (ql_t * scale).astype(BF16)
        for blk in range(ql_ref.shape[1]):
            ql_ref[0, blk, :, j * tq:(j + 1) * tq] = ql_t[:, blk * tq:(blk + 1) * tq]


def _dsa_query(cq_t, wuq_t, wuk, *, batch, seq, scale):
    tm = min(seq, 1024)
    hb = 8
    tq = ATT_TQ
    n_t = seq // tm
    n_heads, kv_rank, hd = wuk.shape
    return pl.pallas_call(
        functools.partial(_dsa_query_kernel, scale=scale),
        grid=(batch, n_t, n_heads // hb),
        in_specs=[pl.BlockSpec((1, DSA_Q_RANK, tm), lambda b, t, h: (b, 0, t)),
                  pl.BlockSpec((hb * hd, DSA_Q_RANK), lambda b, t, h: (h, 0)),
                  pl.BlockSpec((hb, kv_rank, hd), lambda b, t, h: (h, 0, 0))],
        out_specs=pl.BlockSpec((1, tm // tq, kv_rank, hb * tq), lambda b, t, h: (b, t, 0, h)),
        out_shape=jax.ShapeDtypeStruct((batch, seq // tq, kv_rank, n_heads * tq), BF16),
        compiler_params=_cparams(3),
        name="dsa_query",
    )(cq_t, wuq_t, wuk)


def _indexer_kernel(keys_ref, slabt_ref, cqt_ref, widx_ref, bias_ref, qt_scr, sc_scr, *, k_top, w_scale, n_iter):
    i = pl.program_id(1)
    n_t, kt_w, tq = sc_scr.shape
    cq_t = cqt_ref[0]
    for h in range(IDX_N_HEADS):
        qt_scr[h] = jnp.dot(widx_ref[h], cq_t, preferred_element_type=F32).astype(BF16)
    w_t = slabt_ref[0][IDX_HEAD_DIM:IDX_HEAD_DIM + IDX_N_HEADS, :] * w_scale
    q_pos = i * tq + lax.broadcasted_iota(jnp.int32, (kt_w, tq), 1)

    for kt in range(n_t):
        @pl.when(kt <= i)
        def _(kt=kt):
            keys = keys_ref[kt * kt_w:(kt + 1) * kt_w, :]
            acc = jnp.zeros((kt_w, tq), F32)
            for h in range(IDX_N_HEADS):
                s = jnp.dot(keys, qt_scr[h], preferred_element_type=F32)
                acc = acc + jnp.maximum(s, 0.0) * w_t[h:h + 1, :]
            k_pos = kt * kt_w + lax.broadcasted_iota(jnp.int32, (kt_w, tq), 0)
            sc_scr[kt] = jnp.where(k_pos <= q_pos, acc, -jnp.inf)

    def bounds(kt, carry):
        lo, hi = carry
        s = sc_scr[kt]
        hi = jnp.maximum(hi, jnp.max(s, axis=0, keepdims=True))
        lo = jnp.minimum(lo, jnp.min(jnp.where(s == -jnp.inf, jnp.inf, s), axis=0, keepdims=True))
        return lo, hi

    lo, hi = lax.fori_loop(0, i + 1, bounds,
                           (jnp.full((1, tq), jnp.inf, F32), jnp.full((1, tq), -jnp.inf, F32)))

    def halve(carry):
        it, lo, hi, n_lo = carry
        mid = 0.5 * lo + 0.5 * hi

        def count(kt, cnt):
            return cnt + jnp.sum(jnp.where(sc_scr[kt] >= mid, 1.0, 0.0), axis=0, keepdims=True)

        cnt = lax.fori_loop(0, i + 1, count, jnp.zeros((1, tq), F32))
        enough = cnt >= k_top
        return it + 1, jnp.where(enough, mid, lo), jnp.where(enough, hi, mid), jnp.where(enough, cnt, n_lo)

    def unresolved(carry):
        it, _, _, n_lo = carry
        return jnp.logical_and(it < n_iter, jnp.max(n_lo) > k_top)

    n_causal = (i * tq + lax.broadcasted_iota(jnp.int32, (1, tq), 1) + 1).astype(F32)
    _, lo, hi, _ = lax.while_loop(unresolved, halve, (jnp.int32(0), lo, hi, n_causal))

    for kt in range(n_t):
        @pl.when(kt <= i)
        def _(kt=kt):
            bias_ref[0, kt * kt_w:(kt + 1) * kt_w, :] = jnp.where(sc_scr[kt] >= lo, 0.0, -jnp.inf)

        @pl.when(kt > i)
        def _(kt=kt):
            bias_ref[0, kt * kt_w:(kt + 1) * kt_w, :] = jnp.full((kt_w, tq), -jnp.inf, F32)


def _dsa_indexer(keys, slab_t, cq_t, widx_t, *, batch, seq, k_top, w_scale):
    tq = IDX_TQ
    n_t = seq // tq
    return pl.pallas_call(
        functools.partial(_indexer_kernel, k_top=k_top, w_scale=w_scale, n_iter=BISECT_ITERS),
        grid=(batch, n_t),
        in_specs=[pl.BlockSpec((seq, LANES), lambda b, i: (b, 0)),
                  pl.BlockSpec((1, LANES, tq), lambda b, i: (b, 0, i)),
                  pl.BlockSpec((1, DSA_Q_RANK, tq), lambda b, i: (b, 0, i)),
                  pl.BlockSpec(widx_t.shape, lambda b, i: (0, 0, 0))],
        out_specs=pl.BlockSpec((1, seq, tq), lambda b, i: (b, 0, i)),
        out_shape=jax.ShapeDtypeStruct((batch, seq, seq), F32),
        scratch_shapes=[pltpu.VMEM((IDX_N_HEADS, LANES, tq), BF16),
                        pltpu.VMEM((n_t, tq, tq), F32)],
        compiler_params=_cparams(2),
        name="dsa_indexer",
    )(keys, slab_t, cq_t, widx_t)


def _attn_kernel(ql_ref, bias_ref, k_ref, vt_ref, wuvt_ref, gate_ref, *rest, n_keys, hc):
    o_ref = rest[-1]
    tq = bias_ref.shape[2]
    hb, hd, kv_rank = wuvt_ref.shape
    cols = hc * tq
    n_chunks = hb // hc
    keys = k_ref[:n_keys, :]
    bias_t = bias_ref[0]
    bias_rep = jnp.concatenate([bias_t] * hc, axis=1) if hc > 1 else bias_t

    def scores(c):
        q_t = ql_ref[0, 0, :, c * cols:(c + 1) * cols]
        return jnp.dot(keys, q_t, preferred_element_type=F32) + bias_rep

    def finish(c, s):
        m = jnp.max(s, axis=0, keepdims=True)
        p = jnp.exp2(s - m).astype(BF16)
        o_aug = jnp.dot(vt_ref[0], p, preferred_element_type=F32)
        o_t = (o_aug[:kv_rank] / o_aug[kv_rank:kv_rank + 1]).astype(BF16)
        for j in range(hc):
            h = c * hc + j
            oh_t = jnp.dot(wuvt_ref[h], o_t[:, j * tq:(j + 1) * tq], preferred_element_type=F32)
            gate = gate_ref[:, h * hd:(h + 1) * hd].astype(F32)
            o_ref[:, h * hd:(h + 1) * hd] = (oh_t.T * gate).astype(o_ref.dtype)

    s_next = scores(0)
    for c in range(n_chunks):
        s_cur = s_next
        if c + 1 < n_chunks:
            s_next = scores(c + 1)
        finish(c, s_cur)


def _dsa_attention(ql, bias, ckv_t, ckv, wuv_t, gate, *, batch, seq):
    tq = ATT_TQ
    n_q = seq // tq
    n_heads, hd, kv_rank = wuv_t.shape
    width = n_heads * hd
    og = None
    for i in range(n_q):
        n_keys = (i + 1) * tq
        hb = n_heads
        while hb > 1 and hb * tq * n_keys > ATT_STEP_ELEMS:
            hb //= 2
        hc = hb
        while hc > 1 and hc * tq * n_keys > ATT_CHUNK_ELEMS:
            hc //= 2
        in_specs = [pl.BlockSpec((1, 1, kv_rank, hb * tq), lambda b, g, i=i: (b, i, 0, g)),
                    pl.BlockSpec((1, n_keys, tq), lambda b, g, i=i: (b, 0, i)),
                    pl.BlockSpec((seq, kv_rank), lambda b, g: (b, 0)),
                    pl.BlockSpec((1, kv_rank + ONES_ROWS, n_keys), lambda b, g: (b, 0, 0)),
                    pl.BlockSpec((hb, hd, kv_rank), lambda b, g: (g, 0, 0)),
                    pl.BlockSpec((tq, hb * hd), lambda b, g, i=i: (b * n_q + i, g))]
        args = [ql, bias, ckv, ckv_t, wuv_t, gate]
        aliases = {}
        if og is not None:
            in_specs.append(pl.BlockSpec(memory_space=pl.ANY))
            args.append(og)
            aliases = {len(args) - 1: 0}
        og = pl.pallas_call(
            functools.partial(_attn_kernel, n_keys=n_keys, hc=hc),
            grid=(batch, n_heads // hb),
            in_specs=in_specs,
            out_specs=pl.BlockSpec((tq, hb * hd), lambda b, g, i=i: (b * n_q + i, g)),
            out_shape=jax.ShapeDtypeStruct((batch * seq, width), BF16),
            input_output_aliases=aliases,
            compiler_params=_cparams(2),
            name="dsa_attention_q%d" % i,
        )(*args)
    return og


def _ssd_layer(x, x_bf, w_in, conv_w, conv_b, dt_bias, a_log, d_skip, norm_g, w_out, ln_g, ln_b,
               *, batch, seq, alpha):
    d_inner = w_out.shape[0]
    n_heads = a_log.shape[0]
    n_groups = SSD_N_GROUPS
    heads_per_group = n_heads // n_groups
    conv_dim = conv_w.shape[1]
    gw = d_inner // n_groups
    assert gw % LANES == 0 and heads_per_group <= LANES and SSD_D_STATE == LANES
    assert LANES % SSD_HEAD_DIM == 0 and heads_per_group % (LANES // SSD_HEAD_DIM) == 0

    def per_group_lanes(v):
        lead = v.shape[:-1]
        v = v.reshape(lead + (n_groups, heads_per_group))
        v = jnp.pad(v, [(0, 0)] * len(lead) + [(0, 0), (0, LANES - heads_per_group)])
        return v.reshape(lead + (n_groups * LANES,))

    w_z = w_in[:, :d_inner].astype(BF16)
    w_xbc = w_in[:, d_inner:d_inner + conv_dim].astype(BF16)
    w_dt = jnp.pad(w_in[:, d_inner + conv_dim:], ((0, 0), (0, LANES - n_heads))).astype(BF16)
    dt_b = jnp.pad(dt_bias, (0, LANES - n_heads))[None, :]
    head_of_lane = jnp.arange(n_groups * LANES)
    head_of_lane = jnp.where(head_of_lane % LANES < heads_per_group,
                             (head_of_lane // LANES) * heads_per_group + head_of_lane % LANES, -1)
    spread = (jnp.arange(LANES)[:, None] == head_of_lane[None, :]).astype(BF16)
    alog = per_group_lanes(a_log).reshape(n_groups, LANES)
    dskip = jnp.repeat(d_skip, SSD_HEAD_DIM).reshape(n_groups, gw)
    lane_of = jnp.arange(gw) // SSD_HEAD_DIM
    expand_mat = (jnp.arange(LANES)[:, None] == lane_of[None, :]).astype(BF16)

    z = _proj(x_bf, w_z, batch=batch, seq=seq, tn=512)
    xbc = _proj(x_bf, w_xbc, batch=batch, seq=seq, tn=1024, conv=(conv_w, conv_b[None, :]))
    dt = _dt_proj(x_bf, w_dt, dt_b, spread, batch=batch, seq=seq)
    yn = _ssd_scan(xbc, z, dt, alog, dskip, norm_g[None, :], expand_mat, batch=batch, seq=seq, d_inner=d_inner)
    return _outproj_ln(yn, w_out.astype(BF16), x, ln_g[None, :], ln_b[None, :], alpha=alpha, tm=512)


def _dsa_layer(x, x_bf, w_in, q_norm_g, kv_norm_g, w_uq, w_uk, w_uv, w_idx_q, w_out, ln_g, ln_b,
               *, batch, seq, alpha):
    assert seq % IDX_TQ == 0 and IDX_TQ % ATT_TQ == 0 and IDX_HEAD_DIM + IDX_N_HEADS <= LANES
    small = DSA_Q_RANK + DSA_KV_RANK + IDX_HEAD_DIM + IDX_N_HEADS
    w_small = jnp.pad(w_in[:, :small], ((0, 0), (0, DSA_Q_RANK + DSA_KV_RANK + LANES - small))).astype(BF16)
    w_gate = w_in[:, small:].astype(BF16)
    wuq_t = w_uq.T.astype(BF16)
    wuv_t = w_uv.transpose(0, 2, 1).astype(BF16)
    widx_t = w_idx_q.reshape(DSA_Q_RANK, IDX_N_HEADS, IDX_HEAD_DIM).transpose(1, 2, 0)
    widx_t = jnp.pad(widx_t, ((0, 0), (0, LANES - IDX_HEAD_DIM), (0, 0))).astype(BF16)
    k_top = min(IDX_TOPK, seq // 4)

    cq_t, ckv, ckv_t, slab, slab_t = _dsa_latent(x_bf, w_small, q_norm_g[None, :], kv_norm_g[None, :],
                                                     batch=batch, seq=seq)
    gate = _proj(x_bf, w_gate, batch=batch, seq=seq, tn=512)
    ql = _dsa_query(cq_t, wuq_t, w_uk.astype(BF16), batch=batch, seq=seq,
                    scale=DSA_HEAD_DIM ** -0.5 * math.log2(math.e))
    bias = _dsa_indexer(slab, slab_t, cq_t, widx_t, batch=batch, seq=seq, k_top=k_top,
                        w_scale=IDX_N_HEADS ** -0.5 * IDX_HEAD_DIM ** -0.5)
    og = _dsa_attention(ql, bias, ckv_t, ckv, wuv_t, gate, batch=batch, seq=seq)
    return _outproj_ln(og, w_out.astype(BF16), x, ln_g[None, :], ln_b[None, :], alpha=alpha, tm=512)


def kernel(x, ssd_w_in, ssd_conv_w, ssd_conv_b, ssd_dt_bias, ssd_a_log, ssd_d_skip, ssd_norm_g, ssd_w_out,
           dsa_w_in, dsa_q_norm_g, dsa_kv_norm_g, dsa_w_uq, dsa_w_uk, dsa_w_uv, dsa_w_idx_q, dsa_w_out,
           ln_g, ln_b):
    batch, seq, d_model = x.shape
    depth = ln_g.shape[0]
    alpha = (2.0 * depth) ** 0.25
    xf = x.reshape(batch * seq, d_model)
    x_bf = xf.astype(BF16)
    for i in range(depth):
        j = i // 2
        if i % 2 == 0:
            xf, x_bf = _ssd_layer(xf, x_bf, ssd_w_in[j], ssd_conv_w[j], ssd_conv_b[j], ssd_dt_bias[j],
                                  ssd_a_log[j], ssd_d_skip[j], ssd_norm_g[j], ssd_w_out[j], ln_g[i], ln_b[i],
                                  batch=batch, seq=seq, alpha=alpha)
        else:
            xf, x_bf = _dsa_layer(xf, x_bf, dsa_w_in[j], dsa_q_norm_g[j], dsa_kv_norm_g[j], dsa_w_uq[j],
                                  dsa_w_uk[j], dsa_w_uv[j], dsa_w_idx_q[j], dsa_w_out[j], ln_g[i], ln_b[i],
                                  batch=batch, seq=seq, alpha=alpha)
    return xf.reshape(batch, seq, d_model)
```

```python
import functools
import math

import jax
import jax.numpy as jnp
from jax import lax
from jax.experimental import pallas as pl
from jax.experimental.pallas import tpu as pltpu

F32 = jnp.float32
BF16 = jnp.bfloat16

V7X_VMEM_BYTES = 64 * 1024 * 1024
LANES = 128
SUBLANES = 8
VMEM_LIMIT_BYTES = V7X_VMEM_BYTES - 8 * 1024 * 1024

LN_EPS = 1e-5
RMS_EPS = 1e-6

SSD_HEAD_DIM = 64
SSD_N_GROUPS = 8
SSD_D_STATE = 128
SSD_CHUNK = 256
DSA_N_HEADS = 32
DSA_HEAD_DIM = 128
DSA_Q_RANK = 512
DSA_KV_RANK = 256
IDX_N_HEADS = 16
IDX_HEAD_DIM = 64
IDX_TOPK = 256

ATT_TQ = 256
ATT_STEP_ELEMS = 4 * 1024 * 1024
ATT_CHUNK_ELEMS = 1024 * 1024
IDX_TQ = 512
BISECT_ITERS = 40
ONES_ROWS = 16
CONV_ROW_CHUNK = 256


def _cparams(n_axes, flags=None):
    return pltpu.CompilerParams(dimension_semantics=("arbitrary",) * n_axes,
                                vmem_limit_bytes=VMEM_LIMIT_BYTES, flags=flags)


LOG2_E = math.log2(math.e)


def _sigmoid(v):
    return 1.0 / (1.0 + jnp.exp2(v * (-LOG2_E)))


def _proj_silu_kernel(x_ref, w_ref, o_ref, *, chunk):
    w = w_ref[...]
    for r0 in range(0, o_ref.shape[0], chunk):
        y = jnp.dot(x_ref[r0:r0 + chunk, :], w, preferred_element_type=F32)
        o_ref[r0:r0 + chunk, :] = (y * _sigmoid(y)).astype(o_ref.dtype)


def _proj_conv_silu_kernel(x_ref, w_ref, conv_w_ref, conv_b_ref, o_ref, y_scr, *, chunk):
    seq, tn = o_ref.shape
    k_taps = conv_w_ref.shape[0]
    head = y_scr.shape[0] - seq
    y_scr[:head, :] = jnp.zeros((head, tn), F32)
    w = w_ref[...]

    def matmul(r0):
        y_scr[head + r0:head + r0 + chunk, :] = jnp.dot(x_ref[r0:r0 + chunk, :], w, preferred_element_type=F32)

    def finish(r0):
        acc = conv_b_ref[...] + conv_w_ref[k_taps - 1:k_taps, :] * y_scr[head + r0:head + r0 + chunk, :]
        for back in range(1, k_taps):
            tap = conv_w_ref[k_taps - 1 - back:k_taps - back, :]
            acc = acc + tap * y_scr[head + r0 - back:head + r0 - back + chunk, :]
        o_ref[r0:r0 + chunk, :] = (acc * _sigmoid(acc)).astype(o_ref.dtype)

    matmul(0)
    for r0 in range(0, seq, chunk):
        if r0 + chunk < seq:
            matmul(r0 + chunk)
        finish(r0)


def _proj(x, w, *, batch, seq, tn, conv=None, out_dtype=BF16):
    k_dim, n_dim = w.shape
    assert n_dim % tn == 0 and tn % LANES == 0
    in_specs = [pl.BlockSpec((seq, k_dim), lambda b, j: (b, 0)),
                pl.BlockSpec((k_dim, tn), lambda b, j: (0, j))]
    if conv is None:
        body = functools.partial(_proj_silu_kernel, chunk=CONV_ROW_CHUNK)
        args, scratch, name = (x, w), [], "proj_silu"
    else:
        body = functools.partial(_proj_conv_silu_kernel, chunk=CONV_ROW_CHUNK)
        args = (x, w) + tuple(conv)
        in_specs += [pl.BlockSpec((c.shape[0], tn), lambda b, j: (0, j)) for c in conv]
        assert conv[0].shape[0] - 1 <= SUBLANES
        scratch, name = [pltpu.VMEM((SUBLANES + seq, tn), F32)], "proj_conv_silu"
    return pl.pallas_call(
        body,
        grid=(batch, n_dim // tn),
        in_specs=in_specs,
        out_specs=pl.BlockSpec((seq, tn), lambda b, j: (b, j)),
        out_shape=jax.ShapeDtypeStruct((batch * seq, n_dim), out_dtype),
        scratch_shapes=scratch,
        compiler_params=_cparams(2),
        name=name,
    )(*args)


def _dt_kernel(x_ref, w_ref, b_ref, spread_ref, o_ref):
    y = jnp.dot(x_ref[...], w_ref[...], preferred_element_type=F32) + b_ref[...]
    y = jnp.maximum(y, 0.0) + jnp.log1p(jnp.exp(-jnp.abs(y)))
    spread = spread_ref[...]
    hi = y.astype(BF16)
    rest = y - hi.astype(F32)
    mid = rest.astype(BF16)
    lo = (rest - mid.astype(F32)).astype(BF16)
    o_ref[...] = (jnp.dot(hi, spread, preferred_element_type=F32)
                  + jnp.dot(mid, spread, preferred_element_type=F32)
                  + jnp.dot(lo, spread, preferred_element_type=F32))


def _dt_proj(x, w, bias, spread, *, batch, seq):
    k_dim = w.shape[0]
    width = spread.shape[1]
    return pl.pallas_call(
        _dt_kernel,
        grid=(batch,),
        in_specs=[pl.BlockSpec((seq, k_dim), lambda b: (b, 0)),
                  pl.BlockSpec((k_dim, LANES), lambda b: (0, 0)),
                  pl.BlockSpec((1, LANES), lambda b: (0, 0)),
                  pl.BlockSpec((LANES, width), lambda b: (0, 0))],
        out_specs=pl.BlockSpec((seq, width), lambda b: (b, 0)),
        out_shape=jax.ShapeDtypeStruct((batch * seq, width), F32),
        compiler_params=_cparams(1),
        name="dt_proj",
    )(x, w, bias, spread)


def _cumsum_rows(v):
    n = v.shape[0]
    row = lax.broadcasted_iota(jnp.int32, v.shape, 0)
    shift = 1
    while shift < n:
        v = v + jnp.where(row >= shift, pltpu.roll(v, shift, 0), 0.0)
        shift *= 2
    return v


def _ssd_kernel(xs_ref, b_ref, c_ref, z_ref, dt_ref, alog_ref, dskip_ref, ng_ref, e_ref, o_ref,
                state_ref, *, heads_per_group, head_dim):
    g = pl.program_id(1)
    chunk = pl.program_id(2)
    q = xs_ref.shape[0]

    @pl.when(chunk == 0)
    def _():
        state_ref[...] = jnp.zeros_like(state_ref)

    xs = xs_ref[...].astype(F32)
    bm = b_ref[...]
    cm = c_ref[...]
    dt = dt_ref[...]
    a = -jnp.exp(alog_ref[pl.ds(g, 1), :]) * math.log2(math.e)
    acs = _cumsum_rows(dt * a)
    acs_t = acs.T
    last = acs[q - 1:q, :]
    expand_mat = e_ref[...]

    def expand(v):
        return jnp.dot(v.astype(BF16), expand_mat, preferred_element_type=F32)

    def expand_hi_lo(v):
        hi = v.astype(BF16)
        lo = (v - hi.astype(F32)).astype(BF16)
        return (jnp.dot(hi, expand_mat, preferred_element_type=F32)
                + jnp.dot(lo, expand_mat, preferred_element_type=F32))

    dt_x = expand(dt)
    decay_in_x = expand_hi_lo(jnp.exp2(acs))
    decay_out_dt_x = expand(jnp.exp2(last - acs) * dt)
    xdt = xs * dt_x

    cb = lax.dot_general(cm, bm, (((1,), (1,)), ((), ())), preferred_element_type=F32)
    blk = LANES
    tri = (lax.broadcasted_iota(jnp.int32, (blk, blk), 0) >= lax.broadcasted_iota(jnp.int32, (blk, blk), 1))

    def masked_cb(r):
        cols = []
        for s0 in range(0, q, blk):
            parts = []
            if s0 > 0:
                parts.append(jnp.zeros((s0, blk), BF16))
            diff = acs[s0:s0 + blk, r:r + 1] - acs_t[r:r + 1, s0:s0 + blk]
            decay = jnp.exp2(jnp.where(tri, diff, -jnp.inf))
            parts.append((cb[s0:s0 + blk, s0:s0 + blk] * decay).astype(BF16))
            if s0 + blk < q:
                diff = acs[s0 + blk:, r:r + 1] - acs_t[r:r + 1, s0:s0 + blk]
                parts.append((cb[s0 + blk:, s0:s0 + blk] * jnp.exp2(diff)).astype(BF16))
            cols.append(jnp.concatenate(parts, axis=0) if len(parts) > 1 else parts[0])
        return jnp.concatenate(cols, axis=1) if len(cols) > 1 else cols[0]

    heads_per_slab = LANES // head_dim
    lane_head = lax.broadcasted_iota(jnp.int32, (q, LANES), 1) // head_dim
    slabs = []
    for j in range(heads_per_group // heads_per_slab):
        x_slab = xdt[:, j * LANES:(j + 1) * LANES]
        acc = None
        for sub in range(heads_per_slab):
            r = j * heads_per_slab + sub
            m = masked_cb(r)
            x_head = jnp.where(lane_head == sub, x_slab, 0.0).astype(BF16)
            part = jnp.dot(m, x_head, preferred_element_type=F32)
            acc = part if acc is None else acc + part
        slabs.append(acc)
    y_diag = jnp.concatenate(slabs, axis=1)

    s_prev = state_ref[...]
    y_off = jnp.dot(cm, s_prev.astype(BF16), preferred_element_type=F32) * decay_in_x
    y = y_diag + y_off + dskip_ref[pl.ds(g, 1), :] * xs
    b_t = bm.astype(F32).T.astype(BF16)
    s_new = jnp.dot(b_t, (xs * decay_out_dt_x).astype(BF16), preferred_element_type=F32)
    state_ref[...] = s_prev * decay_in_x[q - 1:q, :] + s_new

    yg = y * z_ref[...].astype(F32)
    ms = jnp.mean(yg * yg, axis=-1, keepdims=True)
    o_ref[...] = (yg * lax.rsqrt(ms + RMS_EPS) * ng_ref[...]).astype(o_ref.dtype)


def _ssd_scan(xbc, z, dt, alog, dskip, norm_g, expand_mat, *, batch, seq, d_inner):
    n_groups, d_state, chunk = SSD_N_GROUPS, SSD_D_STATE, math.gcd(SSD_CHUNK, seq)
    gw = d_inner // n_groups
    heads_per_group = gw // SSD_HEAD_DIM
    n_chunks = seq // chunk
    b_col0 = d_inner // d_state
    c_col0 = b_col0 + n_groups
    row = lambda b, g, c: b * n_chunks + c
    return pl.pallas_call(
        functools.partial(_ssd_kernel, heads_per_group=heads_per_group, head_dim=SSD_HEAD_DIM),
        grid=(batch, n_groups, n_chunks),
        in_specs=[
            pl.BlockSpec((chunk, gw), lambda b, g, c: (row(b, g, c), g)),
            pl.BlockSpec((chunk, d_state), lambda b, g, c: (row(b, g, c), b_col0 + g)),
            pl.BlockSpec((chunk, d_state), lambda b, g, c: (row(b, g, c), c_col0 + g)),
            pl.BlockSpec((chunk, gw), lambda b, g, c: (row(b, g, c), g)),
            pl.BlockSpec((chunk, LANES), lambda b, g, c: (row(b, g, c), g)),
            pl.BlockSpec((n_groups, LANES), lambda b, g, c: (0, 0)),
            pl.BlockSpec((n_groups, gw), lambda b, g, c: (0, 0)),
            pl.BlockSpec((1, gw), lambda b, g, c: (0, g)),
            pl.BlockSpec((LANES, gw), lambda b, g, c: (0, 0)),
        ],
        out_specs=pl.BlockSpec((chunk, gw), lambda b, g, c: (row(b, g, c), g)),
        out_shape=jax.ShapeDtypeStruct((batch * seq, d_inner), BF16),
        scratch_shapes=[pltpu.VMEM((d_state, gw), F32)],
        compiler_params=_cparams(3),
        name="ssd_scan",
    )(xbc, xbc, xbc, z, dt, alog, dskip, norm_g, expand_mat)


def _outproj_ln_kernel(y_ref, w_ref, x_ref, g_ref, b_ref, o_ref, obf_ref, *, alpha):
    h = jnp.dot(y_ref[...], w_ref[...], preferred_element_type=F32)
    v = alpha * x_ref[...] + h
    mu = jnp.mean(v, axis=-1, keepdims=True)
    d = v - mu
    var = jnp.mean(d * d, axis=-1, keepdims=True)
    out = d * lax.rsqrt(var + LN_EPS) * g_ref[...] + b_ref[...]
    o_ref[...] = out
    obf_ref[...] = out.astype(BF16)


def _outproj_ln(y, w, x, ln_g, ln_b, *, alpha, tm):
    m, k_dim = y.shape
    d = w.shape[1]
    return pl.pallas_call(
        functools.partial(_outproj_ln_kernel, alpha=alpha),
        grid=(m // tm,),
        in_specs=[pl.BlockSpec((tm, k_dim), lambda i: (i, 0)),
                  pl.BlockSpec((k_dim, d), lambda i: (0, 0), pipeline_mode=pl.Buffered(1)),
                  pl.BlockSpec((tm, d), lambda i: (i, 0)),
                  pl.BlockSpec((1, d), lambda i: (0, 0)),
                  pl.BlockSpec((1, d), lambda i: (0, 0))],
        out_specs=[pl.BlockSpec((tm, d), lambda i: (i, 0)),
                   pl.BlockSpec((tm, d), lambda i: (i, 0))],
        out_shape=[jax.ShapeDtypeStruct((m, d), F32), jax.ShapeDtypeStruct((m, d), BF16)],
        compiler_params=_cparams(1),
        name="outproj_ln",
    )(y, w, x, ln_g, ln_b)


def _dsa_latent_kernel(x_ref, w_ref, qg_ref, kvg_ref, cqt_ref, ckv_ref, ckvt_ref, slab_ref, slabt_ref):
    p = jnp.dot(x_ref[...], w_ref[...], preferred_element_type=F32)
    q_rank = cqt_ref.shape[1]
    kv_rank = ckv_ref.shape[1]
    cq = p[:, :q_rank]
    ckv = p[:, q_rank:q_rank + kv_rank]
    slab = p[:, q_rank + kv_rank:]
    cq = cq * lax.rsqrt(jnp.mean(cq * cq, axis=-1, keepdims=True) + RMS_EPS) * qg_ref[...]
    ckv = ckv * lax.rsqrt(jnp.mean(ckv * ckv, axis=-1, keepdims=True) + RMS_EPS) * kvg_ref[...]
    cqt_ref[0] = cq.T.astype(BF16)
    ckv_ref[...] = ckv.astype(BF16)
    ckvt_ref[0, :kv_rank, :] = ckv.T.astype(BF16)
    ckvt_ref[0, kv_rank:, :] = jnp.ones((ckvt_ref.shape[1] - kv_rank, ckvt_ref.shape[2]), BF16)
    slab_ref[...] = slab.astype(BF16)
    slabt_ref[0] = slab.T


def _dsa_latent(x, w, q_gain, kv_gain, *, batch, seq):
    tm = IDX_TQ
    n_t = seq // tm
    k_dim, n_dim = w.shape
    return pl.pallas_call(
        _dsa_latent_kernel,
        grid=(batch, n_t),
        in_specs=[pl.BlockSpec((tm, k_dim), lambda b, t: (b * n_t + t, 0)),
                  pl.BlockSpec((k_dim, n_dim), lambda b, t: (0, 0)),
                  pl.BlockSpec((1, DSA_Q_RANK), lambda b, t: (0, 0)),
                  pl.BlockSpec((1, DSA_KV_RANK), lambda b, t: (0, 0))],
        out_specs=[pl.BlockSpec((1, DSA_Q_RANK, tm), lambda b, t: (b, 0, t)),
                   pl.BlockSpec((tm, DSA_KV_RANK), lambda b, t: (b * n_t + t, 0)),
                   pl.BlockSpec((1, DSA_KV_RANK + ONES_ROWS, tm), lambda b, t: (b, 0, t)),
                   pl.BlockSpec((tm, LANES), lambda b, t: (b * n_t + t, 0)),
                   pl.BlockSpec((1, LANES, tm), lambda b, t: (b, 0, t))],
        out_shape=[jax.ShapeDtypeStruct((batch, DSA_Q_RANK, seq), BF16),
                   jax.ShapeDtypeStruct((batch * seq, DSA_KV_RANK), BF16),
                   jax.ShapeDtypeStruct((batch, DSA_KV_RANK + ONES_ROWS, seq), BF16),
                   jax.ShapeDtypeStruct((batch * seq, LANES), BF16),
                   jax.ShapeDtypeStruct((batch, LANES, seq), F32)],
        compiler_params=_cparams(2),
        name="dsa_latent",
    )(x, w, q_gain, kv_gain)


def _dsa_query_kernel(cqt_ref, wuqt_ref, wuk_ref, ql_ref, *, scale):
    hb, _, hd = wuk_ref.shape
    tq = ql_ref.shape[3] // hb
    q_t = jnp.dot(wuqt_ref[...], cqt_ref[0], preferred_element_type=F32).astype(BF16)
    for j in range(hb):
        ql_t = jnp.dot(wuk_ref[j], q_t[j * hd:(j + 1) * hd, :], preferred_element_type=F32)
        ql_t = (ql_t * scale).astype(BF16)
        for blk in range(ql_ref.shape[1]):
            ql_ref[0, blk, :, j * tq:(j + 1) * tq] = ql_t[:, blk * tq:(blk + 1) * tq]


def _dsa_query(cq_t, wuq_t, wuk, *, batch, seq, scale):
    tm = min(seq, 1024)
    hb = 8
    tq = ATT_TQ
    n_t = seq // tm
    n_heads, kv_rank, hd = wuk.shape
    return pl.pallas_call(
        functools.partial(_dsa_query_kernel, scale=scale),
        grid=(batch, n_t, n_heads // hb),
        in_specs=[pl.BlockSpec((1, DSA_Q_RANK, tm), lambda b, t, h: (b, 0, t)),
                  pl.BlockSpec((hb * hd, DSA_Q_RANK), lambda b, t, h: (h, 0)),
                  pl.BlockSpec((hb, kv_rank, hd), lambda b, t, h: (h, 0, 0))],
        out_specs=pl.BlockSpec((1, tm // tq, kv_rank, hb * tq), lambda b, t, h: (b, t, 0, h)),
        out_shape=jax.ShapeDtypeStruct((batch, seq // tq, kv_rank, n_heads * tq), BF16),
        compiler_params=_cparams(3),
        name="dsa_query",
    )(cq_t, wuq_t, wuk)


def _indexer_kernel(keys_ref, slabt_ref, cqt_ref, widx_ref, bias_ref, qt_scr, sc_scr, *, k_top, w_scale, n_iter):
    i = pl.program_id(1)
    n_t, kt_w, tq = sc_scr.shape
    cq_t = cqt_ref[0]
    for h in range(IDX_N_HEADS):
        qt_scr[h] = jnp.dot(widx_ref[h], cq_t, preferred_element_type=F32).astype(BF16)
    w_t = slabt_ref[0][IDX_HEAD_DIM:IDX_HEAD_DIM + IDX_N_HEADS, :] * w_scale
    q_pos = i * tq + lax.broadcasted_iota(jnp.int32, (kt_w, tq), 1)

    for kt in range(n_t):
        @pl.when(kt <= i)
        def _(kt=kt):
            keys = keys_ref[kt * kt_w:(kt + 1) * kt_w, :]
            acc = jnp.zeros((kt_w, tq), F32)
            for h in range(IDX_N_HEADS):
                s = jnp.dot(keys, qt_scr[h], preferred_element_type=F32)
                acc = acc + jnp.maximum(s, 0.0) * w_t[h:h + 1, :]
            k_pos = kt * kt_w + lax.broadcasted_iota(jnp.int32, (kt_w, tq), 0)
            sc_scr[kt] = jnp.where(k_pos <= q_pos, acc, -jnp.inf)

    def bounds(kt, carry):
        lo, hi = carry
        s = sc_scr[kt]
        hi = jnp.maximum(hi, jnp.max(s, axis=0, keepdims=True))
        lo = jnp.minimum(lo, jnp.min(jnp.where(s == -jnp.inf, jnp.inf, s), axis=0, keepdims=True))
        return lo, hi

    lo, hi = lax.fori_loop(0, i + 1, bounds,
                           (jnp.full((1, tq), jnp.inf, F32), jnp.full((1, tq), -jnp.inf, F32)))

    def halve(carry):
        it, lo, hi, n_lo = carry
        mid = 0.5 * lo + 0.5 * hi

        def count(kt, cnt):
            return cnt + jnp.sum(jnp.where(sc_scr[kt] >= mid, 1.0, 0.0), axis=0, keepdims=True)

        cnt = lax.fori_loop(0, i + 1, count, jnp.zeros((1, tq), F32))
        enough = cnt >= k_top
        return it + 1, jnp.where(enough, mid, lo), jnp.where(enough, hi, mid), jnp.where(enough, cnt, n_lo)

    def unresolved(carry):
        it, _, _, n_lo = carry
        return jnp.logical_and(it < n_iter, jnp.max(n_lo) > k_top)

    n_causal = (i * tq + lax.broadcasted_iota(jnp.int32, (1, tq), 1) + 1).astype(F32)
    _, lo, hi, _ = lax.while_loop(unresolved, halve, (jnp.int32(0), lo, hi, n_causal))

    for kt in range(n_t):
        @pl.when(kt <= i)
        def _(kt=kt):
            bias_ref[0, kt * kt_w:(kt + 1) * kt_w, :] = jnp.where(sc_scr[kt] >= lo, 0.0, -jnp.inf)

        @pl.when(kt > i)
        def _(kt=kt):
            bias_ref[0, kt * kt_w:(kt + 1) * kt_w, :] = jnp.full((kt_w, tq), -jnp.inf, F32)


def _dsa_indexer(keys, slab_t, cq_t, widx_t, *, batch, seq, k_top, w_scale):
    tq = IDX_TQ
    n_t = seq // tq
    return pl.pallas_call(
        functools.partial(_indexer_kernel, k_top=k_top, w_scale=w_scale, n_iter=BISECT_ITERS),
        grid=(batch, n_t),
        in_specs=[pl.BlockSpec((seq, LANES), lambda b, i: (b, 0)),
                  pl.BlockSpec((1, LANES, tq), lambda b, i: (b, 0, i)),
                  pl.BlockSpec((1, DSA_Q_RANK, tq), lambda b, i: (b, 0, i)),
                  pl.BlockSpec(widx_t.shape, lambda b, i: (0, 0, 0))],
        out_specs=pl.BlockSpec((1, seq, tq), lambda b, i: (b, 0, i)),
        out_shape=jax.ShapeDtypeStruct((batch, seq, seq), F32),
        scratch_shapes=[pltpu.VMEM((IDX_N_HEADS, LANES, tq), BF16),
                        pltpu.VMEM((n_t, tq, tq), F32)],
        compiler_params=_cparams(2),
        name="dsa_indexer",
    )(keys, slab_t, cq_t, widx_t)


def _attn_kernel(ql_ref, bias_ref, k_ref, vt_ref, wuvt_ref, gate_ref, *rest, n_keys, hc):
    o_ref = rest[-1]
    tq = bias_ref.shape[2]
    hb, hd, kv_rank = wuvt_ref.shape
    cols = hc * tq
    n_chunks = hb // hc
    keys = k_ref[:n_keys, :]
    bias_t = bias_ref[0]
    bias_rep = jnp.concatenate([bias_t] * hc, axis=1) if hc > 1 else bias_t

    def scores(c):
        q_t = ql_ref[0, 0, :, c * cols:(c + 1) * cols]
        return jnp.dot(keys, q_t, preferred_element_type=F32) + bias_rep

    def finish(c, s):
        m = jnp.max(s, axis=0, keepdims=True)
        p = jnp.exp2(s - m).astype(BF16)
        o_aug = jnp.dot(vt_ref[0], p, preferred_element_type=F32)
        o_t = (o_aug[:kv_rank] / o_aug[kv_rank:kv_rank + 1]).astype(BF16)
        for j in range(hc):
            h = c * hc + j
            oh_t = jnp.dot(wuvt_ref[h], o_t[:, j * tq:(j + 1) * tq], preferred_element_type=F32)
            gate = gate_ref[:, h * hd:(h + 1) * hd].astype(F32)
            o_ref[:, h * hd:(h + 1) * hd] = (oh_t.T * gate).astype(o_ref.dtype)

    s_next = scores(0)
    for c in range(n_chunks):
        s_cur = s_next
        if c + 1 < n_chunks:
            s_next = scores(c + 1)
        finish(c, s_cur)


def _dsa_attention(ql, bias, ckv_t, ckv, wuv_t, gate, *, batch, seq):
    tq = ATT_TQ
    n_q = seq // tq
    n_heads, hd, kv_rank = wuv_t.shape
    width = n_heads * hd
    og = None
    for i in range(n_q):
        n_keys = (i + 1) * tq
        hb = n_heads
        while hb > 1 and hb * tq * n_keys > ATT_STEP_ELEMS:
            hb //= 2
        hc = hb
        while hc > 1 and hc * tq * n_keys > ATT_CHUNK_ELEMS:
            hc //= 2
        in_specs = [pl.BlockSpec((1, 1, kv_rank, hb * tq), lambda b, g, i=i: (b, i, 0, g)),
                    pl.BlockSpec((1, n_keys, tq), lambda b, g, i=i: (b, 0, i)),
                    pl.BlockSpec((seq, kv_rank), lambda b, g: (b, 0)),
                    pl.BlockSpec((1, kv_rank + ONES_ROWS, n_keys), lambda b, g: (b, 0, 0)),
                    pl.BlockSpec((hb, hd, kv_rank), lambda b, g: (g, 0, 0)),
                    pl.BlockSpec((tq, hb * hd), lambda b, g, i=i: (b * n_q + i, g))]
        args = [ql, bias, ckv, ckv_t, wuv_t, gate]
        aliases = {}
        if og is not None:
            in_specs.append(pl.BlockSpec(memory_space=pl.ANY))
            args.append(og)
            aliases = {len(args) - 1: 0}
        og = pl.pallas_call(
            functools.partial(_attn_kernel, n_keys=n_keys, hc=hc),
            grid=(batch, n_heads // hb),
            in_specs=in_specs,
            out_specs=pl.BlockSpec((tq, hb * hd), lambda b, g, i=i: (b * n_q + i, g)),
            out_shape=jax.ShapeDtypeStruct((batch * seq, width), BF16),
            input_output_aliases=aliases,
            compiler_params=_cparams(2),
            name="dsa_attention_q%d" % i,
        )(*args)
    return og


def _ssd_layer(x, x_bf, w_in, conv_w, conv_b, dt_bias, a_log, d_skip, norm_g, w_out, ln_g, ln_b,
               *, batch, seq, alpha):
    d_inner = w_out.shape[0]
    n_heads = a_log.shape[0]
    n_groups = SSD_N_GROUPS
    heads_per_group = n_heads // n_groups
    conv_dim = conv_w.shape[1]
    gw = d_inner // n_groups
    assert gw % LANES == 0 and heads_per_group <= LANES and SSD_D_STATE == LANES
    assert LANES % SSD_HEAD_DIM == 0 and heads_per_group % (LANES // SSD_HEAD_DIM) == 0

    def per_group_lanes(v):
        lead = v.shape[:-1]
        v = v.reshape(lead + (n_groups, heads_per_group))
        v = jnp.pad(v, [(0, 0)] * len(lead) + [(0, 0), (0, LANES - heads_per_group)])
        return v.reshape(lead + (n_groups * LANES,))

    w_z = w_in[:, :d_inner].astype(BF16)
    w_xbc = w_in[:, d_inner:d_inner + conv_dim].astype(BF16)
    w_dt = jnp.pad(w_in[:, d_inner + conv_dim:], ((0, 0), (0, LANES - n_heads))).astype(BF16)
    dt_b = jnp.pad(dt_bias, (0, LANES - n_heads))[None, :]
    head_of_lane = jnp.arange(n_groups * LANES)
    head_of_lane = jnp.where(head_of_lane % LANES < heads_per_group,
                             (head_of_lane // LANES) * heads_per_group + head_of_lane % LANES, -1)
    spread = (jnp.arange(LANES)[:, None] == head_of_lane[None, :]).astype(BF16)
    alog = per_group_lanes(a_log).reshape(n_groups, LANES)
    dskip = jnp.repeat(d_skip, SSD_HEAD_DIM).reshape(n_groups, gw)
    lane_of = jnp.arange(gw) // SSD_HEAD_DIM
    expand_mat = (jnp.arange(LANES)[:, None] == lane_of[None, :]).astype(BF16)

    z = _proj(x_bf, w_z, batch=batch, seq=seq, tn=512)
    xbc = _proj(x_bf, w_xbc, batch=batch, seq=seq, tn=1024, conv=(conv_w, conv_b[None, :]))
    dt = _dt_proj(x_bf, w_dt, dt_b, spread, batch=batch, seq=seq)
    yn = _ssd_scan(xbc, z, dt, alog, dskip, norm_g[None, :], expand_mat, batch=batch, seq=seq, d_inner=d_inner)
    return _outproj_ln(yn, w_out.astype(BF16), x, ln_g[None, :], ln_b[None, :], alpha=alpha, tm=512)


def _dsa_layer(x, x_bf, w_in, q_norm_g, kv_norm_g, w_uq, w_uk, w_uv, w_idx_q, w_out, ln_g, ln_b,
               *, batch, seq, alpha):
    assert seq % IDX_TQ == 0 and IDX_TQ % ATT_TQ == 0 and IDX_HEAD_DIM + IDX_N_HEADS <= LANES
    small = DSA_Q_RANK + DSA_KV_RANK + IDX_HEAD_DIM + IDX_N_HEADS
    w_small = jnp.pad(w_in[:, :small], ((0, 0), (0, DSA_Q_RANK + DSA_KV_RANK + LANES - small))).astype(BF16)
    w_gate = w_in[:, small:].astype(BF16)
    wuq_t = w_uq.T.astype(BF16)
    wuv_t = w_uv.transpose(0, 2, 1).astype(BF16)
    widx_t = w_idx_q.reshape(DSA_Q_RANK, IDX_N_HEADS, IDX_HEAD_DIM).transpose(1, 2, 0)
    widx_t = jnp.pad(widx_t, ((0, 0), (0, LANES - IDX_HEAD_DIM), (0, 0))).astype(BF16)
    k_top = min(IDX_TOPK, seq // 4)

    cq_t, ckv, ckv_t, slab, slab_t = _dsa_latent(x_bf, w_small, q_norm_g[None, :], kv_norm_g[None, :],
                                                     batch=batch, seq=seq)
    gate = _proj(x_bf, w_gate, batch=batch, seq=seq, tn=512)
    ql = _dsa_query(cq_t, wuq_t, w_uk.astype(BF16), batch=batch, seq=seq,
                    scale=DSA_HEAD_DIM ** -0.5 * math.log2(math.e))
    bias = _dsa_indexer(slab, slab_t, cq_t, widx_t, batch=batch, seq=seq, k_top=k_top,
                        w_scale=IDX_N_HEADS ** -0.5 * IDX_HEAD_DIM ** -0.5)
    og = _dsa_attention(ql, bias, ckv_t, ckv, wuv_t, gate, batch=batch, seq=seq)
    return _outproj_ln(og, w_out.astype(BF16), x, ln_g[None, :], ln_b[None, :], alpha=alpha, tm=512)


def kernel(x, ssd_w_in, ssd_conv_w, ssd_conv_b, ssd_dt_bias, ssd_a_log, ssd_d_skip, ssd_norm_g, ssd_w_out,
           dsa_w_in, dsa_q_norm_g, dsa_kv_norm_g, dsa_w_uq, dsa_w_uk, dsa_w_uv, dsa_w_idx_q, dsa_w_out,
           ln_g, ln_b):
    batch, seq, d_model = x.shape
    depth = ln_g.shape[0]
    alpha = (2.0 * depth) ** 0.25
    xf = x.reshape(batch * seq, d_model)
    x_bf = xf.astype(BF16)
    for i in range(depth):
        j = i // 2
        if i % 2 == 0:
            xf, x_bf = _ssd_layer(xf, x_bf, ssd_w_in[j], ssd_conv_w[j], ssd_conv_b[j], ssd_dt_bias[j],
                                  ssd_a_log[j], ssd_d_skip[j], ssd_norm_g[j], ssd_w_out[j], ln_g[i], ln_b[i],
                                  batch=batch, seq=seq, alpha=alpha)
        else:
            xf, x_bf = _dsa_layer(xf, x_bf, dsa_w_in[j], dsa_q_norm_g[j], dsa_kv_norm_g[j], dsa_w_uq[j],
                                  dsa_w_uk[j], dsa_w_uv[j], dsa_w_idx_q[j], dsa_w_out[j], ln_g[i], ln_b[i],
                                  batch=batch, seq=seq, alpha=alpha)
    return xf.reshape(batch, seq, d_model)
```

```python
import functools
import math

import jax
import jax.numpy as jnp
from jax import lax
from jax.experimental import pallas as pl
from jax.experimental.pallas import tpu as pltpu

F32 = jnp.float32
BF16 = jnp.bfloat16

V7X_VMEM_BYTES = 64 * 1024 * 1024
LANES = 128
SUBLANES = 8
VMEM_LIMIT_BYTES = V7X_VMEM_BYTES - 8 * 1024 * 1024

LN_EPS = 1e-5
RMS_EPS = 1e-6

SSD_HEAD_DIM = 64
SSD_N_GROUPS = 8
SSD_D_STATE = 128
SSD_CHUNK = 256
DSA_N_HEADS = 32
DSA_HEAD_DIM = 128
DSA_Q_RANK = 512
DSA_KV_RANK = 256
IDX_N_HEADS = 16
IDX_HEAD_DIM = 64
IDX_TOPK = 256

ATT_TQ = 256
ATT_STEP_ELEMS = 8 * 1024 * 1024
ATT_CHUNK_ELEMS = 1024 * 1024
IDX_TQ = 512
BISECT_ITERS = 40
ONES_ROWS = 16
CONV_ROW_CHUNK = 256


def _cparams(n_axes, flags=None):
    return pltpu.CompilerParams(dimension_semantics=("arbitrary",) * n_axes,
                                vmem_limit_bytes=VMEM_LIMIT_BYTES, flags=flags)


LOG2_E = math.log2(math.e)


def _sigmoid(v):
    return 1.0 / (1.0 + jnp.exp2(v * (-LOG2_E)))


def _proj_silu_kernel(x_ref, w_ref, o_ref, *, chunk):
    w = w_ref[...]
    for r0 in range(0, o_ref.shape[0], chunk):
        y = jnp.dot(x_ref[r0:r0 + chunk, :], w, preferred_element_type=F32)
        o_ref[r0:r0 + chunk, :] = (y * _sigmoid(y)).astype(o_ref.dtype)


def _proj_conv_silu_kernel(x_ref, w_ref, conv_w_ref, conv_b_ref, o_ref, y_scr, *, chunk):
    seq, tn = o_ref.shape
    k_taps = conv_w_ref.shape[0]
    head = y_scr.shape[0] - seq
    y_scr[:head, :] = jnp.zeros((head, tn), F32)
    w = w_ref[...]

    def matmul(r0):
        y_scr[head + r0:head + r0 + chunk, :] = jnp.dot(x_ref[r0:r0 + chunk, :], w, preferred_element_type=F32)

    def finish(r0):
        acc = conv_b_ref[...] + conv_w_ref[k_taps - 1:k_taps, :] * y_scr[head + r0:head + r0 + chunk, :]
        for back in range(1, k_taps):
            tap = conv_w_ref[k_taps - 1 - back:k_taps - back, :]
            acc = acc + tap * y_scr[head + r0 - back:head + r0 - back + chunk, :]
        o_ref[r0:r0 + chunk, :] = (acc * _sigmoid(acc)).astype(o_ref.dtype)

    matmul(0)
    for r0 in range(0, seq, chunk):
        if r0 + chunk < seq:
            matmul(r0 + chunk)
        finish(r0)


def _proj(x, w, *, batch, seq, tn, conv=None, out_dtype=BF16):
    k_dim, n_dim = w.shape
    assert n_dim % tn == 0 and tn % LANES == 0
    in_specs = [pl.BlockSpec((seq, k_dim), lambda b, j: (b, 0)),
                pl.BlockSpec((k_dim, tn), lambda b, j: (0, j))]
    if conv is None:
        body = functools.partial(_proj_silu_kernel, chunk=CONV_ROW_CHUNK)
        args, scratch, name = (x, w), [], "proj_silu"
    else:
        body = functools.partial(_proj_conv_silu_kernel, chunk=CONV_ROW_CHUNK)
        args = (x, w) + tuple(conv)
        in_specs += [pl.BlockSpec((c.shape[0], tn), lambda b, j: (0, j)) for c in conv]
        assert conv[0].shape[0] - 1 <= SUBLANES
        scratch, name = [pltpu.VMEM((SUBLANES + seq, tn), F32)], "proj_conv_silu"
    return pl.pallas_call(
        body,
        grid=(batch, n_dim // tn),
        in_specs=in_specs,
        out_specs=pl.BlockSpec((seq, tn), lambda b, j: (b, j)),
        out_shape=jax.ShapeDtypeStruct((batch * seq, n_dim), out_dtype),
        scratch_shapes=scratch,
        compiler_params=_cparams(2),
        name=name,
    )(*args)


def _dt_kernel(x_ref, w_ref, b_ref, spread_ref, o_ref):
    y = jnp.dot(x_ref[...], w_ref[...], preferred_element_type=F32) + b_ref[...]
    y = jnp.maximum(y, 0.0) + jnp.log1p(jnp.exp(-jnp.abs(y)))
    spread = spread_ref[...]
    hi = y.astype(BF16)
    rest = y - hi.astype(F32)
    mid = rest.astype(BF16)
    lo = (rest - mid.astype(F32)).astype(BF16)
    o_ref[...] = (jnp.dot(hi, spread, preferred_element_type=F32)
                  + jnp.dot(mid, spread, preferred_element_type=F32)
                  + jnp.dot(lo, spread, preferred_element_type=F32))


def _dt_proj(x, w, bias, spread, *, batch, seq):
    k_dim = w.shape[0]
    width = spread.shape[1]
    return pl.pallas_call(
        _dt_kernel,
        grid=(batch,),
        in_specs=[pl.BlockSpec((seq, k_dim), lambda b: (b, 0)),
                  pl.BlockSpec((k_dim, LANES), lambda b: (0, 0)),
                  pl.BlockSpec((1, LANES), lambda b: (0, 0)),
                  pl.BlockSpec((LANES, width), lambda b: (0, 0))],
        out_specs=pl.BlockSpec((seq, width), lambda b: (b, 0)),
        out_shape=jax.ShapeDtypeStruct((batch * seq, width), F32),
        compiler_params=_cparams(1),
        name="dt_proj",
    )(x, w, bias, spread)


def _cumsum_rows(v):
    n = v.shape[0]
    row = lax.broadcasted_iota(jnp.int32, v.shape, 0)
    shift = 1
    while shift < n:
        v = v + jnp.where(row >= shift, pltpu.roll(v, shift, 0), 0.0)
        shift *= 2
    return v


def _ssd_kernel(xs_ref, b_ref, c_ref, z_ref, dt_ref, alog_ref, dskip_ref, ng_ref, e_ref, o_ref,
                state_ref, *, heads_per_group, head_dim):
    g = pl.program_id(1)
    chunk = pl.program_id(2)
    q = xs_ref.shape[0]

    @pl.when(chunk == 0)
    def _():
        state_ref[...] = jnp.zeros_like(state_ref)

    xs = xs_ref[...].astype(F32)
    bm = b_ref[...]
    cm = c_ref[...]
    dt = dt_ref[...]
    a = -jnp.exp(alog_ref[pl.ds(g, 1), :]) * math.log2(math.e)
    acs = _cumsum_rows(dt * a)
    acs_t = acs.T
    last = acs[q - 1:q, :]
    expand_mat = e_ref[...]

    def expand(v):
        return jnp.dot(v.astype(BF16), expand_mat, preferred_element_type=F32)

    def expand_hi_lo(v):
        hi = v.astype(BF16)
        lo = (v - hi.astype(F32)).astype(BF16)
        return (jnp.dot(hi, expand_mat, preferred_element_type=F32)
                + jnp.dot(lo, expand_mat, preferred_element_type=F32))

    decay_in_x = expand_hi_lo(jnp.exp2(acs))
    decay_out_dt_x = expand(jnp.exp2(last - acs) * dt)

    bc = lax.dot_general(bm, cm, (((1,), (1,)), ((), ())), preferred_element_type=F32)
    blk = LANES
    tri = (lax.broadcasted_iota(jnp.int32, (blk, blk), 1) >= lax.broadcasted_iota(jnp.int32, (blk, blk), 0))
    xs_t = xs.T
    dt_t = dt.T

    def masked_bc(r):
        rows = []
        for s0 in range(0, q, blk):
            parts = []
            if s0 > 0:
                parts.append(jnp.zeros((blk, s0), BF16))
            diff = acs_t[r:r + 1, s0:s0 + blk] - acs[s0:s0 + blk, r:r + 1]
            decay = jnp.exp2(jnp.where(tri, diff, -jnp.inf))
            parts.append((bc[s0:s0 + blk, s0:s0 + blk] * decay).astype(BF16))
            if s0 + blk < q:
                diff = acs_t[r:r + 1, s0 + blk:] - acs[s0:s0 + blk, r:r + 1]
                parts.append((bc[s0:s0 + blk, s0 + blk:] * jnp.exp2(diff)).astype(BF16))
            rows.append(jnp.concatenate(parts, axis=1) if len(parts) > 1 else parts[0])
        return jnp.concatenate(rows, axis=0) if len(rows) > 1 else rows[0]

    y_t = []
    for r in range(heads_per_group):
        x_head_t = (xs_t[r * head_dim:(r + 1) * head_dim, :] * dt_t[r:r + 1, :]).astype(BF16)
        y_t.append(jnp.dot(x_head_t, masked_bc(r), preferred_element_type=F32))
    y_diag = jnp.concatenate(y_t, axis=0).T

    s_prev = state_ref[...]
    y_off = jnp.dot(cm, s_prev.astype(BF16), preferred_element_type=F32) * decay_in_x
    y = y_diag + y_off + dskip_ref[pl.ds(g, 1), :] * xs
    b_t = bm.astype(F32).T.astype(BF16)
    s_new = jnp.dot(b_t, (xs * decay_out_dt_x).astype(BF16), preferred_element_type=F32)
    state_ref[...] = s_prev * decay_in_x[q - 1:q, :] + s_new

    yg = y * z_ref[...].astype(F32)
    ms = jnp.mean(yg * yg, axis=-1, keepdims=True)
    o_ref[...] = (yg * lax.rsqrt(ms + RMS_EPS) * ng_ref[...]).astype(o_ref.dtype)


def _ssd_scan(xbc, z, dt, alog, dskip, norm_g, expand_mat, *, batch, seq, d_inner):
    n_groups, d_state, chunk = SSD_N_GROUPS, SSD_D_STATE, math.gcd(SSD_CHUNK, seq)
    gw = d_inner // n_groups
    heads_per_group = gw // SSD_HEAD_DIM
    n_chunks = seq // chunk
    b_col0 = d_inner // d_state
    c_col0 = b_col0 + n_groups
    row = lambda b, g, c: b * n_chunks + c
    return pl.pallas_call(
        functools.partial(_ssd_kernel, heads_per_group=heads_per_group, head_dim=SSD_HEAD_DIM),
        grid=(batch, n_groups, n_chunks),
        in_specs=[
            pl.BlockSpec((chunk, gw), lambda b, g, c: (row(b, g, c), g)),
            pl.BlockSpec((chunk, d_state), lambda b, g, c: (row(b, g, c), b_col0 + g)),
            pl.BlockSpec((chunk, d_state), lambda b, g, c: (row(b, g, c), c_col0 + g)),
            pl.BlockSpec((chunk, gw), lambda b, g, c: (row(b, g, c), g)),
            pl.BlockSpec((chunk, LANES), lambda b, g, c: (row(b, g, c), g)),
            pl.BlockSpec((n_groups, LANES), lambda b, g, c: (0, 0)),
            pl.BlockSpec((n_groups, gw), lambda b, g, c: (0, 0)),
            pl.BlockSpec((1, gw), lambda b, g, c: (0, g)),
            pl.BlockSpec((LANES, gw), lambda b, g, c: (0, 0)),
        ],
        out_specs=pl.BlockSpec((chunk, gw), lambda b, g, c: (row(b, g, c), g)),
        out_shape=jax.ShapeDtypeStruct((batch * seq, d_inner), BF16),
        scratch_shapes=[pltpu.VMEM((d_state, gw), F32)],
        compiler_params=_cparams(3),
        name="ssd_scan",
    )(xbc, xbc, xbc, z, dt, alog, dskip, norm_g, expand_mat)


def _outproj_ln_kernel(y_ref, w_ref, x_ref, g_ref, b_ref, o_ref, obf_ref, *, alpha):
    h = jnp.dot(y_ref[...], w_ref[...], preferred_element_type=F32)
    v = alpha * x_ref[...] + h
    mu = jnp.mean(v, axis=-1, keepdims=True)
    d = v - mu
    var = jnp.mean(d * d, axis=-1, keepdims=True)
    out = d * lax.rsqrt(var + LN_EPS) * g_ref[...] + b_ref[...]
    o_ref[...] = out
    obf_ref[...] = out.astype(BF16)


def _outproj_ln(y, w, x, ln_g, ln_b, *, alpha, tm):
    m, k_dim = y.shape
    d = w.shape[1]
    return pl.pallas_call(
        functools.partial(_outproj_ln_kernel, alpha=alpha),
        grid=(m // tm,),
        in_specs=[pl.BlockSpec((tm, k_dim), lambda i: (i, 0)),
                  pl.BlockSpec((k_dim, d), lambda i: (0, 0), pipeline_mode=pl.Buffered(1)),
                  pl.BlockSpec((tm, d), lambda i: (i, 0)),
                  pl.BlockSpec((1, d), lambda i: (0, 0)),
                  pl.BlockSpec((1, d), lambda i: (0, 0))],
        out_specs=[pl.BlockSpec((tm, d), lambda i: (i, 0)),
                   pl.BlockSpec((tm, d), lambda i: (i, 0))],
        out_shape=[jax.ShapeDtypeStruct((m, d), F32), jax.ShapeDtypeStruct((m, d), BF16)],
        compiler_params=_cparams(1),
        name="outproj_ln",
    )(y, w, x, ln_g, ln_b)


def _dsa_latent_kernel(x_ref, w_ref, qg_ref, kvg_ref, cqt_ref, ckv_ref, ckvt_ref, slab_ref, slabt_ref):
    p = jnp.dot(x_ref[...], w_ref[...], preferred_element_type=F32)
    q_rank = cqt_ref.shape[1]
    kv_rank = ckv_ref.shape[1]
    cq = p[:, :q_rank]
    ckv = p[:, q_rank:q_rank + kv_rank]
    slab = p[:, q_rank + kv_rank:]
    cq = cq * lax.rsqrt(jnp.mean(cq * cq, axis=-1, keepdims=True) + RMS_EPS) * qg_ref[...]
    ckv = ckv * lax.rsqrt(jnp.mean(ckv * ckv, axis=-1, keepdims=True) + RMS_EPS) * kvg_ref[...]
    cqt_ref[0] = cq.T.astype(BF16)
    ckv_ref[...] = ckv.astype(BF16)
    ckvt_ref[0, :kv_rank, :] = ckv.T.astype(BF16)
    ckvt_ref[0, kv_rank:, :] = jnp.ones((ckvt_ref.shape[1] - kv_rank, ckvt_ref.shape[2]), BF16)
    slab_ref[...] = slab.astype(BF16)
    slabt_ref[0] = slab.T


def _dsa_latent(x, w, q_gain, kv_gain, *, batch, seq):
    tm = IDX_TQ
    n_t = seq // tm
    k_dim, n_dim = w.shape
    return pl.pallas_call(
        _dsa_latent_kernel,
        grid=(batch, n_t),
        in_specs=[pl.BlockSpec((tm, k_dim), lambda b, t: (b * n_t + t, 0)),
                  pl.BlockSpec((k_dim, n_dim), lambda b, t: (0, 0)),
                  pl.BlockSpec((1, DSA_Q_RANK), lambda b, t: (0, 0)),
                  pl.BlockSpec((1, DSA_KV_RANK), lambda b, t: (0, 0))],
        out_specs=[pl.BlockSpec((1, DSA_Q_RANK, tm), lambda b, t: (b, 0, t)),
                   pl.BlockSpec((tm, DSA_KV_RANK), lambda b, t: (b * n_t + t, 0)),
                   pl.BlockSpec((1, DSA_KV_RANK + ONES_ROWS, tm), lambda b, t: (b, 0, t)),
                   pl.BlockSpec((tm, LANES), lambda b, t: (b * n_t + t, 0)),
                   pl.BlockSpec((1, LANES, tm), lambda b, t: (b, 0, t))],
        out_shape=[jax.ShapeDtypeStruct((batch, DSA_Q_RANK, seq), BF16),
                   jax.ShapeDtypeStruct((batch * seq, DSA_KV_RANK), BF16),
                   jax.ShapeDtypeStruct((batch, DSA_KV_RANK + ONES_ROWS, seq), BF16),
                   jax.ShapeDtypeStruct((batch * seq, LANES), BF16),
                   jax.ShapeDtypeStruct((batch, LANES, seq), F32)],
        compiler_params=_cparams(2),
        name="dsa_latent",
    )(x, w, q_gain, kv_gain)


def _dsa_query_kernel(cqt_ref, wuqt_ref, wuk_ref, ql_ref, *, scale):
    hb, _, hd = wuk_ref.shape
    tq = ql_ref.shape[3] // hb
    q_t = jnp.dot(wuqt_ref[...], cqt_ref[0], preferred_element_type=F32).astype(BF16)
    for j in range(hb):
        ql_t = jnp.dot(wuk_ref[j], q_t[j * hd:(j + 1) * hd, :], preferred_element_type=F32)
        ql_t = (ql_t * scale).astype(BF16)
        for blk in range(ql_ref.shape[1]):
            ql_ref[0, blk, :, j * tq:(j + 1) * tq] = ql_t[:, blk * tq:(blk + 1) * tq]


def _dsa_query(cq_t, wuq_t, wuk, *, batch, seq, scale):
    tm = min(seq, 1024)
    hb = 8
    tq = ATT_TQ
    n_t = seq // tm
    n_heads, kv_rank, hd = wuk.shape
    return pl.pallas_call(
        functools.partial(_dsa_query_kernel, scale=scale),
        grid=(batch, n_t, n_heads // hb),
        in_specs=[pl.BlockSpec((1, DSA_Q_RANK, tm), lambda b, t, h: (b, 0, t)),
                  pl.BlockSpec((hb * hd, DSA_Q_RANK), lambda b, t, h: (h, 0)),
                  pl.BlockSpec((hb, kv_rank, hd), lambda b, t, h: (h, 0, 0))],
        out_specs=pl.BlockSpec((1, tm // tq, kv_rank, hb * tq), lambda b, t, h: (b, t, 0, h)),
        out_shape=jax.ShapeDtypeStruct((batch, seq // tq, kv_rank, n_heads * tq), BF16),
        compiler_params=_cparams(3),
        name="dsa_query",
    )(cq_t, wuq_t, wuk)


def _indexer_kernel(keys_ref, slabt_ref, cqt_ref, widx_ref, bias_ref, qt_scr, sc_scr, *, k_top, w_scale, n_iter):
    i = pl.program_id(1)
    n_t, kt_w, tq = sc_scr.shape
    cq_t = cqt_ref[0]
    for h in range(IDX_N_HEADS):
        qt_scr[h] = jnp.dot(widx_ref[h], cq_t, preferred_element_type=F32).astype(BF16)
    w_t = slabt_ref[0][IDX_HEAD_DIM:IDX_HEAD_DIM + IDX_N_HEADS, :] * w_scale
    q_pos = i * tq + lax.broadcasted_iota(jnp.int32, (kt_w, tq), 1)

    for kt in range(n_t):
        @pl.when(kt <= i)
        def _(kt=kt):
            keys = keys_ref[kt * kt_w:(kt + 1) * kt_w, :]
            acc = jnp.zeros((kt_w, tq), F32)
            for h in range(IDX_N_HEADS):
                s = jnp.dot(keys, qt_scr[h], preferred_element_type=F32)
                acc = acc + jnp.maximum(s, 0.0) * w_t[h:h + 1, :]
            k_pos = kt * kt_w + lax.broadcasted_iota(jnp.int32, (kt_w, tq), 0)
            sc_scr[kt] = jnp.where(k_pos <= q_pos, acc, -jnp.inf)

    def bounds(kt, carry):
        lo, hi = carry
        s = sc_scr[kt]
        hi = jnp.maximum(hi, jnp.max(s, axis=0, keepdims=True))
        lo = jnp.minimum(lo, jnp.min(jnp.where(s == -jnp.inf, jnp.inf, s), axis=0, keepdims=True))
        return lo, hi

    lo, hi = lax.fori_loop(0, i + 1, bounds,
                           (jnp.full((1, tq), jnp.inf, F32), jnp.full((1, tq), -jnp.inf, F32)))

    def halve(carry):
        it, lo, hi, n_lo = carry
        mid = 0.5 * lo + 0.5 * hi

        def count(kt, cnt):
            return cnt + jnp.sum(jnp.where(sc_scr[kt] >= mid, 1.0, 0.0), axis=0, keepdims=True)

        cnt = lax.fori_loop(0, i + 1, count, jnp.zeros((1, tq), F32))
        enough = cnt >= k_top
        return it + 1, jnp.where(enough, mid, lo), jnp.where(enough, hi, mid), jnp.where(enough, cnt, n_lo)

    def unresolved(carry):
        it, _, _, n_lo = carry
        return jnp.logical_and(it < n_iter, jnp.max(n_lo) > k_top)

    n_causal = (i * tq + lax.broadcasted_iota(jnp.int32, (1, tq), 1) + 1).astype(F32)
    _, lo, hi, _ = lax.while_loop(unresolved, halve, (jnp.int32(0), lo, hi, n_causal))

    for kt in range(n_t):
        @pl.when(kt <= i)
        def _(kt=kt):
            bias_ref[0, kt * kt_w:(kt + 1) * kt_w, :] = jnp.where(sc_scr[kt] >= lo, 0.0, -jnp.inf)

        @pl.when(kt > i)
        def _(kt=kt):
            bias_ref[0, kt * kt_w:(kt + 1) * kt_w, :] = jnp.full((kt_w, tq), -jnp.inf, F32)


def _dsa_indexer(keys, slab_t, cq_t, widx_t, *, batch, seq, k_top, w_scale):
    tq = IDX_TQ
    n_t = seq // tq
    return pl.pallas_call(
        functools.partial(_indexer_kernel, k_top=k_top, w_scale=w_scale, n_iter=BISECT_ITERS),
        grid=(batch, n_t),
        in_specs=[pl.BlockSpec((seq, LANES), lambda b, i: (b, 0)),
                  pl.BlockSpec((1, LANES, tq), lambda b, i: (b, 0, i)),
                  pl.BlockSpec((1, DSA_Q_RANK, tq), lambda b, i: (b, 0, i)),
                  pl.BlockSpec(widx_t.shape, lambda b, i: (0, 0, 0))],
        out_specs=pl.BlockSpec((1, seq, tq), lambda b, i: (b, 0, i)),
        out_shape=jax.ShapeDtypeStruct((batch, seq, seq), F32),
        scratch_shapes=[pltpu.VMEM((IDX_N_HEADS, LANES, tq), BF16),
                        pltpu.VMEM((n_t, tq, tq), F32)],
        compiler_params=_cparams(2),
        name="dsa_indexer",
    )(keys, slab_t, cq_t, widx_t)


def _attn_kernel(ql_ref, bias_ref, k_ref, vt_ref, wuvt_ref, gate_ref, *rest, n_keys, hc):
    o_ref = rest[-1]
    tq = bias_ref.shape[2]
    hb, hd, kv_rank = wuvt_ref.shape
    cols = hc * tq
    n_chunks = hb // hc
    keys = k_ref[:n_keys, :]
    bias_t = bias_ref[0]
    bias_rep = jnp.concatenate([bias_t] * hc, axis=1) if hc > 1 else bias_t

    def scores(c):
        q_t = ql_ref[0, 0, :, c * cols:(c + 1) * cols]
        return jnp.dot(keys, q_t, preferred_element_type=F32) + bias_rep

    def finish(c, s):
        m = jnp.max(s, axis=0, keepdims=True)
        p = jnp.exp2(s - m).astype(BF16)
        o_aug = jnp.dot(vt_ref[0], p, preferred_element_type=F32)
        o_t = (o_aug[:kv_rank] / o_aug[kv_rank:kv_rank + 1]).astype(BF16)
        for j in range(hc):
            h = c * hc + j
            oh_t = jnp.dot(wuvt_ref[h], o_t[:, j * tq:(j + 1) * tq], preferred_element_type=F32)
            gate = gate_ref[:, h * hd:(h + 1) * hd].astype(F32)
            o_ref[:, h * hd:(h + 1) * hd] = (oh_t.T * gate).astype(o_ref.dtype)

    s_next = scores(0)
    for c in range(n_chunks):
        s_cur = s_next
        if c + 1 < n_chunks:
            s_next = scores(c + 1)
        finish(c, s_cur)


def _dsa_attention(ql, bias, ckv_t, ckv, wuv_t, gate, *, batch, seq):
    tq = ATT_TQ
    n_q = seq // tq
    n_heads, hd, kv_rank = wuv_t.shape
    width = n_heads * hd
    og = None
    for i in range(n_q):
        n_keys = (i + 1) * tq
        hb = n_heads
        while hb > 1 and hb * tq * n_keys > ATT_STEP_ELEMS:
            hb //= 2
        hc = hb
        while hc > 1 and hc * tq * n_keys > ATT_CHUNK_ELEMS:
            hc //= 2
        in_specs = [pl.BlockSpec((1, 1, kv_rank, hb * tq), lambda b, g, i=i: (b, i, 0, g)),
                    pl.BlockSpec((1, n_keys, tq), lambda b, g, i=i: (b, 0, i)),
                    pl.BlockSpec((seq, kv_rank), lambda b, g: (b, 0)),
                    pl.BlockSpec((1, kv_rank + ONES_ROWS, n_keys), lambda b, g: (b, 0, 0)),
                    pl.BlockSpec((hb, hd, kv_rank), lambda b, g: (g, 0, 0)),
                    pl.BlockSpec((tq, hb * hd), lambda b, g, i=i: (b * n_q + i, g))]
        args = [ql, bias, ckv, ckv_t, wuv_t, gate]
        aliases = {}
        if og is not None:
            in_specs.append(pl.BlockSpec(memory_space=pl.ANY))
            args.append(og)
            aliases = {len(args) - 1: 0}
        og = pl.pallas_call(
            functools.partial(_attn_kernel, n_keys=n_keys, hc=hc),
            grid=(batch, n_heads // hb),
            in_specs=in_specs,
            out_specs=pl.BlockSpec((tq, hb * hd), lambda b, g, i=i: (b * n_q + i, g)),
            out_shape=jax.ShapeDtypeStruct((batch * seq, width), BF16),
            input_output_aliases=aliases,
            compiler_params=_cparams(2),
            name="dsa_attention_q%d" % i,
        )(*args)
    return og


def _ssd_layer(x, x_bf, w_in, conv_w, conv_b, dt_bias, a_log, d_skip, norm_g, w_out, ln_g, ln_b,
               *, batch, seq, alpha):
    d_inner = w_out.shape[0]
    n_heads = a_log.shape[0]
    n_groups = SSD_N_GROUPS
    heads_per_group = n_heads // n_groups
    conv_dim = conv_w.shape[1]
    gw = d_inner // n_groups
    assert gw % LANES == 0 and heads_per_group <= LANES and SSD_D_STATE == LANES
    assert LANES % SSD_HEAD_DIM == 0 and heads_per_group % (LANES // SSD_HEAD_DIM) == 0

    def per_group_lanes(v):
        lead = v.shape[:-1]
        v = v.reshape(lead + (n_groups, heads_per_group))
        v = jnp.pad(v, [(0, 0)] * len(lead) + [(0, 0), (0, LANES - heads_per_group)])
        return v.reshape(lead + (n_groups * LANES,))

    w_z = w_in[:, :d_inner].astype(BF16)
    w_xbc = w_in[:, d_inner:d_inner + conv_dim].astype(BF16)
    w_dt = jnp.pad(w_in[:, d_inner + conv_dim:], ((0, 0), (0, LANES - n_heads))).astype(BF16)
    dt_b = jnp.pad(dt_bias, (0, LANES - n_heads))[None, :]
    head_of_lane = jnp.arange(n_groups * LANES)
    head_of_lane = jnp.where(head_of_lane % LANES < heads_per_group,
                             (head_of_lane // LANES) * heads_per_group + head_of_lane % LANES, -1)
    spread = (jnp.arange(LANES)[:, None] == head_of_lane[None, :]).astype(BF16)
    alog = per_group_lanes(a_log).reshape(n_groups, LANES)
    dskip = jnp.repeat(d_skip, SSD_HEAD_DIM).reshape(n_groups, gw)
    lane_of = jnp.arange(gw) // SSD_HEAD_DIM
    expand_mat = (jnp.arange(LANES)[:, None] == lane_of[None, :]).astype(BF16)

    z = _proj(x_bf, w_z, batch=batch, seq=seq, tn=512)
    xbc = _proj(x_bf, w_xbc, batch=batch, seq=seq, tn=1024, conv=(conv_w, conv_b[None, :]))
    dt = _dt_proj(x_bf, w_dt, dt_b, spread, batch=batch, seq=seq)
    yn = _ssd_scan(xbc, z, dt, alog, dskip, norm_g[None, :], expand_mat, batch=batch, seq=seq, d_inner=d_inner)
    return _outproj_ln(yn, w_out.astype(BF16), x, ln_g[None, :], ln_b[None, :], alpha=alpha, tm=512)


def _dsa_layer(x, x_bf, w_in, q_norm_g, kv_norm_g, w_uq, w_uk, w_uv, w_idx_q, w_out, ln_g, ln_b,
               *, batch, seq, alpha):
    assert seq % IDX_TQ == 0 and IDX_TQ % ATT_TQ == 0 and IDX_HEAD_DIM + IDX_N_HEADS <= LANES
    small = DSA_Q_RANK + DSA_KV_RANK + IDX_HEAD_DIM + IDX_N_HEADS
    w_small = jnp.pad(w_in[:, :small], ((0, 0), (0, DSA_Q_RANK + DSA_KV_RANK + LANES - small))).astype(BF16)
    w_gate = w_in[:, small:].astype(BF16)
    wuq_t = w_uq.T.astype(BF16)
    wuv_t = w_uv.transpose(0, 2, 1).astype(BF16)
    widx_t = w_idx_q.reshape(DSA_Q_RANK, IDX_N_HEADS, IDX_HEAD_DIM).transpose(1, 2, 0)
    widx_t = jnp.pad(widx_t, ((0, 0), (0, LANES - IDX_HEAD_DIM), (0, 0))).astype(BF16)
    k_top = min(IDX_TOPK, seq // 4)

    cq_t, ckv, ckv_t, slab, slab_t = _dsa_latent(x_bf, w_small, q_norm_g[None, :], kv_norm_g[None, :],
                                                     batch=batch, seq=seq)
    gate = _proj(x_bf, w_gate, batch=batch, seq=seq, tn=512)
    ql = _dsa_query(cq_t, wuq_t, w_uk.astype(BF16), batch=batch, seq=seq,
                    scale=DSA_HEAD_DIM ** -0.5 * math.log2(math.e))
    bias = _dsa_indexer(slab, slab_t, cq_t, widx_t, batch=batch, seq=seq, k_top=k_top,
                        w_scale=IDX_N_HEADS ** -0.5 * IDX_HEAD_DIM ** -0.5)
    og = _dsa_attention(ql, bias, ckv_t, ckv, wuv_t, gate, batch=batch, seq=seq)
    return _outproj_ln(og, w_out.astype(BF16), x, ln_g[None, :], ln_b[None, :], alpha=alpha, tm=512)


def kernel(x, ssd_w_in, ssd_conv_w, ssd_conv_b, ssd_dt_bias, ssd_a_log, ssd_d_skip, ssd_norm_g, ssd_w_out,
           dsa_w_in, dsa_q_norm_g, dsa_kv_norm_g, dsa_w_uq, dsa_w_uk, dsa_w_uv, dsa_w_idx_q, dsa_w_out,
           ln_g, ln_b):
    batch, seq, d_model = x.shape
    depth = ln_g.shape[0]
    alpha = (2.0 * depth) ** 0.25
    xf = x.reshape(batch * seq, d_model)
    x_bf = xf.astype(BF16)
    for i in range(depth):
        j = i // 2
        if i % 2 == 0:
            xf, x_bf = _ssd_layer(xf, x_bf, ssd_w_in[j], ssd_conv_w[j], ssd_conv_b[j], ssd_dt_bias[j],
                                  ssd_a_log[j], ssd_d_skip[j], ssd_norm_g[j], ssd_w_out[j], ln_g[i], ln_b[i],
                                  batch=batch, seq=seq, alpha=alpha)
        else:
            xf, x_bf = _dsa_layer(xf, x_bf, dsa_w_in[j], dsa_q_norm_g[j], dsa_kv_norm_g[j], dsa_w_uq[j],
                                  dsa_w_uk[j], dsa_w_uv[j], dsa_w_idx_q[j], dsa_w_out[j], ln_g[i], ln_b[i],
                                  batch=batch, seq=seq, alpha=alpha)
    return xf.reshape(batch, seq, d_model)
```

```python
import functools
import math

import jax
import jax.numpy as jnp
from jax import lax
from jax.experimental import pallas as pl
from jax.experimental.pallas import tpu as pltpu

F32 = jnp.float32
BF16 = jnp.bfloat16

V7X_VMEM_BYTES = 64 * 1024 * 1024
LANES = 128
SUBLANES = 8
VMEM_LIMIT_BYTES = V7X_VMEM_BYTES - 8 * 1024 * 1024

LN_EPS = 1e-5
RMS_EPS = 1e-6

SSD_HEAD_DIM = 64
SSD_N_GROUPS = 8
SSD_D_STATE = 128
SSD_CHUNK = 256
DSA_N_HEADS = 32
DSA_HEAD_DIM = 128
DSA_Q_RANK = 512
DSA_KV_RANK = 256
IDX_N_HEADS = 16
IDX_HEAD_DIM = 64
IDX_TOPK = 256

ATT_TQ = 256
ATT_STEP_ELEMS = 8 * 1024 * 1024
ATT_CHUNK_ELEMS = 1024 * 1024
IDX_TQ = 512
BISECT_ITERS = 40
ONES_ROWS = 16
CONV_ROW_CHUNK = 256
LN_ROW_CHUNK = 128


def _cparams(n_axes, flags=None):
    return pltpu.CompilerParams(dimension_semantics=("arbitrary",) * n_axes,
                                vmem_limit_bytes=VMEM_LIMIT_BYTES, flags=flags)


LOG2_E = math.log2(math.e)


def _sigmoid(v):
    return 1.0 / (1.0 + jnp.exp2(v * (-LOG2_E)))


def _proj_silu_kernel(x_ref, w_ref, o_ref, *, chunk):
    w = w_ref[...]
    for r0 in range(0, o_ref.shape[0], chunk):
        y = jnp.dot(x_ref[r0:r0 + chunk, :], w, preferred_element_type=F32)
        o_ref[r0:r0 + chunk, :] = (y * _sigmoid(y)).astype(o_ref.dtype)


def _proj_conv_silu_kernel(x_ref, w_ref, conv_w_ref, conv_b_ref, o_ref, y_scr, *, chunk):
    seq, tn = o_ref.shape
    k_taps = conv_w_ref.shape[0]
    head = y_scr.shape[0] - seq
    y_scr[:head, :] = jnp.zeros((head, tn), F32)
    w = w_ref[...]

    def matmul(r0):
        y_scr[head + r0:head + r0 + chunk, :] = jnp.dot(x_ref[r0:r0 + chunk, :], w, preferred_element_type=F32)

    def finish(r0):
        acc = conv_b_ref[...] + conv_w_ref[k_taps - 1:k_taps, :] * y_scr[head + r0:head + r0 + chunk, :]
        for back in range(1, k_taps):
            tap = conv_w_ref[k_taps - 1 - back:k_taps - back, :]
            acc = acc + tap * y_scr[head + r0 - back:head + r0 - back + chunk, :]
        o_ref[r0:r0 + chunk, :] = (acc * _sigmoid(acc)).astype(o_ref.dtype)

    for r0 in range(0, seq, chunk):
        matmul(r0)
        finish(r0)


def _proj(x, w, *, batch, seq, tn, col0=0, n_dim=None, conv=None, out_dtype=BF16):
    k_dim = w.shape[0]
    n_dim = w.shape[1] - col0 if n_dim is None else n_dim
    assert n_dim % tn == 0 and tn % LANES == 0 and col0 % tn == 0
    j0 = col0 // tn
    in_specs = [pl.BlockSpec((seq, k_dim), lambda b, j: (b, 0)),
                pl.BlockSpec((k_dim, tn), lambda b, j: (0, j0 + j))]
    if conv is None:
        body = functools.partial(_proj_silu_kernel, chunk=CONV_ROW_CHUNK)
        args, scratch, name = (x, w), [], "proj_silu"
    else:
        body = functools.partial(_proj_conv_silu_kernel, chunk=CONV_ROW_CHUNK)
        args = (x, w) + tuple(conv)
        in_specs += [pl.BlockSpec((c.shape[0], tn), lambda b, j: (0, j)) for c in conv]
        assert conv[0].shape[0] - 1 <= SUBLANES
        scratch, name = [pltpu.VMEM((SUBLANES + seq, tn), F32)], "proj_conv_silu"
    return pl.pallas_call(
        body,
        grid=(batch, n_dim // tn),
        in_specs=in_specs,
        out_specs=pl.BlockSpec((seq, tn), lambda b, j: (b, j)),
        out_shape=jax.ShapeDtypeStruct((batch * seq, n_dim), out_dtype),
        scratch_shapes=scratch,
        compiler_params=_cparams(2),
        name=name,
    )(*args)


def _dt_kernel(x_ref, w_ref, b_ref, spread_ref, o_ref, xbf_ref):
    x = x_ref[...].astype(BF16)
    xbf_ref[...] = x
    y = jnp.dot(x, w_ref[...], preferred_element_type=F32) + b_ref[...]
    y = jnp.maximum(y, 0.0) + jnp.log1p(jnp.exp(-jnp.abs(y)))
    spread = spread_ref[...]
    hi = y.astype(BF16)
    rest = y - hi.astype(F32)
    mid = rest.astype(BF16)
    lo = (rest - mid.astype(F32)).astype(BF16)
    o_ref[...] = (jnp.dot(hi, spread, preferred_element_type=F32)
                  + jnp.dot(mid, spread, preferred_element_type=F32)
                  + jnp.dot(lo, spread, preferred_element_type=F32))


def _dt_proj(x, w, bias, spread, *, tm):
    rows, k_dim = x.shape
    width = spread.shape[1]
    return pl.pallas_call(
        _dt_kernel,
        grid=(rows // tm,),
        in_specs=[pl.BlockSpec((tm, k_dim), lambda i: (i, 0)),
                  pl.BlockSpec((k_dim, LANES), lambda i: (0, 0)),
                  pl.BlockSpec((1, LANES), lambda i: (0, 0)),
                  pl.BlockSpec((LANES, width), lambda i: (0, 0))],
        out_specs=[pl.BlockSpec((tm, width), lambda i: (i, 0)),
                   pl.BlockSpec((tm, k_dim), lambda i: (i, 0))],
        out_shape=[jax.ShapeDtypeStruct((rows, width), F32),
                   jax.ShapeDtypeStruct((rows, k_dim), BF16)],
        compiler_params=_cparams(1),
        name="dt_proj",
    )(x, w, bias, spread)


def _cumsum_rows(v):
    n = v.shape[0]
    row = lax.broadcasted_iota(jnp.int32, v.shape, 0)
    shift = 1
    while shift < n:
        v = v + jnp.where(row >= shift, pltpu.roll(v, shift, 0), 0.0)
        shift *= 2
    return v


def _ssd_kernel(xs_ref, b_ref, c_ref, z_ref, dt_ref, alog_ref, dskip_ref, ng_ref, e_ref, o_ref,
                state_ref, *, heads_per_group, head_dim):
    g = pl.program_id(1)
    chunk = pl.program_id(2)
    q = xs_ref.shape[0]

    @pl.when(chunk == 0)
    def _():
        state_ref[...] = jnp.zeros_like(state_ref)

    xs = xs_ref[...].astype(F32)
    bm = b_ref[...]
    cm = c_ref[...]
    dt = dt_ref[...]
    a = -jnp.exp(alog_ref[pl.ds(g, 1), :]) * math.log2(math.e)
    acs = _cumsum_rows(dt * a)
    acs_t = acs.T
    last = acs[q - 1:q, :]
    expand_mat = e_ref[...]

    def expand(v):
        return jnp.dot(v.astype(BF16), expand_mat, preferred_element_type=F32)

    def expand_hi_lo(v):
        hi = v.astype(BF16)
        lo = (v - hi.astype(F32)).astype(BF16)
        return (jnp.dot(hi, expand_mat, preferred_element_type=F32)
                + jnp.dot(lo, expand_mat, preferred_element_type=F32))

    decay_in_x = expand_hi_lo(jnp.exp2(acs))
    decay_out_dt_x = expand(jnp.exp2(last - acs) * dt)

    bc = lax.dot_general(bm, cm, (((1,), (1,)), ((), ())), preferred_element_type=F32)
    blk = LANES
    tri = (lax.broadcasted_iota(jnp.int32, (blk, blk), 1) >= lax.broadcasted_iota(jnp.int32, (blk, blk), 0))
    xs_t = xs.T
    dt_t = dt.T

    def masked_bc(r):
        rows = []
        for s0 in range(0, q, blk):
            parts = []
            if s0 > 0:
                parts.append(jnp.zeros((blk, s0), BF16))
            diff = acs_t[r:r + 1, s0:s0 + blk] - acs[s0:s0 + blk, r:r + 1]
            decay = jnp.exp2(jnp.where(tri, diff, -jnp.inf))
            parts.append((bc[s0:s0 + blk, s0:s0 + blk] * decay).astype(BF16))
            if s0 + blk < q:
                diff = acs_t[r:r + 1, s0 + blk:] - acs[s0:s0 + blk, r:r + 1]
                parts.append((bc[s0:s0 + blk, s0 + blk:] * jnp.exp2(diff)).astype(BF16))
            rows.append(jnp.concatenate(parts, axis=1) if len(parts) > 1 else parts[0])
        return jnp.concatenate(rows, axis=0) if len(rows) > 1 else rows[0]

    y_t = []
    for r in range(heads_per_group):
        x_head_t = (xs_t[r * head_dim:(r + 1) * head_dim, :] * dt_t[r:r + 1, :]).astype(BF16)
        y_t.append(jnp.dot(x_head_t, masked_bc(r), preferred_element_type=F32))
    y_diag = jnp.concatenate(y_t, axis=0).T

    s_prev = state_ref[...]
    y_off = jnp.dot(cm, s_prev.astype(BF16), preferred_element_type=F32) * decay_in_x
    y = y_diag + y_off + dskip_ref[pl.ds(g, 1), :] * xs
    b_t = bm.astype(F32).T.astype(BF16)
    s_new = jnp.dot(b_t, (xs * decay_out_dt_x).astype(BF16), preferred_element_type=F32)
    state_ref[...] = s_prev * decay_in_x[q - 1:q, :] + s_new

    yg = y * z_ref[...].astype(F32)
    ms = jnp.mean(yg * yg, axis=-1, keepdims=True)
    o_ref[...] = (yg * lax.rsqrt(ms + RMS_EPS) * ng_ref[...]).astype(o_ref.dtype)


def _ssd_scan(xbc, z, dt, alog, dskip, norm_g, expand_mat, *, batch, seq, d_inner):
    n_groups, d_state, chunk = SSD_N_GROUPS, SSD_D_STATE, math.gcd(SSD_CHUNK, seq)
    gw = d_inner // n_groups
    heads_per_group = gw // SSD_HEAD_DIM
    n_chunks = seq // chunk
    b_col0 = d_inner // d_state
    c_col0 = b_col0 + n_groups
    row = lambda b, g, c: b * n_chunks + c
    return pl.pallas_call(
        functools.partial(_ssd_kernel, heads_per_group=heads_per_group, head_dim=SSD_HEAD_DIM),
        grid=(batch, n_groups, n_chunks),
        in_specs=[
            pl.BlockSpec((chunk, gw), lambda b, g, c: (row(b, g, c), g)),
            pl.BlockSpec((chunk, d_state), lambda b, g, c: (row(b, g, c), b_col0 + g)),
            pl.BlockSpec((chunk, d_state), lambda b, g, c: (row(b, g, c), c_col0 + g)),
            pl.BlockSpec((chunk, gw), lambda b, g, c: (row(b, g, c), g)),
            pl.BlockSpec((chunk, LANES), lambda b, g, c: (row(b, g, c), g)),
            pl.BlockSpec((n_groups, LANES), lambda b, g, c: (0, 0)),
            pl.BlockSpec((n_groups, gw), lambda b, g, c: (0, 0)),
            pl.BlockSpec((1, gw), lambda b, g, c: (0, g)),
            pl.BlockSpec((LANES, gw), lambda b, g, c: (0, 0)),
        ],
        out_specs=pl.BlockSpec((chunk, gw), lambda b, g, c: (row(b, g, c), g)),
        out_shape=jax.ShapeDtypeStruct((batch * seq, d_inner), BF16),
        scratch_shapes=[pltpu.VMEM((d_state, gw), F32)],
        compiler_params=_cparams(3),
        name="ssd_scan",
    )(xbc, xbc, xbc, z, dt, alog, dskip, norm_g, expand_mat)


def _outproj_ln_kernel(y_ref, w_ref, x_ref, g_ref, b_ref, o_ref, obf_ref, *, alpha, chunk):
    for r0 in range(0, y_ref.shape[0], chunk):
        rows = slice(r0, r0 + chunk)
        h = jnp.dot(y_ref[rows, :], w_ref[...], preferred_element_type=F32)
        v = alpha * x_ref[rows, :] + h
        mu = jnp.mean(v, axis=-1, keepdims=True)
        d = v - mu
        var = jnp.mean(d * d, axis=-1, keepdims=True)
        out = d * lax.rsqrt(var + LN_EPS) * g_ref[...] + b_ref[...]
        o_ref[rows, :] = out
        obf_ref[rows, :] = out.astype(BF16)


def _outproj_ln(y, w, x, ln_g, ln_b, *, alpha, tm):
    m, k_dim = y.shape
    d = w.shape[1]
    return pl.pallas_call(
        functools.partial(_outproj_ln_kernel, alpha=alpha, chunk=LN_ROW_CHUNK),
        grid=(m // tm,),
        in_specs=[pl.BlockSpec((tm, k_dim), lambda i: (i, 0)),
                  pl.BlockSpec((k_dim, d), lambda i: (0, 0), pipeline_mode=pl.Buffered(1)),
                  pl.BlockSpec((tm, d), lambda i: (i, 0)),
                  pl.BlockSpec((1, d), lambda i: (0, 0)),
                  pl.BlockSpec((1, d), lambda i: (0, 0))],
        out_specs=[pl.BlockSpec((tm, d), lambda i: (i, 0)),
                   pl.BlockSpec((tm, d), lambda i: (i, 0))],
        out_shape=[jax.ShapeDtypeStruct((m, d), F32), jax.ShapeDtypeStruct((m, d), BF16)],
        compiler_params=_cparams(1),
        name="outproj_ln",
    )(y, w, x, ln_g, ln_b)


def _dsa_latent_kernel(x_ref, w_ref, qg_ref, kvg_ref, cqt_ref, ckv_ref, ckvt_ref, slab_ref, slabt_ref):
    p = jnp.dot(x_ref[...], w_ref[...], preferred_element_type=F32)
    q_rank = cqt_ref.shape[1]
    kv_rank = ckv_ref.shape[1]
    cq = p[:, :q_rank]
    ckv = p[:, q_rank:q_rank + kv_rank]
    slab = p[:, q_rank + kv_rank:]
    cq = cq * lax.rsqrt(jnp.mean(cq * cq, axis=-1, keepdims=True) + RMS_EPS) * qg_ref[...]
    ckv = ckv * lax.rsqrt(jnp.mean(ckv * ckv, axis=-1, keepdims=True) + RMS_EPS) * kvg_ref[...]
    cqt_ref[0] = cq.T.astype(BF16)
    ckv_ref[...] = ckv.astype(BF16)
    ckvt_ref[0, :kv_rank, :] = ckv.T.astype(BF16)
    ckvt_ref[0, kv_rank:, :] = jnp.ones((ckvt_ref.shape[1] - kv_rank, ckvt_ref.shape[2]), BF16)
    slab_ref[...] = slab.astype(BF16)
    slabt_ref[0] = slab.T


def _dsa_latent(x, w, q_gain, kv_gain, *, batch, seq):
    tm = IDX_TQ
    n_t = seq // tm
    k_dim, n_dim = w.shape
    return pl.pallas_call(
        _dsa_latent_kernel,
        grid=(batch, n_t),
        in_specs=[pl.BlockSpec((tm, k_dim), lambda b, t: (b * n_t + t, 0)),
                  pl.BlockSpec((k_dim, n_dim), lambda b, t: (0, 0)),
                  pl.BlockSpec((1, DSA_Q_RANK), lambda b, t: (0, 0)),
                  pl.BlockSpec((1, DSA_KV_RANK), lambda b, t: (0, 0))],
        out_specs=[pl.BlockSpec((1, DSA_Q_RANK, tm), lambda b, t: (b, 0, t)),
                   pl.BlockSpec((tm, DSA_KV_RANK), lambda b, t: (b * n_t + t, 0)),
                   pl.BlockSpec((1, DSA_KV_RANK + ONES_ROWS, tm), lambda b, t: (b, 0, t)),
                   pl.BlockSpec((tm, LANES), lambda b, t: (b * n_t + t, 0)),
                   pl.BlockSpec((1, LANES, tm), lambda b, t: (b, 0, t))],
        out_shape=[jax.ShapeDtypeStruct((batch, DSA_Q_RANK, seq), BF16),
                   jax.ShapeDtypeStruct((batch * seq, DSA_KV_RANK), BF16),
                   jax.ShapeDtypeStruct((batch, DSA_KV_RANK + ONES_ROWS, seq), BF16),
                   jax.ShapeDtypeStruct((batch * seq, LANES), BF16),
                   jax.ShapeDtypeStruct((batch, LANES, seq), F32)],
        compiler_params=_cparams(2),
        name="dsa_latent",
    )(x, w, q_gain, kv_gain)


def _dsa_query_kernel(cqt_ref, wuqt_ref, wuk_ref, ql_ref, *, scale):
    hb, _, hd = wuk_ref.shape
    tq = ql_ref.shape[3] // hb
    q_t = jnp.dot(wuqt_ref[...], cqt_ref[0], preferred_element_type=F32).astype(BF16)
    for j in range(hb):
        ql_t = jnp.dot(wuk_ref[j], q_t[j * hd:(j + 1) * hd, :], preferred_element_type=F32)
        ql_t = (ql_t * scale).astype(BF16)
        for blk in range(ql_ref.shape[1]):
            ql_ref[0, blk, :, j * tq:(j + 1) * tq] = ql_t[:, blk * tq:(blk + 1) * tq]


def _dsa_query(cq_t, wuq_t, wuk, *, batch, seq, scale):
    tm = min(seq, 1024)
    hb = 8
    tq = ATT_TQ
    n_t = seq // tm
    n_heads, kv_rank, hd = wuk.shape
    return pl.pallas_call(
        functools.partial(_dsa_query_kernel, scale=scale),
        grid=(batch, n_t, n_heads // hb),
        in_specs=[pl.BlockSpec((1, DSA_Q_RANK, tm), lambda b, t, h: (b, 0, t)),
                  pl.BlockSpec((hb * hd, DSA_Q_RANK), lambda b, t, h: (h, 0)),
                  pl.BlockSpec((hb, kv_rank, hd), lambda b, t, h: (h, 0, 0))],
        out_specs=pl.BlockSpec((1, tm // tq, kv_rank, hb * tq), lambda b, t, h: (b, t, 0, h)),
        out_shape=jax.ShapeDtypeStruct((batch, seq // tq, kv_rank, n_heads * tq), BF16),
        compiler_params=_cparams(3),
        name="dsa_query",
    )(cq_t, wuq_t, wuk)


def _indexer_kernel(keys_ref, slabt_ref, cqt_ref, widx_ref, bias_ref, qt_scr, sc_scr, *, k_top, w_scale, n_iter):
    i = pl.program_id(1)
    n_t, kt_w, tq = sc_scr.shape
    cq_t = cqt_ref[0]
    for h in range(IDX_N_HEADS):
        qt_scr[h] = jnp.dot(widx_ref[h], cq_t, preferred_element_type=F32).astype(BF16)
    w_t = slabt_ref[0][IDX_HEAD_DIM:IDX_HEAD_DIM + IDX_N_HEADS, :] * w_scale
    q_pos = i * tq + lax.broadcasted_iota(jnp.int32, (kt_w, tq), 1)

    for kt in range(n_t):
        @pl.when(kt <= i)
        def _(kt=kt):
            keys = keys_ref[kt * kt_w:(kt + 1) * kt_w, :]
            acc = jnp.zeros((kt_w, tq), F32)
            for h in range(IDX_N_HEADS):
                s = jnp.dot(keys, qt_scr[h], preferred_element_type=F32)
                acc = acc + jnp.maximum(s, 0.0) * w_t[h:h + 1, :]
            k_pos = kt * kt_w + lax.broadcasted_iota(jnp.int32, (kt_w, tq), 0)
            sc_scr[kt] = jnp.where(k_pos <= q_pos, acc, -jnp.inf)

    def bounds(kt, carry):
        lo, hi = carry
        s = sc_scr[kt]
        hi = jnp.maximum(hi, jnp.max(s, axis=0, keepdims=True))
        lo = jnp.minimum(lo, jnp.min(jnp.where(s == -jnp.inf, jnp.inf, s), axis=0, keepdims=True))
        return lo, hi

    lo, hi = lax.fori_loop(0, i + 1, bounds,
                           (jnp.full((1, tq), jnp.inf, F32), jnp.full((1, tq), -jnp.inf, F32)))

    def halve(carry):
        it, lo, hi, n_lo = carry
        mid = 0.5 * lo + 0.5 * hi

        def count(kt, cnt):
            return cnt + jnp.sum(jnp.where(sc_scr[kt] >= mid, 1.0, 0.0), axis=0, keepdims=True)

        cnt = lax.fori_loop(0, i + 1, count, jnp.zeros((1, tq), F32))
        enough = cnt >= k_top
        return it + 1, jnp.where(enough, mid, lo), jnp.where(enough, hi, mid), jnp.where(enough, cnt, n_lo)

    def unresolved(carry):
        it, _, _, n_lo = carry
        return jnp.logical_and(it < n_iter, jnp.max(n_lo) > k_top)

    n_causal = (i * tq + lax.broadcasted_iota(jnp.int32, (1, tq), 1) + 1).astype(F32)
    _, lo, hi, _ = lax.while_loop(unresolved, halve, (jnp.int32(0), lo, hi, n_causal))

    for kt in range(n_t):
        @pl.when(kt <= i)
        def _(kt=kt):
            bias_ref[0, kt * kt_w:(kt + 1) * kt_w, :] = jnp.where(sc_scr[kt] >= lo, 0.0, -jnp.inf)

        @pl.when(kt > i)
        def _(kt=kt):
            bias_ref[0, kt * kt_w:(kt + 1) * kt_w, :] = jnp.full((kt_w, tq), -jnp.inf, F32)


def _dsa_indexer(keys, slab_t, cq_t, widx_t, *, batch, seq, k_top, w_scale):
    tq = IDX_TQ
    n_t = seq // tq
    return pl.pallas_call(
        functools.partial(_indexer_kernel, k_top=k_top, w_scale=w_scale, n_iter=BISECT_ITERS),
        grid=(batch, n_t),
        in_specs=[pl.BlockSpec((seq, LANES), lambda b, i: (b, 0)),
                  pl.BlockSpec((1, LANES, tq), lambda b, i: (b, 0, i)),
                  pl.BlockSpec((1, DSA_Q_RANK, tq), lambda b, i: (b, 0, i)),
                  pl.BlockSpec(widx_t.shape, lambda b, i: (0, 0, 0))],
        out_specs=pl.BlockSpec((1, seq, tq), lambda b, i: (b, 0, i)),
        out_shape=jax.ShapeDtypeStruct((batch, seq, seq), F32),
        scratch_shapes=[pltpu.VMEM((IDX_N_HEADS, LANES, tq), BF16),
                        pltpu.VMEM((n_t, tq, tq), F32)],
        compiler_params=_cparams(2),
        name="dsa_indexer",
    )(keys, slab_t, cq_t, widx_t)


def _attn_kernel(ql_ref, bias_ref, k_ref, vt_ref, wuvt_ref, gate_ref, *rest, n_keys, hc):
    o_ref = rest[-1]
    tq = bias_ref.shape[2]
    hb, hd, kv_rank = wuvt_ref.shape
    cols = hc * tq
    n_chunks = hb // hc
    keys = k_ref[:n_keys, :]
    bias_t = bias_ref[0]
    bias_rep = jnp.concatenate([bias_t] * hc, axis=1) if hc > 1 else bias_t

    def scores(c):
        q_t = ql_ref[0, 0, :, c * cols:(c + 1) * cols]
        return jnp.dot(keys, q_t, preferred_element_type=F32) + bias_rep

    def finish(c, s):
        m = jnp.max(s, axis=0, keepdims=True)
        p = jnp.exp2(s - m).astype(BF16)
        o_aug = jnp.dot(vt_ref[0], p, preferred_element_type=F32)
        o_t = (o_aug[:kv_rank] / o_aug[kv_rank:kv_rank + 1]).astype(BF16)
        for j in range(hc):
            h = c * hc + j
            oh_t = jnp.dot(wuvt_ref[h], o_t[:, j * tq:(j + 1) * tq], preferred_element_type=F32)
            gate = gate_ref[:, h * hd:(h + 1) * hd].astype(F32)
            o_ref[:, h * hd:(h + 1) * hd] = (oh_t.T * gate).astype(o_ref.dtype)

    s_next = scores(0)
    for c in range(n_chunks):
        s_cur = s_next
        if c + 1 < n_chunks:
            s_next = scores(c + 1)
        finish(c, s_cur)


def _dsa_attention(ql, bias, ckv_t, ckv, wuv_t, gate, *, batch, seq):
    tq = ATT_TQ
    n_q = seq // tq
    n_heads, hd, kv_rank = wuv_t.shape
    width = n_heads * hd
    og = None
    for i in range(n_q):
        n_keys = (i + 1) * tq
        hb = n_heads
        while hb > 1 and hb * tq * n_keys > ATT_STEP_ELEMS:
            hb //= 2
        hc = hb
        while hc > 1 and hc * tq * n_keys > ATT_CHUNK_ELEMS:
            hc //= 2
        in_specs = [pl.BlockSpec((1, 1, kv_rank, hb * tq), lambda b, g, i=i: (b, i, 0, g)),
                    pl.BlockSpec((1, n_keys, tq), lambda b, g, i=i: (b, 0, i)),
                    pl.BlockSpec((seq, kv_rank), lambda b, g: (b, 0)),
                    pl.BlockSpec((1, kv_rank + ONES_ROWS, n_keys), lambda b, g: (b, 0, 0)),
                    pl.BlockSpec((hb, hd, kv_rank), lambda b, g: (g, 0, 0)),
                    pl.BlockSpec((tq, hb * hd), lambda b, g, i=i: (b * n_q + i, g))]
        args = [ql, bias, ckv, ckv_t, wuv_t, gate]
        aliases = {}
        if og is not None:
            in_specs.append(pl.BlockSpec(memory_space=pl.ANY))
            args.append(og)
            aliases = {len(args) - 1: 0}
        og = pl.pallas_call(
            functools.partial(_attn_kernel, n_keys=n_keys, hc=hc),
            grid=(batch, n_heads // hb),
            in_specs=in_specs,
            out_specs=pl.BlockSpec((tq, hb * hd), lambda b, g, i=i: (b * n_q + i, g)),
            out_shape=jax.ShapeDtypeStruct((batch * seq, width), BF16),
            input_output_aliases=aliases,
            compiler_params=_cparams(2),
            name="dsa_attention_q%d" % i,
        )(*args)
    return og


def _ssd_layer(x, w_in, conv_w, conv_b, dt_bias, a_log, d_skip, norm_g, w_out, ln_g, ln_b,
               *, batch, seq, alpha):
    d_inner = w_out.shape[0]
    n_heads = a_log.shape[0]
    n_groups = SSD_N_GROUPS
    heads_per_group = n_heads // n_groups
    conv_dim = conv_w.shape[1]
    gw = d_inner // n_groups
    assert gw % LANES == 0 and heads_per_group <= LANES and SSD_D_STATE == LANES
    assert LANES % SSD_HEAD_DIM == 0 and heads_per_group % (LANES // SSD_HEAD_DIM) == 0

    def per_group_lanes(v):
        lead = v.shape[:-1]
        v = v.reshape(lead + (n_groups, heads_per_group))
        v = jnp.pad(v, [(0, 0)] * len(lead) + [(0, 0), (0, LANES - heads_per_group)])
        return v.reshape(lead + (n_groups * LANES,))

    w_bf = w_in.astype(BF16)
    w_dt = jnp.pad(w_bf[:, d_inner + conv_dim:], ((0, 0), (0, LANES - n_heads)))
    dt_b = jnp.pad(dt_bias, (0, LANES - n_heads))[None, :]
    head_of_lane = jnp.arange(n_groups * LANES)
    head_of_lane = jnp.where(head_of_lane % LANES < heads_per_group,
                             (head_of_lane // LANES) * heads_per_group + head_of_lane % LANES, -1)
    spread = (jnp.arange(LANES)[:, None] == head_of_lane[None, :]).astype(BF16)
    alog = per_group_lanes(a_log).reshape(n_groups, LANES)
    dskip = jnp.repeat(d_skip, SSD_HEAD_DIM).reshape(n_groups, gw)
    lane_of = jnp.arange(gw) // SSD_HEAD_DIM
    expand_mat = (jnp.arange(LANES)[:, None] == lane_of[None, :]).astype(BF16)

    dt, x_bf = _dt_proj(x, w_dt, dt_b, spread, tm=min(seq, 1024))
    z = _proj(x_bf, w_bf, batch=batch, seq=seq, tn=512, n_dim=d_inner)
    xbc = _proj(x_bf, w_bf, batch=batch, seq=seq, tn=1024, col0=d_inner, n_dim=conv_dim,
                conv=(conv_w, conv_b[None, :]))
    yn = _ssd_scan(xbc, z, dt, alog, dskip, norm_g[None, :], expand_mat, batch=batch, seq=seq, d_inner=d_inner)
    return _outproj_ln(yn, w_out.astype(BF16), x, ln_g[None, :], ln_b[None, :], alpha=alpha, tm=512)


def _dsa_layer(x, x_bf, w_in, q_norm_g, kv_norm_g, w_uq, w_uk, w_uv, w_idx_q, w_out, ln_g, ln_b,
               *, batch, seq, alpha):
    assert seq % IDX_TQ == 0 and IDX_TQ % ATT_TQ == 0 and IDX_HEAD_DIM + IDX_N_HEADS <= LANES
    small = DSA_Q_RANK + DSA_KV_RANK + IDX_HEAD_DIM + IDX_N_HEADS
    w_small = jnp.pad(w_in[:, :small], ((0, 0), (0, DSA_Q_RANK + DSA_KV_RANK + LANES - small))).astype(BF16)
    w_gate = w_in[:, small:].astype(BF16)
    wuq_t = w_uq.T.astype(BF16)
    wuv_t = w_uv.transpose(0, 2, 1).astype(BF16)
    widx_t = w_idx_q.reshape(DSA_Q_RANK, IDX_N_HEADS, IDX_HEAD_DIM).transpose(1, 2, 0)
    widx_t = jnp.pad(widx_t, ((0, 0), (0, LANES - IDX_HEAD_DIM), (0, 0))).astype(BF16)
    k_top = min(IDX_TOPK, seq // 4)

    cq_t, ckv, ckv_t, slab, slab_t = _dsa_latent(x_bf, w_small, q_norm_g[None, :], kv_norm_g[None, :],
                                                     batch=batch, seq=seq)
    gate = _proj(x_bf, w_gate, batch=batch, seq=seq, tn=512)
    ql = _dsa_query(cq_t, wuq_t, w_uk.astype(BF16), batch=batch, seq=seq,
                    scale=DSA_HEAD_DIM ** -0.5 * math.log2(math.e))
    bias = _dsa_indexer(slab, slab_t, cq_t, widx_t, batch=batch, seq=seq, k_top=k_top,
                        w_scale=IDX_N_HEADS ** -0.5 * IDX_HEAD_DIM ** -0.5)
    og = _dsa_attention(ql, bias, ckv_t, ckv, wuv_t, gate, batch=batch, seq=seq)
    return _outproj_ln(og, w_out.astype(BF16), x, ln_g[None, :], ln_b[None, :], alpha=alpha, tm=512)


def kernel(x, ssd_w_in, ssd_conv_w, ssd_conv_b, ssd_dt_bias, ssd_a_log, ssd_d_skip, ssd_norm_g, ssd_w_out,
           dsa_w_in, dsa_q_norm_g, dsa_kv_norm_g, dsa_w_uq, dsa_w_uk, dsa_w_uv, dsa_w_idx_q, dsa_w_out,
           ln_g, ln_b):
    batch, seq, d_model = x.shape
    depth = ln_g.shape[0]
    alpha = (2.0 * depth) ** 0.25
    xf = x.reshape(batch * seq, d_model)
    x_bf = None
    for i in range(depth):
        j = i // 2
        if i % 2 == 0:
            xf, x_bf = _ssd_layer(xf, ssd_w_in[j], ssd_conv_w[j], ssd_conv_b[j], ssd_dt_bias[j],
                                  ssd_a_log[j], ssd_d_skip[j], ssd_norm_g[j], ssd_w_out[j], ln_g[i], ln_b[i],
                                  batch=batch, seq=seq, alpha=alpha)
        else:
            x_bf = xf.astype(BF16) if x_bf is None else x_bf
            xf, x_bf = _dsa_layer(xf, x_bf, dsa_w_in[j], dsa_q_norm_g[j], dsa_kv_norm_g[j], dsa_w_uq[j],
                                  dsa_w_uk[j], dsa_w_uv[j], dsa_w_idx_q[j], dsa_w_out[j], ln_g[i], ln_b[i],
                                  batch=batch, seq=seq, alpha=alpha)
    return xf.reshape(batch, seq, d_model)
```

```python
import functools
import math

import jax
import jax.numpy as jnp
from jax import lax
from jax.experimental import pallas as pl
from jax.experimental.pallas import tpu as pltpu

F32 = jnp.float32
BF16 = jnp.bfloat16

V7X_VMEM_BYTES = 64 * 1024 * 1024
LANES = 128
SUBLANES = 8
VMEM_LIMIT_BYTES = V7X_VMEM_BYTES - 8 * 1024 * 1024

LN_EPS = 1e-5
RMS_EPS = 1e-6

SSD_HEAD_DIM = 64
SSD_N_GROUPS = 8
SSD_D_STATE = 128
SSD_CHUNK = 256
DSA_N_HEADS = 32
DSA_HEAD_DIM = 128
DSA_Q_RANK = 512
DSA_KV_RANK = 256
IDX_N_HEADS = 16
IDX_HEAD_DIM = 64
IDX_TOPK = 256

ATT_TQ = 256
ATT_STEP_ELEMS = 8 * 1024 * 1024
ATT_CHUNK_ELEMS = 1024 * 1024
IDX_TQ = 512
BISECT_ITERS = 40
ONES_ROWS = 16
CONV_ROW_CHUNK = 256
LN_ROW_CHUNK = 128
SSD_GROUPS_PER_STEP = 2


def _cparams(n_axes, flags=None):
    return pltpu.CompilerParams(dimension_semantics=("arbitrary",) * n_axes,
                                vmem_limit_bytes=VMEM_LIMIT_BYTES, flags=flags)


LOG2_E = math.log2(math.e)


def _sigmoid(v):
    return 1.0 / (1.0 + jnp.exp2(v * (-LOG2_E)))


def _proj_silu_kernel(x_ref, w_ref, o_ref, *, chunk):
    w = w_ref[...]
    for r0 in range(0, o_ref.shape[0], chunk):
        y = jnp.dot(x_ref[r0:r0 + chunk, :], w, preferred_element_type=F32)
        o_ref[r0:r0 + chunk, :] = (y * _sigmoid(y)).astype(o_ref.dtype)


def _proj_conv_silu_kernel(x_ref, w_ref, conv_w_ref, conv_b_ref, o_ref, y_scr, *, chunk):
    seq, tn = o_ref.shape
    k_taps = conv_w_ref.shape[0]
    head = y_scr.shape[0] - seq
    y_scr[:head, :] = jnp.zeros((head, tn), F32)
    w = w_ref[...]

    def matmul(r0):
        y_scr[head + r0:head + r0 + chunk, :] = jnp.dot(x_ref[r0:r0 + chunk, :], w, preferred_element_type=F32)

    def finish(r0):
        acc = conv_b_ref[...] + conv_w_ref[k_taps - 1:k_taps, :] * y_scr[head + r0:head + r0 + chunk, :]
        for back in range(1, k_taps):
            tap = conv_w_ref[k_taps - 1 - back:k_taps - back, :]
            acc = acc + tap * y_scr[head + r0 - back:head + r0 - back + chunk, :]
        o_ref[r0:r0 + chunk, :] = (acc * _sigmoid(acc)).astype(o_ref.dtype)

    for r0 in range(0, seq, chunk):
        matmul(r0)
        finish(r0)


def _proj(x, w, *, batch, seq, tn, col0=0, n_dim=None, conv=None, out_dtype=BF16):
    k_dim = w.shape[0]
    n_dim = w.shape[1] - col0 if n_dim is None else n_dim
    assert n_dim % tn == 0 and tn % LANES == 0 and col0 % tn == 0
    j0 = col0 // tn
    in_specs = [pl.BlockSpec((seq, k_dim), lambda b, j: (b, 0)),
                pl.BlockSpec((k_dim, tn), lambda b, j: (0, j0 + j))]
    if conv is None:
        body = functools.partial(_proj_silu_kernel, chunk=CONV_ROW_CHUNK)
        args, scratch, name = (x, w), [], "proj_silu"
    else:
        body = functools.partial(_proj_conv_silu_kernel, chunk=CONV_ROW_CHUNK)
        args = (x, w) + tuple(conv)
        in_specs += [pl.BlockSpec((c.shape[0], tn), lambda b, j: (0, j)) for c in conv]
        assert conv[0].shape[0] - 1 <= SUBLANES
        scratch, name = [pltpu.VMEM((SUBLANES + seq, tn), F32)], "proj_conv_silu"
    return pl.pallas_call(
        body,
        grid=(batch, n_dim // tn),
        in_specs=in_specs,
        out_specs=pl.BlockSpec((seq, tn), lambda b, j: (b, j)),
        out_shape=jax.ShapeDtypeStruct((batch * seq, n_dim), out_dtype),
        scratch_shapes=scratch,
        compiler_params=_cparams(2),
        name=name,
    )(*args)


def _dt_kernel(x_ref, w_ref, b_ref, spread_ref, o_ref, xbf_ref):
    x = x_ref[...].astype(BF16)
    xbf_ref[...] = x
    y = jnp.dot(x, w_ref[...], preferred_element_type=F32) + b_ref[...]
    y = jnp.maximum(y, 0.0) + jnp.log1p(jnp.exp(-jnp.abs(y)))
    spread = spread_ref[...]
    hi = y.astype(BF16)
    rest = y - hi.astype(F32)
    mid = rest.astype(BF16)
    lo = (rest - mid.astype(F32)).astype(BF16)
    o_ref[...] = (jnp.dot(hi, spread, preferred_element_type=F32)
                  + jnp.dot(mid, spread, preferred_element_type=F32)
                  + jnp.dot(lo, spread, preferred_element_type=F32))


def _dt_proj(x, w, bias, spread, *, tm):
    rows, k_dim = x.shape
    width = spread.shape[1]
    return pl.pallas_call(
        _dt_kernel,
        grid=(rows // tm,),
        in_specs=[pl.BlockSpec((tm, k_dim), lambda i: (i, 0)),
                  pl.BlockSpec((k_dim, LANES), lambda i: (0, 0)),
                  pl.BlockSpec((1, LANES), lambda i: (0, 0)),
                  pl.BlockSpec((LANES, width), lambda i: (0, 0))],
        out_specs=[pl.BlockSpec((tm, width), lambda i: (i, 0)),
                   pl.BlockSpec((tm, k_dim), lambda i: (i, 0))],
        out_shape=[jax.ShapeDtypeStruct((rows, width), F32),
                   jax.ShapeDtypeStruct((rows, k_dim), BF16)],
        compiler_params=_cparams(1),
        name="dt_proj",
    )(x, w, bias, spread)


def _cumsum_rows(v):
    n = v.shape[0]
    row = lax.broadcasted_iota(jnp.int32, v.shape, 0)
    shift = 1
    while shift < n:
        v = v + jnp.where(row >= shift, pltpu.roll(v, shift, 0), 0.0)
        shift *= 2
    return v


def _ssd_kernel(xs_ref, b_ref, c_ref, z_ref, dt_ref, alog_ref, dskip_ref, ng_ref, e_ref, o_ref,
                state_ref, *, groups, heads_per_group, head_dim):
    @pl.when(pl.program_id(2) == 0)
    def _():
        state_ref[...] = jnp.zeros_like(state_ref)

    gw = e_ref.shape[1]
    n = b_ref.shape[1] // groups
    for s in range(groups):
        g = pl.program_id(1) * groups + s
        lanes = slice(s * gw, (s + 1) * gw)
        _ssd_group(xs_ref.at[:, lanes], b_ref.at[:, s * n:(s + 1) * n], c_ref.at[:, s * n:(s + 1) * n],
                   z_ref.at[:, lanes], dt_ref.at[:, s * LANES:(s + 1) * LANES], alog_ref[pl.ds(g, 1), :],
                   dskip_ref[pl.ds(g, 1), :], ng_ref.at[:, lanes], e_ref, o_ref.at[:, lanes], state_ref.at[s],
                   heads_per_group=heads_per_group, head_dim=head_dim)


def _ssd_group(xs_ref, b_ref, c_ref, z_ref, dt_ref, alog, dskip, ng_ref, e_ref, o_ref, state_ref,
               *, heads_per_group, head_dim):
    q = xs_ref.shape[0]
    xs = xs_ref[...].astype(F32)
    bm = b_ref[...]
    cm = c_ref[...]
    dt = dt_ref[...]
    a = -jnp.exp(alog) * LOG2_E
    acs = _cumsum_rows(dt * a)
    acs_t = acs.T
    last = acs[q - 1:q, :]
    expand_mat = e_ref[...]

    def expand(v):
        return jnp.dot(v.astype(BF16), expand_mat, preferred_element_type=F32)

    def expand_hi_lo(v):
        hi = v.astype(BF16)
        lo = (v - hi.astype(F32)).astype(BF16)
        return (jnp.dot(hi, expand_mat, preferred_element_type=F32)
                + jnp.dot(lo, expand_mat, preferred_element_type=F32))

    decay_in_x = expand_hi_lo(jnp.exp2(acs))
    decay_out_dt_x = expand(jnp.exp2(last - acs) * dt)

    bc = lax.dot_general(bm, cm, (((1,), (1,)), ((), ())), preferred_element_type=F32)
    blk = LANES
    tri = (lax.broadcasted_iota(jnp.int32, (blk, blk), 1) >= lax.broadcasted_iota(jnp.int32, (blk, blk), 0))
    xs_t = xs.T
    dt_t = dt.T

    def masked_bc(r):
        rows = []
        for s0 in range(0, q, blk):
            parts = []
            if s0 > 0:
                parts.append(jnp.zeros((blk, s0), BF16))
            diff = acs_t[r:r + 1, s0:s0 + blk] - acs[s0:s0 + blk, r:r + 1]
            decay = jnp.exp2(jnp.where(tri, diff, -jnp.inf))
            parts.append((bc[s0:s0 + blk, s0:s0 + blk] * decay).astype(BF16))
            if s0 + blk < q:
                diff = acs_t[r:r + 1, s0 + blk:] - acs[s0:s0 + blk, r:r + 1]
                parts.append((bc[s0:s0 + blk, s0 + blk:] * jnp.exp2(diff)).astype(BF16))
            rows.append(jnp.concatenate(parts, axis=1) if len(parts) > 1 else parts[0])
        return jnp.concatenate(rows, axis=0) if len(rows) > 1 else rows[0]

    y_t = []
    for r in range(heads_per_group):
        x_head_t = (xs_t[r * head_dim:(r + 1) * head_dim, :] * dt_t[r:r + 1, :]).astype(BF16)
        y_t.append(jnp.dot(x_head_t, masked_bc(r), preferred_element_type=F32))
    y_diag = jnp.concatenate(y_t, axis=0).T

    s_prev = state_ref[...]
    y_off = jnp.dot(cm, s_prev.astype(BF16), preferred_element_type=F32) * decay_in_x
    y = y_diag + y_off + dskip * xs
    b_t = bm.astype(F32).T.astype(BF16)
    s_new = jnp.dot(b_t, (xs * decay_out_dt_x).astype(BF16), preferred_element_type=F32)
    state_ref[...] = s_prev * decay_in_x[q - 1:q, :] + s_new

    yg = y * z_ref[...].astype(F32)
    ms = jnp.mean(yg * yg, axis=-1, keepdims=True)
    o_ref[...] = (yg * lax.rsqrt(ms + RMS_EPS) * ng_ref[...]).astype(o_ref.dtype)


def _ssd_scan(xbc, z, dt, alog, dskip, norm_g, expand_mat, *, batch, seq, d_inner):
    n_groups, d_state, chunk = SSD_N_GROUPS, SSD_D_STATE, math.gcd(SSD_CHUNK, seq)
    gw = d_inner // n_groups
    heads_per_group = gw // SSD_HEAD_DIM
    n_chunks = seq // chunk
    gps = SSD_GROUPS_PER_STEP
    assert n_groups % gps == 0
    b_col0 = d_inner // (gps * d_state)
    c_col0 = b_col0 + n_groups // gps
    row = lambda b, g, c: b * n_chunks + c
    return pl.pallas_call(
        functools.partial(_ssd_kernel, groups=gps, heads_per_group=heads_per_group, head_dim=SSD_HEAD_DIM),
        grid=(batch, n_groups // gps, n_chunks),
        in_specs=[
            pl.BlockSpec((chunk, gps * gw), lambda b, g, c: (row(b, g, c), g)),
            pl.BlockSpec((chunk, gps * d_state), lambda b, g, c: (row(b, g, c), b_col0 + g)),
            pl.BlockSpec((chunk, gps * d_state), lambda b, g, c: (row(b, g, c), c_col0 + g)),
            pl.BlockSpec((chunk, gps * gw), lambda b, g, c: (row(b, g, c), g)),
            pl.BlockSpec((chunk, gps * LANES), lambda b, g, c: (row(b, g, c), g)),
            pl.BlockSpec((n_groups, LANES), lambda b, g, c: (0, 0)),
            pl.BlockSpec((n_groups, gw), lambda b, g, c: (0, 0)),
            pl.BlockSpec((1, gps * gw), lambda b, g, c: (0, g)),
            pl.BlockSpec((LANES, gw), lambda b, g, c: (0, 0)),
        ],
        out_specs=pl.BlockSpec((chunk, gps * gw), lambda b, g, c: (row(b, g, c), g)),
        out_shape=jax.ShapeDtypeStruct((batch * seq, d_inner), BF16),
        scratch_shapes=[pltpu.VMEM((gps, d_state, gw), F32)],
        compiler_params=_cparams(3),
        name="ssd_scan",
    )(xbc, xbc, xbc, z, dt, alog, dskip, norm_g, expand_mat)


def _outproj_ln_kernel(y_ref, w_ref, x_ref, g_ref, b_ref, o_ref, obf_ref, *, alpha, chunk):
    for r0 in range(0, y_ref.shape[0], chunk):
        rows = slice(r0, r0 + chunk)
        h = jnp.dot(y_ref[rows, :], w_ref[...], preferred_element_type=F32)
        v = alpha * x_ref[rows, :] + h
        mu = jnp.mean(v, axis=-1, keepdims=True)
        d = v - mu
        var = jnp.mean(d * d, axis=-1, keepdims=True)
        out = d * lax.rsqrt(var + LN_EPS) * g_ref[...] + b_ref[...]
        o_ref[rows, :] = out
        obf_ref[rows, :] = out.astype(BF16)


def _outproj_ln(y, w, x, ln_g, ln_b, *, alpha, tm):
    m, k_dim = y.shape
    d = w.shape[1]
    return pl.pallas_call(
        functools.partial(_outproj_ln_kernel, alpha=alpha, chunk=LN_ROW_CHUNK),
        grid=(m // tm,),
        in_specs=[pl.BlockSpec((tm, k_dim), lambda i: (i, 0)),
                  pl.BlockSpec((k_dim, d), lambda i: (0, 0), pipeline_mode=pl.Buffered(1)),
                  pl.BlockSpec((tm, d), lambda i: (i, 0)),
                  pl.BlockSpec((1, d), lambda i: (0, 0)),
                  pl.BlockSpec((1, d), lambda i: (0, 0))],
        out_specs=[pl.BlockSpec((tm, d), lambda i: (i, 0)),
                   pl.BlockSpec((tm, d), lambda i: (i, 0))],
        out_shape=[jax.ShapeDtypeStruct((m, d), F32), jax.ShapeDtypeStruct((m, d), BF16)],
        compiler_params=_cparams(1),
        name="outproj_ln",
    )(y, w, x, ln_g, ln_b)


def _dsa_latent_kernel(x_ref, w_ref, qg_ref, kvg_ref, cqt_ref, ckv_ref, ckvt_ref, slab_ref, slabt_ref):
    p = jnp.dot(x_ref[...], w_ref[...], preferred_element_type=F32)
    q_rank = cqt_ref.shape[1]
    kv_rank = ckv_ref.shape[1]
    cq = p[:, :q_rank]
    ckv = p[:, q_rank:q_rank + kv_rank]
    slab = p[:, q_rank + kv_rank:]
    cq = cq * lax.rsqrt(jnp.mean(cq * cq, axis=-1, keepdims=True) + RMS_EPS) * qg_ref[...]
    ckv = ckv * lax.rsqrt(jnp.mean(ckv * ckv, axis=-1, keepdims=True) + RMS_EPS) * kvg_ref[...]
    cqt_ref[0] = cq.T.astype(BF16)
    ckv_ref[...] = ckv.astype(BF16)
    ckvt_ref[0, :kv_rank, :] = ckv.T.astype(BF16)
    ckvt_ref[0, kv_rank:, :] = jnp.ones((ckvt_ref.shape[1] - kv_rank, ckvt_ref.shape[2]), BF16)
    slab_ref[...] = slab.astype(BF16)
    slabt_ref[0] = slab.T


def _dsa_latent(x, w, q_gain, kv_gain, *, batch, seq):
    tm = IDX_TQ
    n_t = seq // tm
    k_dim, n_dim = w.shape
    return pl.pallas_call(
        _dsa_latent_kernel,
        grid=(batch, n_t),
        in_specs=[pl.BlockSpec((tm, k_dim), lambda b, t: (b * n_t + t, 0)),
                  pl.BlockSpec((k_dim, n_dim), lambda b, t: (0, 0)),
                  pl.BlockSpec((1, DSA_Q_RANK), lambda b, t: (0, 0)),
                  pl.BlockSpec((1, DSA_KV_RANK), lambda b, t: (0, 0))],
        out_specs=[pl.BlockSpec((1, DSA_Q_RANK, tm), lambda b, t: (b, 0, t)),
                   pl.BlockSpec((tm, DSA_KV_RANK), lambda b, t: (b * n_t + t, 0)),
                   pl.BlockSpec((1, DSA_KV_RANK + ONES_ROWS, tm), lambda b, t: (b, 0, t)),
                   pl.BlockSpec((tm, LANES), lambda b, t: (b * n_t + t, 0)),
                   pl.BlockSpec((1, LANES, tm), lambda b, t: (b, 0, t))],
        out_shape=[jax.ShapeDtypeStruct((batch, DSA_Q_RANK, seq), BF16),
                   jax.ShapeDtypeStruct((batch * seq, DSA_KV_RANK), BF16),
                   jax.ShapeDtypeStruct((batch, DSA_KV_RANK + ONES_ROWS, seq), BF16),
                   jax.ShapeDtypeStruct((batch * seq, LANES), BF16),
                   jax.ShapeDtypeStruct((batch, LANES, seq), F32)],
        compiler_params=_cparams(2),
        name="dsa_latent",
    )(x, w, q_gain, kv_gain)


def _dsa_query_kernel(cqt_ref, wuqt_ref, wuk_ref, ql_ref, *, scale):
    hb, _, hd = wuk_ref.shape
    tq = ql_ref.shape[3] // hb
    q_t = jnp.dot(wuqt_ref[...], cqt_ref[0], preferred_element_type=F32).astype(BF16)
    for j in range(hb):
        ql_t = jnp.dot(wuk_ref[j], q_t[j * hd:(j + 1) * hd, :], preferred_element_type=F32)
        ql_t = (ql_t * scale).astype(BF16)
        for blk in range(ql_ref.shape[1]):
            ql_ref[0, blk, :, j * tq:(j + 1) * tq] = ql_t[:, blk * tq:(blk + 1) * tq]


def _dsa_query(cq_t, wuq_t, wuk, *, batch, seq, scale):
    tm = min(seq, 1024)
    hb = 8
    tq = ATT_TQ
    n_t = seq // tm
    n_heads, kv_rank, hd = wuk.shape
    return pl.pallas_call(
        functools.partial(_dsa_query_kernel, scale=scale),
        grid=(batch, n_t, n_heads // hb),
        in_specs=[pl.BlockSpec((1, DSA_Q_RANK, tm), lambda b, t, h: (b, 0, t)),
                  pl.BlockSpec((hb * hd, DSA_Q_RANK), lambda b, t, h: (h, 0)),
                  pl.BlockSpec((hb, kv_rank, hd), lambda b, t, h: (h, 0, 0))],
        out_specs=pl.BlockSpec((1, tm // tq, kv_rank, hb * tq), lambda b, t, h: (b, t, 0, h)),
        out_shape=jax.ShapeDtypeStruct((batch, seq // tq, kv_rank, n_heads * tq), BF16),
        compiler_params=_cparams(3),
        name="dsa_query",
    )(cq_t, wuq_t, wuk)


def _indexer_kernel(keys_ref, slabt_ref, cqt_ref, widx_ref, bias_ref, qt_scr, sc_scr, *, k_top, w_scale, n_iter):
    i = pl.program_id(1)
    n_t, kt_w, tq = sc_scr.shape
    cq_t = cqt_ref[0]
    for h in range(IDX_N_HEADS):
        qt_scr[h] = jnp.dot(widx_ref[h], cq_t, preferred_element_type=F32).astype(BF16)
    w_t = slabt_ref[0][IDX_HEAD_DIM:IDX_HEAD_DIM + IDX_N_HEADS, :] * w_scale
    q_pos = i * tq + lax.broadcasted_iota(jnp.int32, (kt_w, tq), 1)

    for kt in range(n_t):
        @pl.when(kt <= i)
        def _(kt=kt):
            keys = keys_ref[kt * kt_w:(kt + 1) * kt_w, :]
            acc = jnp.zeros((kt_w, tq), F32)
            for h in range(IDX_N_HEADS):
                s = jnp.dot(keys, qt_scr[h], preferred_element_type=F32)
                acc = acc + jnp.maximum(s, 0.0) * w_t[h:h + 1, :]
            k_pos = kt * kt_w + lax.broadcasted_iota(jnp.int32, (kt_w, tq), 0)
            sc_scr[kt] = jnp.where(k_pos <= q_pos, acc, -jnp.inf)

    def bounds(kt, carry):
        lo, hi = carry
        s = sc_scr[kt]
        hi = jnp.maximum(hi, jnp.max(s, axis=0, keepdims=True))
        lo = jnp.minimum(lo, jnp.min(jnp.where(s == -jnp.inf, jnp.inf, s), axis=0, keepdims=True))
        return lo, hi

    lo, hi = lax.fori_loop(0, i + 1, bounds,
                           (jnp.full((1, tq), jnp.inf, F32), jnp.full((1, tq), -jnp.inf, F32)))

    def halve(carry):
        it, lo, hi, n_lo = carry
        mid = 0.5 * lo + 0.5 * hi

        def count(kt, cnt):
            return cnt + jnp.sum(jnp.where(sc_scr[kt] >= mid, 1.0, 0.0), axis=0, keepdims=True)

        cnt = lax.fori_loop(0, i + 1, count, jnp.zeros((1, tq), F32))
        enough = cnt >= k_top
        return it + 1, jnp.where(enough, mid, lo), jnp.where(enough, hi, mid), jnp.where(enough, cnt, n_lo)

    def unresolved(carry):
        it, _, _, n_lo = carry
        return jnp.logical_and(it < n_iter, jnp.max(n_lo) > k_top)

    n_causal = (i * tq + lax.broadcasted_iota(jnp.int32, (1, tq), 1) + 1).astype(F32)
    _, lo, hi, _ = lax.while_loop(unresolved, lambda c: halve(halve(c)), (jnp.int32(0), lo, hi, n_causal))

    for kt in range(n_t):
        @pl.when(kt <= i)
        def _(kt=kt):
            bias_ref[0, kt * kt_w:(kt + 1) * kt_w, :] = jnp.where(sc_scr[kt] >= lo, 0.0, -jnp.inf)

        @pl.when(kt > i)
        def _(kt=kt):
            bias_ref[0, kt * kt_w:(kt + 1) * kt_w, :] = jnp.full((kt_w, tq), -jnp.inf, F32)


def _dsa_indexer(keys, slab_t, cq_t, widx_t, *, batch, seq, k_top, w_scale):
    tq = IDX_TQ
    n_t = seq // tq
    return pl.pallas_call(
        functools.partial(_indexer_kernel, k_top=k_top, w_scale=w_scale, n_iter=BISECT_ITERS),
        grid=(batch, n_t),
        in_specs=[pl.BlockSpec((seq, LANES), lambda b, i: (b, 0)),
                  pl.BlockSpec((1, LANES, tq), lambda b, i: (b, 0, i)),
                  pl.BlockSpec((1, DSA_Q_RANK, tq), lambda b, i: (b, 0, i)),
                  pl.BlockSpec(widx_t.shape, lambda b, i: (0, 0, 0))],
        out_specs=pl.BlockSpec((1, seq, tq), lambda b, i: (b, 0, i)),
        out_shape=jax.ShapeDtypeStruct((batch, seq, seq), F32),
        scratch_shapes=[pltpu.VMEM((IDX_N_HEADS, LANES, tq), BF16),
                        pltpu.VMEM((n_t, tq, tq), F32)],
        compiler_params=_cparams(2),
        name="dsa_indexer",
    )(keys, slab_t, cq_t, widx_t)


def _attn_kernel(ql_ref, bias_ref, k_ref, vt_ref, wuvt_ref, gate_ref, *rest, n_keys, hc):
    o_ref = rest[-1]
    tq = bias_ref.shape[2]
    hb, hd, kv_rank = wuvt_ref.shape
    cols = hc * tq
    n_chunks = hb // hc
    keys = k_ref[:n_keys, :]
    bias_t = bias_ref[0]
    bias_rep = jnp.concatenate([bias_t] * hc, axis=1) if hc > 1 else bias_t

    def scores(c):
        q_t = ql_ref[0, 0, :, c * cols:(c + 1) * cols]
        return jnp.dot(keys, q_t, preferred_element_type=F32) + bias_rep

    def finish(c, s):
        m = jnp.max(s, axis=0, keepdims=True)
        p = jnp.exp2(s - m).astype(BF16)
        o_aug = jnp.dot(vt_ref[0], p, preferred_element_type=F32)
        o_t = (o_aug[:kv_rank] / o_aug[kv_rank:kv_rank + 1]).astype(BF16)
        for j in range(hc):
            h = c * hc + j
            oh_t = jnp.dot(wuvt_ref[h], o_t[:, j * tq:(j + 1) * tq], preferred_element_type=F32)
            gate = gate_ref[:, h * hd:(h + 1) * hd].astype(F32)
            o_ref[:, h * hd:(h + 1) * hd] = (oh_t.T * gate).astype(o_ref.dtype)

    s_next = scores(0)
    for c in range(n_chunks):
        s_cur = s_next
        if c + 1 < n_chunks:
            s_next = scores(c + 1)
        finish(c, s_cur)


def _dsa_attention(ql, bias, ckv_t, ckv, wuv_t, gate, *, batch, seq):
    tq = ATT_TQ
    n_q = seq // tq
    n_heads, hd, kv_rank = wuv_t.shape
    width = n_heads * hd
    og = None
    for i in range(n_q):
        n_keys = (i + 1) * tq
        hb = n_heads
        while hb > 1 and hb * tq * n_keys > ATT_STEP_ELEMS:
            hb //= 2
        hc = hb
        while hc > 1 and hc * tq * n_keys > ATT_CHUNK_ELEMS:
            hc //= 2
        in_specs = [pl.BlockSpec((1, 1, kv_rank, hb * tq), lambda b, g, i=i: (b, i, 0, g)),
                    pl.BlockSpec((1, n_keys, tq), lambda b, g, i=i: (b, 0, i)),
                    pl.BlockSpec((seq, kv_rank), lambda b, g: (b, 0)),
                    pl.BlockSpec((1, kv_rank + ONES_ROWS, n_keys), lambda b, g: (b, 0, 0)),
                    pl.BlockSpec((hb, hd, kv_rank), lambda b, g: (g, 0, 0)),
                    pl.BlockSpec((tq, hb * hd), lambda b, g, i=i: (b * n_q + i, g))]
        args = [ql, bias, ckv, ckv_t, wuv_t, gate]
        aliases = {}
        if og is not None:
            in_specs.append(pl.BlockSpec(memory_space=pl.ANY))
            args.append(og)
            aliases = {len(args) - 1: 0}
        og = pl.pallas_call(
            functools.partial(_attn_kernel, n_keys=n_keys, hc=hc),
            grid=(batch, n_heads // hb),
            in_specs=in_specs,
            out_specs=pl.BlockSpec((tq, hb * hd), lambda b, g, i=i: (b * n_q + i, g)),
            out_shape=jax.ShapeDtypeStruct((batch * seq, width), BF16),
            input_output_aliases=aliases,
            compiler_params=_cparams(2),
            name="dsa_attention_q%d" % i,
        )(*args)
    return og


def _ssd_layer(x, w_in, conv_w, conv_b, dt_bias, a_log, d_skip, norm_g, w_out, ln_g, ln_b,
               *, batch, seq, alpha):
    d_inner = w_out.shape[0]
    n_heads = a_log.shape[0]
    n_groups = SSD_N_GROUPS
    heads_per_group = n_heads // n_groups
    conv_dim = conv_w.shape[1]
    gw = d_inner // n_groups
    assert gw % LANES == 0 and heads_per_group <= LANES and SSD_D_STATE == LANES
    assert LANES % SSD_HEAD_DIM == 0 and heads_per_group % (LANES // SSD_HEAD_DIM) == 0

    def per_group_lanes(v):
        lead = v.shape[:-1]
        v = v.reshape(lead + (n_groups, heads_per_group))
        v = jnp.pad(v, [(0, 0)] * len(lead) + [(0, 0), (0, LANES - heads_per_group)])
        return v.reshape(lead + (n_groups * LANES,))

    w_bf = w_in.astype(BF16)
    w_dt = jnp.pad(w_bf[:, d_inner + conv_dim:], ((0, 0), (0, LANES - n_heads)))
    dt_b = jnp.pad(dt_bias, (0, LANES - n_heads))[None, :]
    head_of_lane = jnp.arange(n_groups * LANES)
    head_of_lane = jnp.where(head_of_lane % LANES < heads_per_group,
                             (head_of_lane // LANES) * heads_per_group + head_of_lane % LANES, -1)
    spread = (jnp.arange(LANES)[:, None] == head_of_lane[None, :]).astype(BF16)
    alog = per_group_lanes(a_log).reshape(n_groups, LANES)
    dskip = jnp.repeat(d_skip, SSD_HEAD_DIM).reshape(n_groups, gw)
    lane_of = jnp.arange(gw) // SSD_HEAD_DIM
    expand_mat = (jnp.arange(LANES)[:, None] == lane_of[None, :]).astype(BF16)

    dt, x_bf = _dt_proj(x, w_dt, dt_b, spread, tm=min(seq, 1024))
    z = _proj(x_bf, w_bf, batch=batch, seq=seq, tn=512, n_dim=d_inner)
    xbc = _proj(x_bf, w_bf, batch=batch, seq=seq, tn=1024, col0=d_inner, n_dim=conv_dim,
                conv=(conv_w, conv_b[None, :]))
    yn = _ssd_scan(xbc, z, dt, alog, dskip, norm_g[None, :], expand_mat, batch=batch, seq=seq, d_inner=d_inner)
    return _outproj_ln(yn, w_out.astype(BF16), x, ln_g[None, :], ln_b[None, :], alpha=alpha, tm=512)


def _dsa_layer(x, x_bf, w_in, q_norm_g, kv_norm_g, w_uq, w_uk, w_uv, w_idx_q, w_out, ln_g, ln_b,
               *, batch, seq, alpha):
    assert seq % IDX_TQ == 0 and IDX_TQ % ATT_TQ == 0 and IDX_HEAD_DIM + IDX_N_HEADS <= LANES
    small = DSA_Q_RANK + DSA_KV_RANK + IDX_HEAD_DIM + IDX_N_HEADS
    w_small = jnp.pad(w_in[:, :small], ((0, 0), (0, DSA_Q_RANK + DSA_KV_RANK + LANES - small))).astype(BF16)
    w_gate = w_in[:, small:].astype(BF16)
    wuq_t = w_uq.T.astype(BF16)
    wuv_t = w_uv.transpose(0, 2, 1).astype(BF16)
    widx_t = w_idx_q.reshape(DSA_Q_RANK, IDX_N_HEADS, IDX_HEAD_DIM).transpose(1, 2, 0)
    widx_t = jnp.pad(widx_t, ((0, 0), (0, LANES - IDX_HEAD_DIM), (0, 0))).astype(BF16)
    k_top = min(IDX_TOPK, seq // 4)

    cq_t, ckv, ckv_t, slab, slab_t = _dsa_latent(x_bf, w_small, q_norm_g[None, :], kv_norm_g[None, :],
                                                     batch=batch, seq=seq)
    gate = _proj(x_bf, w_gate, batch=batch, seq=seq, tn=512)
    ql = _dsa_query(cq_t, wuq_t, w_uk.astype(BF16), batch=batch, seq=seq,
                    scale=DSA_HEAD_DIM ** -0.5 * math.log2(math.e))
    bias = _dsa_indexer(slab, slab_t, cq_t, widx_t, batch=batch, seq=seq, k_top=k_top,
                        w_scale=IDX_N_HEADS ** -0.5 * IDX_HEAD_DIM ** -0.5)
    og = _dsa_attention(ql, bias, ckv_t, ckv, wuv_t, gate, batch=batch, seq=seq)
    return _outproj_ln(og, w_out.astype(BF16), x, ln_g[None, :], ln_b[None, :], alpha=alpha, tm=512)


def kernel(x, ssd_w_in, ssd_conv_w, ssd_conv_b, ssd_dt_bias, ssd_a_log, ssd_d_skip, ssd_norm_g, ssd_w_out,
           dsa_w_in, dsa_q_norm_g, dsa_kv_norm_g, dsa_w_uq, dsa_w_uk, dsa_w_uv, dsa_w_idx_q, dsa_w_out,
           ln_g, ln_b):
    batch, seq, d_model = x.shape
    depth = ln_g.shape[0]
    alpha = (2.0 * depth) ** 0.25
    xf = x.reshape(batch * seq, d_model)
    x_bf = None
    for i in range(depth):
        j = i // 2
        if i % 2 == 0:
            xf, x_bf = _ssd_layer(xf, ssd_w_in[j], ssd_conv_w[j], ssd_conv_b[j], ssd_dt_bias[j],
                                  ssd_a_log[j], ssd_d_skip[j], ssd_norm_g[j], ssd_w_out[j], ln_g[i], ln_b[i],
                                  batch=batch, seq=seq, alpha=alpha)
        else:
            x_bf = xf.astype(BF16) if x_bf is None else x_bf
            xf, x_bf = _dsa_layer(xf, x_bf, dsa_w_in[j], dsa_q_norm_g[j], dsa_kv_norm_g[j], dsa_w_uq[j],
                                  dsa_w_uk[j], dsa_w_uv[j], dsa_w_idx_q[j], dsa_w_out[j], ln_g[i], ln_b[i],
                                  batch=batch, seq=seq, alpha=alpha)
    return xf.reshape(batch, seq, d_model)
```

```python
import functools
import math

import jax
import jax.numpy as jnp
from jax import lax
from jax.experimental import pallas as pl
from jax.experimental.pallas import tpu as pltpu

F32 = jnp.float32
BF16 = jnp.bfloat16

V7X_VMEM_BYTES = 64 * 1024 * 1024
LANES = 128
SUBLANES = 8
VMEM_LIMIT_BYTES = V7X_VMEM_BYTES - 8 * 1024 * 1024

LN_EPS = 1e-5
RMS_EPS = 1e-6

SSD_HEAD_DIM = 64
SSD_N_GROUPS = 8
SSD_D_STATE = 128
SSD_CHUNK = 256
DSA_N_HEADS = 32
DSA_HEAD_DIM = 128
DSA_Q_RANK = 512
DSA_KV_RANK = 256
IDX_N_HEADS = 16
IDX_HEAD_DIM = 64
IDX_TOPK = 256

ATT_TQ = 256
ATT_STEP_ELEMS = 8 * 1024 * 1024
ATT_HEADS_PER_CHUNK = 8
IDX_TQ = 512
BISECT_ITERS = 40
ONES_ROWS = 16
CONV_ROW_CHUNK = 256
LN_ROW_CHUNK = 128
SSD_GROUPS_PER_STEP = 2


def _cparams(n_axes, flags=None):
    return pltpu.CompilerParams(dimension_semantics=("arbitrary",) * n_axes,
                                vmem_limit_bytes=VMEM_LIMIT_BYTES, flags=flags)


LOG2_E = math.log2(math.e)


def _sigmoid(v):
    return 1.0 / (1.0 + jnp.exp2(v * (-LOG2_E)))


def _proj_silu_kernel(x_ref, w_ref, o_ref, *, chunk):
    w = w_ref[...]
    for r0 in range(0, o_ref.shape[0], chunk):
        y = jnp.dot(x_ref[r0:r0 + chunk, :], w, preferred_element_type=F32)
        o_ref[r0:r0 + chunk, :] = (y * _sigmoid(y)).astype(o_ref.dtype)


def _proj_conv_silu_kernel(x_ref, w_ref, conv_w_ref, conv_b_ref, o_ref, y_scr, *, chunk):
    seq, tn = o_ref.shape
    k_taps = conv_w_ref.shape[0]
    head = y_scr.shape[0] - seq
    y_scr[:head, :] = jnp.zeros((head, tn), F32)
    w = w_ref[...]

    def matmul(r0):
        y_scr[head + r0:head + r0 + chunk, :] = jnp.dot(x_ref[r0:r0 + chunk, :], w, preferred_element_type=F32)

    def finish(r0):
        acc = conv_b_ref[...] + conv_w_ref[k_taps - 1:k_taps, :] * y_scr[head + r0:head + r0 + chunk, :]
        for back in range(1, k_taps):
            tap = conv_w_ref[k_taps - 1 - back:k_taps - back, :]
            acc = acc + tap * y_scr[head + r0 - back:head + r0 - back + chunk, :]
        o_ref[r0:r0 + chunk, :] = (acc * _sigmoid(acc)).astype(o_ref.dtype)

    for r0 in range(0, seq, chunk):
        matmul(r0)
        finish(r0)


def _proj(x, w, *, batch, seq, tn, col0=0, n_dim=None, conv=None, out_dtype=BF16):
    k_dim = w.shape[0]
    n_dim = w.shape[1] - col0 if n_dim is None else n_dim
    assert n_dim % tn == 0 and tn % LANES == 0 and col0 % tn == 0
    j0 = col0 // tn
    in_specs = [pl.BlockSpec((seq, k_dim), lambda b, j: (b, 0)),
                pl.BlockSpec((k_dim, tn), lambda b, j: (0, j0 + j))]
    if conv is None:
        body = functools.partial(_proj_silu_kernel, chunk=CONV_ROW_CHUNK)
        args, scratch, name = (x, w), [], "proj_silu"
    else:
        body = functools.partial(_proj_conv_silu_kernel, chunk=CONV_ROW_CHUNK)
        args = (x, w) + tuple(conv)
        in_specs += [pl.BlockSpec((c.shape[0], tn), lambda b, j: (0, j)) for c in conv]
        assert conv[0].shape[0] - 1 <= SUBLANES
        scratch, name = [pltpu.VMEM((SUBLANES + seq, tn), F32)], "proj_conv_silu"
    return pl.pallas_call(
        body,
        grid=(batch, n_dim // tn),
        in_specs=in_specs,
        out_specs=pl.BlockSpec((seq, tn), lambda b, j: (b, j)),
        out_shape=jax.ShapeDtypeStruct((batch * seq, n_dim), out_dtype),
        scratch_shapes=scratch,
        compiler_params=_cparams(2),
        name=name,
    )(*args)


def _dt_kernel(x_ref, w_ref, b_ref, spread_ref, o_ref, xbf_ref):
    x = x_ref[...].astype(BF16)
    xbf_ref[...] = x
    y = jnp.dot(x, w_ref[...], preferred_element_type=F32) + b_ref[...]
    y = jnp.maximum(y, 0.0) + jnp.log1p(jnp.exp(-jnp.abs(y)))
    spread = spread_ref[...]
    hi = y.astype(BF16)
    rest = y - hi.astype(F32)
    mid = rest.astype(BF16)
    lo = (rest - mid.astype(F32)).astype(BF16)
    o_ref[...] = (jnp.dot(hi, spread, preferred_element_type=F32)
                  + jnp.dot(mid, spread, preferred_element_type=F32)
                  + jnp.dot(lo, spread, preferred_element_type=F32))


def _dt_proj(x, w, bias, spread, *, tm):
    rows, k_dim = x.shape
    width = spread.shape[1]
    return pl.pallas_call(
        _dt_kernel,
        grid=(rows // tm,),
        in_specs=[pl.BlockSpec((tm, k_dim), lambda i: (i, 0)),
                  pl.BlockSpec((k_dim, LANES), lambda i: (0, 0)),
                  pl.BlockSpec((1, LANES), lambda i: (0, 0)),
                  pl.BlockSpec((LANES, width), lambda i: (0, 0))],
        out_specs=[pl.BlockSpec((tm, width), lambda i: (i, 0)),
                   pl.BlockSpec((tm, k_dim), lambda i: (i, 0))],
        out_shape=[jax.ShapeDtypeStruct((rows, width), F32),
                   jax.ShapeDtypeStruct((rows, k_dim), BF16)],
        compiler_params=_cparams(1),
        name="dt_proj",
    )(x, w, bias, spread)


def _cumsum_rows(v):
    n = v.shape[0]
    row = lax.broadcasted_iota(jnp.int32, v.shape, 0)
    shift = 1
    while shift < n:
        v = v + jnp.where(row >= shift, pltpu.roll(v, shift, 0), 0.0)
        shift *= 2
    return v


def _ssd_kernel(xs_ref, b_ref, c_ref, z_ref, dt_ref, alog_ref, dskip_ref, ng_ref, e_ref, o_ref,
                state_ref, *, groups, heads_per_group, head_dim):
    @pl.when(pl.program_id(2) == 0)
    def _():
        state_ref[...] = jnp.zeros_like(state_ref)

    gw = e_ref.shape[1]
    n = b_ref.shape[1] // groups
    for s in range(groups):
        g = pl.program_id(1) * groups + s
        lanes = slice(s * gw, (s + 1) * gw)
        _ssd_group(xs_ref.at[:, lanes], b_ref.at[:, s * n:(s + 1) * n], c_ref.at[:, s * n:(s + 1) * n],
                   z_ref.at[:, lanes], dt_ref.at[:, s * LANES:(s + 1) * LANES], alog_ref[pl.ds(g, 1), :],
                   dskip_ref[pl.ds(g, 1), :], ng_ref.at[:, lanes], e_ref, o_ref.at[:, lanes], state_ref.at[s],
                   heads_per_group=heads_per_group, head_dim=head_dim)


def _ssd_group(xs_ref, b_ref, c_ref, z_ref, dt_ref, alog, dskip, ng_ref, e_ref, o_ref, state_ref,
               *, heads_per_group, head_dim):
    q = xs_ref.shape[0]
    xs = xs_ref[...].astype(F32)
    bm = b_ref[...]
    cm = c_ref[...]
    dt = dt_ref[...]
    a = -jnp.exp(alog) * LOG2_E
    acs = _cumsum_rows(dt * a)
    acs_t = acs.T
    last = acs[q - 1:q, :]
    expand_mat = e_ref[...]

    def expand(v):
        return jnp.dot(v.astype(BF16), expand_mat, preferred_element_type=F32)

    def expand_hi_lo(v):
        hi = v.astype(BF16)
        lo = (v - hi.astype(F32)).astype(BF16)
        return (jnp.dot(hi, expand_mat, preferred_element_type=F32)
                + jnp.dot(lo, expand_mat, preferred_element_type=F32))

    decay_in_x = expand_hi_lo(jnp.exp2(acs))
    decay_out_dt_x = expand(jnp.exp2(last - acs) * dt)

    bc = lax.dot_general(bm, cm, (((1,), (1,)), ((), ())), preferred_element_type=F32)
    blk = LANES
    tri = (lax.broadcasted_iota(jnp.int32, (blk, blk), 1) >= lax.broadcasted_iota(jnp.int32, (blk, blk), 0))
    xs_t = xs.T
    dt_t = dt.T

    def masked_bc(r):
        rows = []
        for s0 in range(0, q, blk):
            parts = []
            if s0 > 0:
                parts.append(jnp.zeros((blk, s0), BF16))
            diff = acs_t[r:r + 1, s0:s0 + blk] - acs[s0:s0 + blk, r:r + 1]
            decay = jnp.exp2(jnp.where(tri, diff, -jnp.inf))
            parts.append((bc[s0:s0 + blk, s0:s0 + blk] * decay).astype(BF16))
            if s0 + blk < q:
                diff = acs_t[r:r + 1, s0 + blk:] - acs[s0:s0 + blk, r:r + 1]
                parts.append((bc[s0:s0 + blk, s0 + blk:] * jnp.exp2(diff)).astype(BF16))
            rows.append(jnp.concatenate(parts, axis=1) if len(parts) > 1 else parts[0])
        return jnp.concatenate(rows, axis=0) if len(rows) > 1 else rows[0]

    y_t = []
    for r in range(heads_per_group):
        x_head_t = (xs_t[r * head_dim:(r + 1) * head_dim, :] * dt_t[r:r + 1, :]).astype(BF16)
        y_t.append(jnp.dot(x_head_t, masked_bc(r), preferred_element_type=F32))
    y_diag = jnp.concatenate(y_t, axis=0).T

    s_prev = state_ref[...]
    y_off = jnp.dot(cm, s_prev.astype(BF16), preferred_element_type=F32) * decay_in_x
    y = y_diag + y_off + dskip * xs
    b_t = bm.astype(F32).T.astype(BF16)
    s_new = jnp.dot(b_t, (xs * decay_out_dt_x).astype(BF16), preferred_element_type=F32)
    state_ref[...] = s_prev * decay_in_x[q - 1:q, :] + s_new

    yg = y * z_ref[...].astype(F32)
    ms = jnp.mean(yg * yg, axis=-1, keepdims=True)
    o_ref[...] = (yg * lax.rsqrt(ms + RMS_EPS) * ng_ref[...]).astype(o_ref.dtype)


def _ssd_scan(xbc, z, dt, alog, dskip, norm_g, expand_mat, *, batch, seq, d_inner):
    n_groups, d_state, chunk = SSD_N_GROUPS, SSD_D_STATE, math.gcd(SSD_CHUNK, seq)
    gw = d_inner // n_groups
    heads_per_group = gw // SSD_HEAD_DIM
    n_chunks = seq // chunk
    gps = SSD_GROUPS_PER_STEP
    assert n_groups % gps == 0
    b_col0 = d_inner // (gps * d_state)
    c_col0 = b_col0 + n_groups // gps
    row = lambda b, g, c: b * n_chunks + c
    return pl.pallas_call(
        functools.partial(_ssd_kernel, groups=gps, heads_per_group=heads_per_group, head_dim=SSD_HEAD_DIM),
        grid=(batch, n_groups // gps, n_chunks),
        in_specs=[
            pl.BlockSpec((chunk, gps * gw), lambda b, g, c: (row(b, g, c), g)),
            pl.BlockSpec((chunk, gps * d_state), lambda b, g, c: (row(b, g, c), b_col0 + g)),
            pl.BlockSpec((chunk, gps * d_state), lambda b, g, c: (row(b, g, c), c_col0 + g)),
            pl.BlockSpec((chunk, gps * gw), lambda b, g, c: (row(b, g, c), g)),
            pl.BlockSpec((chunk, gps * LANES), lambda b, g, c: (row(b, g, c), g)),
            pl.BlockSpec((n_groups, LANES), lambda b, g, c: (0, 0)),
            pl.BlockSpec((n_groups, gw), lambda b, g, c: (0, 0)),
            pl.BlockSpec((1, gps * gw), lambda b, g, c: (0, g)),
            pl.BlockSpec((LANES, gw), lambda b, g, c: (0, 0)),
        ],
        out_specs=pl.BlockSpec((chunk, gps * gw), lambda b, g, c: (row(b, g, c), g)),
        out_shape=jax.ShapeDtypeStruct((batch * seq, d_inner), BF16),
        scratch_shapes=[pltpu.VMEM((gps, d_state, gw), F32)],
        compiler_params=_cparams(3),
        name="ssd_scan",
    )(xbc, xbc, xbc, z, dt, alog, dskip, norm_g, expand_mat)


def _outproj_ln_kernel(y_ref, w_ref, x_ref, g_ref, b_ref, o_ref, obf_ref, *, alpha, chunk):
    for r0 in range(0, y_ref.shape[0], chunk):
        rows = slice(r0, r0 + chunk)
        h = jnp.dot(y_ref[rows, :], w_ref[...], preferred_element_type=F32)
        v = alpha * x_ref[rows, :] + h
        mu = jnp.mean(v, axis=-1, keepdims=True)
        d = v - mu
        var = jnp.mean(d * d, axis=-1, keepdims=True)
        out = d * lax.rsqrt(var + LN_EPS) * g_ref[...] + b_ref[...]
        o_ref[rows, :] = out
        obf_ref[rows, :] = out.astype(BF16)


def _outproj_ln(y, w, x, ln_g, ln_b, *, alpha, tm):
    m, k_dim = y.shape
    d = w.shape[1]
    return pl.pallas_call(
        functools.partial(_outproj_ln_kernel, alpha=alpha, chunk=LN_ROW_CHUNK),
        grid=(m // tm,),
        in_specs=[pl.BlockSpec((tm, k_dim), lambda i: (i, 0)),
                  pl.BlockSpec((k_dim, d), lambda i: (0, 0), pipeline_mode=pl.Buffered(1)),
                  pl.BlockSpec((tm, d), lambda i: (i, 0)),
                  pl.BlockSpec((1, d), lambda i: (0, 0)),
                  pl.BlockSpec((1, d), lambda i: (0, 0))],
        out_specs=[pl.BlockSpec((tm, d), lambda i: (i, 0)),
                   pl.BlockSpec((tm, d), lambda i: (i, 0))],
        out_shape=[jax.ShapeDtypeStruct((m, d), F32), jax.ShapeDtypeStruct((m, d), BF16)],
        compiler_params=_cparams(1),
        name="outproj_ln",
    )(y, w, x, ln_g, ln_b)


def _dsa_latent_kernel(x_ref, w_ref, qg_ref, kvg_ref, cqt_ref, ckv_ref, ckvt_ref, slab_ref, slabt_ref):
    p = jnp.dot(x_ref[...], w_ref[...], preferred_element_type=F32)
    q_rank = cqt_ref.shape[1]
    kv_rank = ckv_ref.shape[1]
    cq = p[:, :q_rank]
    ckv = p[:, q_rank:q_rank + kv_rank]
    slab = p[:, q_rank + kv_rank:]
    cq = cq * lax.rsqrt(jnp.mean(cq * cq, axis=-1, keepdims=True) + RMS_EPS) * qg_ref[...]
    ckv = ckv * lax.rsqrt(jnp.mean(ckv * ckv, axis=-1, keepdims=True) + RMS_EPS) * kvg_ref[...]
    cqt_ref[0] = cq.T.astype(BF16)
    ckv_ref[...] = ckv.astype(BF16)
    ckvt_ref[0, :kv_rank, :] = ckv.T.astype(BF16)
    ckvt_ref[0, kv_rank:, :] = jnp.ones((ckvt_ref.shape[1] - kv_rank, ckvt_ref.shape[2]), BF16)
    slab_ref[...] = slab.astype(BF16)
    slabt_ref[0] = slab.T


def _dsa_latent(x, w, q_gain, kv_gain, *, batch, seq):
    tm = IDX_TQ
    n_t = seq // tm
    k_dim, n_dim = w.shape
    return pl.pallas_call(
        _dsa_latent_kernel,
        grid=(batch, n_t),
        in_specs=[pl.BlockSpec((tm, k_dim), lambda b, t: (b * n_t + t, 0)),
                  pl.BlockSpec((k_dim, n_dim), lambda b, t: (0, 0)),
                  pl.BlockSpec((1, DSA_Q_RANK), lambda b, t: (0, 0)),
                  pl.BlockSpec((1, DSA_KV_RANK), lambda b, t: (0, 0))],
        out_specs=[pl.BlockSpec((1, DSA_Q_RANK, tm), lambda b, t: (b, 0, t)),
                   pl.BlockSpec((tm, DSA_KV_RANK), lambda b, t: (b * n_t + t, 0)),
                   pl.BlockSpec((1, DSA_KV_RANK + ONES_ROWS, tm), lambda b, t: (b, 0, t)),
                   pl.BlockSpec((tm, LANES), lambda b, t: (b * n_t + t, 0)),
                   pl.BlockSpec((1, LANES, tm), lambda b, t: (b, 0, t))],
        out_shape=[jax.ShapeDtypeStruct((batch, DSA_Q_RANK, seq), BF16),
                   jax.ShapeDtypeStruct((batch * seq, DSA_KV_RANK), BF16),
                   jax.ShapeDtypeStruct((batch, DSA_KV_RANK + ONES_ROWS, seq), BF16),
                   jax.ShapeDtypeStruct((batch * seq, LANES), BF16),
                   jax.ShapeDtypeStruct((batch, LANES, seq), F32)],
        compiler_params=_cparams(2),
        name="dsa_latent",
    )(x, w, q_gain, kv_gain)


def _dsa_query_kernel(cqt_ref, wuqt_ref, wuk_ref, ql_ref, *, scale):
    hb, _, hd = wuk_ref.shape
    tq = ql_ref.shape[3] // hb
    q_t = jnp.dot(wuqt_ref[...], cqt_ref[0], preferred_element_type=F32).astype(BF16)
    for j in range(hb):
        ql_t = jnp.dot(wuk_ref[j], q_t[j * hd:(j + 1) * hd, :], preferred_element_type=F32)
        ql_t = (ql_t * scale).astype(BF16)
        for blk in range(ql_ref.shape[1]):
            ql_ref[0, blk, :, j * tq:(j + 1) * tq] = ql_t[:, blk * tq:(blk + 1) * tq]


def _dsa_query(cq_t, wuq_t, wuk, *, batch, seq, scale):
    tm = min(seq, 1024)
    hb = 8
    tq = ATT_TQ
    n_t = seq // tm
    n_heads, kv_rank, hd = wuk.shape
    return pl.pallas_call(
        functools.partial(_dsa_query_kernel, scale=scale),
        grid=(batch, n_t, n_heads // hb),
        in_specs=[pl.BlockSpec((1, DSA_Q_RANK, tm), lambda b, t, h: (b, 0, t)),
                  pl.BlockSpec((hb * hd, DSA_Q_RANK), lambda b, t, h: (h, 0)),
                  pl.BlockSpec((hb, kv_rank, hd), lambda b, t, h: (h, 0, 0))],
        out_specs=pl.BlockSpec((1, tm // tq, kv_rank, hb * tq), lambda b, t, h: (b, t, 0, h)),
        out_shape=jax.ShapeDtypeStruct((batch, seq // tq, kv_rank, n_heads * tq), BF16),
        compiler_params=_cparams(3),
        name="dsa_query",
    )(cq_t, wuq_t, wuk)


def _indexer_kernel(keys_ref, slabt_ref, cqt_ref, widx_ref, bias_ref, qt_scr, sc_scr, *, k_top, w_scale, n_iter):
    i = pl.program_id(1)
    n_t, kt_w, tq = sc_scr.shape
    cq_t = cqt_ref[0]
    for h in range(IDX_N_HEADS):
        qt_scr[h] = jnp.dot(widx_ref[h], cq_t, preferred_element_type=F32).astype(BF16)
    w_t = slabt_ref[0][IDX_HEAD_DIM:IDX_HEAD_DIM + IDX_N_HEADS, :] * w_scale
    q_pos = i * tq + lax.broadcasted_iota(jnp.int32, (kt_w, tq), 1)

    for kt in range(n_t):
        @pl.when(kt <= i)
        def _(kt=kt):
            keys = keys_ref[kt * kt_w:(kt + 1) * kt_w, :]
            acc = jnp.zeros((kt_w, tq), F32)
            for h in range(IDX_N_HEADS):
                s = jnp.dot(keys, qt_scr[h], preferred_element_type=F32)
                acc = acc + jnp.maximum(s, 0.0) * w_t[h:h + 1, :]
            k_pos = kt * kt_w + lax.broadcasted_iota(jnp.int32, (kt_w, tq), 0)
            sc_scr[kt] = jnp.where(k_pos <= q_pos, acc, -jnp.inf)

    def bounds(kt, carry):
        lo, hi = carry
        s = sc_scr[kt]
        hi = jnp.maximum(hi, jnp.max(s, axis=0, keepdims=True))
        lo = jnp.minimum(lo, jnp.min(jnp.where(s == -jnp.inf, jnp.inf, s), axis=0, keepdims=True))
        return lo, hi

    lo, hi = lax.fori_loop(0, i + 1, bounds,
                           (jnp.full((1, tq), jnp.inf, F32), jnp.full((1, tq), -jnp.inf, F32)))

    def halve(carry):
        it, lo, hi, n_lo = carry
        mid = 0.5 * lo + 0.5 * hi

        mid_rows = jnp.broadcast_to(mid, (SUBLANES, tq))

        def count(kt, cnt):
            for r0 in range(0, kt_w, SUBLANES):
                cnt = jnp.where(sc_scr[kt, r0:r0 + SUBLANES, :] >= mid_rows, cnt + 1.0, cnt)
            return cnt

        cnt = lax.fori_loop(0, i + 1, count, jnp.zeros((SUBLANES, tq), F32))
        cnt = jnp.sum(cnt, axis=0, keepdims=True)
        enough = cnt >= k_top
        return it + 1, jnp.where(enough, mid, lo), jnp.where(enough, hi, mid), jnp.where(enough, cnt, n_lo)

    def unresolved(carry):
        it, _, _, n_lo = carry
        return jnp.logical_and(it < n_iter, jnp.max(n_lo) > k_top)

    n_causal = (i * tq + lax.broadcasted_iota(jnp.int32, (1, tq), 1) + 1).astype(F32)
    _, lo, hi, _ = lax.while_loop(unresolved, lambda c: halve(halve(c)), (jnp.int32(0), lo, hi, n_causal))

    for kt in range(n_t):
        @pl.when(kt <= i)
        def _(kt=kt):
            bias_ref[0, kt * kt_w:(kt + 1) * kt_w, :] = jnp.where(sc_scr[kt] >= lo, 0.0, -jnp.inf)

        @pl.when(kt > i)
        def _(kt=kt):
            bias_ref[0, kt * kt_w:(kt + 1) * kt_w, :] = jnp.full((kt_w, tq), -jnp.inf, F32)


def _dsa_indexer(keys, slab_t, cq_t, widx_t, *, batch, seq, k_top, w_scale):
    tq = IDX_TQ
    n_t = seq // tq
    return pl.pallas_call(
        functools.partial(_indexer_kernel, k_top=k_top, w_scale=w_scale, n_iter=BISECT_ITERS),
        grid=(batch, n_t),
        in_specs=[pl.BlockSpec((seq, LANES), lambda b, i: (b, 0)),
                  pl.BlockSpec((1, LANES, tq), lambda b, i: (b, 0, i)),
                  pl.BlockSpec((1, DSA_Q_RANK, tq), lambda b, i: (b, 0, i)),
                  pl.BlockSpec(widx_t.shape, lambda b, i: (0, 0, 0))],
        out_specs=pl.BlockSpec((1, seq, tq), lambda b, i: (b, 0, i)),
        out_shape=jax.ShapeDtypeStruct((batch, seq, seq), F32),
        scratch_shapes=[pltpu.VMEM((IDX_N_HEADS, LANES, tq), BF16),
                        pltpu.VMEM((n_t, tq, tq), F32)],
        compiler_params=_cparams(2),
        name="dsa_indexer",
    )(keys, slab_t, cq_t, widx_t)


def _attn_kernel(ql_ref, bias_ref, k_ref, vt_ref, wuvt_ref, gate_ref, *rest, n_keys, hc):
    o_ref = rest[-1]
    tq = bias_ref.shape[2]
    hb, hd, kv_rank = wuvt_ref.shape
    cols = hc * tq
    n_chunks = hb // hc
    keys = k_ref[:n_keys, :]
    bias_t = bias_ref[0]
    bias_rep = jnp.concatenate([bias_t] * hc, axis=1) if hc > 1 else bias_t

    def scores(c):
        q_t = ql_ref[0, 0, :, c * cols:(c + 1) * cols]
        return jnp.dot(keys, q_t, preferred_element_type=F32) + bias_rep

    def finish(c, s):
        m = jnp.max(s, axis=0, keepdims=True)
        p = jnp.exp2(s - m).astype(BF16)
        o_aug = jnp.dot(vt_ref[0], p, preferred_element_type=F32)
        o_t = (o_aug[:kv_rank] / o_aug[kv_rank:kv_rank + 1]).astype(BF16)
        for j in range(hc):
            h = c * hc + j
            oh_t = jnp.dot(wuvt_ref[h], o_t[:, j * tq:(j + 1) * tq], preferred_element_type=F32)
            gate = gate_ref[:, h * hd:(h + 1) * hd].astype(F32)
            o_ref[:, h * hd:(h + 1) * hd] = (oh_t.T * gate).astype(o_ref.dtype)

    s_next = scores(0)
    for c in range(n_chunks):
        s_cur = s_next
        if c + 1 < n_chunks:
            s_next = scores(c + 1)
        finish(c, s_cur)


def _dsa_attention(ql, bias, ckv_t, ckv, wuv_t, gate, *, batch, seq):
    tq = ATT_TQ
    n_q = seq // tq
    n_heads, hd, kv_rank = wuv_t.shape
    width = n_heads * hd
    og = None
    for i in range(n_q):
        n_keys = (i + 1) * tq
        hb = n_heads
        while hb > 1 and hb * tq * n_keys > ATT_STEP_ELEMS:
            hb //= 2
        hc = min(hb, ATT_HEADS_PER_CHUNK)
        in_specs = [pl.BlockSpec((1, 1, kv_rank, hb * tq), lambda b, g, i=i: (b, i, 0, g)),
                    pl.BlockSpec((1, n_keys, tq), lambda b, g, i=i: (b, 0, i)),
                    pl.BlockSpec((seq, kv_rank), lambda b, g: (b, 0)),
                    pl.BlockSpec((1, kv_rank + ONES_ROWS, n_keys), lambda b, g: (b, 0, 0)),
                    pl.BlockSpec((hb, hd, kv_rank), lambda b, g: (g, 0, 0)),
                    pl.BlockSpec((tq, hb * hd), lambda b, g, i=i: (b * n_q + i, g))]
        args = [ql, bias, ckv, ckv_t, wuv_t, gate]
        aliases = {}
        if og is not None:
            in_specs.append(pl.BlockSpec(memory_space=pl.ANY))
            args.append(og)
            aliases = {len(args) - 1: 0}
        og = pl.pallas_call(
            functools.partial(_attn_kernel, n_keys=n_keys, hc=hc),
            grid=(batch, n_heads // hb),
            in_specs=in_specs,
            out_specs=pl.BlockSpec((tq, hb * hd), lambda b, g, i=i: (b * n_q + i, g)),
            out_shape=jax.ShapeDtypeStruct((batch * seq, width), BF16),
            input_output_aliases=aliases,
            compiler_params=_cparams(2),
            name="dsa_attention_q%d" % i,
        )(*args)
    return og


def _ssd_layer(x, w_in, conv_w, conv_b, dt_bias, a_log, d_skip, norm_g, w_out, ln_g, ln_b,
               *, batch, seq, alpha):
    d_inner = w_out.shape[0]
    n_heads = a_log.shape[0]
    n_groups = SSD_N_GROUPS
    heads_per_group = n_heads // n_groups
    conv_dim = conv_w.shape[1]
    gw = d_inner // n_groups
    assert gw % LANES == 0 and heads_per_group <= LANES and SSD_D_STATE == LANES
    assert LANES % SSD_HEAD_DIM == 0 and heads_per_group % (LANES // SSD_HEAD_DIM) == 0

    def per_group_lanes(v):
        lead = v.shape[:-1]
        v = v.reshape(lead + (n_groups, heads_per_group))
        v = jnp.pad(v, [(0, 0)] * len(lead) + [(0, 0), (0, LANES - heads_per_group)])
        return v.reshape(lead + (n_groups * LANES,))

    w_bf = w_in.astype(BF16)
    w_dt = jnp.pad(w_bf[:, d_inner + conv_dim:], ((0, 0), (0, LANES - n_heads)))
    dt_b = jnp.pad(dt_bias, (0, LANES - n_heads))[None, :]
    head_of_lane = jnp.arange(n_groups * LANES)
    head_of_lane = jnp.where(head_of_lane % LANES < heads_per_group,
                             (head_of_lane // LANES) * heads_per_group + head_of_lane % LANES, -1)
    spread = (jnp.arange(LANES)[:, None] == head_of_lane[None, :]).astype(BF16)
    alog = per_group_lanes(a_log).reshape(n_groups, LANES)
    dskip = jnp.repeat(d_skip, SSD_HEAD_DIM).reshape(n_groups, gw)
    lane_of = jnp.arange(gw) // SSD_HEAD_DIM
    expand_mat = (jnp.arange(LANES)[:, None] == lane_of[None, :]).astype(BF16)

    dt, x_bf = _dt_proj(x, w_dt, dt_b, spread, tm=min(seq, 1024))
    z = _proj(x_bf, w_bf, batch=batch, seq=seq, tn=512, n_dim=d_inner)
    xbc = _proj(x_bf, w_bf, batch=batch, seq=seq, tn=1024, col0=d_inner, n_dim=conv_dim,
                conv=(conv_w, conv_b[None, :]))
    yn = _ssd_scan(xbc, z, dt, alog, dskip, norm_g[None, :], expand_mat, batch=batch, seq=seq, d_inner=d_inner)
    return _outproj_ln(yn, w_out.astype(BF16), x, ln_g[None, :], ln_b[None, :], alpha=alpha, tm=512)


def _dsa_layer(x, x_bf, w_in, q_norm_g, kv_norm_g, w_uq, w_uk, w_uv, w_idx_q, w_out, ln_g, ln_b,
               *, batch, seq, alpha):
    assert seq % IDX_TQ == 0 and IDX_TQ % ATT_TQ == 0 and IDX_HEAD_DIM + IDX_N_HEADS <= LANES
    small = DSA_Q_RANK + DSA_KV_RANK + IDX_HEAD_DIM + IDX_N_HEADS
    w_small = jnp.pad(w_in[:, :small], ((0, 0), (0, DSA_Q_RANK + DSA_KV_RANK + LANES - small))).astype(BF16)
    w_gate = w_in[:, small:].astype(BF16)
    wuq_t = w_uq.T.astype(BF16)
    wuv_t = w_uv.transpose(0, 2, 1).astype(BF16)
    widx_t = w_idx_q.reshape(DSA_Q_RANK, IDX_N_HEADS, IDX_HEAD_DIM).transpose(1, 2, 0)
    widx_t = jnp.pad(widx_t, ((0, 0), (0, LANES - IDX_HEAD_DIM), (0, 0))).astype(BF16)
    k_top = min(IDX_TOPK, seq // 4)

    cq_t, ckv, ckv_t, slab, slab_t = _dsa_latent(x_bf, w_small, q_norm_g[None, :], kv_norm_g[None, :],
                                                     batch=batch, seq=seq)
    gate = _proj(x_bf, w_gate, batch=batch, seq=seq, tn=512)
    ql = _dsa_query(cq_t, wuq_t, w_uk.astype(BF16), batch=batch, seq=seq,
                    scale=DSA_HEAD_DIM ** -0.5 * math.log2(math.e))
    bias = _dsa_indexer(slab, slab_t, cq_t, widx_t, batch=batch, seq=seq, k_top=k_top,
                        w_scale=IDX_N_HEADS ** -0.5 * IDX_HEAD_DIM ** -0.5)
    og = _dsa_attention(ql, bias, ckv_t, ckv, wuv_t, gate, batch=batch, seq=seq)
    return _outproj_ln(og, w_out.astype(BF16), x, ln_g[None, :], ln_b[None, :], alpha=alpha, tm=512)


def kernel(x, ssd_w_in, ssd_conv_w, ssd_conv_b, ssd_dt_bias, ssd_a_log, ssd_d_skip, ssd_norm_g, ssd_w_out,
           dsa_w_in, dsa_q_norm_g, dsa_kv_norm_g, dsa_w_uq, dsa_w_uk, dsa_w_uv, dsa_w_idx_q, dsa_w_out,
           ln_g, ln_b):
    batch, seq, d_model = x.shape
    depth = ln_g.shape[0]
    alpha = (2.0 * depth) ** 0.25
    xf = x.reshape(batch * seq, d_model)
    x_bf = None
    for i in range(depth):
        j = i // 2
        if i % 2 == 0:
            xf, x_bf = _ssd_layer(xf, ssd_w_in[j], ssd_conv_w[j], ssd_conv_b[j], ssd_dt_bias[j],
                                  ssd_a_log[j], ssd_d_skip[j], ssd_norm_g[j], ssd_w_out[j], ln_g[i], ln_b[i],
                                  batch=batch, seq=seq, alpha=alpha)
        else:
            x_bf = xf.astype(BF16) if x_bf is None else x_bf
            xf, x_bf = _dsa_layer(xf, x_bf, dsa_w_in[j], dsa_q_norm_g[j], dsa_kv_norm_g[j], dsa_w_uq[j],
                                  dsa_w_uk[j], dsa_w_uv[j], dsa_w_idx_q[j], dsa_w_out[j], ln_g[i], ln_b[i],
                                  batch=batch, seq=seq, alpha=alpha)
    return xf.reshape(batch, seq, d_model)
```

```python
import functools
import math

import jax
import jax.numpy as jnp
from jax import lax
from jax.experimental import pallas as pl
from jax.experimental.pallas import tpu as pltpu

F32 = jnp.float32
BF16 = jnp.bfloat16

V7X_VMEM_BYTES = 64 * 1024 * 1024
LANES = 128
SUBLANES = 8
VMEM_LIMIT_BYTES = V7X_VMEM_BYTES - 8 * 1024 * 1024

LN_EPS = 1e-5
RMS_EPS = 1e-6

SSD_HEAD_DIM = 64
SSD_N_GROUPS = 8
SSD_D_STATE = 128
SSD_CHUNK = 256
DSA_N_HEADS = 32
DSA_HEAD_DIM = 128
DSA_Q_RANK = 512
DSA_KV_RANK = 256
IDX_N_HEADS = 16
IDX_HEAD_DIM = 64
IDX_TOPK = 256

ATT_TQ = 256
ATT_STEP_ELEMS = 8 * 1024 * 1024
ATT_CHUNK_COLS = 2048
IDX_TQ = 512
BISECT_ITERS = 40
ONES_ROWS = 16
CONV_ROW_CHUNK = 256
LN_ROW_CHUNK = 128
SSD_GROUPS_PER_STEP = 2


def _cparams(n_axes, flags=None):
    return pltpu.CompilerParams(dimension_semantics=("arbitrary",) * n_axes,
                                vmem_limit_bytes=VMEM_LIMIT_BYTES, flags=flags)


LOG2_E = math.log2(math.e)


def _sigmoid(v):
    return 1.0 / (1.0 + jnp.exp2(v * (-LOG2_E)))


def _proj_silu_kernel(x_ref, w_ref, o_ref, *, chunk):
    w = w_ref[...]
    for r0 in range(0, o_ref.shape[0], chunk):
        y = jnp.dot(x_ref[r0:r0 + chunk, :], w, preferred_element_type=F32)
        o_ref[r0:r0 + chunk, :] = (y * _sigmoid(y)).astype(o_ref.dtype)


def _proj_conv_silu_kernel(x_ref, w_ref, conv_w_ref, conv_b_ref, o_ref, y_scr, *, chunk):
    seq, tn = o_ref.shape
    k_taps = conv_w_ref.shape[0]
    head = y_scr.shape[0] - seq
    y_scr[:head, :] = jnp.zeros((head, tn), F32)
    w = w_ref[...]

    def matmul(r0):
        y_scr[head + r0:head + r0 + chunk, :] = jnp.dot(x_ref[r0:r0 + chunk, :], w, preferred_element_type=F32)

    def finish(r0):
        acc = conv_b_ref[...] + conv_w_ref[k_taps - 1:k_taps, :] * y_scr[head + r0:head + r0 + chunk, :]
        for back in range(1, k_taps):
            tap = conv_w_ref[k_taps - 1 - back:k_taps - back, :]
            acc = acc + tap * y_scr[head + r0 - back:head + r0 - back + chunk, :]
        o_ref[r0:r0 + chunk, :] = (acc * _sigmoid(acc)).astype(o_ref.dtype)

    for r0 in range(0, seq, chunk):
        matmul(r0)
        finish(r0)


def _proj(x, w, *, batch, seq, tn, col0=0, n_dim=None, conv=None, out_dtype=BF16):
    k_dim = w.shape[0]
    n_dim = w.shape[1] - col0 if n_dim is None else n_dim
    assert n_dim % tn == 0 and tn % LANES == 0 and col0 % tn == 0
    j0 = col0 // tn
    in_specs = [pl.BlockSpec((seq, k_dim), lambda b, j: (b, 0)),
                pl.BlockSpec((k_dim, tn), lambda b, j: (0, j0 + j))]
    if conv is None:
        body = functools.partial(_proj_silu_kernel, chunk=CONV_ROW_CHUNK)
        args, scratch, name = (x, w), [], "proj_silu"
    else:
        body = functools.partial(_proj_conv_silu_kernel, chunk=CONV_ROW_CHUNK)
        args = (x, w) + tuple(conv)
        in_specs += [pl.BlockSpec((c.shape[0], tn), lambda b, j: (0, j)) for c in conv]
        assert conv[0].shape[0] - 1 <= SUBLANES
        scratch, name = [pltpu.VMEM((SUBLANES + seq, tn), F32)], "proj_conv_silu"
    return pl.pallas_call(
        body,
        grid=(batch, n_dim // tn),
        in_specs=in_specs,
        out_specs=pl.BlockSpec((seq, tn), lambda b, j: (b, j)),
        out_shape=jax.ShapeDtypeStruct((batch * seq, n_dim), out_dtype),
        scratch_shapes=scratch,
        compiler_params=_cparams(2),
        name=name,
    )(*args)


def _dt_kernel(x_ref, w_ref, b_ref, spread_ref, o_ref, xbf_ref):
    x = x_ref[...].astype(BF16)
    xbf_ref[...] = x
    y = jnp.dot(x, w_ref[...], preferred_element_type=F32) + b_ref[...]
    y = jnp.maximum(y, 0.0) + jnp.log1p(jnp.exp(-jnp.abs(y)))
    spread = spread_ref[...]
    hi = y.astype(BF16)
    rest = y - hi.astype(F32)
    mid = rest.astype(BF16)
    lo = (rest - mid.astype(F32)).astype(BF16)
    o_ref[...] = (jnp.dot(hi, spread, preferred_element_type=F32)
                  + jnp.dot(mid, spread, preferred_element_type=F32)
                  + jnp.dot(lo, spread, preferred_element_type=F32))


def _dt_proj(x, w, bias, spread, *, tm):
    rows, k_dim = x.shape
    width = spread.shape[1]
    return pl.pallas_call(
        _dt_kernel,
        grid=(rows // tm,),
        in_specs=[pl.BlockSpec((tm, k_dim), lambda i: (i, 0)),
                  pl.BlockSpec((k_dim, LANES), lambda i: (0, 0)),
                  pl.BlockSpec((1, LANES), lambda i: (0, 0)),
                  pl.BlockSpec((LANES, width), lambda i: (0, 0))],
        out_specs=[pl.BlockSpec((tm, width), lambda i: (i, 0)),
                   pl.BlockSpec((tm, k_dim), lambda i: (i, 0))],
        out_shape=[jax.ShapeDtypeStruct((rows, width), F32),
                   jax.ShapeDtypeStruct((rows, k_dim), BF16)],
        compiler_params=_cparams(1),
        name="dt_proj",
    )(x, w, bias, spread)


def _cumsum_rows(v):
    n = v.shape[0]
    row = lax.broadcasted_iota(jnp.int32, v.shape, 0)
    shift = 1
    while shift < n:
        v = v + jnp.where(row >= shift, pltpu.roll(v, shift, 0), 0.0)
        shift *= 2
    return v


def _ssd_kernel(xs_ref, b_ref, c_ref, z_ref, dt_ref, alog_ref, dskip_ref, ng_ref, e_ref, o_ref,
                state_ref, *, groups, heads_per_group, head_dim):
    @pl.when(pl.program_id(2) == 0)
    def _():
        state_ref[...] = jnp.zeros_like(state_ref)

    gw = e_ref.shape[1]
    n = b_ref.shape[1] // groups
    for s in range(groups):
        g = pl.program_id(1) * groups + s
        lanes = slice(s * gw, (s + 1) * gw)
        _ssd_group(xs_ref.at[:, lanes], b_ref.at[:, s * n:(s + 1) * n], c_ref.at[:, s * n:(s + 1) * n],
                   z_ref.at[:, lanes], dt_ref.at[:, s * LANES:(s + 1) * LANES], alog_ref[pl.ds(g, 1), :],
                   dskip_ref[pl.ds(g, 1), :], ng_ref.at[:, lanes], e_ref, o_ref.at[:, lanes], state_ref.at[s],
                   heads_per_group=heads_per_group, head_dim=head_dim)


def _ssd_group(xs_ref, b_ref, c_ref, z_ref, dt_ref, alog, dskip, ng_ref, e_ref, o_ref, state_ref,
               *, heads_per_group, head_dim):
    q = xs_ref.shape[0]
    xs = xs_ref[...].astype(F32)
    bm = b_ref[...]
    cm = c_ref[...]
    dt = dt_ref[...]
    a = -jnp.exp(alog) * LOG2_E
    acs = _cumsum_rows(dt * a)
    acs_t = acs.T
    last = acs[q - 1:q, :]
    expand_mat = e_ref[...]

    def expand(v):
        return jnp.dot(v.astype(BF16), expand_mat, preferred_element_type=F32)

    def expand_hi_lo(v):
        hi = v.astype(BF16)
        lo = (v - hi.astype(F32)).astype(BF16)
        return (jnp.dot(hi, expand_mat, preferred_element_type=F32)
                + jnp.dot(lo, expand_mat, preferred_element_type=F32))

    decay_in_x = expand_hi_lo(jnp.exp2(acs))
    decay_out_dt_x = expand(jnp.exp2(last - acs) * dt)

    bc = lax.dot_general(bm, cm, (((1,), (1,)), ((), ())), preferred_element_type=F32)
    blk = LANES
    tri = (lax.broadcasted_iota(jnp.int32, (blk, blk), 1) >= lax.broadcasted_iota(jnp.int32, (blk, blk), 0))
    xs_t = xs.T
    dt_t = dt.T

    def masked_bc(r):
        rows = []
        for s0 in range(0, q, blk):
            parts = []
            if s0 > 0:
                parts.append(jnp.zeros((blk, s0), BF16))
            diff = acs_t[r:r + 1, s0:s0 + blk] - acs[s0:s0 + blk, r:r + 1]
            decay = jnp.exp2(jnp.where(tri, diff, -jnp.inf))
            parts.append((bc[s0:s0 + blk, s0:s0 + blk] * decay).astype(BF16))
            if s0 + blk < q:
                diff = acs_t[r:r + 1, s0 + blk:] - acs[s0:s0 + blk, r:r + 1]
                parts.append((bc[s0:s0 + blk, s0 + blk:] * jnp.exp2(diff)).astype(BF16))
            rows.append(jnp.concatenate(parts, axis=1) if len(parts) > 1 else parts[0])
        return jnp.concatenate(rows, axis=0) if len(rows) > 1 else rows[0]

    y_t = []
    for r in range(heads_per_group):
        x_head_t = (xs_t[r * head_dim:(r + 1) * head_dim, :] * dt_t[r:r + 1, :]).astype(BF16)
        y_t.append(jnp.dot(x_head_t, masked_bc(r), preferred_element_type=F32))
    y_diag = jnp.concatenate(y_t, axis=0).T

    s_prev = state_ref[...]
    y_off = jnp.dot(cm, s_prev.astype(BF16), preferred_element_type=F32) * decay_in_x
    y = y_diag + y_off + dskip * xs
    b_t = bm.astype(F32).T.astype(BF16)
    s_new = jnp.dot(b_t, (xs * decay_out_dt_x).astype(BF16), preferred_element_type=F32)
    state_ref[...] = s_prev * decay_in_x[q - 1:q, :] + s_new

    yg = y * z_ref[...].astype(F32)
    ms = jnp.mean(yg * yg, axis=-1, keepdims=True)
    o_ref[...] = (yg * lax.rsqrt(ms + RMS_EPS) * ng_ref[...]).astype(o_ref.dtype)


def _ssd_scan(xbc, z, dt, alog, dskip, norm_g, expand_mat, *, batch, seq, d_inner):
    n_groups, d_state, chunk = SSD_N_GROUPS, SSD_D_STATE, math.gcd(SSD_CHUNK, seq)
    gw = d_inner // n_groups
    heads_per_group = gw // SSD_HEAD_DIM
    n_chunks = seq // chunk
    gps = SSD_GROUPS_PER_STEP
    assert n_groups % gps == 0
    b_col0 = d_inner // (gps * d_state)
    c_col0 = b_col0 + n_groups // gps
    row = lambda b, g, c: b * n_chunks + c
    return pl.pallas_call(
        functools.partial(_ssd_kernel, groups=gps, heads_per_group=heads_per_group, head_dim=SSD_HEAD_DIM),
        grid=(batch, n_groups // gps, n_chunks),
        in_specs=[
            pl.BlockSpec((chunk, gps * gw), lambda b, g, c: (row(b, g, c), g)),
            pl.BlockSpec((chunk, gps * d_state), lambda b, g, c: (row(b, g, c), b_col0 + g)),
            pl.BlockSpec((chunk, gps * d_state), lambda b, g, c: (row(b, g, c), c_col0 + g)),
            pl.BlockSpec((chunk, gps * gw), lambda b, g, c: (row(b, g, c), g)),
            pl.BlockSpec((chunk, gps * LANES), lambda b, g, c: (row(b, g, c), g)),
            pl.BlockSpec((n_groups, LANES), lambda b, g, c: (0, 0)),
            pl.BlockSpec((n_groups, gw), lambda b, g, c: (0, 0)),
            pl.BlockSpec((1, gps * gw), lambda b, g, c: (0, g)),
            pl.BlockSpec((LANES, gw), lambda b, g, c: (0, 0)),
        ],
        out_specs=pl.BlockSpec((chunk, gps * gw), lambda b, g, c: (row(b, g, c), g)),
        out_shape=jax.ShapeDtypeStruct((batch * seq, d_inner), BF16),
        scratch_shapes=[pltpu.VMEM((gps, d_state, gw), F32)],
        compiler_params=_cparams(3),
        name="ssd_scan",
    )(xbc, xbc, xbc, z, dt, alog, dskip, norm_g, expand_mat)


def _outproj_ln_kernel(y_ref, w_ref, x_ref, g_ref, b_ref, o_ref, obf_ref, *, alpha, chunk):
    for r0 in range(0, y_ref.shape[0], chunk):
        rows = slice(r0, r0 + chunk)
        h = jnp.dot(y_ref[rows, :], w_ref[...], preferred_element_type=F32)
        v = alpha * x_ref[rows, :] + h
        mu = jnp.mean(v, axis=-1, keepdims=True)
        d = v - mu
        var = jnp.mean(d * d, axis=-1, keepdims=True)
        out = d * lax.rsqrt(var + LN_EPS) * g_ref[...] + b_ref[...]
        o_ref[rows, :] = out
        obf_ref[rows, :] = out.astype(BF16)


def _outproj_ln(y, w, x, ln_g, ln_b, *, alpha, tm):
    m, k_dim = y.shape
    d = w.shape[1]
    return pl.pallas_call(
        functools.partial(_outproj_ln_kernel, alpha=alpha, chunk=LN_ROW_CHUNK),
        grid=(m // tm,),
        in_specs=[pl.BlockSpec((tm, k_dim), lambda i: (i, 0)),
                  pl.BlockSpec((k_dim, d), lambda i: (0, 0), pipeline_mode=pl.Buffered(1)),
                  pl.BlockSpec((tm, d), lambda i: (i, 0)),
                  pl.BlockSpec((1, d), lambda i: (0, 0)),
                  pl.BlockSpec((1, d), lambda i: (0, 0))],
        out_specs=[pl.BlockSpec((tm, d), lambda i: (i, 0)),
                   pl.BlockSpec((tm, d), lambda i: (i, 0))],
        out_shape=[jax.ShapeDtypeStruct((m, d), F32), jax.ShapeDtypeStruct((m, d), BF16)],
        compiler_params=_cparams(1),
        name="outproj_ln",
    )(y, w, x, ln_g, ln_b)


def _dsa_latent_kernel(x_ref, w_ref, qg_ref, kvg_ref, cqt_ref, ckv_ref, ckvt_ref, slab_ref, slabt_ref):
    p = jnp.dot(x_ref[...], w_ref[...], preferred_element_type=F32)
    q_rank = cqt_ref.shape[1]
    kv_rank = ckv_ref.shape[1]
    cq = p[:, :q_rank]
    ckv = p[:, q_rank:q_rank + kv_rank]
    slab = p[:, q_rank + kv_rank:]
    cq = cq * lax.rsqrt(jnp.mean(cq * cq, axis=-1, keepdims=True) + RMS_EPS) * qg_ref[...]
    ckv = ckv * lax.rsqrt(jnp.mean(ckv * ckv, axis=-1, keepdims=True) + RMS_EPS) * kvg_ref[...]
    cqt_ref[0] = cq.T.astype(BF16)
    ckv_ref[...] = ckv.astype(BF16)
    ckvt_ref[0, :kv_rank, :] = ckv.T.astype(BF16)
    ckvt_ref[0, kv_rank:, :] = jnp.ones((ckvt_ref.shape[1] - kv_rank, ckvt_ref.shape[2]), BF16)
    slab_ref[...] = slab.astype(BF16)
    slabt_ref[0] = slab.T


def _dsa_latent(x, w, q_gain, kv_gain, *, batch, seq):
    tm = IDX_TQ
    n_t = seq // tm
    k_dim, n_dim = w.shape
    return pl.pallas_call(
        _dsa_latent_kernel,
        grid=(batch, n_t),
        in_specs=[pl.BlockSpec((tm, k_dim), lambda b, t: (b * n_t + t, 0)),
                  pl.BlockSpec((k_dim, n_dim), lambda b, t: (0, 0)),
                  pl.BlockSpec((1, DSA_Q_RANK), lambda b, t: (0, 0)),
                  pl.BlockSpec((1, DSA_KV_RANK), lambda b, t: (0, 0))],
        out_specs=[pl.BlockSpec((1, DSA_Q_RANK, tm), lambda b, t: (b, 0, t)),
                   pl.BlockSpec((tm, DSA_KV_RANK), lambda b, t: (b * n_t + t, 0)),
                   pl.BlockSpec((1, DSA_KV_RANK + ONES_ROWS, tm), lambda b, t: (b, 0, t)),
                   pl.BlockSpec((tm, LANES), lambda b, t: (b * n_t + t, 0)),
                   pl.BlockSpec((1, LANES, tm), lambda b, t: (b, 0, t))],
        out_shape=[jax.ShapeDtypeStruct((batch, DSA_Q_RANK, seq), BF16),
                   jax.ShapeDtypeStruct((batch * seq, DSA_KV_RANK), BF16),
                   jax.ShapeDtypeStruct((batch, DSA_KV_RANK + ONES_ROWS, seq), BF16),
                   jax.ShapeDtypeStruct((batch * seq, LANES), BF16),
                   jax.ShapeDtypeStruct((batch, LANES, seq), F32)],
        compiler_params=_cparams(2),
        name="dsa_latent",
    )(x, w, q_gain, kv_gain)


def _dsa_query_kernel(cqt_ref, wuqt_ref, wuk_ref, ql_ref, *, scale):
    hb, _, hd = wuk_ref.shape
    tq = ql_ref.shape[3] // hb
    q_t = jnp.dot(wuqt_ref[...], cqt_ref[0], preferred_element_type=F32).astype(BF16)
    for j in range(hb):
        ql_t = jnp.dot(wuk_ref[j], q_t[j * hd:(j + 1) * hd, :], preferred_element_type=F32)
        ql_t = (ql_t * scale).astype(BF16)
        for blk in range(ql_ref.shape[1]):
            ql_ref[0, blk, :, j * tq:(j + 1) * tq] = ql_t[:, blk * tq:(blk + 1) * tq]


def _dsa_query(cq_t, wuq_t, wuk, *, batch, seq, scale):
    tm = min(seq, 1024)
    hb = 8
    tq = ATT_TQ
    n_t = seq // tm
    n_heads, kv_rank, hd = wuk.shape
    return pl.pallas_call(
        functools.partial(_dsa_query_kernel, scale=scale),
        grid=(batch, n_t, n_heads // hb),
        in_specs=[pl.BlockSpec((1, DSA_Q_RANK, tm), lambda b, t, h: (b, 0, t)),
                  pl.BlockSpec((hb * hd, DSA_Q_RANK), lambda b, t, h: (h, 0)),
                  pl.BlockSpec((hb, kv_rank, hd), lambda b, t, h: (h, 0, 0))],
        out_specs=pl.BlockSpec((1, tm // tq, kv_rank, hb * tq), lambda b, t, h: (b, t, 0, h)),
        out_shape=jax.ShapeDtypeStruct((batch, seq // tq, kv_rank, n_heads * tq), BF16),
        compiler_params=_cparams(3),
        name="dsa_query",
    )(cq_t, wuq_t, wuk)


def _indexer_kernel(keys_ref, slabt_ref, cqt_ref, widx_ref, bias_ref, qt_scr, sc_scr, *, k_top, w_scale, n_iter):
    i = pl.program_id(1)
    n_t, kt_w, tq = sc_scr.shape
    cq_t = cqt_ref[0]
    for h in range(IDX_N_HEADS):
        qt_scr[h] = jnp.dot(widx_ref[h], cq_t, preferred_element_type=F32).astype(BF16)
    w_t = slabt_ref[0][IDX_HEAD_DIM:IDX_HEAD_DIM + IDX_N_HEADS, :] * w_scale
    q_pos = i * tq + lax.broadcasted_iota(jnp.int32, (kt_w, tq), 1)

    for kt in range(n_t):
        @pl.when(kt <= i)
        def _(kt=kt):
            keys = keys_ref[kt * kt_w:(kt + 1) * kt_w, :]
            acc = jnp.zeros((kt_w, tq), F32)
            for h in range(IDX_N_HEADS):
                s = jnp.dot(keys, qt_scr[h], preferred_element_type=F32)
                acc = acc + jnp.maximum(s, 0.0) * w_t[h:h + 1, :]
            k_pos = kt * kt_w + lax.broadcasted_iota(jnp.int32, (kt_w, tq), 0)
            sc_scr[kt] = jnp.where(k_pos <= q_pos, acc, -jnp.inf)

    def bounds(kt, carry):
        lo, hi = carry
        s = sc_scr[kt]
        hi = jnp.maximum(hi, jnp.max(s, axis=0, keepdims=True))
        lo = jnp.minimum(lo, jnp.min(jnp.where(s == -jnp.inf, jnp.inf, s), axis=0, keepdims=True))
        return lo, hi

    lo, hi = lax.fori_loop(0, i + 1, bounds,
                           (jnp.full((1, tq), jnp.inf, F32), jnp.full((1, tq), -jnp.inf, F32)))

    def halve(carry):
        it, lo, hi, n_lo = carry
        mid = 0.5 * lo + 0.5 * hi

        mid_rows = jnp.broadcast_to(mid, (SUBLANES, tq))

        def count(kt, cnt):
            for r0 in range(0, kt_w, SUBLANES):
                cnt = jnp.where(sc_scr[kt, r0:r0 + SUBLANES, :] >= mid_rows, cnt + 1.0, cnt)
            return cnt

        cnt = lax.fori_loop(0, i + 1, count, jnp.zeros((SUBLANES, tq), F32))
        cnt = jnp.sum(cnt, axis=0, keepdims=True)
        enough = cnt >= k_top
        return it + 1, jnp.where(enough, mid, lo), jnp.where(enough, hi, mid), jnp.where(enough, cnt, n_lo)

    def unresolved(carry):
        it, _, _, n_lo = carry
        return jnp.logical_and(it < n_iter, jnp.max(n_lo) > k_top)

    n_causal = (i * tq + lax.broadcasted_iota(jnp.int32, (1, tq), 1) + 1).astype(F32)
    _, lo, hi, _ = lax.while_loop(unresolved, lambda c: halve(halve(c)), (jnp.int32(0), lo, hi, n_causal))

    for kt in range(n_t):
        @pl.when(kt <= i)
        def _(kt=kt):
            bias_ref[0, kt * kt_w:(kt + 1) * kt_w, :] = jnp.where(sc_scr[kt] >= lo, 0.0, -jnp.inf)

        @pl.when(kt > i)
        def _(kt=kt):
            bias_ref[0, kt * kt_w:(kt + 1) * kt_w, :] = jnp.full((kt_w, tq), -jnp.inf, F32)


def _dsa_indexer(keys, slab_t, cq_t, widx_t, *, batch, seq, k_top, w_scale):
    tq = IDX_TQ
    n_t = seq // tq
    return pl.pallas_call(
        functools.partial(_indexer_kernel, k_top=k_top, w_scale=w_scale, n_iter=BISECT_ITERS),
        grid=(batch, n_t),
        in_specs=[pl.BlockSpec((seq, LANES), lambda b, i: (b, 0)),
                  pl.BlockSpec((1, LANES, tq), lambda b, i: (b, 0, i)),
                  pl.BlockSpec((1, DSA_Q_RANK, tq), lambda b, i: (b, 0, i)),
                  pl.BlockSpec(widx_t.shape, lambda b, i: (0, 0, 0))],
        out_specs=pl.BlockSpec((1, seq, tq), lambda b, i: (b, 0, i)),
        out_shape=jax.ShapeDtypeStruct((batch, seq, seq), F32),
        scratch_shapes=[pltpu.VMEM((IDX_N_HEADS, LANES, tq), BF16),
                        pltpu.VMEM((n_t, tq, tq), F32)],
        compiler_params=_cparams(2),
        name="dsa_indexer",
    )(keys, slab_t, cq_t, widx_t)


def _attn_kernel(ql_ref, bias_ref, k_ref, vt_ref, wuvt_ref, gate_ref, *rest, n_keys, hc):
    o_ref = rest[-1]
    tq = bias_ref.shape[2]
    hb, hd, kv_rank = wuvt_ref.shape
    cols = hc * tq
    n_chunks = hb // hc
    keys = k_ref[:n_keys, :]
    bias_t = bias_ref[0]
    bias_rep = jnp.concatenate([bias_t] * hc, axis=1) if hc > 1 else bias_t

    def scores(c):
        q_t = ql_ref[0, 0, :, c * cols:(c + 1) * cols]
        return jnp.dot(keys, q_t, preferred_element_type=F32) + bias_rep

    def finish(c, s):
        m = jnp.max(s, axis=0, keepdims=True)
        p = jnp.exp2(s - m).astype(BF16)
        o_aug = jnp.dot(vt_ref[0], p, preferred_element_type=F32)
        o_t = (o_aug[:kv_rank] / o_aug[kv_rank:kv_rank + 1]).astype(BF16)
        for j in range(hc):
            h = c * hc + j
            oh_t = jnp.dot(wuvt_ref[h], o_t[:, j * tq:(j + 1) * tq], preferred_element_type=F32)
            gate = gate_ref[:, h * hd:(h + 1) * hd].astype(F32)
            o_ref[:, h * hd:(h + 1) * hd] = (oh_t.T * gate).astype(o_ref.dtype)

    s_next = scores(0)
    for c in range(n_chunks):
        s_cur = s_next
        if c + 1 < n_chunks:
            s_next = scores(c + 1)
        finish(c, s_cur)


def _dsa_attention(ql, bias, ckv_t, ckv, wuv_t, gate, *, batch, seq):
    tq = ATT_TQ
    n_q = seq // tq
    n_heads, hd, kv_rank = wuv_t.shape
    width = n_heads * hd
    og = None
    for i in range(n_q):
        n_keys = (i + 1) * tq
        hb = n_heads
        while hb > 1 and hb * tq * n_keys > ATT_STEP_ELEMS:
            hb //= 2
        hc = max(1, min(hb, ATT_CHUNK_COLS // tq))
        in_specs = [pl.BlockSpec((1, 1, kv_rank, hb * tq), lambda b, g, i=i: (b, i, 0, g)),
                    pl.BlockSpec((1, n_keys, tq), lambda b, g, i=i: (b, 0, i)),
                    pl.BlockSpec((seq, kv_rank), lambda b, g: (b, 0)),
                    pl.BlockSpec((1, kv_rank + ONES_ROWS, n_keys), lambda b, g: (b, 0, 0)),
                    pl.BlockSpec((hb, hd, kv_rank), lambda b, g: (g, 0, 0)),
                    pl.BlockSpec((tq, hb * hd), lambda b, g, i=i: (b * n_q + i, g))]
        args = [ql, bias, ckv, ckv_t, wuv_t, gate]
        aliases = {}
        if og is not None:
            in_specs.append(pl.BlockSpec(memory_space=pl.ANY))
            args.append(og)
            aliases = {len(args) - 1: 0}
        og = pl.pallas_call(
            functools.partial(_attn_kernel, n_keys=n_keys, hc=hc),
            grid=(batch, n_heads // hb),
            in_specs=in_specs,
            out_specs=pl.BlockSpec((tq, hb * hd), lambda b, g, i=i: (b * n_q + i, g)),
            out_shape=jax.ShapeDtypeStruct((batch * seq, width), BF16),
            input_output_aliases=aliases,
            compiler_params=_cparams(2),
            name="dsa_attention_q%d" % i,
        )(*args)
    return og


def _ssd_layer(x, w_in, conv_w, conv_b, dt_bias, a_log, d_skip, norm_g, w_out, ln_g, ln_b,
               *, batch, seq, alpha):
    d_inner = w_out.shape[0]
    n_heads = a_log.shape[0]
    n_groups = SSD_N_GROUPS
    heads_per_group = n_heads // n_groups
    conv_dim = conv_w.shape[1]
    gw = d_inner // n_groups
    assert gw % LANES == 0 and heads_per_group <= LANES and SSD_D_STATE == LANES
    assert LANES % SSD_HEAD_DIM == 0 and heads_per_group % (LANES // SSD_HEAD_DIM) == 0

    def per_group_lanes(v):
        lead = v.shape[:-1]
        v = v.reshape(lead + (n_groups, heads_per_group))
        v = jnp.pad(v, [(0, 0)] * len(lead) + [(0, 0), (0, LANES - heads_per_group)])
        return v.reshape(lead + (n_groups * LANES,))

    w_bf = w_in.astype(BF16)
    w_dt = jnp.pad(w_bf[:, d_inner + conv_dim:], ((0, 0), (0, LANES - n_heads)))
    dt_b = jnp.pad(dt_bias, (0, LANES - n_heads))[None, :]
    head_of_lane = jnp.arange(n_groups * LANES)
    head_of_lane = jnp.where(head_of_lane % LANES < heads_per_group,
                             (head_of_lane // LANES) * heads_per_group + head_of_lane % LANES, -1)
    spread = (jnp.arange(LANES)[:, None] == head_of_lane[None, :]).astype(BF16)
    alog = per_group_lanes(a_log).reshape(n_groups, LANES)
    dskip = jnp.repeat(d_skip, SSD_HEAD_DIM).reshape(n_groups, gw)
    lane_of = jnp.arange(gw) // SSD_HEAD_DIM
    expand_mat = (jnp.arange(LANES)[:, None] == lane_of[None, :]).astype(BF16)

    dt, x_bf = _dt_proj(x, w_dt, dt_b, spread, tm=min(seq, 1024))
    z = _proj(x_bf, w_bf, batch=batch, seq=seq, tn=1024, n_dim=d_inner)
    xbc = _proj(x_bf, w_bf, batch=batch, seq=seq, tn=1024, col0=d_inner, n_dim=conv_dim,
                conv=(conv_w, conv_b[None, :]))
    yn = _ssd_scan(xbc, z, dt, alog, dskip, norm_g[None, :], expand_mat, batch=batch, seq=seq, d_inner=d_inner)
    return _outproj_ln(yn, w_out.astype(BF16), x, ln_g[None, :], ln_b[None, :], alpha=alpha, tm=512)


def _dsa_layer(x, x_bf, w_in, q_norm_g, kv_norm_g, w_uq, w_uk, w_uv, w_idx_q, w_out, ln_g, ln_b,
               *, batch, seq, alpha):
    assert seq % IDX_TQ == 0 and IDX_TQ % ATT_TQ == 0 and IDX_HEAD_DIM + IDX_N_HEADS <= LANES
    small = DSA_Q_RANK + DSA_KV_RANK + IDX_HEAD_DIM + IDX_N_HEADS
    w_small = jnp.pad(w_in[:, :small], ((0, 0), (0, DSA_Q_RANK + DSA_KV_RANK + LANES - small))).astype(BF16)
    w_gate = w_in[:, small:].astype(BF16)
    wuq_t = w_uq.T.astype(BF16)
    wuv_t = w_uv.transpose(0, 2, 1).astype(BF16)
    widx_t = w_idx_q.reshape(DSA_Q_RANK, IDX_N_HEADS, IDX_HEAD_DIM).transpose(1, 2, 0)
    widx_t = jnp.pad(widx_t, ((0, 0), (0, LANES - IDX_HEAD_DIM), (0, 0))).astype(BF16)
    k_top = min(IDX_TOPK, seq // 4)

    cq_t, ckv, ckv_t, slab, slab_t = _dsa_latent(x_bf, w_small, q_norm_g[None, :], kv_norm_g[None, :],
                                                     batch=batch, seq=seq)
    gate = _proj(x_bf, w_gate, batch=batch, seq=seq, tn=1024)
    ql = _dsa_query(cq_t, wuq_t, w_uk.astype(BF16), batch=batch, seq=seq,
                    scale=DSA_HEAD_DIM ** -0.5 * math.log2(math.e))
    bias = _dsa_indexer(slab, slab_t, cq_t, widx_t, batch=batch, seq=seq, k_top=k_top,
                        w_scale=IDX_N_HEADS ** -0.5 * IDX_HEAD_DIM ** -0.5)
    og = _dsa_attention(ql, bias, ckv_t, ckv, wuv_t, gate, batch=batch, seq=seq)
    return _outproj_ln(og, w_out.astype(BF16), x, ln_g[None, :], ln_b[None, :], alpha=alpha, tm=512)


def kernel(x, ssd_w_in, ssd_conv_w, ssd_conv_b, ssd_dt_bias, ssd_a_log, ssd_d_skip, ssd_norm_g, ssd_w_out,
           dsa_w_in, dsa_q_norm_g, dsa_kv_norm_g, dsa_w_uq, dsa_w_uk, dsa_w_uv, dsa_w_idx_q, dsa_w_out,
           ln_g, ln_b):
    batch, seq, d_model = x.shape
    depth = ln_g.shape[0]
    alpha = (2.0 * depth) ** 0.25
    xf = x.reshape(batch * seq, d_model)
    x_bf = None
    for i in range(depth):
        j = i // 2
        if i % 2 == 0:
            xf, x_bf = _ssd_layer(xf, ssd_w_in[j], ssd_conv_w[j], ssd_conv_b[j], ssd_dt_bias[j],
                                  ssd_a_log[j], ssd_d_skip[j], ssd_norm_g[j], ssd_w_out[j], ln_g[i], ln_b[i],
                                  batch=batch, seq=seq, alpha=alpha)
        else:
            x_bf = xf.astype(BF16) if x_bf is None else x_bf
            xf, x_bf = _dsa_layer(xf, x_bf, dsa_w_in[j], dsa_q_norm_g[j], dsa_kv_norm_g[j], dsa_w_uq[j],
                                  dsa_w_uk[j], dsa_w_uv[j], dsa_w_idx_q[j], dsa_w_out[j], ln_g[i], ln_b[i],
                                  batch=batch, seq=seq, alpha=alpha)
    return xf.reshape(batch, seq, d_model)
```

```python
import functools
import math

import jax
import jax.numpy as jnp
from jax import lax
from jax.experimental import pallas as pl
from jax.experimental.pallas import tpu as pltpu

F32 = jnp.float32
BF16 = jnp.bfloat16

V7X_VMEM_BYTES = 64 * 1024 * 1024
LANES = 128
SUBLANES = 8
VMEM_LIMIT_BYTES = V7X_VMEM_BYTES - 8 * 1024 * 1024

LN_EPS = 1e-5
RMS_EPS = 1e-6

SSD_HEAD_DIM = 64
SSD_N_GROUPS = 8
SSD_D_STATE = 128
SSD_CHUNK = 256
DSA_N_HEADS = 32
DSA_HEAD_DIM = 128
DSA_Q_RANK = 512
DSA_KV_RANK = 256
IDX_N_HEADS = 16
IDX_HEAD_DIM = 64
IDX_TOPK = 256

ATT_TQ = 256
ATT_STEP_ELEMS = 8 * 1024 * 1024
ATT_CHUNK_COLS = 2048
IDX_TQ = 512
BISECT_ITERS = 40
ONES_ROWS = 16
CONV_ROW_CHUNK = 256
LN_ROW_CHUNK = 128
SSD_GROUPS_PER_STEP = 2


def _cparams(n_axes, flags=None):
    return pltpu.CompilerParams(dimension_semantics=("arbitrary",) * n_axes,
                                vmem_limit_bytes=VMEM_LIMIT_BYTES, flags=flags)


LOG2_E = math.log2(math.e)


def _sigmoid(v):
    return 1.0 / (1.0 + jnp.exp2(v * (-LOG2_E)))


def _proj_silu_kernel(x_ref, w_ref, o_ref, *, chunk):
    w = w_ref[...].astype(BF16)
    for r0 in range(0, o_ref.shape[0], chunk):
        y = jnp.dot(x_ref[r0:r0 + chunk, :], w, preferred_element_type=F32)
        o_ref[r0:r0 + chunk, :] = (y * _sigmoid(y)).astype(o_ref.dtype)


def _proj_conv_silu_kernel(x_ref, w_ref, conv_w_ref, conv_b_ref, o_ref, y_scr, *, chunk):
    seq, tn = o_ref.shape
    k_taps = conv_w_ref.shape[0]
    head = y_scr.shape[0] - seq
    y_scr[:head, :] = jnp.zeros((head, tn), F32)
    w = w_ref[...]

    def matmul(r0):
        y_scr[head + r0:head + r0 + chunk, :] = jnp.dot(x_ref[r0:r0 + chunk, :], w, preferred_element_type=F32)

    def finish(r0):
        acc = conv_b_ref[...] + conv_w_ref[k_taps - 1:k_taps, :] * y_scr[head + r0:head + r0 + chunk, :]
        for back in range(1, k_taps):
            tap = conv_w_ref[k_taps - 1 - back:k_taps - back, :]
            acc = acc + tap * y_scr[head + r0 - back:head + r0 - back + chunk, :]
        o_ref[r0:r0 + chunk, :] = (acc * _sigmoid(acc)).astype(o_ref.dtype)

    for r0 in range(0, seq, chunk):
        matmul(r0)
        finish(r0)


def _proj(x, w, *, batch, seq, tn, col0=0, n_dim=None, conv=None, out_dtype=BF16):
    k_dim = w.shape[0]
    n_dim = w.shape[1] - col0 if n_dim is None else n_dim
    assert n_dim % tn == 0 and tn % LANES == 0 and col0 % tn == 0
    j0 = col0 // tn
    in_specs = [pl.BlockSpec((seq, k_dim), lambda b, j: (b, 0)),
                pl.BlockSpec((k_dim, tn), lambda b, j: (0, j0 + j))]
    if conv is None:
        body = functools.partial(_proj_silu_kernel, chunk=CONV_ROW_CHUNK)
        args, scratch, name = (x, w), [], "proj_silu"
    else:
        body = functools.partial(_proj_conv_silu_kernel, chunk=CONV_ROW_CHUNK)
        args = (x, w) + tuple(conv)
        in_specs += [pl.BlockSpec((c.shape[0], tn), lambda b, j: (0, j)) for c in conv]
        assert conv[0].shape[0] - 1 <= SUBLANES
        scratch, name = [pltpu.VMEM((SUBLANES + seq, tn), F32)], "proj_conv_silu"
    return pl.pallas_call(
        body,
        grid=(batch, n_dim // tn),
        in_specs=in_specs,
        out_specs=pl.BlockSpec((seq, tn), lambda b, j: (b, j)),
        out_shape=jax.ShapeDtypeStruct((batch * seq, n_dim), out_dtype),
        scratch_shapes=scratch,
        compiler_params=_cparams(2),
        name=name,
    )(*args)


def _dt_kernel(x_ref, w_ref, b_ref, spread_ref, o_ref, xbf_ref):
    x = x_ref[...].astype(BF16)
    xbf_ref[...] = x
    y = jnp.dot(x, w_ref[...], preferred_element_type=F32) + b_ref[...]
    y = jnp.maximum(y, 0.0) + jnp.log1p(jnp.exp(-jnp.abs(y)))
    spread = spread_ref[...]
    hi = y.astype(BF16)
    rest = y - hi.astype(F32)
    mid = rest.astype(BF16)
    lo = (rest - mid.astype(F32)).astype(BF16)
    o_ref[...] = (jnp.dot(hi, spread, preferred_element_type=F32)
                  + jnp.dot(mid, spread, preferred_element_type=F32)
                  + jnp.dot(lo, spread, preferred_element_type=F32))


def _dt_proj(x, w, bias, spread, *, tm):
    rows, k_dim = x.shape
    width = spread.shape[1]
    return pl.pallas_call(
        _dt_kernel,
        grid=(rows // tm,),
        in_specs=[pl.BlockSpec((tm, k_dim), lambda i: (i, 0)),
                  pl.BlockSpec((k_dim, LANES), lambda i: (0, 0)),
                  pl.BlockSpec((1, LANES), lambda i: (0, 0)),
                  pl.BlockSpec((LANES, width), lambda i: (0, 0))],
        out_specs=[pl.BlockSpec((tm, width), lambda i: (i, 0)),
                   pl.BlockSpec((tm, k_dim), lambda i: (i, 0))],
        out_shape=[jax.ShapeDtypeStruct((rows, width), F32),
                   jax.ShapeDtypeStruct((rows, k_dim), BF16)],
        compiler_params=_cparams(1),
        name="dt_proj",
    )(x, w, bias, spread)


def _cumsum_rows(v):
    n = v.shape[0]
    row = lax.broadcasted_iota(jnp.int32, v.shape, 0)
    shift = 1
    while shift < n:
        v = v + jnp.where(row >= shift, pltpu.roll(v, shift, 0), 0.0)
        shift *= 2
    return v


def _ssd_kernel(xs_ref, b_ref, c_ref, z_ref, dt_ref, alog_ref, dskip_ref, ng_ref, e_ref, o_ref,
                state_ref, *, groups, heads_per_group, head_dim):
    @pl.when(pl.program_id(2) == 0)
    def _():
        state_ref[...] = jnp.zeros_like(state_ref)

    gw = e_ref.shape[1]
    n = b_ref.shape[1] // groups
    for s in range(groups):
        g = pl.program_id(1) * groups + s
        lanes = slice(s * gw, (s + 1) * gw)
        _ssd_group(xs_ref.at[:, lanes], b_ref.at[:, s * n:(s + 1) * n], c_ref.at[:, s * n:(s + 1) * n],
                   z_ref.at[:, lanes], dt_ref.at[:, s * LANES:(s + 1) * LANES], alog_ref[pl.ds(g, 1), :],
                   dskip_ref[pl.ds(g, 1), :], ng_ref.at[:, lanes], e_ref, o_ref.at[:, lanes], state_ref.at[s],
                   heads_per_group=heads_per_group, head_dim=head_dim)


def _ssd_group(xs_ref, b_ref, c_ref, z_ref, dt_ref, alog, dskip, ng_ref, e_ref, o_ref, state_ref,
               *, heads_per_group, head_dim):
    q = xs_ref.shape[0]
    xs = xs_ref[...].astype(F32)
    bm = b_ref[...]
    cm = c_ref[...]
    dt = dt_ref[...]
    a = -jnp.exp(alog) * LOG2_E
    acs = _cumsum_rows(dt * a)
    acs_t = acs.T
    last = acs[q - 1:q, :]
    expand_mat = e_ref[...]

    def expand(v):
        return jnp.dot(v.astype(BF16), expand_mat, preferred_element_type=F32)

    def expand_hi_lo(v):
        hi = v.astype(BF16)
        lo = (v - hi.astype(F32)).astype(BF16)
        return (jnp.dot(hi, expand_mat, preferred_element_type=F32)
                + jnp.dot(lo, expand_mat, preferred_element_type=F32))

    decay_in_x = expand_hi_lo(jnp.exp2(acs))
    decay_out_dt_x = expand(jnp.exp2(last - acs) * dt)

    bc = lax.dot_general(bm, cm, (((1,), (1,)), ((), ())), preferred_element_type=F32)
    blk = LANES
    tri = (lax.broadcasted_iota(jnp.int32, (blk, blk), 1) >= lax.broadcasted_iota(jnp.int32, (blk, blk), 0))
    xs_t = xs.T
    dt_t = dt.T

    def masked_bc(r):
        rows = []
        for s0 in range(0, q, blk):
            parts = []
            if s0 > 0:
                parts.append(jnp.zeros((blk, s0), BF16))
            diff = acs_t[r:r + 1, s0:s0 + blk] - acs[s0:s0 + blk, r:r + 1]
            decay = jnp.exp2(jnp.where(tri, diff, -jnp.inf))
            parts.append((bc[s0:s0 + blk, s0:s0 + blk] * decay).astype(BF16))
            if s0 + blk < q:
                diff = acs_t[r:r + 1, s0 + blk:] - acs[s0:s0 + blk, r:r + 1]
                parts.append((bc[s0:s0 + blk, s0 + blk:] * jnp.exp2(diff)).astype(BF16))
            rows.append(jnp.concatenate(parts, axis=1) if len(parts) > 1 else parts[0])
        return jnp.concatenate(rows, axis=0) if len(rows) > 1 else rows[0]

    y_t = []
    for r in range(heads_per_group):
        x_head_t = (xs_t[r * head_dim:(r + 1) * head_dim, :] * dt_t[r:r + 1, :]).astype(BF16)
        y_t.append(jnp.dot(x_head_t, masked_bc(r), preferred_element_type=F32))
    y_diag = jnp.concatenate(y_t, axis=0).T

    s_prev = state_ref[...]
    y_off = jnp.dot(cm, s_prev.astype(BF16), preferred_element_type=F32) * decay_in_x
    y = y_diag + y_off + dskip * xs
    b_t = bm.astype(F32).T.astype(BF16)
    s_new = jnp.dot(b_t, (xs * decay_out_dt_x).astype(BF16), preferred_element_type=F32)
    state_ref[...] = s_prev * decay_in_x[q - 1:q, :] + s_new

    yg = y * z_ref[...].astype(F32)
    ms = jnp.mean(yg * yg, axis=-1, keepdims=True)
    o_ref[...] = (yg * lax.rsqrt(ms + RMS_EPS) * ng_ref[...]).astype(o_ref.dtype)


def _ssd_scan(xbc, z, dt, alog, dskip, norm_g, expand_mat, *, batch, seq, d_inner):
    n_groups, d_state, chunk = SSD_N_GROUPS, SSD_D_STATE, math.gcd(SSD_CHUNK, seq)
    gw = d_inner // n_groups
    heads_per_group = gw // SSD_HEAD_DIM
    n_chunks = seq // chunk
    gps = SSD_GROUPS_PER_STEP
    assert n_groups % gps == 0
    b_col0 = d_inner // (gps * d_state)
    c_col0 = b_col0 + n_groups // gps
    row = lambda b, g, c: b * n_chunks + c
    return pl.pallas_call(
        functools.partial(_ssd_kernel, groups=gps, heads_per_group=heads_per_group, head_dim=SSD_HEAD_DIM),
        grid=(batch, n_groups // gps, n_chunks),
        in_specs=[
            pl.BlockSpec((chunk, gps * gw), lambda b, g, c: (row(b, g, c), g)),
            pl.BlockSpec((chunk, gps * d_state), lambda b, g, c: (row(b, g, c), b_col0 + g)),
            pl.BlockSpec((chunk, gps * d_state), lambda b, g, c: (row(b, g, c), c_col0 + g)),
            pl.BlockSpec((chunk, gps * gw), lambda b, g, c: (row(b, g, c), g)),
            pl.BlockSpec((chunk, gps * LANES), lambda b, g, c: (row(b, g, c), g)),
            pl.BlockSpec((n_groups, LANES), lambda b, g, c: (0, 0)),
            pl.BlockSpec((n_groups, gw), lambda b, g, c: (0, 0)),
            pl.BlockSpec((1, gps * gw), lambda b, g, c: (0, g)),
            pl.BlockSpec((LANES, gw), lambda b, g, c: (0, 0)),
        ],
        out_specs=pl.BlockSpec((chunk, gps * gw), lambda b, g, c: (row(b, g, c), g)),
        out_shape=jax.ShapeDtypeStruct((batch * seq, d_inner), BF16),
        scratch_shapes=[pltpu.VMEM((gps, d_state, gw), F32)],
        compiler_params=_cparams(3),
        name="ssd_scan",
    )(xbc, xbc, xbc, z, dt, alog, dskip, norm_g, expand_mat)


def _outproj_ln_kernel(y_ref, w_ref, x_ref, g_ref, b_ref, o_ref, obf_ref, *, alpha, chunk):
    for r0 in range(0, y_ref.shape[0], chunk):
        rows = slice(r0, r0 + chunk)
        h = jnp.dot(y_ref[rows, :], w_ref[...], preferred_element_type=F32)
        v = alpha * x_ref[rows, :] + h
        mu = jnp.mean(v, axis=-1, keepdims=True)
        d = v - mu
        var = jnp.mean(d * d, axis=-1, keepdims=True)
        out = d * lax.rsqrt(var + LN_EPS) * g_ref[...] + b_ref[...]
        o_ref[rows, :] = out
        obf_ref[rows, :] = out.astype(BF16)


def _outproj_ln(y, w, x, ln_g, ln_b, *, alpha, tm):
    m, k_dim = y.shape
    d = w.shape[1]
    return pl.pallas_call(
        functools.partial(_outproj_ln_kernel, alpha=alpha, chunk=LN_ROW_CHUNK),
        grid=(m // tm,),
        in_specs=[pl.BlockSpec((tm, k_dim), lambda i: (i, 0)),
                  pl.BlockSpec((k_dim, d), lambda i: (0, 0), pipeline_mode=pl.Buffered(1)),
                  pl.BlockSpec((tm, d), lambda i: (i, 0)),
                  pl.BlockSpec((1, d), lambda i: (0, 0)),
                  pl.BlockSpec((1, d), lambda i: (0, 0))],
        out_specs=[pl.BlockSpec((tm, d), lambda i: (i, 0)),
                   pl.BlockSpec((tm, d), lambda i: (i, 0))],
        out_shape=[jax.ShapeDtypeStruct((m, d), F32), jax.ShapeDtypeStruct((m, d), BF16)],
        compiler_params=_cparams(1),
        name="outproj_ln",
    )(y, w, x, ln_g, ln_b)


def _dsa_latent_kernel(x_ref, w_ref, qg_ref, kvg_ref, cqt_ref, ckv_ref, ckvt_ref, slab_ref, slabt_ref):
    p = jnp.dot(x_ref[...], w_ref[...], preferred_element_type=F32)
    q_rank = cqt_ref.shape[1]
    kv_rank = ckv_ref.shape[1]
    cq = p[:, :q_rank]
    ckv = p[:, q_rank:q_rank + kv_rank]
    slab = p[:, q_rank + kv_rank:]
    cq = cq * lax.rsqrt(jnp.mean(cq * cq, axis=-1, keepdims=True) + RMS_EPS) * qg_ref[...]
    ckv = ckv * lax.rsqrt(jnp.mean(ckv * ckv, axis=-1, keepdims=True) + RMS_EPS) * kvg_ref[...]
    cqt_ref[0] = cq.T.astype(BF16)
    ckv_ref[...] = ckv.astype(BF16)
    ckvt_ref[0, :kv_rank, :] = ckv.T.astype(BF16)
    ckvt_ref[0, kv_rank:, :] = jnp.ones((ckvt_ref.shape[1] - kv_rank, ckvt_ref.shape[2]), BF16)
    slab_ref[...] = slab.astype(BF16)
    slabt_ref[0] = slab.T


def _dsa_latent(x, w, q_gain, kv_gain, *, batch, seq):
    tm = IDX_TQ
    n_t = seq // tm
    k_dim, n_dim = w.shape
    return pl.pallas_call(
        _dsa_latent_kernel,
        grid=(batch, n_t),
        in_specs=[pl.BlockSpec((tm, k_dim), lambda b, t: (b * n_t + t, 0)),
                  pl.BlockSpec((k_dim, n_dim), lambda b, t: (0, 0)),
                  pl.BlockSpec((1, DSA_Q_RANK), lambda b, t: (0, 0)),
                  pl.BlockSpec((1, DSA_KV_RANK), lambda b, t: (0, 0))],
        out_specs=[pl.BlockSpec((1, DSA_Q_RANK, tm), lambda b, t: (b, 0, t)),
                   pl.BlockSpec((tm, DSA_KV_RANK), lambda b, t: (b * n_t + t, 0)),
                   pl.BlockSpec((1, DSA_KV_RANK + ONES_ROWS, tm), lambda b, t: (b, 0, t)),
                   pl.BlockSpec((tm, LANES), lambda b, t: (b * n_t + t, 0)),
                   pl.BlockSpec((1, LANES, tm), lambda b, t: (b, 0, t))],
        out_shape=[jax.ShapeDtypeStruct((batch, DSA_Q_RANK, seq), BF16),
                   jax.ShapeDtypeStruct((batch * seq, DSA_KV_RANK), BF16),
                   jax.ShapeDtypeStruct((batch, DSA_KV_RANK + ONES_ROWS, seq), BF16),
                   jax.ShapeDtypeStruct((batch * seq, LANES), BF16),
                   jax.ShapeDtypeStruct((batch, LANES, seq), F32)],
        compiler_params=_cparams(2),
        name="dsa_latent",
    )(x, w, q_gain, kv_gain)


def _dsa_query_kernel(cqt_ref, wuqt_ref, wuk_ref, ql_ref, *, scale):
    hb, _, hd = wuk_ref.shape
    tq = ql_ref.shape[3] // hb
    q_t = jnp.dot(wuqt_ref[...], cqt_ref[0], preferred_element_type=F32).astype(BF16)
    for j in range(hb):
        ql_t = jnp.dot(wuk_ref[j], q_t[j * hd:(j + 1) * hd, :], preferred_element_type=F32)
        ql_t = (ql_t * scale).astype(BF16)
        for blk in range(ql_ref.shape[1]):
            ql_ref[0, blk, :, j * tq:(j + 1) * tq] = ql_t[:, blk * tq:(blk + 1) * tq]


def _dsa_query(cq_t, wuq_t, wuk, *, batch, seq, scale):
    tm = min(seq, 1024)
    hb = 8
    tq = ATT_TQ
    n_t = seq // tm
    n_heads, kv_rank, hd = wuk.shape
    return pl.pallas_call(
        functools.partial(_dsa_query_kernel, scale=scale),
        grid=(batch, n_t, n_heads // hb),
        in_specs=[pl.BlockSpec((1, DSA_Q_RANK, tm), lambda b, t, h: (b, 0, t)),
                  pl.BlockSpec((hb * hd, DSA_Q_RANK), lambda b, t, h: (h, 0)),
                  pl.BlockSpec((hb, kv_rank, hd), lambda b, t, h: (h, 0, 0))],
        out_specs=pl.BlockSpec((1, tm // tq, kv_rank, hb * tq), lambda b, t, h: (b, t, 0, h)),
        out_shape=jax.ShapeDtypeStruct((batch, seq // tq, kv_rank, n_heads * tq), BF16),
        compiler_params=_cparams(3),
        name="dsa_query",
    )(cq_t, wuq_t, wuk)


def _indexer_kernel(keys_ref, slabt_ref, cqt_ref, widx_ref, bias_ref, qt_scr, sc_scr, *, k_top, w_scale, n_iter):
    i = pl.program_id(1)
    n_t, kt_w, tq = sc_scr.shape
    cq_t = cqt_ref[0]
    for h in range(IDX_N_HEADS):
        qt_scr[h] = jnp.dot(widx_ref[h], cq_t, preferred_element_type=F32).astype(BF16)
    w_t = slabt_ref[0][IDX_HEAD_DIM:IDX_HEAD_DIM + IDX_N_HEADS, :] * w_scale
    q_pos = i * tq + lax.broadcasted_iota(jnp.int32, (kt_w, tq), 1)

    for kt in range(n_t):
        @pl.when(kt <= i)
        def _(kt=kt):
            keys = keys_ref[kt * kt_w:(kt + 1) * kt_w, :]
            acc = jnp.zeros((kt_w, tq), F32)
            for h in range(IDX_N_HEADS):
                s = jnp.dot(keys, qt_scr[h], preferred_element_type=F32)
                acc = acc + jnp.maximum(s, 0.0) * w_t[h:h + 1, :]
            k_pos = kt * kt_w + lax.broadcasted_iota(jnp.int32, (kt_w, tq), 0)
            sc_scr[kt] = jnp.where(k_pos <= q_pos, acc, -jnp.inf)

    def bounds(kt, carry):
        lo, hi = carry
        s = sc_scr[kt]
        hi = jnp.maximum(hi, jnp.max(s, axis=0, keepdims=True))
        lo = jnp.minimum(lo, jnp.min(jnp.where(s == -jnp.inf, jnp.inf, s), axis=0, keepdims=True))
        return lo, hi

    lo, hi = lax.fori_loop(0, i + 1, bounds,
                           (jnp.full((1, tq), jnp.inf, F32), jnp.full((1, tq), -jnp.inf, F32)))

    def halve(carry):
        it, lo, hi, n_lo = carry
        mid = 0.5 * lo + 0.5 * hi

        mid_rows = jnp.broadcast_to(mid, (SUBLANES, tq))

        def count(kt, cnt):
            for r0 in range(0, kt_w, SUBLANES):
                cnt = jnp.where(sc_scr[kt, r0:r0 + SUBLANES, :] >= mid_rows, cnt + 1.0, cnt)
            return cnt

        cnt = lax.fori_loop(0, i + 1, count, jnp.zeros((SUBLANES, tq), F32))
        cnt = jnp.sum(cnt, axis=0, keepdims=True)
        enough = cnt >= k_top
        return it + 1, jnp.where(enough, mid, lo), jnp.where(enough, hi, mid), jnp.where(enough, cnt, n_lo)

    def unresolved(carry):
        it, _, _, n_lo = carry
        return jnp.logical_and(it < n_iter, jnp.max(n_lo) > k_top)

    n_causal = (i * tq + lax.broadcasted_iota(jnp.int32, (1, tq), 1) + 1).astype(F32)
    _, lo, hi, _ = lax.while_loop(unresolved, lambda c: halve(halve(c)), (jnp.int32(0), lo, hi, n_causal))

    for kt in range(n_t):
        @pl.when(kt <= i)
        def _(kt=kt):
            bias_ref[0, kt * kt_w:(kt + 1) * kt_w, :] = jnp.where(sc_scr[kt] >= lo, 0.0, -jnp.inf)

        @pl.when(kt > i)
        def _(kt=kt):
            bias_ref[0, kt * kt_w:(kt + 1) * kt_w, :] = jnp.full((kt_w, tq), -jnp.inf, F32)


def _dsa_indexer(keys, slab_t, cq_t, widx_t, *, batch, seq, k_top, w_scale):
    tq = IDX_TQ
    n_t = seq // tq
    return pl.pallas_call(
        functools.partial(_indexer_kernel, k_top=k_top, w_scale=w_scale, n_iter=BISECT_ITERS),
        grid=(batch, n_t),
        in_specs=[pl.BlockSpec((seq, LANES), lambda b, i: (b, 0)),
                  pl.BlockSpec((1, LANES, tq), lambda b, i: (b, 0, i)),
                  pl.BlockSpec((1, DSA_Q_RANK, tq), lambda b, i: (b, 0, i)),
                  pl.BlockSpec(widx_t.shape, lambda b, i: (0, 0, 0))],
        out_specs=pl.BlockSpec((1, seq, tq), lambda b, i: (b, 0, i)),
        out_shape=jax.ShapeDtypeStruct((batch, seq, seq), F32),
        scratch_shapes=[pltpu.VMEM((IDX_N_HEADS, LANES, tq), BF16),
                        pltpu.VMEM((n_t, tq, tq), F32)],
        compiler_params=_cparams(2),
        name="dsa_indexer",
    )(keys, slab_t, cq_t, widx_t)


def _attn_kernel(ql_ref, bias_ref, k_ref, vt_ref, wuvt_ref, gate_ref, *rest, n_keys, hc):
    o_ref = rest[-1]
    tq = bias_ref.shape[2]
    hb, hd, kv_rank = wuvt_ref.shape
    cols = hc * tq
    n_chunks = hb // hc
    keys = k_ref[:n_keys, :]
    bias_t = bias_ref[0]
    bias_rep = jnp.concatenate([bias_t] * hc, axis=1) if hc > 1 else bias_t

    def scores(c):
        q_t = ql_ref[0, 0, :, c * cols:(c + 1) * cols]
        return jnp.dot(keys, q_t, preferred_element_type=F32) + bias_rep

    def finish(c, s):
        m = jnp.max(s, axis=0, keepdims=True)
        p = jnp.exp2(s - m).astype(BF16)
        o_aug = jnp.dot(vt_ref[0], p, preferred_element_type=F32)
        o_t = (o_aug[:kv_rank] / o_aug[kv_rank:kv_rank + 1]).astype(BF16)
        for j in range(hc):
            h = c * hc + j
            oh_t = jnp.dot(wuvt_ref[h], o_t[:, j * tq:(j + 1) * tq], preferred_element_type=F32)
            gate = gate_ref[:, h * hd:(h + 1) * hd].astype(F32)
            o_ref[:, h * hd:(h + 1) * hd] = (oh_t.T * gate).astype(o_ref.dtype)

    s_next = scores(0)
    for c in range(n_chunks):
        s_cur = s_next
        if c + 1 < n_chunks:
            s_next = scores(c + 1)
        finish(c, s_cur)


def _dsa_attention(ql, bias, ckv_t, ckv, wuv_t, gate, *, batch, seq):
    tq = ATT_TQ
    n_q = seq // tq
    n_heads, hd, kv_rank = wuv_t.shape
    width = n_heads * hd
    og = None
    for i in range(n_q):
        n_keys = (i + 1) * tq
        hb = n_heads
        while hb > 1 and hb * tq * n_keys > ATT_STEP_ELEMS:
            hb //= 2
        hc = max(1, min(hb, ATT_CHUNK_COLS // tq))
        in_specs = [pl.BlockSpec((1, 1, kv_rank, hb * tq), lambda b, g, i=i: (b, i, 0, g)),
                    pl.BlockSpec((1, n_keys, tq), lambda b, g, i=i: (b, 0, i)),
                    pl.BlockSpec((seq, kv_rank), lambda b, g: (b, 0)),
                    pl.BlockSpec((1, kv_rank + ONES_ROWS, n_keys), lambda b, g: (b, 0, 0)),
                    pl.BlockSpec((hb, hd, kv_rank), lambda b, g: (g, 0, 0)),
                    pl.BlockSpec((tq, hb * hd), lambda b, g, i=i: (b * n_q + i, g))]
        args = [ql, bias, ckv, ckv_t, wuv_t, gate]
        aliases = {}
        if og is not None:
            in_specs.append(pl.BlockSpec(memory_space=pl.ANY))
            args.append(og)
            aliases = {len(args) - 1: 0}
        og = pl.pallas_call(
            functools.partial(_attn_kernel, n_keys=n_keys, hc=hc),
            grid=(batch, n_heads // hb),
            in_specs=in_specs,
            out_specs=pl.BlockSpec((tq, hb * hd), lambda b, g, i=i: (b * n_q + i, g)),
            out_shape=jax.ShapeDtypeStruct((batch * seq, width), BF16),
            input_output_aliases=aliases,
            compiler_params=_cparams(2),
            name="dsa_attention_q%d" % i,
        )(*args)
    return og


def _ssd_layer(x, w_in, conv_w, conv_b, dt_bias, a_log, d_skip, norm_g, w_out, ln_g, ln_b,
               *, batch, seq, alpha):
    d_inner = w_out.shape[0]
    n_heads = a_log.shape[0]
    n_groups = SSD_N_GROUPS
    heads_per_group = n_heads // n_groups
    conv_dim = conv_w.shape[1]
    gw = d_inner // n_groups
    assert gw % LANES == 0 and heads_per_group <= LANES and SSD_D_STATE == LANES
    assert LANES % SSD_HEAD_DIM == 0 and heads_per_group % (LANES // SSD_HEAD_DIM) == 0

    def per_group_lanes(v):
        lead = v.shape[:-1]
        v = v.reshape(lead + (n_groups, heads_per_group))
        v = jnp.pad(v, [(0, 0)] * len(lead) + [(0, 0), (0, LANES - heads_per_group)])
        return v.reshape(lead + (n_groups * LANES,))

    w_xbc = w_in[:, d_inner:d_inner + conv_dim].astype(BF16)
    w_dt = jnp.pad(w_in[:, d_inner + conv_dim:], ((0, 0), (0, LANES - n_heads))).astype(BF16)
    dt_b = jnp.pad(dt_bias, (0, LANES - n_heads))[None, :]
    head_of_lane = jnp.arange(n_groups * LANES)
    head_of_lane = jnp.where(head_of_lane % LANES < heads_per_group,
                             (head_of_lane // LANES) * heads_per_group + head_of_lane % LANES, -1)
    spread = (jnp.arange(LANES)[:, None] == head_of_lane[None, :]).astype(BF16)
    alog = per_group_lanes(a_log).reshape(n_groups, LANES)
    dskip = jnp.repeat(d_skip, SSD_HEAD_DIM).reshape(n_groups, gw)
    lane_of = jnp.arange(gw) // SSD_HEAD_DIM
    expand_mat = (jnp.arange(LANES)[:, None] == lane_of[None, :]).astype(BF16)

    dt, x_bf = _dt_proj(x, w_dt, dt_b, spread, tm=min(seq, 1024))
    z = _proj(x_bf, w_in, batch=batch, seq=seq, tn=1024, n_dim=d_inner)
    xbc = _proj(x_bf, w_xbc, batch=batch, seq=seq, tn=1024, conv=(conv_w, conv_b[None, :]))
    yn = _ssd_scan(xbc, z, dt, alog, dskip, norm_g[None, :], expand_mat, batch=batch, seq=seq, d_inner=d_inner)
    return _outproj_ln(yn, w_out.astype(BF16), x, ln_g[None, :], ln_b[None, :], alpha=alpha, tm=512)


def _dsa_layer(x, x_bf, w_in, q_norm_g, kv_norm_g, w_uq, w_uk, w_uv, w_idx_q, w_out, ln_g, ln_b,
               *, batch, seq, alpha):
    assert seq % IDX_TQ == 0 and IDX_TQ % ATT_TQ == 0 and IDX_HEAD_DIM + IDX_N_HEADS <= LANES
    small = DSA_Q_RANK + DSA_KV_RANK + IDX_HEAD_DIM + IDX_N_HEADS
    w_small = jnp.pad(w_in[:, :small], ((0, 0), (0, DSA_Q_RANK + DSA_KV_RANK + LANES - small))).astype(BF16)
    w_gate = w_in[:, small:]
    wuq_t = w_uq.T.astype(BF16)
    wuv_t = w_uv.transpose(0, 2, 1).astype(BF16)
    widx_t = w_idx_q.reshape(DSA_Q_RANK, IDX_N_HEADS, IDX_HEAD_DIM).transpose(1, 2, 0)
    widx_t = jnp.pad(widx_t, ((0, 0), (0, LANES - IDX_HEAD_DIM), (0, 0))).astype(BF16)
    k_top = min(IDX_TOPK, seq // 4)

    cq_t, ckv, ckv_t, slab, slab_t = _dsa_latent(x_bf, w_small, q_norm_g[None, :], kv_norm_g[None, :],
                                                     batch=batch, seq=seq)
    gate = _proj(x_bf, w_gate, batch=batch, seq=seq, tn=1024)
    ql = _dsa_query(cq_t, wuq_t, w_uk.astype(BF16), batch=batch, seq=seq,
                    scale=DSA_HEAD_DIM ** -0.5 * math.log2(math.e))
    bias = _dsa_indexer(slab, slab_t, cq_t, widx_t, batch=batch, seq=seq, k_top=k_top,
                        w_scale=IDX_N_HEADS ** -0.5 * IDX_HEAD_DIM ** -0.5)
    og = _dsa_attention(ql, bias, ckv_t, ckv, wuv_t, gate, batch=batch, seq=seq)
    return _outproj_ln(og, w_out.astype(BF16), x, ln_g[None, :], ln_b[None, :], alpha=alpha, tm=512)


def kernel(x, ssd_w_in, ssd_conv_w, ssd_conv_b, ssd_dt_bias, ssd_a_log, ssd_d_skip, ssd_norm_g, ssd_w_out,
           dsa_w_in, dsa_q_norm_g, dsa_kv_norm_g, dsa_w_uq, dsa_w_uk, dsa_w_uv, dsa_w_idx_q, dsa_w_out,
           ln_g, ln_b):
    batch, seq, d_model = x.shape
    depth = ln_g.shape[0]
    alpha = (2.0 * depth) ** 0.25
    xf = x.reshape(batch * seq, d_model)
    x_bf = None
    for i in range(depth):
        j = i // 2
        if i % 2 == 0:
            xf, x_bf = _ssd_layer(xf, ssd_w_in[j], ssd_conv_w[j], ssd_conv_b[j], ssd_dt_bias[j],
                                  ssd_a_log[j], ssd_d_skip[j], ssd_norm_g[j], ssd_w_out[j], ln_g[i], ln_b[i],
                                  batch=batch, seq=seq, alpha=alpha)
        else:
            x_bf = xf.astype(BF16) if x_bf is None else x_bf
            xf, x_bf = _dsa_layer(xf, x_bf, dsa_w_in[j], dsa_q_norm_g[j], dsa_kv_norm_g[j], dsa_w_uq[j],
                                  dsa_w_uk[j], dsa_w_uv[j], dsa_w_idx_q[j], dsa_w_out[j], ln_g[i], ln_b[i],
                                  batch=batch, seq=seq, alpha=alpha)
    return xf.reshape(batch, seq, d_model)
```

```python
import functools
import math

import jax
import jax.numpy as jnp
from jax import lax
from jax.experimental import pallas as pl
from jax.experimental.pallas import tpu as pltpu

F32 = jnp.float32
BF16 = jnp.bfloat16

V7X_VMEM_BYTES = 64 * 1024 * 1024
LANES = 128
SUBLANES = 8
VMEM_LIMIT_BYTES = V7X_VMEM_BYTES - 8 * 1024 * 1024

LN_EPS = 1e-5
RMS_EPS = 1e-6

SSD_HEAD_DIM = 64
SSD_N_GROUPS = 8
SSD_D_STATE = 128
SSD_CHUNK = 256
DSA_N_HEADS = 32
DSA_HEAD_DIM = 128
DSA_Q_RANK = 512
DSA_KV_RANK = 256
IDX_N_HEADS = 16
IDX_HEAD_DIM = 64
IDX_TOPK = 256

ATT_TQ = 256
ATT_STEP_ELEMS = 8 * 1024 * 1024
ATT_CHUNK_COLS = 2048
IDX_TQ = 512
BISECT_ITERS = 40
ONES_ROWS = 16
CONV_ROW_CHUNK = 256
LN_ROW_CHUNK = 128
SSD_GROUPS_PER_STEP = 2


def _cparams(n_axes, flags=None):
    return pltpu.CompilerParams(dimension_semantics=("arbitrary",) * n_axes,
                                vmem_limit_bytes=VMEM_LIMIT_BYTES, flags=flags)


LOG2_E = math.log2(math.e)


def _sigmoid(v):
    return 1.0 / (1.0 + jnp.exp2(v * (-LOG2_E)))


def _proj_silu_kernel(x_ref, w_ref, o_ref, *, chunk):
    w = w_ref[...]
    for r0 in range(0, o_ref.shape[0], chunk):
        y = jnp.dot(x_ref[r0:r0 + chunk, :], w, preferred_element_type=F32)
        o_ref[r0:r0 + chunk, :] = (y * _sigmoid(y)).astype(o_ref.dtype)


def _proj_conv_silu_kernel(x_ref, w_ref, conv_w_ref, conv_b_ref, o_ref, y_scr, *, chunk):
    seq, tn = o_ref.shape
    k_taps = conv_w_ref.shape[0]
    head = y_scr.shape[0] - seq
    y_scr[:head, :] = jnp.zeros((head, tn), F32)
    w = w_ref[...]

    def matmul(r0):
        y_scr[head + r0:head + r0 + chunk, :] = jnp.dot(x_ref[r0:r0 + chunk, :], w, preferred_element_type=F32)

    def finish(r0):
        acc = conv_b_ref[...] + conv_w_ref[k_taps - 1:k_taps, :] * y_scr[head + r0:head + r0 + chunk, :]
        for back in range(1, k_taps):
            tap = conv_w_ref[k_taps - 1 - back:k_taps - back, :]
            acc = acc + tap * y_scr[head + r0 - back:head + r0 - back + chunk, :]
        o_ref[r0:r0 + chunk, :] = (acc * _sigmoid(acc)).astype(o_ref.dtype)

    for r0 in range(0, seq, chunk):
        matmul(r0)
        finish(r0)


def _proj(x, w, *, batch, seq, tn, col0=0, n_dim=None, conv=None, out_dtype=BF16):
    k_dim = w.shape[0]
    n_dim = w.shape[1] - col0 if n_dim is None else n_dim
    assert n_dim % tn == 0 and tn % LANES == 0 and col0 % tn == 0
    j0 = col0 // tn
    in_specs = [pl.BlockSpec((seq, k_dim), lambda b, j: (b, 0)),
                pl.BlockSpec((k_dim, tn), lambda b, j: (0, j0 + j))]
    if conv is None:
        body = functools.partial(_proj_silu_kernel, chunk=CONV_ROW_CHUNK)
        args, scratch, name = (x, w), [], "proj_silu"
    else:
        body = functools.partial(_proj_conv_silu_kernel, chunk=CONV_ROW_CHUNK)
        args = (x, w) + tuple(conv)
        in_specs += [pl.BlockSpec((c.shape[0], tn), lambda b, j: (0, j)) for c in conv]
        assert conv[0].shape[0] - 1 <= SUBLANES
        scratch, name = [pltpu.VMEM((SUBLANES + seq, tn), F32)], "proj_conv_silu"
    return pl.pallas_call(
        body,
        grid=(batch, n_dim // tn),
        in_specs=in_specs,
        out_specs=pl.BlockSpec((seq, tn), lambda b, j: (b, j)),
        out_shape=jax.ShapeDtypeStruct((batch * seq, n_dim), out_dtype),
        scratch_shapes=scratch,
        compiler_params=_cparams(2),
        name=name,
    )(*args)


def _dt_kernel(x_ref, w_ref, b_ref, spread_ref, o_ref, xbf_ref):
    x = x_ref[...].astype(BF16)
    xbf_ref[...] = x
    y = jnp.dot(x, w_ref[...], preferred_element_type=F32) + b_ref[...]
    y = jnp.maximum(y, 0.0) + jnp.log1p(jnp.exp(-jnp.abs(y)))
    spread = spread_ref[...]
    hi = y.astype(BF16)
    rest = y - hi.astype(F32)
    mid = rest.astype(BF16)
    lo = (rest - mid.astype(F32)).astype(BF16)
    o_ref[...] = (jnp.dot(hi, spread, preferred_element_type=F32)
                  + jnp.dot(mid, spread, preferred_element_type=F32)
                  + jnp.dot(lo, spread, preferred_element_type=F32))


def _dt_proj(x, w, bias, spread, *, tm):
    rows, k_dim = x.shape
    width = spread.shape[1]
    return pl.pallas_call(
        _dt_kernel,
        grid=(rows // tm,),
        in_specs=[pl.BlockSpec((tm, k_dim), lambda i: (i, 0)),
                  pl.BlockSpec((k_dim, LANES), lambda i: (0, 0)),
                  pl.BlockSpec((1, LANES), lambda i: (0, 0)),
                  pl.BlockSpec((LANES, width), lambda i: (0, 0))],
        out_specs=[pl.BlockSpec((tm, width), lambda i: (i, 0)),
                   pl.BlockSpec((tm, k_dim), lambda i: (i, 0))],
        out_shape=[jax.ShapeDtypeStruct((rows, width), F32),
                   jax.ShapeDtypeStruct((rows, k_dim), BF16)],
        compiler_params=_cparams(1),
        name="dt_proj",
    )(x, w, bias, spread)


def _cumsum_rows(v):
    n = v.shape[0]
    row = lax.broadcasted_iota(jnp.int32, v.shape, 0)
    shift = 1
    while shift < n:
        v = v + jnp.where(row >= shift, pltpu.roll(v, shift, 0), 0.0)
        shift *= 2
    return v


def _ssd_kernel(xs_ref, b_ref, c_ref, z_ref, dt_ref, alog_ref, dskip_ref, ng_ref, e_ref, o_ref,
                state_ref, *, groups, heads_per_group, head_dim):
    @pl.when(pl.program_id(2) == 0)
    def _():
        state_ref[...] = jnp.zeros_like(state_ref)

    gw = e_ref.shape[1]
    n = b_ref.shape[1] // groups
    for s in range(groups):
        g = pl.program_id(1) * groups + s
        lanes = slice(s * gw, (s + 1) * gw)
        _ssd_group(xs_ref.at[:, lanes], b_ref.at[:, s * n:(s + 1) * n], c_ref.at[:, s * n:(s + 1) * n],
                   z_ref.at[:, lanes], dt_ref.at[:, s * LANES:(s + 1) * LANES], alog_ref[pl.ds(g, 1), :],
                   dskip_ref[pl.ds(g, 1), :], ng_ref.at[:, lanes], e_ref, o_ref.at[:, lanes], state_ref.at[s],
                   heads_per_group=heads_per_group, head_dim=head_dim)


def _ssd_group(xs_ref, b_ref, c_ref, z_ref, dt_ref, alog, dskip, ng_ref, e_ref, o_ref, state_ref,
               *, heads_per_group, head_dim):
    q = xs_ref.shape[0]
    xs = xs_ref[...].astype(F32)
    bm = b_ref[...]
    cm = c_ref[...]
    dt = dt_ref[...]
    a = -jnp.exp(alog) * LOG2_E
    acs = _cumsum_rows(dt * a)
    acs_t = acs.T
    last = acs[q - 1:q, :]
    expand_mat = e_ref[...]

    def expand(v):
        return jnp.dot(v.astype(BF16), expand_mat, preferred_element_type=F32)

    def expand_hi_lo(v):
        hi = v.astype(BF16)
        lo = (v - hi.astype(F32)).astype(BF16)
        return (jnp.dot(hi, expand_mat, preferred_element_type=F32)
                + jnp.dot(lo, expand_mat, preferred_element_type=F32))

    decay_in_x = expand_hi_lo(jnp.exp2(acs))
    decay_out_dt_x = expand(jnp.exp2(last - acs) * dt)

    bc = lax.dot_general(bm, cm, (((1,), (1,)), ((), ())), preferred_element_type=F32)
    blk = LANES
    tri = (lax.broadcasted_iota(jnp.int32, (blk, blk), 1) >= lax.broadcasted_iota(jnp.int32, (blk, blk), 0))
    xs_t = xs.T
    dt_t = dt.T

    def masked_bc(r):
        rows = []
        for s0 in range(0, q, blk):
            parts = []
            if s0 > 0:
                parts.append(jnp.zeros((blk, s0), BF16))
            diff = acs_t[r:r + 1, s0:s0 + blk] - acs[s0:s0 + blk, r:r + 1]
            decay = jnp.exp2(jnp.where(tri, diff, -jnp.inf))
            parts.append((bc[s0:s0 + blk, s0:s0 + blk] * decay).astype(BF16))
            if s0 + blk < q:
                diff = acs_t[r:r + 1, s0 + blk:] - acs[s0:s0 + blk, r:r + 1]
                parts.append((bc[s0:s0 + blk, s0 + blk:] * jnp.exp2(diff)).astype(BF16))
            rows.append(jnp.concatenate(parts, axis=1) if len(parts) > 1 else parts[0])
        return jnp.concatenate(rows, axis=0) if len(rows) > 1 else rows[0]

    y_t = []
    for r in range(heads_per_group):
        x_head_t = (xs_t[r * head_dim:(r + 1) * head_dim, :] * dt_t[r:r + 1, :]).astype(BF16)
        y_t.append(jnp.dot(x_head_t, masked_bc(r), preferred_element_type=F32))
    y_diag = jnp.concatenate(y_t, axis=0).T

    s_prev = state_ref[...]
    y_off = jnp.dot(cm, s_prev.astype(BF16), preferred_element_type=F32) * decay_in_x
    y = y_diag + y_off + dskip * xs
    b_t = bm.astype(F32).T.astype(BF16)
    s_new = jnp.dot(b_t, (xs * decay_out_dt_x).astype(BF16), preferred_element_type=F32)
    state_ref[...] = s_prev * decay_in_x[q - 1:q, :] + s_new

    yg = y * z_ref[...].astype(F32)
    ms = jnp.mean(yg * yg, axis=-1, keepdims=True)
    o_ref[...] = (yg * lax.rsqrt(ms + RMS_EPS) * ng_ref[...]).astype(o_ref.dtype)


def _ssd_scan(xbc, z, dt, alog, dskip, norm_g, expand_mat, *, batch, seq, d_inner):
    n_groups, d_state, chunk = SSD_N_GROUPS, SSD_D_STATE, math.gcd(SSD_CHUNK, seq)
    gw = d_inner // n_groups
    heads_per_group = gw // SSD_HEAD_DIM
    n_chunks = seq // chunk
    gps = SSD_GROUPS_PER_STEP
    assert n_groups % gps == 0
    b_col0 = d_inner // (gps * d_state)
    c_col0 = b_col0 + n_groups // gps
    row = lambda b, g, c: b * n_chunks + c
    return pl.pallas_call(
        functools.partial(_ssd_kernel, groups=gps, heads_per_group=heads_per_group, head_dim=SSD_HEAD_DIM),
        grid=(batch, n_groups // gps, n_chunks),
        in_specs=[
            pl.BlockSpec((chunk, gps * gw), lambda b, g, c: (row(b, g, c), g)),
            pl.BlockSpec((chunk, gps * d_state), lambda b, g, c: (row(b, g, c), b_col0 + g)),
            pl.BlockSpec((chunk, gps * d_state), lambda b, g, c: (row(b, g, c), c_col0 + g)),
            pl.BlockSpec((chunk, gps * gw), lambda b, g, c: (row(b, g, c), g)),
            pl.BlockSpec((chunk, gps * LANES), lambda b, g, c: (row(b, g, c), g)),
            pl.BlockSpec((n_groups, LANES), lambda b, g, c: (0, 0)),
            pl.BlockSpec((n_groups, gw), lambda b, g, c: (0, 0)),
            pl.BlockSpec((1, gps * gw), lambda b, g, c: (0, g)),
            pl.BlockSpec((LANES, gw), lambda b, g, c: (0, 0)),
        ],
        out_specs=pl.BlockSpec((chunk, gps * gw), lambda b, g, c: (row(b, g, c), g)),
        out_shape=jax.ShapeDtypeStruct((batch * seq, d_inner), BF16),
        scratch_shapes=[pltpu.VMEM((gps, d_state, gw), F32)],
        compiler_params=_cparams(3),
        name="ssd_scan",
    )(xbc, xbc, xbc, z, dt, alog, dskip, norm_g, expand_mat)


def _outproj_ln_kernel(y_ref, w_ref, x_ref, g_ref, b_ref, o_ref, obf_ref, *, alpha, chunk):
    for r0 in range(0, y_ref.shape[0], chunk):
        rows = slice(r0, r0 + chunk)
        h = jnp.dot(y_ref[rows, :], w_ref[...], preferred_element_type=F32)
        v = alpha * x_ref[rows, :] + h
        mu = jnp.mean(v, axis=-1, keepdims=True)
        d = v - mu
        var = jnp.mean(d * d, axis=-1, keepdims=True)
        out = d * lax.rsqrt(var + LN_EPS) * g_ref[...] + b_ref[...]
        o_ref[rows, :] = out
        obf_ref[rows, :] = out.astype(BF16)


def _outproj_ln(y, w, x, ln_g, ln_b, *, alpha, tm):
    m, k_dim = y.shape
    d = w.shape[1]
    return pl.pallas_call(
        functools.partial(_outproj_ln_kernel, alpha=alpha, chunk=LN_ROW_CHUNK),
        grid=(m // tm,),
        in_specs=[pl.BlockSpec((tm, k_dim), lambda i: (i, 0)),
                  pl.BlockSpec((k_dim, d), lambda i: (0, 0), pipeline_mode=pl.Buffered(1)),
                  pl.BlockSpec((tm, d), lambda i: (i, 0)),
                  pl.BlockSpec((1, d), lambda i: (0, 0)),
                  pl.BlockSpec((1, d), lambda i: (0, 0))],
        out_specs=[pl.BlockSpec((tm, d), lambda i: (i, 0)),
                   pl.BlockSpec((tm, d), lambda i: (i, 0))],
        out_shape=[jax.ShapeDtypeStruct((m, d), F32), jax.ShapeDtypeStruct((m, d), BF16)],
        compiler_params=_cparams(1),
        name="outproj_ln",
    )(y, w, x, ln_g, ln_b)


def _dsa_latent_kernel(x_ref, w_ref, qg_ref, kvg_ref, cqt_ref, ckv_ref, ckvt_ref, slab_ref, slabt_ref):
    p = jnp.dot(x_ref[...], w_ref[...], preferred_element_type=F32)
    q_rank = cqt_ref.shape[1]
    kv_rank = ckv_ref.shape[1]
    cq = p[:, :q_rank]
    ckv = p[:, q_rank:q_rank + kv_rank]
    slab = p[:, q_rank + kv_rank:]
    cq = cq * lax.rsqrt(jnp.mean(cq * cq, axis=-1, keepdims=True) + RMS_EPS) * qg_ref[...]
    ckv = ckv * lax.rsqrt(jnp.mean(ckv * ckv, axis=-1, keepdims=True) + RMS_EPS) * kvg_ref[...]
    cqt_ref[0] = cq.T.astype(BF16)
    ckv_ref[...] = ckv.astype(BF16)
    ckvt_ref[0, :kv_rank, :] = ckv.T.astype(BF16)
    ckvt_ref[0, kv_rank:, :] = jnp.ones((ckvt_ref.shape[1] - kv_rank, ckvt_ref.shape[2]), BF16)
    slab_ref[...] = slab.astype(BF16)
    slabt_ref[0] = slab.T


def _dsa_latent(x, w, q_gain, kv_gain, *, batch, seq):
    tm = IDX_TQ
    n_t = seq // tm
    k_dim, n_dim = w.shape
    return pl.pallas_call(
        _dsa_latent_kernel,
        grid=(batch, n_t),
        in_specs=[pl.BlockSpec((tm, k_dim), lambda b, t: (b * n_t + t, 0)),
                  pl.BlockSpec((k_dim, n_dim), lambda b, t: (0, 0)),
                  pl.BlockSpec((1, DSA_Q_RANK), lambda b, t: (0, 0)),
                  pl.BlockSpec((1, DSA_KV_RANK), lambda b, t: (0, 0))],
        out_specs=[pl.BlockSpec((1, DSA_Q_RANK, tm), lambda b, t: (b, 0, t)),
                   pl.BlockSpec((tm, DSA_KV_RANK), lambda b, t: (b * n_t + t, 0)),
                   pl.BlockSpec((1, DSA_KV_RANK + ONES_ROWS, tm), lambda b, t: (b, 0, t)),
                   pl.BlockSpec((tm, LANES), lambda b, t: (b * n_t + t, 0)),
                   pl.BlockSpec((1, LANES, tm), lambda b, t: (b, 0, t))],
        out_shape=[jax.ShapeDtypeStruct((batch, DSA_Q_RANK, seq), BF16),
                   jax.ShapeDtypeStruct((batch * seq, DSA_KV_RANK), BF16),
                   jax.ShapeDtypeStruct((batch, DSA_KV_RANK + ONES_ROWS, seq), BF16),
                   jax.ShapeDtypeStruct((batch * seq, LANES), BF16),
                   jax.ShapeDtypeStruct((batch, LANES, seq), F32)],
        compiler_params=_cparams(2),
        name="dsa_latent",
    )(x, w, q_gain, kv_gain)


def _dsa_query_kernel(cqt_ref, wuqt_ref, wuk_ref, ql_ref, *, scale):
    hb, _, hd = wuk_ref.shape
    tq = ql_ref.shape[3] // hb
    q_t = jnp.dot(wuqt_ref[...], cqt_ref[0], preferred_element_type=F32).astype(BF16)
    for j in range(hb):
        ql_t = jnp.dot(wuk_ref[j], q_t[j * hd:(j + 1) * hd, :], preferred_element_type=F32)
        ql_t = (ql_t * scale).astype(BF16)
        for blk in range(ql_ref.shape[1]):
            ql_ref[0, blk, :, j * tq:(j + 1) * tq] = ql_t[:, blk * tq:(blk + 1) * tq]


def _dsa_query(cq_t, wuq_t, wuk, *, batch, seq, scale):
    tm = min(seq, 2048)
    hb = 8
    tq = ATT_TQ
    n_t = seq // tm
    n_heads, kv_rank, hd = wuk.shape
    return pl.pallas_call(
        functools.partial(_dsa_query_kernel, scale=scale),
        grid=(batch, n_t, n_heads // hb),
        in_specs=[pl.BlockSpec((1, DSA_Q_RANK, tm), lambda b, t, h: (b, 0, t)),
                  pl.BlockSpec((hb * hd, DSA_Q_RANK), lambda b, t, h: (h, 0)),
                  pl.BlockSpec((hb, kv_rank, hd), lambda b, t, h: (h, 0, 0))],
        out_specs=pl.BlockSpec((1, tm // tq, kv_rank, hb * tq), lambda b, t, h: (b, t, 0, h)),
        out_shape=jax.ShapeDtypeStruct((batch, seq // tq, kv_rank, n_heads * tq), BF16),
        compiler_params=_cparams(3),
        name="dsa_query",
    )(cq_t, wuq_t, wuk)


def _indexer_kernel(keys_ref, slabt_ref, cqt_ref, widx_ref, bias_ref, qt_scr, sc_scr, *, k_top, w_scale, n_iter):
    i = pl.program_id(1)
    n_t, kt_w, tq = sc_scr.shape
    cq_t = cqt_ref[0]
    n_h, hd_pad, q_rank = widx_ref.shape
    q_all = jnp.dot(widx_ref[...].reshape(n_h * hd_pad, q_rank), cq_t, preferred_element_type=F32)
    qt_scr[...] = q_all.astype(BF16).reshape(n_h, hd_pad, tq)
    w_t = slabt_ref[0][IDX_HEAD_DIM:IDX_HEAD_DIM + IDX_N_HEADS, :] * w_scale
    q_pos = i * tq + lax.broadcasted_iota(jnp.int32, (kt_w, tq), 1)

    for kt in range(n_t):
        @pl.when(kt <= i)
        def _(kt=kt):
            keys = keys_ref[kt * kt_w:(kt + 1) * kt_w, :]
            acc = jnp.zeros((kt_w, tq), F32)
            for h in range(IDX_N_HEADS):
                s = jnp.dot(keys, qt_scr[h], preferred_element_type=F32)
                acc = acc + jnp.maximum(s, 0.0) * w_t[h:h + 1, :]
            k_pos = kt * kt_w + lax.broadcasted_iota(jnp.int32, (kt_w, tq), 0)
            sc_scr[kt] = jnp.where(k_pos <= q_pos, acc, -jnp.inf)

    def bounds(kt, carry):
        lo, hi = carry
        s = sc_scr[kt]
        hi = jnp.maximum(hi, jnp.max(s, axis=0, keepdims=True))
        lo = jnp.minimum(lo, jnp.min(jnp.where(s == -jnp.inf, jnp.inf, s), axis=0, keepdims=True))
        return lo, hi

    lo, hi = lax.fori_loop(0, i + 1, bounds,
                           (jnp.full((1, tq), jnp.inf, F32), jnp.full((1, tq), -jnp.inf, F32)))

    def halve(carry):
        it, lo, hi, n_lo = carry
        mid = 0.5 * lo + 0.5 * hi

        mid_rows = jnp.broadcast_to(mid, (SUBLANES, tq))

        def count(kt, cnt):
            for r0 in range(0, kt_w, SUBLANES):
                cnt = jnp.where(sc_scr[kt, r0:r0 + SUBLANES, :] >= mid_rows, cnt + 1.0, cnt)
            return cnt

        cnt = lax.fori_loop(0, i + 1, count, jnp.zeros((SUBLANES, tq), F32))
        cnt = jnp.sum(cnt, axis=0, keepdims=True)
        enough = cnt >= k_top
        return it + 1, jnp.where(enough, mid, lo), jnp.where(enough, hi, mid), jnp.where(enough, cnt, n_lo)

    def unresolved(carry):
        it, _, _, n_lo = carry
        return jnp.logical_and(it < n_iter, jnp.max(n_lo) > k_top)

    n_causal = (i * tq + lax.broadcasted_iota(jnp.int32, (1, tq), 1) + 1).astype(F32)
    _, lo, hi, _ = lax.while_loop(unresolved, lambda c: halve(halve(c)), (jnp.int32(0), lo, hi, n_causal))

    for kt in range(n_t):
        @pl.when(kt <= i)
        def _(kt=kt):
            bias_ref[0, kt * kt_w:(kt + 1) * kt_w, :] = jnp.where(sc_scr[kt] >= lo, 0.0, -jnp.inf)

        @pl.when(kt > i)
        def _(kt=kt):
            bias_ref[0, kt * kt_w:(kt + 1) * kt_w, :] = jnp.full((kt_w, tq), -jnp.inf, F32)


def _dsa_indexer(keys, slab_t, cq_t, widx_t, *, batch, seq, k_top, w_scale):
    tq = IDX_TQ
    n_t = seq // tq
    return pl.pallas_call(
        functools.partial(_indexer_kernel, k_top=k_top, w_scale=w_scale, n_iter=BISECT_ITERS),
        grid=(batch, n_t),
        in_specs=[pl.BlockSpec((seq, LANES), lambda b, i: (b, 0)),
                  pl.BlockSpec((1, LANES, tq), lambda b, i: (b, 0, i)),
                  pl.BlockSpec((1, DSA_Q_RANK, tq), lambda b, i: (b, 0, i)),
                  pl.BlockSpec(widx_t.shape, lambda b, i: (0, 0, 0))],
        out_specs=pl.BlockSpec((1, seq, tq), lambda b, i: (b, 0, i)),
        out_shape=jax.ShapeDtypeStruct((batch, seq, seq), F32),
        scratch_shapes=[pltpu.VMEM((IDX_N_HEADS, LANES, tq), BF16),
                        pltpu.VMEM((n_t, tq, tq), F32)],
        compiler_params=_cparams(2),
        name="dsa_indexer",
    )(keys, slab_t, cq_t, widx_t)


def _attn_kernel(ql_ref, bias_ref, k_ref, vt_ref, wuvt_ref, gate_ref, *rest, n_keys, hc):
    o_ref = rest[-1]
    tq = bias_ref.shape[2]
    hb, hd, kv_rank = wuvt_ref.shape
    cols = hc * tq
    n_chunks = hb // hc
    keys = k_ref[:n_keys, :]
    bias_t = bias_ref[0]
    bias_rep = jnp.concatenate([bias_t] * hc, axis=1) if hc > 1 else bias_t

    def scores(c):
        q_t = ql_ref[0, 0, :, c * cols:(c + 1) * cols]
        return jnp.dot(keys, q_t, preferred_element_type=F32) + bias_rep

    def finish(c, s):
        m = jnp.max(s, axis=0, keepdims=True)
        p = jnp.exp2(s - m).astype(BF16)
        o_aug = jnp.dot(vt_ref[0], p, preferred_element_type=F32)
        o_t = (o_aug[:kv_rank] / o_aug[kv_rank:kv_rank + 1]).astype(BF16)
        for j in range(hc):
            h = c * hc + j
            oh_t = jnp.dot(wuvt_ref[h], o_t[:, j * tq:(j + 1) * tq], preferred_element_type=F32)
            gate = gate_ref[:, h * hd:(h + 1) * hd].astype(F32)
            o_ref[:, h * hd:(h + 1) * hd] = (oh_t.T * gate).astype(o_ref.dtype)

    s_next = scores(0)
    for c in range(n_chunks):
        s_cur = s_next
        if c + 1 < n_chunks:
            s_next = scores(c + 1)
        finish(c, s_cur)


def _dsa_attention(ql, bias, ckv_t, ckv, wuv_t, gate, *, batch, seq):
    tq = ATT_TQ
    n_q = seq // tq
    n_heads, hd, kv_rank = wuv_t.shape
    width = n_heads * hd
    og = None
    for i in range(n_q):
        n_keys = (i + 1) * tq
        hb = n_heads
        while hb > 1 and hb * tq * n_keys > ATT_STEP_ELEMS:
            hb //= 2
        hc = max(1, min(hb, ATT_CHUNK_COLS // tq))
        in_specs = [pl.BlockSpec((1, 1, kv_rank, hb * tq), lambda b, g, i=i: (b, i, 0, g)),
                    pl.BlockSpec((1, n_keys, tq), lambda b, g, i=i: (b, 0, i)),
                    pl.BlockSpec((seq, kv_rank), lambda b, g: (b, 0)),
                    pl.BlockSpec((1, kv_rank + ONES_ROWS, n_keys), lambda b, g: (b, 0, 0)),
                    pl.BlockSpec((hb, hd, kv_rank), lambda b, g: (g, 0, 0)),
                    pl.BlockSpec((tq, hb * hd), lambda b, g, i=i: (b * n_q + i, g))]
        args = [ql, bias, ckv, ckv_t, wuv_t, gate]
        aliases = {}
        if og is not None:
            in_specs.append(pl.BlockSpec(memory_space=pl.ANY))
            args.append(og)
            aliases = {len(args) - 1: 0}
        og = pl.pallas_call(
            functools.partial(_attn_kernel, n_keys=n_keys, hc=hc),
            grid=(batch, n_heads // hb),
            in_specs=in_specs,
            out_specs=pl.BlockSpec((tq, hb * hd), lambda b, g, i=i: (b * n_q + i, g)),
            out_shape=jax.ShapeDtypeStruct((batch * seq, width), BF16),
            input_output_aliases=aliases,
            compiler_params=_cparams(2),
            name="dsa_attention_q%d" % i,
        )(*args)
    return og


def _ssd_layer(x, w_in, conv_w, conv_b, dt_bias, a_log, d_skip, norm_g, w_out, ln_g, ln_b,
               *, batch, seq, alpha):
    d_inner = w_out.shape[0]
    n_heads = a_log.shape[0]
    n_groups = SSD_N_GROUPS
    heads_per_group = n_heads // n_groups
    conv_dim = conv_w.shape[1]
    gw = d_inner // n_groups
    assert gw % LANES == 0 and heads_per_group <= LANES and SSD_D_STATE == LANES
    assert LANES % SSD_HEAD_DIM == 0 and heads_per_group % (LANES // SSD_HEAD_DIM) == 0

    def per_group_lanes(v):
        lead = v.shape[:-1]
        v = v.reshape(lead + (n_groups, heads_per_group))
        v = jnp.pad(v, [(0, 0)] * len(lead) + [(0, 0), (0, LANES - heads_per_group)])
        return v.reshape(lead + (n_groups * LANES,))

    w_bf = w_in.astype(BF16)
    w_dt = jnp.pad(w_bf[:, d_inner + conv_dim:], ((0, 0), (0, LANES - n_heads)))
    dt_b = jnp.pad(dt_bias, (0, LANES - n_heads))[None, :]
    head_of_lane = jnp.arange(n_groups * LANES)
    head_of_lane = jnp.where(head_of_lane % LANES < heads_per_group,
                             (head_of_lane // LANES) * heads_per_group + head_of_lane % LANES, -1)
    spread = (jnp.arange(LANES)[:, None] == head_of_lane[None, :]).astype(BF16)
    alog = per_group_lanes(a_log).reshape(n_groups, LANES)
    dskip = jnp.repeat(d_skip, SSD_HEAD_DIM).reshape(n_groups, gw)
    lane_of = jnp.arange(gw) // SSD_HEAD_DIM
    expand_mat = (jnp.arange(LANES)[:, None] == lane_of[None, :]).astype(BF16)

    dt, x_bf = _dt_proj(x, w_dt, dt_b, spread, tm=min(seq, 1024))
    z = _proj(x_bf, w_bf, batch=batch, seq=seq, tn=1024, n_dim=d_inner)
    xbc = _proj(x_bf, w_bf, batch=batch, seq=seq, tn=1024, col0=d_inner, n_dim=conv_dim,
                conv=(conv_w, conv_b[None, :]))
    yn = _ssd_scan(xbc, z, dt, alog, dskip, norm_g[None, :], expand_mat, batch=batch, seq=seq, d_inner=d_inner)
    return _outproj_ln(yn, w_out.astype(BF16), x, ln_g[None, :], ln_b[None, :], alpha=alpha, tm=512)


def _dsa_layer(x, x_bf, w_in, q_norm_g, kv_norm_g, w_uq, w_uk, w_uv, w_idx_q, w_out, ln_g, ln_b,
               *, batch, seq, alpha):
    assert seq % IDX_TQ == 0 and IDX_TQ % ATT_TQ == 0 and IDX_HEAD_DIM + IDX_N_HEADS <= LANES
    small = DSA_Q_RANK + DSA_KV_RANK + IDX_HEAD_DIM + IDX_N_HEADS
    w_small = jnp.pad(w_in[:, :small], ((0, 0), (0, DSA_Q_RANK + DSA_KV_RANK + LANES - small))).astype(BF16)
    w_gate = w_in[:, small:].astype(BF16)
    wuq_t = w_uq.T.astype(BF16)
    wuv_t = w_uv.transpose(0, 2, 1).astype(BF16)
    widx_t = w_idx_q.reshape(DSA_Q_RANK, IDX_N_HEADS, IDX_HEAD_DIM).transpose(1, 2, 0)
    widx_t = jnp.pad(widx_t, ((0, 0), (0, LANES - IDX_HEAD_DIM), (0, 0))).astype(BF16)
    k_top = min(IDX_TOPK, seq // 4)

    cq_t, ckv, ckv_t, slab, slab_t = _dsa_latent(x_bf, w_small, q_norm_g[None, :], kv_norm_g[None, :],
                                                     batch=batch, seq=seq)
    gate = _proj(x_bf, w_gate, batch=batch, seq=seq, tn=1024)
    ql = _dsa_query(cq_t, wuq_t, w_uk.astype(BF16), batch=batch, seq=seq,
                    scale=DSA_HEAD_DIM ** -0.5 * math.log2(math.e))
    bias = _dsa_indexer(slab, slab_t, cq_t, widx_t, batch=batch, seq=seq, k_top=k_top,
                        w_scale=IDX_N_HEADS ** -0.5 * IDX_HEAD_DIM ** -0.5)
    og = _dsa_attention(ql, bias, ckv_t, ckv, wuv_t, gate, batch=batch, seq=seq)
    return _outproj_ln(og, w_out.astype(BF16), x, ln_g[None, :], ln_b[None, :], alpha=alpha, tm=512)


def kernel(x, ssd_w_in, ssd_conv_w, ssd_conv_b, ssd_dt_bias, ssd_a_log, ssd_d_skip, ssd_norm_g, ssd_w_out,
           dsa_w_in, dsa_q_norm_g, dsa_kv_norm_g, dsa_w_uq, dsa_w_uk, dsa_w_uv, dsa_w_idx_q, dsa_w_out,
           ln_g, ln_b):
    batch, seq, d_model = x.shape
    depth = ln_g.shape[0]
    alpha = (2.0 * depth) ** 0.25
    xf = x.reshape(batch * seq, d_model)
    x_bf = None
    for i in range(depth):
        j = i // 2
        if i % 2 == 0:
            xf, x_bf = _ssd_layer(xf, ssd_w_in[j], ssd_conv_w[j], ssd_conv_b[j], ssd_dt_bias[j],
                                  ssd_a_log[j], ssd_d_skip[j], ssd_norm_g[j], ssd_w_out[j], ln_g[i], ln_b[i],
                                  batch=batch, seq=seq, alpha=alpha)
        else:
            x_bf = xf.astype(BF16) if x_bf is None else x_bf
            xf, x_bf = _dsa_layer(xf, x_bf, dsa_w_in[j], dsa_q_norm_g[j], dsa_kv_norm_g[j], dsa_w_uq[j],
                                  dsa_w_uk[j], dsa_w_uv[j], dsa_w_idx_q[j], dsa_w_out[j], ln_g[i], ln_b[i],
                                  batch=batch, seq=seq, alpha=alpha)
    return xf.reshape(batch, seq, d_model)
```

```python
import functools
import math

import jax
import jax.numpy as jnp
from jax import lax
from jax.experimental import pallas as pl
from jax.experimental.pallas import tpu as pltpu

F32 = jnp.float32
BF16 = jnp.bfloat16

V7X_VMEM_BYTES = 64 * 1024 * 1024
LANES = 128
SUBLANES = 8
VMEM_LIMIT_BYTES = V7X_VMEM_BYTES - 8 * 1024 * 1024

LN_EPS = 1e-5
RMS_EPS = 1e-6

SSD_HEAD_DIM = 64
SSD_N_GROUPS = 8
SSD_D_STATE = 128
SSD_CHUNK = 256
DSA_N_HEADS = 32
DSA_HEAD_DIM = 128
DSA_Q_RANK = 512
DSA_KV_RANK = 256
IDX_N_HEADS = 16
IDX_HEAD_DIM = 64
IDX_TOPK = 256

ATT_TQ = 256
ATT_STEP_ELEMS = 8 * 1024 * 1024
ATT_CHUNK_COLS = 2048
IDX_TQ = 512
BISECT_ITERS = 40
ONES_ROWS = 16
CONV_ROW_CHUNK = 256
LN_ROW_CHUNK = 128
SSD_GROUPS_PER_STEP = 2


def _cparams(n_axes):
    return pltpu.CompilerParams(dimension_semantics=("arbitrary",) * n_axes,
                                vmem_limit_bytes=VMEM_LIMIT_BYTES)


LOG2_E = math.log2(math.e)


def _sigmoid(v):
    return 1.0 / (1.0 + jnp.exp2(v * (-LOG2_E)))


def _proj_silu_kernel(x_ref, w_ref, o_ref, *, chunk):
    w = w_ref[...]
    for r0 in range(0, o_ref.shape[0], chunk):
        y = jnp.dot(x_ref[r0:r0 + chunk, :], w, preferred_element_type=F32)
        o_ref[r0:r0 + chunk, :] = (y * _sigmoid(y)).astype(o_ref.dtype)


def _proj_conv_silu_kernel(x_ref, w_ref, conv_w_ref, conv_b_ref, o_ref, y_scr, *, chunk):
    seq, tn = o_ref.shape
    k_taps = conv_w_ref.shape[0]
    head = y_scr.shape[0] - seq
    y_scr[:head, :] = jnp.zeros((head, tn), F32)
    w = w_ref[...]

    def matmul(r0):
        y_scr[head + r0:head + r0 + chunk, :] = jnp.dot(x_ref[r0:r0 + chunk, :], w, preferred_element_type=F32)

    def finish(r0):
        acc = conv_b_ref[...] + conv_w_ref[k_taps - 1:k_taps, :] * y_scr[head + r0:head + r0 + chunk, :]
        for back in range(1, k_taps):
            tap = conv_w_ref[k_taps - 1 - back:k_taps - back, :]
            acc = acc + tap * y_scr[head + r0 - back:head + r0 - back + chunk, :]
        o_ref[r0:r0 + chunk, :] = (acc * _sigmoid(acc)).astype(o_ref.dtype)

    for r0 in range(0, seq, chunk):
        matmul(r0)
        finish(r0)


def _proj(x, w, *, batch, seq, tn, col0=0, n_dim=None, conv=None, out_dtype=BF16):
    k_dim = w.shape[0]
    n_dim = w.shape[1] - col0 if n_dim is None else n_dim
    assert n_dim % tn == 0 and tn % LANES == 0 and col0 % tn == 0
    j0 = col0 // tn
    in_specs = [pl.BlockSpec((seq, k_dim), lambda b, j: (b, 0)),
                pl.BlockSpec((k_dim, tn), lambda b, j: (0, j0 + j))]
    if conv is None:
        body = functools.partial(_proj_silu_kernel, chunk=CONV_ROW_CHUNK)
        args, scratch, name = (x, w), [], "proj_silu"
    else:
        body = functools.partial(_proj_conv_silu_kernel, chunk=CONV_ROW_CHUNK)
        args = (x, w) + tuple(conv)
        in_specs += [pl.BlockSpec((c.shape[0], tn), lambda b, j: (0, j)) for c in conv]
        assert conv[0].shape[0] - 1 <= SUBLANES
        scratch, name = [pltpu.VMEM((SUBLANES + seq, tn), F32)], "proj_conv_silu"
    return pl.pallas_call(
        body,
        grid=(batch, n_dim // tn),
        in_specs=in_specs,
        out_specs=pl.BlockSpec((seq, tn), lambda b, j: (b, j)),
        out_shape=jax.ShapeDtypeStruct((batch * seq, n_dim), out_dtype),
        scratch_shapes=scratch,
        compiler_params=_cparams(2),
        name=name,
    )(*args)


def _dt_kernel(x_ref, w_ref, b_ref, spread_ref, o_ref, xbf_ref):
    x = x_ref[...].astype(BF16)
    xbf_ref[...] = x
    y = jnp.dot(x, w_ref[...], preferred_element_type=F32) + b_ref[...]
    y = jnp.maximum(y, 0.0) + jnp.log1p(jnp.exp(-jnp.abs(y)))
    spread = spread_ref[...]
    hi = y.astype(BF16)
    rest = y - hi.astype(F32)
    mid = rest.astype(BF16)
    lo = (rest - mid.astype(F32)).astype(BF16)
    o_ref[...] = (jnp.dot(hi, spread, preferred_element_type=F32)
                  + jnp.dot(mid, spread, preferred_element_type=F32)
                  + jnp.dot(lo, spread, preferred_element_type=F32))


def _dt_proj(x, w, bias, spread, *, tm):
    rows, k_dim = x.shape
    width = spread.shape[1]
    return pl.pallas_call(
        _dt_kernel,
        grid=(rows // tm,),
        in_specs=[pl.BlockSpec((tm, k_dim), lambda i: (i, 0)),
                  pl.BlockSpec((k_dim, LANES), lambda i: (0, 0)),
                  pl.BlockSpec((1, LANES), lambda i: (0, 0)),
                  pl.BlockSpec((LANES, width), lambda i: (0, 0))],
        out_specs=[pl.BlockSpec((tm, width), lambda i: (i, 0)),
                   pl.BlockSpec((tm, k_dim), lambda i: (i, 0))],
        out_shape=[jax.ShapeDtypeStruct((rows, width), F32),
                   jax.ShapeDtypeStruct((rows, k_dim), BF16)],
        compiler_params=_cparams(1),
        name="dt_proj",
    )(x, w, bias, spread)


def _cumsum_rows(v):
    n = v.shape[0]
    row = lax.broadcasted_iota(jnp.int32, v.shape, 0)
    shift = 1
    while shift < n:
        v = v + jnp.where(row >= shift, pltpu.roll(v, shift, 0), 0.0)
        shift *= 2
    return v


def _ssd_kernel(xs_ref, b_ref, c_ref, z_ref, dt_ref, alog_ref, dskip_ref, ng_ref, e_ref, o_ref,
                state_ref, *, groups, heads_per_group, head_dim):
    @pl.when(pl.program_id(2) == 0)
    def _():
        state_ref[...] = jnp.zeros_like(state_ref)

    gw = e_ref.shape[1]
    n = b_ref.shape[1] // groups
    for s in range(groups):
        g = pl.program_id(1) * groups + s
        lanes = slice(s * gw, (s + 1) * gw)
        _ssd_group(xs_ref.at[:, lanes], b_ref.at[:, s * n:(s + 1) * n], c_ref.at[:, s * n:(s + 1) * n],
                   z_ref.at[:, lanes], dt_ref.at[:, s * LANES:(s + 1) * LANES], alog_ref[pl.ds(g, 1), :],
                   dskip_ref[pl.ds(g, 1), :], ng_ref.at[:, lanes], e_ref, o_ref.at[:, lanes], state_ref.at[s],
                   heads_per_group=heads_per_group, head_dim=head_dim)


def _ssd_group(xs_ref, b_ref, c_ref, z_ref, dt_ref, alog, dskip, ng_ref, e_ref, o_ref, state_ref,
               *, heads_per_group, head_dim):
    q = xs_ref.shape[0]
    xs = xs_ref[...].astype(F32)
    bm = b_ref[...]
    cm = c_ref[...]
    dt = dt_ref[...]
    a = -jnp.exp(alog) * LOG2_E
    acs = _cumsum_rows(dt * a)
    acs_t = acs.T
    last = acs[q - 1:q, :]
    expand_mat = e_ref[...]

    def expand(v):
        return jnp.dot(v.astype(BF16), expand_mat, preferred_element_type=F32)

    def expand_hi_lo(v):
        hi = v.astype(BF16)
        lo = (v - hi.astype(F32)).astype(BF16)
        return (jnp.dot(hi, expand_mat, preferred_element_type=F32)
                + jnp.dot(lo, expand_mat, preferred_element_type=F32))

    decay_in_x = expand_hi_lo(jnp.exp2(acs))
    decay_out_dt_x = expand(jnp.exp2(last - acs) * dt)

    bc = lax.dot_general(bm, cm, (((1,), (1,)), ((), ())), preferred_element_type=F32)
    blk = LANES
    tri = (lax.broadcasted_iota(jnp.int32, (blk, blk), 1) >= lax.broadcasted_iota(jnp.int32, (blk, blk), 0))
    xs_t = xs.T
    dt_t = dt.T

    def masked_bc(r):
        rows = []
        for s0 in range(0, q, blk):
            parts = []
            if s0 > 0:
                parts.append(jnp.zeros((blk, s0), BF16))
            diff = acs_t[r:r + 1, s0:s0 + blk] - acs[s0:s0 + blk, r:r + 1]
            decay = jnp.exp2(jnp.where(tri, diff, -jnp.inf))
            parts.append((bc[s0:s0 + blk, s0:s0 + blk] * decay).astype(BF16))
            if s0 + blk < q:
                diff = acs_t[r:r + 1, s0 + blk:] - acs[s0:s0 + blk, r:r + 1]
                parts.append((bc[s0:s0 + blk, s0 + blk:] * jnp.exp2(diff)).astype(BF16))
            rows.append(jnp.concatenate(parts, axis=1) if len(parts) > 1 else parts[0])
        return jnp.concatenate(rows, axis=0) if len(rows) > 1 else rows[0]

    y_t = []
    for r in range(heads_per_group):
        x_head_t = (xs_t[r * head_dim:(r + 1) * head_dim, :] * dt_t[r:r + 1, :]).astype(BF16)
        y_t.append(jnp.dot(x_head_t, masked_bc(r), preferred_element_type=F32))
    y_diag = jnp.concatenate(y_t, axis=0).T

    s_prev = state_ref[...]
    y_off = jnp.dot(cm, s_prev.astype(BF16), preferred_element_type=F32) * decay_in_x
    y = y_diag + y_off + dskip * xs
    b_t = bm.astype(F32).T.astype(BF16)
    s_new = jnp.dot(b_t, (xs * decay_out_dt_x).astype(BF16), preferred_element_type=F32)
    state_ref[...] = s_prev * decay_in_x[q - 1:q, :] + s_new

    yg = y * z_ref[...].astype(F32)
    ms = jnp.mean(yg * yg, axis=-1, keepdims=True)
    o_ref[...] = (yg * lax.rsqrt(ms + RMS_EPS) * ng_ref[...]).astype(o_ref.dtype)


def _ssd_scan(xbc, z, dt, alog, dskip, norm_g, expand_mat, *, batch, seq, d_inner):
    n_groups, d_state, chunk = SSD_N_GROUPS, SSD_D_STATE, math.gcd(SSD_CHUNK, seq)
    gw = d_inner // n_groups
    heads_per_group = gw // SSD_HEAD_DIM
    n_chunks = seq // chunk
    gps = SSD_GROUPS_PER_STEP
    assert n_groups % gps == 0
    b_col0 = d_inner // (gps * d_state)
    c_col0 = b_col0 + n_groups // gps
    row = lambda b, g, c: b * n_chunks + c
    return pl.pallas_call(
        functools.partial(_ssd_kernel, groups=gps, heads_per_group=heads_per_group, head_dim=SSD_HEAD_DIM),
        grid=(batch, n_groups // gps, n_chunks),
        in_specs=[
            pl.BlockSpec((chunk, gps * gw), lambda b, g, c: (row(b, g, c), g)),
            pl.BlockSpec((chunk, gps * d_state), lambda b, g, c: (row(b, g, c), b_col0 + g)),
            pl.BlockSpec((chunk, gps * d_state), lambda b, g, c: (row(b, g, c), c_col0 + g)),
            pl.BlockSpec((chunk, gps * gw), lambda b, g, c: (row(b, g, c), g)),
            pl.BlockSpec((chunk, gps * LANES), lambda b, g, c: (row(b, g, c), g)),
            pl.BlockSpec((n_groups, LANES), lambda b, g, c: (0, 0)),
            pl.BlockSpec((n_groups, gw), lambda b, g, c: (0, 0)),
            pl.BlockSpec((1, gps * gw), lambda b, g, c: (0, g)),
            pl.BlockSpec((LANES, gw), lambda b, g, c: (0, 0)),
        ],
        out_specs=pl.BlockSpec((chunk, gps * gw), lambda b, g, c: (row(b, g, c), g)),
        out_shape=jax.ShapeDtypeStruct((batch * seq, d_inner), BF16),
        scratch_shapes=[pltpu.VMEM((gps, d_state, gw), F32)],
        compiler_params=_cparams(3),
        name="ssd_scan",
    )(xbc, xbc, xbc, z, dt, alog, dskip, norm_g, expand_mat)


def _outproj_ln_kernel(y_ref, w_ref, x_ref, g_ref, b_ref, o_ref, obf_ref, *, alpha, chunk):
    for r0 in range(0, y_ref.shape[0], chunk):
        rows = slice(r0, r0 + chunk)
        h = jnp.dot(y_ref[rows, :], w_ref[...], preferred_element_type=F32)
        v = alpha * x_ref[rows, :] + h
        mu = jnp.mean(v, axis=-1, keepdims=True)
        d = v - mu
        var = jnp.mean(d * d, axis=-1, keepdims=True)
        out = d * lax.rsqrt(var + LN_EPS) * g_ref[...] + b_ref[...]
        o_ref[rows, :] = out
        obf_ref[rows, :] = out.astype(BF16)


def _outproj_ln(y, w, x, ln_g, ln_b, *, alpha, tm):
    m, k_dim = y.shape
    d = w.shape[1]
    return pl.pallas_call(
        functools.partial(_outproj_ln_kernel, alpha=alpha, chunk=LN_ROW_CHUNK),
        grid=(m // tm,),
        in_specs=[pl.BlockSpec((tm, k_dim), lambda i: (i, 0)),
                  pl.BlockSpec((k_dim, d), lambda i: (0, 0), pipeline_mode=pl.Buffered(1)),
                  pl.BlockSpec((tm, d), lambda i: (i, 0)),
                  pl.BlockSpec((1, d), lambda i: (0, 0)),
                  pl.BlockSpec((1, d), lambda i: (0, 0))],
        out_specs=[pl.BlockSpec((tm, d), lambda i: (i, 0)),
                   pl.BlockSpec((tm, d), lambda i: (i, 0))],
        out_shape=[jax.ShapeDtypeStruct((m, d), F32), jax.ShapeDtypeStruct((m, d), BF16)],
        compiler_params=_cparams(1),
        name="outproj_ln",
    )(y, w, x, ln_g, ln_b)


def _dsa_latent_kernel(x_ref, w_ref, qg_ref, kvg_ref, cqt_ref, ckv_ref, ckvt_ref, slab_ref, slabt_ref):
    p = jnp.dot(x_ref[...], w_ref[...], preferred_element_type=F32)
    q_rank = cqt_ref.shape[1]
    kv_rank = ckv_ref.shape[1]
    cq = p[:, :q_rank]
    ckv = p[:, q_rank:q_rank + kv_rank]
    slab = p[:, q_rank + kv_rank:]
    cq = cq * lax.rsqrt(jnp.mean(cq * cq, axis=-1, keepdims=True) + RMS_EPS) * qg_ref[...]
    ckv = ckv * lax.rsqrt(jnp.mean(ckv * ckv, axis=-1, keepdims=True) + RMS_EPS) * kvg_ref[...]
    cqt_ref[0] = cq.T.astype(BF16)
    ckv_ref[...] = ckv.astype(BF16)
    ckvt_ref[0, :kv_rank, :] = ckv.T.astype(BF16)
    ckvt_ref[0, kv_rank:, :] = jnp.ones((ckvt_ref.shape[1] - kv_rank, ckvt_ref.shape[2]), BF16)
    slab_ref[...] = slab.astype(BF16)
    slabt_ref[0] = slab.T


def _dsa_latent(x, w, q_gain, kv_gain, *, batch, seq):
    tm = IDX_TQ
    n_t = seq // tm
    k_dim, n_dim = w.shape
    return pl.pallas_call(
        _dsa_latent_kernel,
        grid=(batch, n_t),
        in_specs=[pl.BlockSpec((tm, k_dim), lambda b, t: (b * n_t + t, 0)),
                  pl.BlockSpec((k_dim, n_dim), lambda b, t: (0, 0)),
                  pl.BlockSpec((1, DSA_Q_RANK), lambda b, t: (0, 0)),
                  pl.BlockSpec((1, DSA_KV_RANK), lambda b, t: (0, 0))],
        out_specs=[pl.BlockSpec((1, DSA_Q_RANK, tm), lambda b, t: (b, 0, t)),
                   pl.BlockSpec((tm, DSA_KV_RANK), lambda b, t: (b * n_t + t, 0)),
                   pl.BlockSpec((1, DSA_KV_RANK + ONES_ROWS, tm), lambda b, t: (b, 0, t)),
                   pl.BlockSpec((tm, LANES), lambda b, t: (b * n_t + t, 0)),
                   pl.BlockSpec((1, LANES, tm), lambda b, t: (b, 0, t))],
        out_shape=[jax.ShapeDtypeStruct((batch, DSA_Q_RANK, seq), BF16),
                   jax.ShapeDtypeStruct((batch * seq, DSA_KV_RANK), BF16),
                   jax.ShapeDtypeStruct((batch, DSA_KV_RANK + ONES_ROWS, seq), BF16),
                   jax.ShapeDtypeStruct((batch * seq, LANES), BF16),
                   jax.ShapeDtypeStruct((batch, LANES, seq), F32)],
        compiler_params=_cparams(2),
        name="dsa_latent",
    )(x, w, q_gain, kv_gain)


def _dsa_query_kernel(cqt_ref, wuqt_ref, wuk_ref, ql_ref, *, scale):
    hb, _, hd = wuk_ref.shape
    tq = ql_ref.shape[3] // hb
    q_t = jnp.dot(wuqt_ref[...], cqt_ref[0], preferred_element_type=F32).astype(BF16)
    for j in range(hb):
        ql_t = jnp.dot(wuk_ref[j], q_t[j * hd:(j + 1) * hd, :], preferred_element_type=F32)
        ql_t = (ql_t * scale).astype(BF16)
        for blk in range(ql_ref.shape[1]):
            ql_ref[0, blk, :, j * tq:(j + 1) * tq] = ql_t[:, blk * tq:(blk + 1) * tq]


def _dsa_query(cq_t, wuq_t, wuk, *, batch, seq, scale):
    tm = min(seq, 2048)
    hb = 8
    tq = ATT_TQ
    n_t = seq // tm
    n_heads, kv_rank, hd = wuk.shape
    return pl.pallas_call(
        functools.partial(_dsa_query_kernel, scale=scale),
        grid=(batch, n_t, n_heads // hb),
        in_specs=[pl.BlockSpec((1, DSA_Q_RANK, tm), lambda b, t, h: (b, 0, t)),
                  pl.BlockSpec((hb * hd, DSA_Q_RANK), lambda b, t, h: (h, 0)),
                  pl.BlockSpec((hb, kv_rank, hd), lambda b, t, h: (h, 0, 0))],
        out_specs=pl.BlockSpec((1, tm // tq, kv_rank, hb * tq), lambda b, t, h: (b, t, 0, h)),
        out_shape=jax.ShapeDtypeStruct((batch, seq // tq, kv_rank, n_heads * tq), BF16),
        compiler_params=_cparams(3),
        name="dsa_query",
    )(cq_t, wuq_t, wuk)


def _indexer_kernel(keys_ref, slabt_ref, cqt_ref, widx_ref, bias_ref, qt_scr, sc_scr, *, k_top, w_scale, n_iter):
    i = pl.program_id(1)
    n_t, kt_w, tq = sc_scr.shape
    cq_t = cqt_ref[0]
    n_h, hd_pad, q_rank = widx_ref.shape
    q_all = jnp.dot(widx_ref[...].reshape(n_h * hd_pad, q_rank), cq_t, preferred_element_type=F32)
    qt_scr[...] = q_all.astype(BF16).reshape(n_h, hd_pad, tq)
    w_t = slabt_ref[0][IDX_HEAD_DIM:IDX_HEAD_DIM + IDX_N_HEADS, :] * w_scale
    q_pos = i * tq + lax.broadcasted_iota(jnp.int32, (kt_w, tq), 1)

    for kt in range(n_t):
        @pl.when(kt <= i)
        def _(kt=kt):
            keys = keys_ref[kt * kt_w:(kt + 1) * kt_w, :]
            acc = jnp.zeros((kt_w, tq), F32)
            for h in range(IDX_N_HEADS):
                s = jnp.dot(keys, qt_scr[h], preferred_element_type=F32)
                acc = acc + jnp.maximum(s, 0.0) * w_t[h:h + 1, :]
            k_pos = kt * kt_w + lax.broadcasted_iota(jnp.int32, (kt_w, tq), 0)
            sc_scr[kt] = jnp.where(k_pos <= q_pos, acc, -jnp.inf)

    def bounds(kt, carry):
        lo, hi = carry
        s = sc_scr[kt]
        hi = jnp.maximum(hi, jnp.max(s, axis=0, keepdims=True))
        lo = jnp.minimum(lo, jnp.min(jnp.where(s == -jnp.inf, jnp.inf, s), axis=0, keepdims=True))
        return lo, hi

    lo, hi = lax.fori_loop(0, i + 1, bounds,
                           (jnp.full((1, tq), jnp.inf, F32), jnp.full((1, tq), -jnp.inf, F32)))

    def halve(carry):
        it, lo, hi, n_lo = carry
        mid = 0.5 * lo + 0.5 * hi

        mid_rows = jnp.broadcast_to(mid, (SUBLANES, tq))

        def count(kt, cnt):
            for r0 in range(0, kt_w, SUBLANES):
                cnt = jnp.where(sc_scr[kt, r0:r0 + SUBLANES, :] >= mid_rows, cnt + 1.0, cnt)
            return cnt

        cnt = lax.fori_loop(0, i + 1, count, jnp.zeros((SUBLANES, tq), F32))
        cnt = jnp.sum(cnt, axis=0, keepdims=True)
        enough = cnt >= k_top
        return it + 1, jnp.where(enough, mid, lo), jnp.where(enough, hi, mid), jnp.where(enough, cnt, n_lo)

    def unresolved(carry):
        it, _, _, n_lo = carry
        return jnp.logical_and(it < n_iter, jnp.max(n_lo) > k_top)

    n_causal = (i * tq + lax.broadcasted_iota(jnp.int32, (1, tq), 1) + 1).astype(F32)
    _, lo, hi, _ = lax.while_loop(unresolved, lambda c: halve(halve(c)), (jnp.int32(0), lo, hi, n_causal))

    for kt in range(n_t):
        @pl.when(kt <= i)
        def _(kt=kt):
            bias_ref[0, kt * kt_w:(kt + 1) * kt_w, :] = jnp.where(sc_scr[kt] >= lo, 0.0, -jnp.inf)

        @pl.when(kt > i)
        def _(kt=kt):
            bias_ref[0, kt * kt_w:(kt + 1) * kt_w, :] = jnp.full((kt_w, tq), -jnp.inf, F32)


def _dsa_indexer(keys, slab_t, cq_t, widx_t, *, batch, seq, k_top, w_scale):
    tq = IDX_TQ
    n_t = seq // tq
    return pl.pallas_call(
        functools.partial(_indexer_kernel, k_top=k_top, w_scale=w_scale, n_iter=BISECT_ITERS),
        grid=(batch, n_t),
        in_specs=[pl.BlockSpec((seq, LANES), lambda b, i: (b, 0)),
                  pl.BlockSpec((1, LANES, tq), lambda b, i: (b, 0, i)),
                  pl.BlockSpec((1, DSA_Q_RANK, tq), lambda b, i: (b, 0, i)),
                  pl.BlockSpec(widx_t.shape, lambda b, i: (0, 0, 0))],
        out_specs=pl.BlockSpec((1, seq, tq), lambda b, i: (b, 0, i)),
        out_shape=jax.ShapeDtypeStruct((batch, seq, seq), F32),
        scratch_shapes=[pltpu.VMEM((IDX_N_HEADS, LANES, tq), BF16),
                        pltpu.VMEM((n_t, tq, tq), F32)],
        compiler_params=_cparams(2),
        name="dsa_indexer",
    )(keys, slab_t, cq_t, widx_t)


def _attn_kernel(ql_ref, bias_ref, k_ref, vt_ref, wuvt_ref, gate_ref, *rest, n_keys, hc):
    o_ref = rest[-1]
    tq = bias_ref.shape[2]
    hb, hd, kv_rank = wuvt_ref.shape
    cols = hc * tq
    n_chunks = hb // hc
    keys = k_ref[:n_keys, :]
    bias_t = bias_ref[0]
    bias_rep = jnp.concatenate([bias_t] * hc, axis=1) if hc > 1 else bias_t

    def scores(c):
        q_t = ql_ref[0, 0, :, c * cols:(c + 1) * cols]
        return jnp.dot(keys, q_t, preferred_element_type=F32) + bias_rep

    def finish(c, s):
        m = jnp.max(s, axis=0, keepdims=True)
        p = jnp.exp2(s - m).astype(BF16)
        o_aug = jnp.dot(vt_ref[0], p, preferred_element_type=F32)
        o_t = (o_aug[:kv_rank] / o_aug[kv_rank:kv_rank + 1]).astype(BF16)
        for j in range(hc):
            h = c * hc + j
            oh_t = jnp.dot(wuvt_ref[h], o_t[:, j * tq:(j + 1) * tq], preferred_element_type=F32)
            gate = gate_ref[:, h * hd:(h + 1) * hd].astype(F32)
            o_ref[:, h * hd:(h + 1) * hd] = (oh_t.T * gate).astype(o_ref.dtype)

    s_next = scores(0)
    for c in range(n_chunks):
        s_cur = s_next
        if c + 1 < n_chunks:
            s_next = scores(c + 1)
        finish(c, s_cur)


def _dsa_attention(ql, bias, ckv_t, ckv, wuv_t, gate, *, batch, seq):
    tq = ATT_TQ
    n_q = seq // tq
    n_heads, hd, kv_rank = wuv_t.shape
    width = n_heads * hd
    og = None
    for i in range(n_q):
        n_keys = (i + 1) * tq
        hb = n_heads
        while hb > 1 and hb * tq * n_keys > ATT_STEP_ELEMS:
            hb //= 2
        hc = max(1, min(hb, ATT_CHUNK_COLS // tq))
        in_specs = [pl.BlockSpec((1, 1, kv_rank, hb * tq), lambda b, g, i=i: (b, i, 0, g)),
                    pl.BlockSpec((1, n_keys, tq), lambda b, g, i=i: (b, 0, i)),
                    pl.BlockSpec((seq, kv_rank), lambda b, g: (b, 0)),
                    pl.BlockSpec((1, kv_rank + ONES_ROWS, n_keys), lambda b, g: (b, 0, 0)),
                    pl.BlockSpec((hb, hd, kv_rank), lambda b, g: (g, 0, 0)),
                    pl.BlockSpec((tq, hb * hd), lambda b, g, i=i: (b * n_q + i, g))]
        args = [ql, bias, ckv, ckv_t, wuv_t, gate]
        aliases = {}
        if og is not None:
            in_specs.append(pl.BlockSpec(memory_space=pl.ANY))
            args.append(og)
            aliases = {len(args) - 1: 0}
        og = pl.pallas_call(
            functools.partial(_attn_kernel, n_keys=n_keys, hc=hc),
            grid=(batch, n_heads // hb),
            in_specs=in_specs,
            out_specs=pl.BlockSpec((tq, hb * hd), lambda b, g, i=i: (b * n_q + i, g)),
            out_shape=jax.ShapeDtypeStruct((batch * seq, width), BF16),
            input_output_aliases=aliases,
            compiler_params=_cparams(2),
            name="dsa_attention_q%d" % i,
        )(*args)
    return og


def _ssd_layer(x, w_in, conv_w, conv_b, dt_bias, a_log, d_skip, norm_g, w_out, ln_g, ln_b,
               *, batch, seq, alpha):
    d_inner = w_out.shape[0]
    n_heads = a_log.shape[0]
    n_groups = SSD_N_GROUPS
    heads_per_group = n_heads // n_groups
    conv_dim = conv_w.shape[1]
    gw = d_inner // n_groups
    assert gw % LANES == 0 and heads_per_group <= LANES and SSD_D_STATE == LANES
    assert SSD_HEAD_DIM % SUBLANES == 0 and n_heads <= LANES and conv_w.shape[1] == d_inner + 2 * n_groups * SSD_D_STATE

    def per_group_lanes(v):
        lead = v.shape[:-1]
        v = v.reshape(lead + (n_groups, heads_per_group))
        v = jnp.pad(v, [(0, 0)] * len(lead) + [(0, 0), (0, LANES - heads_per_group)])
        return v.reshape(lead + (n_groups * LANES,))

    w_bf = w_in.astype(BF16)
    w_dt = jnp.pad(w_bf[:, d_inner + conv_dim:], ((0, 0), (0, LANES - n_heads)))
    dt_b = jnp.pad(dt_bias, (0, LANES - n_heads))[None, :]
    head_of_lane = jnp.arange(n_groups * LANES)
    head_of_lane = jnp.where(head_of_lane % LANES < heads_per_group,
                             (head_of_lane // LANES) * heads_per_group + head_of_lane % LANES, -1)
    spread = (jnp.arange(LANES)[:, None] == head_of_lane[None, :]).astype(BF16)
    alog = per_group_lanes(a_log).reshape(n_groups, LANES)
    dskip = jnp.repeat(d_skip, SSD_HEAD_DIM).reshape(n_groups, gw)
    lane_of = jnp.arange(gw) // SSD_HEAD_DIM
    expand_mat = (jnp.arange(LANES)[:, None] == lane_of[None, :]).astype(BF16)

    dt, x_bf = _dt_proj(x, w_dt, dt_b, spread, tm=min(seq, 1024))
    z = _proj(x_bf, w_bf, batch=batch, seq=seq, tn=1024, n_dim=d_inner)
    xbc = _proj(x_bf, w_bf, batch=batch, seq=seq, tn=1024, col0=d_inner, n_dim=conv_dim,
                conv=(conv_w, conv_b[None, :]))
    yn = _ssd_scan(xbc, z, dt, alog, dskip, norm_g[None, :], expand_mat, batch=batch, seq=seq, d_inner=d_inner)
    return _outproj_ln(yn, w_out.astype(BF16), x, ln_g[None, :], ln_b[None, :], alpha=alpha, tm=512)


def _dsa_layer(x, x_bf, w_in, q_norm_g, kv_norm_g, w_uq, w_uk, w_uv, w_idx_q, w_out, ln_g, ln_b,
               *, batch, seq, alpha):
    assert seq % IDX_TQ == 0 and seq % ATT_TQ == 0 and IDX_HEAD_DIM + IDX_N_HEADS <= LANES
    small = DSA_Q_RANK + DSA_KV_RANK + IDX_HEAD_DIM + IDX_N_HEADS
    w_small = jnp.pad(w_in[:, :small], ((0, 0), (0, DSA_Q_RANK + DSA_KV_RANK + LANES - small))).astype(BF16)
    w_gate = w_in[:, small:].astype(BF16)
    wuq_t = w_uq.T.astype(BF16)
    wuv_t = w_uv.transpose(0, 2, 1).astype(BF16)
    widx_t = w_idx_q.reshape(DSA_Q_RANK, IDX_N_HEADS, IDX_HEAD_DIM).transpose(1, 2, 0)
    widx_t = jnp.pad(widx_t, ((0, 0), (0, LANES - IDX_HEAD_DIM), (0, 0))).astype(BF16)
    k_top = min(IDX_TOPK, seq // 4)

    cq_t, ckv, ckv_t, slab, slab_t = _dsa_latent(x_bf, w_small, q_norm_g[None, :], kv_norm_g[None, :],
                                                     batch=batch, seq=seq)
    gate = _proj(x_bf, w_gate, batch=batch, seq=seq, tn=1024)
    ql = _dsa_query(cq_t, wuq_t, w_uk.astype(BF16), batch=batch, seq=seq,
                    scale=DSA_HEAD_DIM ** -0.5 * math.log2(math.e))
    bias = _dsa_indexer(slab, slab_t, cq_t, widx_t, batch=batch, seq=seq, k_top=k_top,
                        w_scale=IDX_N_HEADS ** -0.5 * IDX_HEAD_DIM ** -0.5)
    og = _dsa_attention(ql, bias, ckv_t, ckv, wuv_t, gate, batch=batch, seq=seq)
    return _outproj_ln(og, w_out.astype(BF16), x, ln_g[None, :], ln_b[None, :], alpha=alpha, tm=512)


def kernel(x, ssd_w_in, ssd_conv_w, ssd_conv_b, ssd_dt_bias, ssd_a_log, ssd_d_skip, ssd_norm_g, ssd_w_out,
           dsa_w_in, dsa_q_norm_g, dsa_kv_norm_g, dsa_w_uq, dsa_w_uk, dsa_w_uv, dsa_w_idx_q, dsa_w_out,
           ln_g, ln_b):
    batch, seq, d_model = x.shape
    depth = ln_g.shape[0]
    alpha = (2.0 * depth) ** 0.25
    xf = x.reshape(batch * seq, d_model)
    x_bf = None
    for i in range(depth):
        j = i // 2
        if i % 2 == 0:
            xf, x_bf = _ssd_layer(xf, ssd_w_in[j], ssd_conv_w[j], ssd_conv_b[j], ssd_dt_bias[j],
                                  ssd_a_log[j], ssd_d_skip[j], ssd_norm_g[j], ssd_w_out[j], ln_g[i], ln_b[i],
                                  batch=batch, seq=seq, alpha=alpha)
        else:
            x_bf = xf.astype(BF16) if x_bf is None else x_bf
            xf, x_bf = _dsa_layer(xf, x_bf, dsa_w_in[j], dsa_q_norm_g[j], dsa_kv_norm_g[j], dsa_w_uq[j],
                                  dsa_w_uk[j], dsa_w_uv[j], dsa_w_idx_q[j], dsa_w_out[j], ln_g[i], ln_b[i],
                                  batch=batch, seq=seq, alpha=alpha)
    return xf.reshape(batch, seq, d_model)
```

```python
import functools
import math

import jax
import jax.numpy as jnp
from jax import lax
from jax.experimental import pallas as pl
from jax.experimental.pallas import tpu as pltpu

F32 = jnp.float32
BF16 = jnp.bfloat16

V7X_VMEM_BYTES = 64 * 1024 * 1024
LANES = 128
SUBLANES = 8
VMEM_LIMIT_BYTES = V7X_VMEM_BYTES - 8 * 1024 * 1024

LN_EPS = 1e-5
RMS_EPS = 1e-6

SSD_HEAD_DIM = 64
SSD_N_GROUPS = 8
SSD_D_STATE = 128
SSD_CHUNK = 256
DSA_N_HEADS = 32
DSA_HEAD_DIM = 128
DSA_Q_RANK = 512
DSA_KV_RANK = 256
IDX_N_HEADS = 16
IDX_HEAD_DIM = 64
IDX_TOPK = 256

ATT_TQ = 256
ATT_STEP_ELEMS = 8 * 1024 * 1024
ATT_CHUNK_COLS = 2048
IDX_TQ = 512
BISECT_ITERS = 40
ONES_ROWS = 16
CONV_ROW_CHUNK = 256
LN_ROW_CHUNK = 128
SSD_GROUPS_PER_STEP = 2


def _cparams(n_axes):
    return pltpu.CompilerParams(dimension_semantics=("arbitrary",) * n_axes,
                                vmem_limit_bytes=VMEM_LIMIT_BYTES)


LOG2_E = math.log2(math.e)


def _sigmoid(v):
    return 1.0 / (1.0 + jnp.exp2(v * (-LOG2_E)))


def _proj_silu_kernel(x_ref, w_ref, o_ref, *, chunk):
    w = w_ref[...]
    for r0 in range(0, o_ref.shape[0], chunk):
        y = jnp.dot(x_ref[r0:r0 + chunk, :], w, preferred_element_type=F32)
        o_ref[r0:r0 + chunk, :] = (y * _sigmoid(y)).astype(o_ref.dtype)


def _proj_conv_silu_kernel(x_ref, w_ref, conv_w_ref, conv_b_ref, o_ref, y_scr, *, chunk):
    seq, tn = o_ref.shape
    k_taps = conv_w_ref.shape[0]
    head = y_scr.shape[0] - seq
    y_scr[:head, :] = jnp.zeros((head, tn), F32)
    w = w_ref[...]

    def matmul(r0):
        y_scr[head + r0:head + r0 + chunk, :] = jnp.dot(x_ref[r0:r0 + chunk, :], w, preferred_element_type=F32)

    def finish(r0):
        acc = conv_b_ref[...] + conv_w_ref[k_taps - 1:k_taps, :] * y_scr[head + r0:head + r0 + chunk, :]
        for back in range(1, k_taps):
            tap = conv_w_ref[k_taps - 1 - back:k_taps - back, :]
            acc = acc + tap * y_scr[head + r0 - back:head + r0 - back + chunk, :]
        o_ref[r0:r0 + chunk, :] = (acc * _sigmoid(acc)).astype(o_ref.dtype)

    for r0 in range(0, seq, chunk):
        matmul(r0)
        finish(r0)


def _proj(x, w, *, batch, seq, tn, col0=0, n_dim=None, conv=None, out_dtype=BF16):
    k_dim = w.shape[0]
    n_dim = w.shape[1] - col0 if n_dim is None else n_dim
    assert n_dim % tn == 0 and tn % LANES == 0 and col0 % tn == 0
    j0 = col0 // tn
    in_specs = [pl.BlockSpec((seq, k_dim), lambda b, j: (b, 0)),
                pl.BlockSpec((k_dim, tn), lambda b, j: (0, j0 + j))]
    if conv is None:
        body = functools.partial(_proj_silu_kernel, chunk=CONV_ROW_CHUNK)
        args, scratch, name = (x, w), [], "proj_silu"
    else:
        body = functools.partial(_proj_conv_silu_kernel, chunk=CONV_ROW_CHUNK)
        args = (x, w) + tuple(conv)
        in_specs += [pl.BlockSpec((c.shape[0], tn), lambda b, j: (0, j)) for c in conv]
        assert conv[0].shape[0] - 1 <= SUBLANES
        scratch, name = [pltpu.VMEM((SUBLANES + seq, tn), F32)], "proj_conv_silu"
    return pl.pallas_call(
        body,
        grid=(batch, n_dim // tn),
        in_specs=in_specs,
        out_specs=pl.BlockSpec((seq, tn), lambda b, j: (b, j)),
        out_shape=jax.ShapeDtypeStruct((batch * seq, n_dim), out_dtype),
        scratch_shapes=scratch,
        compiler_params=_cparams(2),
        name=name,
    )(*args)


def _dt_kernel(x_ref, w_ref, b_ref, spread_ref, o_ref, xbf_ref):
    x = x_ref[...].astype(BF16)
    xbf_ref[...] = x
    y = jnp.dot(x, w_ref[...], preferred_element_type=F32) + b_ref[...]
    y = jnp.maximum(y, 0.0) + jnp.log1p(jnp.exp(-jnp.abs(y)))
    spread = spread_ref[...]
    hi = y.astype(BF16)
    rest = y - hi.astype(F32)
    mid = rest.astype(BF16)
    lo = (rest - mid.astype(F32)).astype(BF16)
    o_ref[...] = (jnp.dot(hi, spread, preferred_element_type=F32)
                  + jnp.dot(mid, spread, preferred_element_type=F32)
                  + jnp.dot(lo, spread, preferred_element_type=F32))


def _dt_proj(x, w, bias, spread, *, tm):
    rows, k_dim = x.shape
    width = spread.shape[1]
    return pl.pallas_call(
        _dt_kernel,
        grid=(rows // tm,),
        in_specs=[pl.BlockSpec((tm, k_dim), lambda i: (i, 0)),
                  pl.BlockSpec((k_dim, LANES), lambda i: (0, 0)),
                  pl.BlockSpec((1, LANES), lambda i: (0, 0)),
                  pl.BlockSpec((LANES, width), lambda i: (0, 0))],
        out_specs=[pl.BlockSpec((tm, width), lambda i: (i, 0)),
                   pl.BlockSpec((tm, k_dim), lambda i: (i, 0))],
        out_shape=[jax.ShapeDtypeStruct((rows, width), F32),
                   jax.ShapeDtypeStruct((rows, k_dim), BF16)],
        compiler_params=_cparams(1),
        name="dt_proj",
    )(x, w, bias, spread)


def _cumsum_rows(v):
    n = v.shape[0]
    row = lax.broadcasted_iota(jnp.int32, v.shape, 0)
    shift = 1
    while shift < n:
        v = v + jnp.where(row >= shift, pltpu.roll(v, shift, 0), 0.0)
        shift *= 2
    return v


def _ssd_kernel(xs_ref, b_ref, c_ref, z_ref, dt_ref, alog_ref, dskip_ref, ng_ref, e_ref, o_ref,
                state_ref, *, groups, heads_per_group, head_dim):
    @pl.when(pl.program_id(2) == 0)
    def _():
        state_ref[...] = jnp.zeros_like(state_ref)

    gw = e_ref.shape[1]
    n = b_ref.shape[1] // groups
    for s in range(groups):
        g = pl.program_id(1) * groups + s
        lanes = slice(s * gw, (s + 1) * gw)
        _ssd_group(xs_ref.at[:, lanes], b_ref.at[:, s * n:(s + 1) * n], c_ref.at[:, s * n:(s + 1) * n],
                   z_ref.at[:, lanes], dt_ref.at[:, s * LANES:(s + 1) * LANES], alog_ref[pl.ds(g, 1), :],
                   dskip_ref[pl.ds(g, 1), :], ng_ref.at[:, lanes], e_ref, o_ref.at[:, lanes], state_ref.at[s],
                   heads_per_group=heads_per_group, head_dim=head_dim)


def _ssd_group(xs_ref, b_ref, c_ref, z_ref, dt_ref, alog, dskip, ng_ref, e_ref, o_ref, state_ref,
               *, heads_per_group, head_dim):
    q = xs_ref.shape[0]
    xs = xs_ref[...].astype(F32)
    bm = b_ref[...]
    cm = c_ref[...]
    dt = dt_ref[...]
    a = -jnp.exp(alog) * LOG2_E
    acs = _cumsum_rows(dt * a)
    acs_t = acs.T
    last = acs[q - 1:q, :]
    expand_mat = e_ref[...]

    def expand(v):
        return jnp.dot(v.astype(BF16), expand_mat, preferred_element_type=F32)

    def expand_hi_lo(v):
        hi = v.astype(BF16)
        lo = (v - hi.astype(F32)).astype(BF16)
        return (jnp.dot(hi, expand_mat, preferred_element_type=F32)
                + jnp.dot(lo, expand_mat, preferred_element_type=F32))

    decay_in_x = expand_hi_lo(jnp.exp2(acs))
    decay_out_dt_x = expand(jnp.exp2(last - acs) * dt)

    bc = lax.dot_general(bm, cm, (((1,), (1,)), ((), ())), preferred_element_type=F32)
    blk = LANES
    tri = (lax.broadcasted_iota(jnp.int32, (blk, blk), 1) >= lax.broadcasted_iota(jnp.int32, (blk, blk), 0))
    xs_t = xs.T
    dt_t = dt.T

    def masked_bc(r):
        rows = []
        for s0 in range(0, q, blk):
            parts = []
            if s0 > 0:
                parts.append(jnp.zeros((blk, s0), BF16))
            diff = acs_t[r:r + 1, s0:s0 + blk] - acs[s0:s0 + blk, r:r + 1]
            decay = jnp.exp2(jnp.where(tri, diff, -jnp.inf))
            parts.append((bc[s0:s0 + blk, s0:s0 + blk] * decay).astype(BF16))
            if s0 + blk < q:
                diff = acs_t[r:r + 1, s0 + blk:] - acs[s0:s0 + blk, r:r + 1]
                parts.append((bc[s0:s0 + blk, s0 + blk:] * jnp.exp2(diff)).astype(BF16))
            rows.append(jnp.concatenate(parts, axis=1) if len(parts) > 1 else parts[0])
        return jnp.concatenate(rows, axis=0) if len(rows) > 1 else rows[0]

    y_t = []
    for r in range(heads_per_group):
        x_head_t = (xs_t[r * head_dim:(r + 1) * head_dim, :] * dt_t[r:r + 1, :]).astype(BF16)
        y_t.append(jnp.dot(x_head_t, masked_bc(r), preferred_element_type=F32))
    y_diag = jnp.concatenate(y_t, axis=0).T

    s_prev = state_ref[...]
    y_off = jnp.dot(cm, s_prev.astype(BF16), preferred_element_type=F32) * decay_in_x
    y = y_diag + y_off + dskip * xs
    b_t = bm.astype(F32).T.astype(BF16)
    s_new = jnp.dot(b_t, (xs * decay_out_dt_x).astype(BF16), preferred_element_type=F32)
    state_ref[...] = s_prev * decay_in_x[q - 1:q, :] + s_new

    yg = y * z_ref[...].astype(F32)
    ms = jnp.mean(yg * yg, axis=-1, keepdims=True)
    o_ref[...] = (yg * lax.rsqrt(ms + RMS_EPS) * ng_ref[...]).astype(o_ref.dtype)


def _ssd_scan(xbc, z, dt, alog, dskip, norm_g, expand_mat, *, batch, seq, d_inner):
    n_groups, d_state, chunk = SSD_N_GROUPS, SSD_D_STATE, math.gcd(SSD_CHUNK, seq)
    gw = d_inner // n_groups
    heads_per_group = gw // SSD_HEAD_DIM
    n_chunks = seq // chunk
    gps = SSD_GROUPS_PER_STEP
    assert n_groups % gps == 0
    b_col0 = d_inner // (gps * d_state)
    c_col0 = b_col0 + n_groups // gps
    row = lambda b, g, c: b * n_chunks + c
    return pl.pallas_call(
        functools.partial(_ssd_kernel, groups=gps, heads_per_group=heads_per_group, head_dim=SSD_HEAD_DIM),
        grid=(batch, n_groups // gps, n_chunks),
        in_specs=[
            pl.BlockSpec((chunk, gps * gw), lambda b, g, c: (row(b, g, c), g)),
            pl.BlockSpec((chunk, gps * d_state), lambda b, g, c: (row(b, g, c), b_col0 + g)),
            pl.BlockSpec((chunk, gps * d_state), lambda b, g, c: (row(b, g, c), c_col0 + g)),
            pl.BlockSpec((chunk, gps * gw), lambda b, g, c: (row(b, g, c), g)),
            pl.BlockSpec((chunk, gps * LANES), lambda b, g, c: (row(b, g, c), g)),
            pl.BlockSpec((n_groups, LANES), lambda b, g, c: (0, 0)),
            pl.BlockSpec((n_groups, gw), lambda b, g, c: (0, 0)),
            pl.BlockSpec((1, gps * gw), lambda b, g, c: (0, g)),
            pl.BlockSpec((LANES, gw), lambda b, g, c: (0, 0)),
        ],
        out_specs=pl.BlockSpec((chunk, gps * gw), lambda b, g, c: (row(b, g, c), g)),
        out_shape=jax.ShapeDtypeStruct((batch * seq, d_inner), BF16),
        scratch_shapes=[pltpu.VMEM((gps, d_state, gw), F32)],
        compiler_params=_cparams(3),
        name="ssd_scan",
    )(xbc, xbc, xbc, z, dt, alog, dskip, norm_g, expand_mat)


def _outproj_ln_kernel(y_ref, w_ref, x_ref, g_ref, b_ref, o_ref, maybe_obf_ref=None, *, alpha, chunk):
    for r0 in range(0, y_ref.shape[0], chunk):
        rows = slice(r0, r0 + chunk)
        h = jnp.dot(y_ref[rows, :], w_ref[...], preferred_element_type=F32)
        v = alpha * x_ref[rows, :] + h
        mu = jnp.mean(v, axis=-1, keepdims=True)
        d = v - mu
        var = jnp.mean(d * d, axis=-1, keepdims=True)
        out = d * lax.rsqrt(var + LN_EPS) * g_ref[...] + b_ref[...]
        o_ref[rows, :] = out
        if maybe_obf_ref is not None:
            maybe_obf_ref[rows, :] = out.astype(BF16)


def _outproj_ln(y, w, x, ln_g, ln_b, *, alpha, tm, want_bf16):
    m, k_dim = y.shape
    d = w.shape[1]
    n_out = 2 if want_bf16 else 1
    outs = pl.pallas_call(
        functools.partial(_outproj_ln_kernel, alpha=alpha, chunk=LN_ROW_CHUNK),
        grid=(m // tm,),
        in_specs=[pl.BlockSpec((tm, k_dim), lambda i: (i, 0)),
                  pl.BlockSpec((k_dim, d), lambda i: (0, 0), pipeline_mode=pl.Buffered(1)),
                  pl.BlockSpec((tm, d), lambda i: (i, 0)),
                  pl.BlockSpec((1, d), lambda i: (0, 0)),
                  pl.BlockSpec((1, d), lambda i: (0, 0))],
        out_specs=[pl.BlockSpec((tm, d), lambda i: (i, 0))] * n_out,
        out_shape=[jax.ShapeDtypeStruct((m, d), F32), jax.ShapeDtypeStruct((m, d), BF16)][:n_out],
        compiler_params=_cparams(1),
        name="outproj_ln",
    )(y, w, x, ln_g, ln_b)
    return (outs[0], outs[1]) if want_bf16 else (outs[0], None)


def _dsa_latent_kernel(x_ref, w_ref, qg_ref, kvg_ref, cqt_ref, ckv_ref, ckvt_ref, slab_ref, slabt_ref):
    p = jnp.dot(x_ref[...], w_ref[...], preferred_element_type=F32)
    q_rank = cqt_ref.shape[1]
    kv_rank = ckv_ref.shape[1]
    cq = p[:, :q_rank]
    ckv = p[:, q_rank:q_rank + kv_rank]
    slab = p[:, q_rank + kv_rank:]
    cq = cq * lax.rsqrt(jnp.mean(cq * cq, axis=-1, keepdims=True) + RMS_EPS) * qg_ref[...]
    ckv = ckv * lax.rsqrt(jnp.mean(ckv * ckv, axis=-1, keepdims=True) + RMS_EPS) * kvg_ref[...]
    cqt_ref[0] = cq.T.astype(BF16)
    ckv_ref[...] = ckv.astype(BF16)
    ckvt_ref[0, :kv_rank, :] = ckv.T.astype(BF16)
    ckvt_ref[0, kv_rank:, :] = jnp.ones((ckvt_ref.shape[1] - kv_rank, ckvt_ref.shape[2]), BF16)
    slab_ref[...] = slab.astype(BF16)
    slabt_ref[0] = slab.T


def _dsa_latent(x, w, q_gain, kv_gain, *, batch, seq):
    tm = IDX_TQ
    n_t = seq // tm
    k_dim, n_dim = w.shape
    return pl.pallas_call(
        _dsa_latent_kernel,
        grid=(batch, n_t),
        in_specs=[pl.BlockSpec((tm, k_dim), lambda b, t: (b * n_t + t, 0)),
                  pl.BlockSpec((k_dim, n_dim), lambda b, t: (0, 0)),
                  pl.BlockSpec((1, DSA_Q_RANK), lambda b, t: (0, 0)),
                  pl.BlockSpec((1, DSA_KV_RANK), lambda b, t: (0, 0))],
        out_specs=[pl.BlockSpec((1, DSA_Q_RANK, tm), lambda b, t: (b, 0, t)),
                   pl.BlockSpec((tm, DSA_KV_RANK), lambda b, t: (b * n_t + t, 0)),
                   pl.BlockSpec((1, DSA_KV_RANK + ONES_ROWS, tm), lambda b, t: (b, 0, t)),
                   pl.BlockSpec((tm, LANES), lambda b, t: (b * n_t + t, 0)),
                   pl.BlockSpec((1, LANES, tm), lambda b, t: (b, 0, t))],
        out_shape=[jax.ShapeDtypeStruct((batch, DSA_Q_RANK, seq), BF16),
                   jax.ShapeDtypeStruct((batch * seq, DSA_KV_RANK), BF16),
                   jax.ShapeDtypeStruct((batch, DSA_KV_RANK + ONES_ROWS, seq), BF16),
                   jax.ShapeDtypeStruct((batch * seq, LANES), BF16),
                   jax.ShapeDtypeStruct((batch, LANES, seq), F32)],
        compiler_params=_cparams(2),
        name="dsa_latent",
    )(x, w, q_gain, kv_gain)


def _dsa_query_kernel(cqt_ref, wuqt_ref, wuk_ref, ql_ref, *, scale):
    hb, _, hd = wuk_ref.shape
    tq = ql_ref.shape[3] // hb
    q_t = jnp.dot(wuqt_ref[...], cqt_ref[0], preferred_element_type=F32).astype(BF16)
    for j in range(hb):
        ql_t = jnp.dot(wuk_ref[j], q_t[j * hd:(j + 1) * hd, :], preferred_element_type=F32)
        ql_t = (ql_t * scale).astype(BF16)
        for blk in range(ql_ref.shape[1]):
            ql_ref[0, blk, :, j * tq:(j + 1) * tq] = ql_t[:, blk * tq:(blk + 1) * tq]


def _dsa_query(cq_t, wuq_t, wuk, *, batch, seq, scale):
    tm = min(seq, 2048)
    hb = 8
    tq = ATT_TQ
    n_t = seq // tm
    n_heads, kv_rank, hd = wuk.shape
    return pl.pallas_call(
        functools.partial(_dsa_query_kernel, scale=scale),
        grid=(batch, n_t, n_heads // hb),
        in_specs=[pl.BlockSpec((1, DSA_Q_RANK, tm), lambda b, t, h: (b, 0, t)),
                  pl.BlockSpec((hb * hd, DSA_Q_RANK), lambda b, t, h: (h, 0)),
                  pl.BlockSpec((hb, kv_rank, hd), lambda b, t, h: (h, 0, 0))],
        out_specs=pl.BlockSpec((1, tm // tq, kv_rank, hb * tq), lambda b, t, h: (b, t, 0, h)),
        out_shape=jax.ShapeDtypeStruct((batch, seq // tq, kv_rank, n_heads * tq), BF16),
        compiler_params=_cparams(3),
        name="dsa_query",
    )(cq_t, wuq_t, wuk)


def _indexer_kernel(keys_ref, slabt_ref, cqt_ref, widx_ref, bias_ref, qt_scr, sc_scr, *, k_top, w_scale, n_iter):
    i = pl.program_id(1)
    n_t, kt_w, tq = sc_scr.shape
    cq_t = cqt_ref[0]
    n_h, hd_pad, q_rank = widx_ref.shape
    q_all = jnp.dot(widx_ref[...].reshape(n_h * hd_pad, q_rank), cq_t, preferred_element_type=F32)
    qt_scr[...] = q_all.astype(BF16).reshape(n_h, hd_pad, tq)
    w_t = slabt_ref[0][IDX_HEAD_DIM:IDX_HEAD_DIM + IDX_N_HEADS, :] * w_scale
    q_pos = i * tq + lax.broadcasted_iota(jnp.int32, (kt_w, tq), 1)

    for kt in range(n_t):
        @pl.when(kt <= i)
        def _(kt=kt):
            keys = keys_ref[kt * kt_w:(kt + 1) * kt_w, :]
            acc = jnp.zeros((kt_w, tq), F32)
            for h in range(IDX_N_HEADS):
                s = jnp.dot(keys, qt_scr[h], preferred_element_type=F32)
                acc = acc + jnp.maximum(s, 0.0) * w_t[h:h + 1, :]
            k_pos = kt * kt_w + lax.broadcasted_iota(jnp.int32, (kt_w, tq), 0)
            sc_scr[kt] = jnp.where(k_pos <= q_pos, acc, -jnp.inf)

    def bounds(kt, carry):
        lo, hi = carry
        s = sc_scr[kt]
        hi = jnp.maximum(hi, jnp.max(s, axis=0, keepdims=True))
        lo = jnp.minimum(lo, jnp.min(jnp.where(s == -jnp.inf, jnp.inf, s), axis=0, keepdims=True))
        return lo, hi

    lo, hi = lax.fori_loop(0, i + 1, bounds,
                           (jnp.full((1, tq), jnp.inf, F32), jnp.full((1, tq), -jnp.inf, F32)))

    def halve(carry):
        it, lo, hi, n_lo = carry
        mid = 0.5 * lo + 0.5 * hi

        mid_rows = jnp.broadcast_to(mid, (SUBLANES, tq))

        def count(kt, cnt):
            for r0 in range(0, kt_w, SUBLANES):
                cnt = jnp.where(sc_scr[kt, r0:r0 + SUBLANES, :] >= mid_rows, cnt + 1.0, cnt)
            return cnt

        cnt = lax.fori_loop(0, i + 1, count, jnp.zeros((SUBLANES, tq), F32))
        cnt = jnp.sum(cnt, axis=0, keepdims=True)
        enough = cnt >= k_top
        return it + 1, jnp.where(enough, mid, lo), jnp.where(enough, hi, mid), jnp.where(enough, cnt, n_lo)

    def unresolved(carry):
        it, _, _, n_lo = carry
        return jnp.logical_and(it < n_iter, jnp.max(n_lo) > k_top)

    n_causal = (i * tq + lax.broadcasted_iota(jnp.int32, (1, tq), 1) + 1).astype(F32)
    _, lo, hi, _ = lax.while_loop(unresolved, lambda c: halve(halve(c)), (jnp.int32(0), lo, hi, n_causal))

    for kt in range(n_t):
        @pl.when(kt <= i)
        def _(kt=kt):
            bias_ref[0, kt * kt_w:(kt + 1) * kt_w, :] = jnp.where(sc_scr[kt] >= lo, 0.0, -jnp.inf)

        @pl.when(kt > i)
        def _(kt=kt):
            bias_ref[0, kt * kt_w:(kt + 1) * kt_w, :] = jnp.full((kt_w, tq), -jnp.inf, F32)


def _dsa_indexer(keys, slab_t, cq_t, widx_t, *, batch, seq, k_top, w_scale):
    tq = IDX_TQ
    n_t = seq // tq
    return pl.pallas_call(
        functools.partial(_indexer_kernel, k_top=k_top, w_scale=w_scale, n_iter=BISECT_ITERS),
        grid=(batch, n_t),
        in_specs=[pl.BlockSpec((seq, LANES), lambda b, i: (b, 0)),
                  pl.BlockSpec((1, LANES, tq), lambda b, i: (b, 0, i)),
                  pl.BlockSpec((1, DSA_Q_RANK, tq), lambda b, i: (b, 0, i)),
                  pl.BlockSpec(widx_t.shape, lambda b, i: (0, 0, 0))],
        out_specs=pl.BlockSpec((1, seq, tq), lambda b, i: (b, 0, i)),
        out_shape=jax.ShapeDtypeStruct((batch, seq, seq), F32),
        scratch_shapes=[pltpu.VMEM((IDX_N_HEADS, LANES, tq), BF16),
                        pltpu.VMEM((n_t, tq, tq), F32)],
        compiler_params=_cparams(2),
        name="dsa_indexer",
    )(keys, slab_t, cq_t, widx_t)


def _attn_kernel(ql_ref, bias_ref, k_ref, vt_ref, wuvt_ref, gate_ref, *rest, n_keys, hc):
    o_ref = rest[-1]
    tq = bias_ref.shape[2]
    hb, hd, kv_rank = wuvt_ref.shape
    cols = hc * tq
    n_chunks = hb // hc
    keys = k_ref[:n_keys, :]
    bias_t = bias_ref[0]
    bias_rep = jnp.concatenate([bias_t] * hc, axis=1) if hc > 1 else bias_t

    def scores(c):
        q_t = ql_ref[0, 0, :, c * cols:(c + 1) * cols]
        return jnp.dot(keys, q_t, preferred_element_type=F32) + bias_rep

    def finish(c, s):
        m = jnp.max(s, axis=0, keepdims=True)
        p = jnp.exp2(s - m).astype(BF16)
        o_aug = jnp.dot(vt_ref[0], p, preferred_element_type=F32)
        o_t = (o_aug[:kv_rank] / o_aug[kv_rank:kv_rank + 1]).astype(BF16)
        for j in range(hc):
            h = c * hc + j
            oh_t = jnp.dot(wuvt_ref[h], o_t[:, j * tq:(j + 1) * tq], preferred_element_type=F32)
            gate = gate_ref[:, h * hd:(h + 1) * hd].astype(F32)
            o_ref[:, h * hd:(h + 1) * hd] = (oh_t.T * gate).astype(o_ref.dtype)

    s_next = scores(0)
    for c in range(n_chunks):
        s_cur = s_next
        if c + 1 < n_chunks:
            s_next = scores(c + 1)
        finish(c, s_cur)


def _dsa_attention(ql, bias, ckv_t, ckv, wuv_t, gate, *, batch, seq):
    tq = ATT_TQ
    n_q = seq // tq
    n_heads, hd, kv_rank = wuv_t.shape
    width = n_heads * hd
    og = None
    for i in range(n_q):
        n_keys = (i + 1) * tq
        hb = n_heads
        while hb > 1 and hb * tq * n_keys > ATT_STEP_ELEMS:
            hb //= 2
        hc = max(1, min(hb, ATT_CHUNK_COLS // tq))
        in_specs = [pl.BlockSpec((1, 1, kv_rank, hb * tq), lambda b, g, i=i: (b, i, 0, g)),
                    pl.BlockSpec((1, n_keys, tq), lambda b, g, i=i: (b, 0, i)),
                    pl.BlockSpec((seq, kv_rank), lambda b, g: (b, 0)),
                    pl.BlockSpec((1, kv_rank + ONES_ROWS, n_keys), lambda b, g: (b, 0, 0)),
                    pl.BlockSpec((hb, hd, kv_rank), lambda b, g: (g, 0, 0)),
                    pl.BlockSpec((tq, hb * hd), lambda b, g, i=i: (b * n_q + i, g))]
        args = [ql, bias, ckv, ckv_t, wuv_t, gate]
        aliases = {}
        if og is not None:
            in_specs.append(pl.BlockSpec(memory_space=pl.ANY))
            args.append(og)
            aliases = {len(args) - 1: 0}
        og = pl.pallas_call(
            functools.partial(_attn_kernel, n_keys=n_keys, hc=hc),
            grid=(batch, n_heads // hb),
            in_specs=in_specs,
            out_specs=pl.BlockSpec((tq, hb * hd), lambda b, g, i=i: (b * n_q + i, g)),
            out_shape=jax.ShapeDtypeStruct((batch * seq, width), BF16),
            input_output_aliases=aliases,
            compiler_params=_cparams(2),
            name="dsa_attention_q%d" % i,
        )(*args)
    return og


def _ssd_layer(x, w_in, conv_w, conv_b, dt_bias, a_log, d_skip, norm_g, w_out, ln_g, ln_b,
               *, batch, seq, alpha, want_bf16):
    d_inner = w_out.shape[0]
    n_heads = a_log.shape[0]
    n_groups = SSD_N_GROUPS
    heads_per_group = n_heads // n_groups
    conv_dim = conv_w.shape[1]
    gw = d_inner // n_groups
    assert gw % LANES == 0 and heads_per_group <= LANES and SSD_D_STATE == LANES
    assert SSD_HEAD_DIM % SUBLANES == 0 and n_heads <= LANES and conv_w.shape[1] == d_inner + 2 * n_groups * SSD_D_STATE

    def per_group_lanes(v):
        lead = v.shape[:-1]
        v = v.reshape(lead + (n_groups, heads_per_group))
        v = jnp.pad(v, [(0, 0)] * len(lead) + [(0, 0), (0, LANES - heads_per_group)])
        return v.reshape(lead + (n_groups * LANES,))

    w_bf = w_in.astype(BF16)
    w_dt = jnp.pad(w_bf[:, d_inner + conv_dim:], ((0, 0), (0, LANES - n_heads)))
    dt_b = jnp.pad(dt_bias, (0, LANES - n_heads))[None, :]
    head_of_lane = jnp.arange(n_groups * LANES)
    head_of_lane = jnp.where(head_of_lane % LANES < heads_per_group,
                             (head_of_lane // LANES) * heads_per_group + head_of_lane % LANES, -1)
    spread = (jnp.arange(LANES)[:, None] == head_of_lane[None, :]).astype(BF16)
    alog = per_group_lanes(a_log).reshape(n_groups, LANES)
    dskip = jnp.repeat(d_skip, SSD_HEAD_DIM).reshape(n_groups, gw)
    lane_of = jnp.arange(gw) // SSD_HEAD_DIM
    expand_mat = (jnp.arange(LANES)[:, None] == lane_of[None, :]).astype(BF16)

    dt, x_bf = _dt_proj(x, w_dt, dt_b, spread, tm=min(seq, 1024))
    z = _proj(x_bf, w_bf, batch=batch, seq=seq, tn=1024, n_dim=d_inner)
    xbc = _proj(x_bf, w_bf, batch=batch, seq=seq, tn=1024, col0=d_inner, n_dim=conv_dim,
                conv=(conv_w, conv_b[None, :]))
    yn = _ssd_scan(xbc, z, dt, alog, dskip, norm_g[None, :], expand_mat, batch=batch, seq=seq, d_inner=d_inner)
    return _outproj_ln(yn, w_out.astype(BF16), x, ln_g[None, :], ln_b[None, :], alpha=alpha, tm=512,
                       want_bf16=want_bf16)


def _dsa_layer(x, x_bf, w_in, q_norm_g, kv_norm_g, w_uq, w_uk, w_uv, w_idx_q, w_out, ln_g, ln_b,
               *, batch, seq, alpha, want_bf16):
    assert seq % IDX_TQ == 0 and seq % ATT_TQ == 0 and IDX_HEAD_DIM + IDX_N_HEADS <= LANES
    small = DSA_Q_RANK + DSA_KV_RANK + IDX_HEAD_DIM + IDX_N_HEADS
    w_small = jnp.pad(w_in[:, :small], ((0, 0), (0, DSA_Q_RANK + DSA_KV_RANK + LANES - small))).astype(BF16)
    w_gate = w_in[:, small:].astype(BF16)
    wuq_t = w_uq.T.astype(BF16)
    wuv_t = w_uv.transpose(0, 2, 1).astype(BF16)
    widx_t = w_idx_q.reshape(DSA_Q_RANK, IDX_N_HEADS, IDX_HEAD_DIM).transpose(1, 2, 0)
    widx_t = jnp.pad(widx_t, ((0, 0), (0, LANES - IDX_HEAD_DIM), (0, 0))).astype(BF16)
    k_top = min(IDX_TOPK, seq // 4)

    cq_t, ckv, ckv_t, slab, slab_t = _dsa_latent(x_bf, w_small, q_norm_g[None, :], kv_norm_g[None, :],
                                                     batch=batch, seq=seq)
    gate = _proj(x_bf, w_gate, batch=batch, seq=seq, tn=1024)
    ql = _dsa_query(cq_t, wuq_t, w_uk.astype(BF16), batch=batch, seq=seq,
                    scale=DSA_HEAD_DIM ** -0.5 * math.log2(math.e))
    bias = _dsa_indexer(slab, slab_t, cq_t, widx_t, batch=batch, seq=seq, k_top=k_top,
                        w_scale=IDX_N_HEADS ** -0.5 * IDX_HEAD_DIM ** -0.5)
    og = _dsa_attention(ql, bias, ckv_t, ckv, wuv_t, gate, batch=batch, seq=seq)
    return _outproj_ln(og, w_out.astype(BF16), x, ln_g[None, :], ln_b[None, :], alpha=alpha, tm=512,
                       want_bf16=want_bf16)


def kernel(x, ssd_w_in, ssd_conv_w, ssd_conv_b, ssd_dt_bias, ssd_a_log, ssd_d_skip, ssd_norm_g, ssd_w_out,
           dsa_w_in, dsa_q_norm_g, dsa_kv_norm_g, dsa_w_uq, dsa_w_uk, dsa_w_uv, dsa_w_idx_q, dsa_w_out,
           ln_g, ln_b):
    batch, seq, d_model = x.shape
    depth = ln_g.shape[0]
    alpha = (2.0 * depth) ** 0.25
    xf = x.reshape(batch * seq, d_model)
    x_bf = None
    for i in range(depth):
        j = i // 2
        next_is_dsa = i + 1 < depth and (i + 1) % 2 == 1
        if i % 2 == 0:
            xf, x_bf = _ssd_layer(xf, ssd_w_in[j], ssd_conv_w[j], ssd_conv_b[j], ssd_dt_bias[j],
                                  ssd_a_log[j], ssd_d_skip[j], ssd_norm_g[j], ssd_w_out[j], ln_g[i], ln_b[i],
                                  batch=batch, seq=seq, alpha=alpha, want_bf16=next_is_dsa)
        else:
            x_bf = xf.astype(BF16) if x_bf is None else x_bf
            xf, x_bf = _dsa_layer(xf, x_bf, dsa_w_in[j], dsa_q_norm_g[j], dsa_kv_norm_g[j], dsa_w_uq[j],
                                  dsa_w_uk[j], dsa_w_uv[j], dsa_w_idx_q[j], dsa_w_out[j], ln_g[i], ln_b[i],
                                  batch=batch, seq=seq, alpha=alpha, want_bf16=next_is_dsa)
    return xf.reshape(batch, seq, d_model)
```

```python
import functools
import math

import jax
import jax.numpy as jnp
from jax import lax
from jax.experimental import pallas as pl
from jax.experimental.pallas import tpu as pltpu

F32 = jnp.float32
BF16 = jnp.bfloat16

V7X_VMEM_BYTES = 64 * 1024 * 1024
LANES = 128
SUBLANES = 8
VMEM_LIMIT_BYTES = V7X_VMEM_BYTES - 8 * 1024 * 1024

LN_EPS = 1e-5
RMS_EPS = 1e-6

SSD_HEAD_DIM = 64
SSD_N_GROUPS = 8
SSD_D_STATE = 128
SSD_CHUNK = 256
DSA_N_HEADS = 32
DSA_HEAD_DIM = 128
DSA_Q_RANK = 512
DSA_KV_RANK = 256
IDX_N_HEADS = 16
IDX_HEAD_DIM = 64
IDX_TOPK = 256

ATT_TQ = 256
ATT_STEP_ELEMS = 8 * 1024 * 1024
ATT_CHUNK_COLS = 2048
IDX_TQ = 512
LATENT_ROWS = 1024
BISECT_ITERS = 40
ONES_ROWS = 16
CONV_ROW_CHUNK = 256
LN_ROW_CHUNK = 128
SSD_GROUPS_PER_STEP = 2


def _cparams(n_axes):
    return pltpu.CompilerParams(dimension_semantics=("arbitrary",) * n_axes,
                                vmem_limit_bytes=VMEM_LIMIT_BYTES)


LOG2_E = math.log2(math.e)


def _sigmoid(v):
    return 1.0 / (1.0 + jnp.exp2(v * (-LOG2_E)))


def _proj_silu_kernel(x_ref, w_ref, o_ref, *, chunk):
    w = w_ref[...]
    for r0 in range(0, o_ref.shape[0], chunk):
        y = jnp.dot(x_ref[r0:r0 + chunk, :], w, preferred_element_type=F32)
        o_ref[r0:r0 + chunk, :] = (y * _sigmoid(y)).astype(o_ref.dtype)


def _proj_conv_silu_kernel(x_ref, w_ref, conv_w_ref, conv_b_ref, o_ref, y_scr, *, chunk):
    seq, tn = o_ref.shape
    k_taps = conv_w_ref.shape[0]
    head = y_scr.shape[0] - seq
    y_scr[:head, :] = jnp.zeros((head, tn), F32)
    w = w_ref[...]

    def matmul(r0):
        y_scr[head + r0:head + r0 + chunk, :] = jnp.dot(x_ref[r0:r0 + chunk, :], w, preferred_element_type=F32)

    def finish(r0):
        acc = conv_b_ref[...] + conv_w_ref[k_taps - 1:k_taps, :] * y_scr[head + r0:head + r0 + chunk, :]
        for back in range(1, k_taps):
            tap = conv_w_ref[k_taps - 1 - back:k_taps - back, :]
            acc = acc + tap * y_scr[head + r0 - back:head + r0 - back + chunk, :]
        o_ref[r0:r0 + chunk, :] = (acc * _sigmoid(acc)).astype(o_ref.dtype)

    for r0 in range(0, seq, chunk):
        matmul(r0)
        finish(r0)


def _proj(x, w, *, batch, seq, tn, col0=0, n_dim=None, conv=None, out_dtype=BF16):
    k_dim = w.shape[0]
    n_dim = w.shape[1] - col0 if n_dim is None else n_dim
    assert n_dim % tn == 0 and tn % LANES == 0 and col0 % tn == 0
    j0 = col0 // tn
    in_specs = [pl.BlockSpec((seq, k_dim), lambda b, j: (b, 0)),
                pl.BlockSpec((k_dim, tn), lambda b, j: (0, j0 + j))]
    if conv is None:
        body = functools.partial(_proj_silu_kernel, chunk=CONV_ROW_CHUNK)
        args, scratch, name = (x, w), [], "proj_silu"
    else:
        body = functools.partial(_proj_conv_silu_kernel, chunk=CONV_ROW_CHUNK)
        args = (x, w) + tuple(conv)
        in_specs += [pl.BlockSpec((c.shape[0], tn), lambda b, j: (0, j)) for c in conv]
        assert conv[0].shape[0] - 1 <= SUBLANES
        scratch, name = [pltpu.VMEM((SUBLANES + seq, tn), F32)], "proj_conv_silu"
    return pl.pallas_call(
        body,
        grid=(batch, n_dim // tn),
        in_specs=in_specs,
        out_specs=pl.BlockSpec((seq, tn), lambda b, j: (b, j)),
        out_shape=jax.ShapeDtypeStruct((batch * seq, n_dim), out_dtype),
        scratch_shapes=scratch,
        compiler_params=_cparams(2),
        name=name,
    )(*args)


def _dt_kernel(x_ref, w_ref, b_ref, spread_ref, o_ref, xbf_ref):
    x = x_ref[...].astype(BF16)
    xbf_ref[...] = x
    y = jnp.dot(x, w_ref[...], preferred_element_type=F32) + b_ref[...]
    y = jnp.maximum(y, 0.0) + jnp.log1p(jnp.exp(-jnp.abs(y)))
    spread = spread_ref[...]
    hi = y.astype(BF16)
    rest = y - hi.astype(F32)
    mid = rest.astype(BF16)
    lo = (rest - mid.astype(F32)).astype(BF16)
    o_ref[...] = (jnp.dot(hi, spread, preferred_element_type=F32)
                  + jnp.dot(mid, spread, preferred_element_type=F32)
                  + jnp.dot(lo, spread, preferred_element_type=F32))


def _dt_proj(x, w, bias, spread, *, tm):
    rows, k_dim = x.shape
    width = spread.shape[1]
    return pl.pallas_call(
        _dt_kernel,
        grid=(rows // tm,),
        in_specs=[pl.BlockSpec((tm, k_dim), lambda i: (i, 0)),
                  pl.BlockSpec((k_dim, LANES), lambda i: (0, 0)),
                  pl.BlockSpec((1, LANES), lambda i: (0, 0)),
                  pl.BlockSpec((LANES, width), lambda i: (0, 0))],
        out_specs=[pl.BlockSpec((tm, width), lambda i: (i, 0)),
                   pl.BlockSpec((tm, k_dim), lambda i: (i, 0))],
        out_shape=[jax.ShapeDtypeStruct((rows, width), F32),
                   jax.ShapeDtypeStruct((rows, k_dim), BF16)],
        compiler_params=_cparams(1),
        name="dt_proj",
    )(x, w, bias, spread)


def _cumsum_rows(v):
    n = v.shape[0]
    row = lax.broadcasted_iota(jnp.int32, v.shape, 0)
    shift = 1
    while shift < n:
        v = v + jnp.where(row >= shift, pltpu.roll(v, shift, 0), 0.0)
        shift *= 2
    return v


def _ssd_kernel(xs_ref, b_ref, c_ref, z_ref, dt_ref, alog_ref, dskip_ref, ng_ref, e_ref, o_ref,
                state_ref, *, groups, heads_per_group, head_dim):
    @pl.when(pl.program_id(2) == 0)
    def _():
        state_ref[...] = jnp.zeros_like(state_ref)

    gw = e_ref.shape[1]
    n = b_ref.shape[1] // groups
    for s in range(groups):
        g = pl.program_id(1) * groups + s
        lanes = slice(s * gw, (s + 1) * gw)
        _ssd_group(xs_ref.at[:, lanes], b_ref.at[:, s * n:(s + 1) * n], c_ref.at[:, s * n:(s + 1) * n],
                   z_ref.at[:, lanes], dt_ref.at[:, s * LANES:(s + 1) * LANES], alog_ref[pl.ds(g, 1), :],
                   dskip_ref[pl.ds(g, 1), :], ng_ref.at[:, lanes], e_ref, o_ref.at[:, lanes], state_ref.at[s],
                   heads_per_group=heads_per_group, head_dim=head_dim)


def _ssd_group(xs_ref, b_ref, c_ref, z_ref, dt_ref, alog, dskip, ng_ref, e_ref, o_ref, state_ref,
               *, heads_per_group, head_dim):
    q = xs_ref.shape[0]
    xs = xs_ref[...].astype(F32)
    bm = b_ref[...]
    cm = c_ref[...]
    dt = dt_ref[...]
    a = -jnp.exp(alog) * LOG2_E
    acs = _cumsum_rows(dt * a)
    acs_t = acs.T
    last = acs[q - 1:q, :]
    expand_mat = e_ref[...]

    def expand(v):
        return jnp.dot(v.astype(BF16), expand_mat, preferred_element_type=F32)

    def expand_hi_lo(v):
        hi = v.astype(BF16)
        lo = (v - hi.astype(F32)).astype(BF16)
        return (jnp.dot(hi, expand_mat, preferred_element_type=F32)
                + jnp.dot(lo, expand_mat, preferred_element_type=F32))

    decay_in_x = expand_hi_lo(jnp.exp2(acs))
    decay_out_dt_x = expand(jnp.exp2(last - acs) * dt)

    bc = lax.dot_general(bm, cm, (((1,), (1,)), ((), ())), preferred_element_type=F32)
    blk = LANES
    tri = (lax.broadcasted_iota(jnp.int32, (blk, blk), 1) >= lax.broadcasted_iota(jnp.int32, (blk, blk), 0))
    xs_t = xs.T
    dt_t = dt.T

    def masked_bc(r):
        rows = []
        for s0 in range(0, q, blk):
            parts = []
            if s0 > 0:
                parts.append(jnp.zeros((blk, s0), BF16))
            diff = acs_t[r:r + 1, s0:s0 + blk] - acs[s0:s0 + blk, r:r + 1]
            decay = jnp.exp2(jnp.where(tri, diff, -jnp.inf))
            parts.append((bc[s0:s0 + blk, s0:s0 + blk] * decay).astype(BF16))
            if s0 + blk < q:
                diff = acs_t[r:r + 1, s0 + blk:] - acs[s0:s0 + blk, r:r + 1]
                parts.append((bc[s0:s0 + blk, s0 + blk:] * jnp.exp2(diff)).astype(BF16))
            rows.append(jnp.concatenate(parts, axis=1) if len(parts) > 1 else parts[0])
        return jnp.concatenate(rows, axis=0) if len(rows) > 1 else rows[0]

    y_t = []
    for r in range(heads_per_group):
        x_head_t = (xs_t[r * head_dim:(r + 1) * head_dim, :] * dt_t[r:r + 1, :]).astype(BF16)
        y_t.append(jnp.dot(x_head_t, masked_bc(r), preferred_element_type=F32))
    y_diag = jnp.concatenate(y_t, axis=0).T

    s_prev = state_ref[...]
    y_off = jnp.dot(cm, s_prev.astype(BF16), preferred_element_type=F32) * decay_in_x
    y = y_diag + y_off + dskip * xs
    b_t = bm.astype(F32).T.astype(BF16)
    s_new = jnp.dot(b_t, (xs * decay_out_dt_x).astype(BF16), preferred_element_type=F32)
    state_ref[...] = s_prev * decay_in_x[q - 1:q, :] + s_new

    yg = y * z_ref[...].astype(F32)
    ms = jnp.mean(yg * yg, axis=-1, keepdims=True)
    o_ref[...] = (yg * lax.rsqrt(ms + RMS_EPS) * ng_ref[...]).astype(o_ref.dtype)


def _ssd_scan(xbc, z, dt, alog, dskip, norm_g, expand_mat, *, batch, seq, d_inner):
    n_groups, d_state, chunk = SSD_N_GROUPS, SSD_D_STATE, math.gcd(SSD_CHUNK, seq)
    gw = d_inner // n_groups
    heads_per_group = gw // SSD_HEAD_DIM
    n_chunks = seq // chunk
    gps = SSD_GROUPS_PER_STEP
    assert n_groups % gps == 0
    b_col0 = d_inner // (gps * d_state)
    c_col0 = b_col0 + n_groups // gps
    row = lambda b, g, c: b * n_chunks + c
    return pl.pallas_call(
        functools.partial(_ssd_kernel, groups=gps, heads_per_group=heads_per_group, head_dim=SSD_HEAD_DIM),
        grid=(batch, n_groups // gps, n_chunks),
        in_specs=[
            pl.BlockSpec((chunk, gps * gw), lambda b, g, c: (row(b, g, c), g)),
            pl.BlockSpec((chunk, gps * d_state), lambda b, g, c: (row(b, g, c), b_col0 + g)),
            pl.BlockSpec((chunk, gps * d_state), lambda b, g, c: (row(b, g, c), c_col0 + g)),
            pl.BlockSpec((chunk, gps * gw), lambda b, g, c: (row(b, g, c), g)),
            pl.BlockSpec((chunk, gps * LANES), lambda b, g, c: (row(b, g, c), g)),
            pl.BlockSpec((n_groups, LANES), lambda b, g, c: (0, 0)),
            pl.BlockSpec((n_groups, gw), lambda b, g, c: (0, 0)),
            pl.BlockSpec((1, gps * gw), lambda b, g, c: (0, g)),
            pl.BlockSpec((LANES, gw), lambda b, g, c: (0, 0)),
        ],
        out_specs=pl.BlockSpec((chunk, gps * gw), lambda b, g, c: (row(b, g, c), g)),
        out_shape=jax.ShapeDtypeStruct((batch * seq, d_inner), BF16),
        scratch_shapes=[pltpu.VMEM((gps, d_state, gw), F32)],
        compiler_params=_cparams(3),
        name="ssd_scan",
    )(xbc, xbc, xbc, z, dt, alog, dskip, norm_g, expand_mat)


def _outproj_ln_kernel(y_ref, w_ref, x_ref, g_ref, b_ref, o_ref, maybe_obf_ref=None, *, alpha, chunk):
    for r0 in range(0, y_ref.shape[0], chunk):
        rows = slice(r0, r0 + chunk)
        h = jnp.dot(y_ref[rows, :], w_ref[...], preferred_element_type=F32)
        v = alpha * x_ref[rows, :] + h
        mu = jnp.mean(v, axis=-1, keepdims=True)
        d = v - mu
        var = jnp.mean(d * d, axis=-1, keepdims=True)
        out = d * lax.rsqrt(var + LN_EPS) * g_ref[...] + b_ref[...]
        o_ref[rows, :] = out
        if maybe_obf_ref is not None:
            maybe_obf_ref[rows, :] = out.astype(BF16)


def _outproj_ln(y, w, x, ln_g, ln_b, *, alpha, tm, want_bf16):
    m, k_dim = y.shape
    d = w.shape[1]
    n_out = 2 if want_bf16 else 1
    outs = pl.pallas_call(
        functools.partial(_outproj_ln_kernel, alpha=alpha, chunk=LN_ROW_CHUNK),
        grid=(m // tm,),
        in_specs=[pl.BlockSpec((tm, k_dim), lambda i: (i, 0)),
                  pl.BlockSpec((k_dim, d), lambda i: (0, 0), pipeline_mode=pl.Buffered(1)),
                  pl.BlockSpec((tm, d), lambda i: (i, 0)),
                  pl.BlockSpec((1, d), lambda i: (0, 0)),
                  pl.BlockSpec((1, d), lambda i: (0, 0))],
        out_specs=[pl.BlockSpec((tm, d), lambda i: (i, 0))] * n_out,
        out_shape=[jax.ShapeDtypeStruct((m, d), F32), jax.ShapeDtypeStruct((m, d), BF16)][:n_out],
        compiler_params=_cparams(1),
        name="outproj_ln",
    )(y, w, x, ln_g, ln_b)
    return (outs[0], outs[1]) if want_bf16 else (outs[0], None)


def _dsa_latent_kernel(x_ref, w_ref, qg_ref, kvg_ref, cqt_ref, ckv_ref, ckvt_ref, slab_ref, slabt_ref):
    p = jnp.dot(x_ref[...], w_ref[...], preferred_element_type=F32)
    q_rank = cqt_ref.shape[1]
    kv_rank = ckv_ref.shape[1]
    cq = p[:, :q_rank]
    ckv = p[:, q_rank:q_rank + kv_rank]
    slab = p[:, q_rank + kv_rank:]
    cq = cq * lax.rsqrt(jnp.mean(cq * cq, axis=-1, keepdims=True) + RMS_EPS) * qg_ref[...]
    ckv = ckv * lax.rsqrt(jnp.mean(ckv * ckv, axis=-1, keepdims=True) + RMS_EPS) * kvg_ref[...]
    cqt_ref[0] = cq.T.astype(BF16)
    ckv_ref[...] = ckv.astype(BF16)
    ckvt_ref[0, :kv_rank, :] = ckv.T.astype(BF16)
    ckvt_ref[0, kv_rank:, :] = jnp.ones((ckvt_ref.shape[1] - kv_rank, ckvt_ref.shape[2]), BF16)
    slab_ref[...] = slab.astype(BF16)
    slabt_ref[0] = slab.T


def _dsa_latent(x, w, q_gain, kv_gain, *, batch, seq):
    tm = min(seq, LATENT_ROWS)
    n_t = seq // tm
    k_dim, n_dim = w.shape
    return pl.pallas_call(
        _dsa_latent_kernel,
        grid=(batch, n_t),
        in_specs=[pl.BlockSpec((tm, k_dim), lambda b, t: (b * n_t + t, 0)),
                  pl.BlockSpec((k_dim, n_dim), lambda b, t: (0, 0)),
                  pl.BlockSpec((1, DSA_Q_RANK), lambda b, t: (0, 0)),
                  pl.BlockSpec((1, DSA_KV_RANK), lambda b, t: (0, 0))],
        out_specs=[pl.BlockSpec((1, DSA_Q_RANK, tm), lambda b, t: (b, 0, t)),
                   pl.BlockSpec((tm, DSA_KV_RANK), lambda b, t: (b * n_t + t, 0)),
                   pl.BlockSpec((1, DSA_KV_RANK + ONES_ROWS, tm), lambda b, t: (b, 0, t)),
                   pl.BlockSpec((tm, LANES), lambda b, t: (b * n_t + t, 0)),
                   pl.BlockSpec((1, LANES, tm), lambda b, t: (b, 0, t))],
        out_shape=[jax.ShapeDtypeStruct((batch, DSA_Q_RANK, seq), BF16),
                   jax.ShapeDtypeStruct((batch * seq, DSA_KV_RANK), BF16),
                   jax.ShapeDtypeStruct((batch, DSA_KV_RANK + ONES_ROWS, seq), BF16),
                   jax.ShapeDtypeStruct((batch * seq, LANES), BF16),
                   jax.ShapeDtypeStruct((batch, LANES, seq), F32)],
        compiler_params=_cparams(2),
        name="dsa_latent",
    )(x, w, q_gain, kv_gain)


def _dsa_query_kernel(cqt_ref, wuqt_ref, wuk_ref, ql_ref, *, scale):
    hb, _, hd = wuk_ref.shape
    tq = ql_ref.shape[3] // hb
    q_t = jnp.dot(wuqt_ref[...], cqt_ref[0], preferred_element_type=F32).astype(BF16)
    for j in range(hb):
        ql_t = jnp.dot(wuk_ref[j], q_t[j * hd:(j + 1) * hd, :], preferred_element_type=F32)
        ql_t = (ql_t * scale).astype(BF16)
        for blk in range(ql_ref.shape[1]):
            ql_ref[0, blk, :, j * tq:(j + 1) * tq] = ql_t[:, blk * tq:(blk + 1) * tq]


def _dsa_query(cq_t, wuq_t, wuk, *, batch, seq, scale):
    tm = min(seq, 2048)
    hb = 8
    tq = ATT_TQ
    n_t = seq // tm
    n_heads, kv_rank, hd = wuk.shape
    return pl.pallas_call(
        functools.partial(_dsa_query_kernel, scale=scale),
        grid=(batch, n_t, n_heads // hb),
        in_specs=[pl.BlockSpec((1, DSA_Q_RANK, tm), lambda b, t, h: (b, 0, t)),
                  pl.BlockSpec((hb * hd, DSA_Q_RANK), lambda b, t, h: (h, 0)),
                  pl.BlockSpec((hb, kv_rank, hd), lambda b, t, h: (h, 0, 0))],
        out_specs=pl.BlockSpec((1, tm // tq, kv_rank, hb * tq), lambda b, t, h: (b, t, 0, h)),
        out_shape=jax.ShapeDtypeStruct((batch, seq // tq, kv_rank, n_heads * tq), BF16),
        compiler_params=_cparams(3),
        name="dsa_query",
    )(cq_t, wuq_t, wuk)


def _indexer_kernel(keys_ref, slabt_ref, cqt_ref, widx_ref, bias_ref, qt_scr, sc_scr, *, k_top, w_scale, n_iter):
    i = pl.program_id(1)
    n_t, kt_w, tq = sc_scr.shape
    cq_t = cqt_ref[0]
    n_h, hd_pad, q_rank = widx_ref.shape
    q_all = jnp.dot(widx_ref[...].reshape(n_h * hd_pad, q_rank), cq_t, preferred_element_type=F32)
    qt_scr[...] = q_all.astype(BF16).reshape(n_h, hd_pad, tq)
    w_t = slabt_ref[0][IDX_HEAD_DIM:IDX_HEAD_DIM + IDX_N_HEADS, :] * w_scale
    q_pos = i * tq + lax.broadcasted_iota(jnp.int32, (kt_w, tq), 1)

    for kt in range(n_t):
        @pl.when(kt <= i)
        def _(kt=kt):
            keys = keys_ref[kt * kt_w:(kt + 1) * kt_w, :]
            acc = jnp.zeros((kt_w, tq), F32)
            for h in range(IDX_N_HEADS):
                s = jnp.dot(keys, qt_scr[h], preferred_element_type=F32)
                acc = acc + jnp.maximum(s, 0.0) * w_t[h:h + 1, :]
            k_pos = kt * kt_w + lax.broadcasted_iota(jnp.int32, (kt_w, tq), 0)
            sc_scr[kt] = jnp.where(k_pos <= q_pos, acc, -jnp.inf)

    def bounds(kt, carry):
        lo, hi = carry
        s = sc_scr[kt]
        hi = jnp.maximum(hi, jnp.max(s, axis=0, keepdims=True))
        lo = jnp.minimum(lo, jnp.min(jnp.where(s == -jnp.inf, jnp.inf, s), axis=0, keepdims=True))
        return lo, hi

    lo, hi = lax.fori_loop(0, i + 1, bounds,
                           (jnp.full((1, tq), jnp.inf, F32), jnp.full((1, tq), -jnp.inf, F32)))

    def halve(carry):
        it, lo, hi, n_lo = carry
        mid = 0.5 * lo + 0.5 * hi

        mid_rows = jnp.broadcast_to(mid, (SUBLANES, tq))

        def count(kt, cnt):
            for r0 in range(0, kt_w, SUBLANES):
                cnt = jnp.where(sc_scr[kt, r0:r0 + SUBLANES, :] >= mid_rows, cnt + 1.0, cnt)
            return cnt

        cnt = lax.fori_loop(0, i + 1, count, jnp.zeros((SUBLANES, tq), F32))
        cnt = jnp.sum(cnt, axis=0, keepdims=True)
        enough = cnt >= k_top
        return it + 1, jnp.where(enough, mid, lo), jnp.where(enough, hi, mid), jnp.where(enough, cnt, n_lo)

    def unresolved(carry):
        it, _, _, n_lo = carry
        return jnp.logical_and(it < n_iter, jnp.max(n_lo) > k_top)

    n_causal = (i * tq + lax.broadcasted_iota(jnp.int32, (1, tq), 1) + 1).astype(F32)
    _, lo, hi, _ = lax.while_loop(unresolved, lambda c: halve(halve(c)), (jnp.int32(0), lo, hi, n_causal))

    for kt in range(n_t):
        @pl.when(kt <= i)
        def _(kt=kt):
            bias_ref[0, kt * kt_w:(kt + 1) * kt_w, :] = jnp.where(sc_scr[kt] >= lo, 0.0, -jnp.inf)

        @pl.when(kt > i)
        def _(kt=kt):
            bias_ref[0, kt * kt_w:(kt + 1) * kt_w, :] = jnp.full((kt_w, tq), -jnp.inf, F32)


def _dsa_indexer(keys, slab_t, cq_t, widx_t, *, batch, seq, k_top, w_scale):
    tq = IDX_TQ
    n_t = seq // tq
    return pl.pallas_call(
        functools.partial(_indexer_kernel, k_top=k_top, w_scale=w_scale, n_iter=BISECT_ITERS),
        grid=(batch, n_t),
        in_specs=[pl.BlockSpec((seq, LANES), lambda b, i: (b, 0)),
                  pl.BlockSpec((1, LANES, tq), lambda b, i: (b, 0, i)),
                  pl.BlockSpec((1, DSA_Q_RANK, tq), lambda b, i: (b, 0, i)),
                  pl.BlockSpec(widx_t.shape, lambda b, i: (0, 0, 0))],
        out_specs=pl.BlockSpec((1, seq, tq), lambda b, i: (b, 0, i)),
        out_shape=jax.ShapeDtypeStruct((batch, seq, seq), F32),
        scratch_shapes=[pltpu.VMEM((IDX_N_HEADS, LANES, tq), BF16),
                        pltpu.VMEM((n_t, tq, tq), F32)],
        compiler_params=_cparams(2),
        name="dsa_indexer",
    )(keys, slab_t, cq_t, widx_t)


def _attn_kernel(ql_ref, bias_ref, k_ref, vt_ref, wuvt_ref, gate_ref, *rest, n_keys, hc):
    o_ref = rest[-1]
    tq = bias_ref.shape[2]
    hb, hd, kv_rank = wuvt_ref.shape
    cols = hc * tq
    n_chunks = hb // hc
    keys = k_ref[:n_keys, :]
    bias_t = bias_ref[0]
    bias_rep = jnp.concatenate([bias_t] * hc, axis=1) if hc > 1 else bias_t

    def scores(c):
        q_t = ql_ref[0, 0, :, c * cols:(c + 1) * cols]
        return jnp.dot(keys, q_t, preferred_element_type=F32) + bias_rep

    def finish(c, s):
        m = jnp.max(s, axis=0, keepdims=True)
        p = jnp.exp2(s - m).astype(BF16)
        o_aug = jnp.dot(vt_ref[0], p, preferred_element_type=F32)
        o_t = (o_aug[:kv_rank] / o_aug[kv_rank:kv_rank + 1]).astype(BF16)
        for j in range(hc):
            h = c * hc + j
            oh_t = jnp.dot(wuvt_ref[h], o_t[:, j * tq:(j + 1) * tq], preferred_element_type=F32)
            gate = gate_ref[:, h * hd:(h + 1) * hd].astype(F32)
            o_ref[:, h * hd:(h + 1) * hd] = (oh_t.T * gate).astype(o_ref.dtype)

    s_next = scores(0)
    for c in range(n_chunks):
        s_cur = s_next
        if c + 1 < n_chunks:
            s_next = scores(c + 1)
        finish(c, s_cur)


def _dsa_attention(ql, bias, ckv_t, ckv, wuv_t, gate, *, batch, seq):
    tq = ATT_TQ
    n_q = seq // tq
    n_heads, hd, kv_rank = wuv_t.shape
    width = n_heads * hd
    og = None
    for i in range(n_q):
        n_keys = (i + 1) * tq
        hb = n_heads
        while hb > 1 and hb * tq * n_keys > ATT_STEP_ELEMS:
            hb //= 2
        hc = max(1, min(hb, ATT_CHUNK_COLS // tq))
        in_specs = [pl.BlockSpec((1, 1, kv_rank, hb * tq), lambda b, g, i=i: (b, i, 0, g)),
                    pl.BlockSpec((1, n_keys, tq), lambda b, g, i=i: (b, 0, i)),
                    pl.BlockSpec((seq, kv_rank), lambda b, g: (b, 0)),
                    pl.BlockSpec((1, kv_rank + ONES_ROWS, n_keys), lambda b, g: (b, 0, 0)),
                    pl.BlockSpec((hb, hd, kv_rank), lambda b, g: (g, 0, 0)),
                    pl.BlockSpec((tq, hb * hd), lambda b, g, i=i: (b * n_q + i, g))]
        args = [ql, bias, ckv, ckv_t, wuv_t, gate]
        aliases = {}
        if og is not None:
            in_specs.append(pl.BlockSpec(memory_space=pl.ANY))
            args.append(og)
            aliases = {len(args) - 1: 0}
        og = pl.pallas_call(
            functools.partial(_attn_kernel, n_keys=n_keys, hc=hc),
            grid=(batch, n_heads // hb),
            in_specs=in_specs,
            out_specs=pl.BlockSpec((tq, hb * hd), lambda b, g, i=i: (b * n_q + i, g)),
            out_shape=jax.ShapeDtypeStruct((batch * seq, width), BF16),
            input_output_aliases=aliases,
            compiler_params=_cparams(2),
            name="dsa_attention_q%d" % i,
        )(*args)
    return og


def _ssd_layer(x, w_in, conv_w, conv_b, dt_bias, a_log, d_skip, norm_g, w_out, ln_g, ln_b,
               *, batch, seq, alpha, want_bf16):
    d_inner = w_out.shape[0]
    n_heads = a_log.shape[0]
    n_groups = SSD_N_GROUPS
    heads_per_group = n_heads // n_groups
    conv_dim = conv_w.shape[1]
    gw = d_inner // n_groups
    assert gw % LANES == 0 and heads_per_group <= LANES and SSD_D_STATE == LANES
    assert SSD_HEAD_DIM % SUBLANES == 0 and n_heads <= LANES and conv_w.shape[1] == d_inner + 2 * n_groups * SSD_D_STATE

    def per_group_lanes(v):
        lead = v.shape[:-1]
        v = v.reshape(lead + (n_groups, heads_per_group))
        v = jnp.pad(v, [(0, 0)] * len(lead) + [(0, 0), (0, LANES - heads_per_group)])
        return v.reshape(lead + (n_groups * LANES,))

    w_bf = w_in.astype(BF16)
    w_dt = jnp.pad(w_bf[:, d_inner + conv_dim:], ((0, 0), (0, LANES - n_heads)))
    dt_b = jnp.pad(dt_bias, (0, LANES - n_heads))[None, :]
    head_of_lane = jnp.arange(n_groups * LANES)
    head_of_lane = jnp.where(head_of_lane % LANES < heads_per_group,
                             (head_of_lane // LANES) * heads_per_group + head_of_lane % LANES, -1)
    spread = (jnp.arange(LANES)[:, None] == head_of_lane[None, :]).astype(BF16)
    alog = per_group_lanes(a_log).reshape(n_groups, LANES)
    dskip = jnp.repeat(d_skip, SSD_HEAD_DIM).reshape(n_groups, gw)
    lane_of = jnp.arange(gw) // SSD_HEAD_DIM
    expand_mat = (jnp.arange(LANES)[:, None] == lane_of[None, :]).astype(BF16)

    dt, x_bf = _dt_proj(x, w_dt, dt_b, spread, tm=min(seq, 1024))
    z = _proj(x_bf, w_bf, batch=batch, seq=seq, tn=1024, n_dim=d_inner)
    xbc = _proj(x_bf, w_bf, batch=batch, seq=seq, tn=1024, col0=d_inner, n_dim=conv_dim,
                conv=(conv_w, conv_b[None, :]))
    yn = _ssd_scan(xbc, z, dt, alog, dskip, norm_g[None, :], expand_mat, batch=batch, seq=seq, d_inner=d_inner)
    return _outproj_ln(yn, w_out.astype(BF16), x, ln_g[None, :], ln_b[None, :], alpha=alpha, tm=512,
                       want_bf16=want_bf16)


def _dsa_layer(x, x_bf, w_in, q_norm_g, kv_norm_g, w_uq, w_uk, w_uv, w_idx_q, w_out, ln_g, ln_b,
               *, batch, seq, alpha, want_bf16):
    assert seq % IDX_TQ == 0 and seq % ATT_TQ == 0 and IDX_HEAD_DIM + IDX_N_HEADS <= LANES
    small = DSA_Q_RANK + DSA_KV_RANK + IDX_HEAD_DIM + IDX_N_HEADS
    w_small = jnp.pad(w_in[:, :small], ((0, 0), (0, DSA_Q_RANK + DSA_KV_RANK + LANES - small))).astype(BF16)
    w_gate = w_in[:, small:].astype(BF16)
    wuq_t = w_uq.T.astype(BF16)
    wuv_t = w_uv.transpose(0, 2, 1).astype(BF16)
    widx_t = w_idx_q.reshape(DSA_Q_RANK, IDX_N_HEADS, IDX_HEAD_DIM).transpose(1, 2, 0)
    widx_t = jnp.pad(widx_t, ((0, 0), (0, LANES - IDX_HEAD_DIM), (0, 0))).astype(BF16)
    k_top = min(IDX_TOPK, seq // 4)

    cq_t, ckv, ckv_t, slab, slab_t = _dsa_latent(x_bf, w_small, q_norm_g[None, :], kv_norm_g[None, :],
                                                     batch=batch, seq=seq)
    gate = _proj(x_bf, w_gate, batch=batch, seq=seq, tn=1024)
    ql = _dsa_query(cq_t, wuq_t, w_uk.astype(BF16), batch=batch, seq=seq,
                    scale=DSA_HEAD_DIM ** -0.5 * math.log2(math.e))
    bias = _dsa_indexer(slab, slab_t, cq_t, widx_t, batch=batch, seq=seq, k_top=k_top,
                        w_scale=IDX_N_HEADS ** -0.5 * IDX_HEAD_DIM ** -0.5)
    og = _dsa_attention(ql, bias, ckv_t, ckv, wuv_t, gate, batch=batch, seq=seq)
    return _outproj_ln(og, w_out.astype(BF16), x, ln_g[None, :], ln_b[None, :], alpha=alpha, tm=512,
                       want_bf16=want_bf16)


def kernel(x, ssd_w_in, ssd_conv_w, ssd_conv_b, ssd_dt_bias, ssd_a_log, ssd_d_skip, ssd_norm_g, ssd_w_out,
           dsa_w_in, dsa_q_norm_g, dsa_kv_norm_g, dsa_w_uq, dsa_w_uk, dsa_w_uv, dsa_w_idx_q, dsa_w_out,
           ln_g, ln_b):
    batch, seq, d_model = x.shape
    depth = ln_g.shape[0]
    alpha = (2.0 * depth) ** 0.25
    xf = x.reshape(batch * seq, d_model)
    x_bf = None
    for i in range(depth):
        j = i // 2
        next_is_dsa = i + 1 < depth and (i + 1) % 2 == 1
        if i % 2 == 0:
            xf, x_bf = _ssd_layer(xf, ssd_w_in[j], ssd_conv_w[j], ssd_conv_b[j], ssd_dt_bias[j],
                                  ssd_a_log[j], ssd_d_skip[j], ssd_norm_g[j], ssd_w_out[j], ln_g[i], ln_b[i],
                                  batch=batch, seq=seq, alpha=alpha, want_bf16=next_is_dsa)
        else:
            x_bf = xf.astype(BF16) if x_bf is None else x_bf
            xf, x_bf = _dsa_layer(xf, x_bf, dsa_w_in[j], dsa_q_norm_g[j], dsa_kv_norm_g[j], dsa_w_uq[j],
                                  dsa_w_uk[j], dsa_w_uv[j], dsa_w_idx_q[j], dsa_w_out[j], ln_g[i], ln_b[i],
                                  batch=batch, seq=seq, alpha=alpha, want_bf16=next_is_dsa)
    return xf.reshape(batch, seq, d_model)
```

```python
import functools
import math

import jax
import jax.numpy as jnp
from jax import lax
from jax.experimental import pallas as pl
from jax.experimental.pallas import tpu as pltpu

F32 = jnp.float32
BF16 = jnp.bfloat16

V7X_VMEM_BYTES = 64 * 1024 * 1024
LANES = 128
SUBLANES = 8
VMEM_LIMIT_BYTES = V7X_VMEM_BYTES - 8 * 1024 * 1024

LN_EPS = 1e-5
RMS_EPS = 1e-6

SSD_HEAD_DIM = 64
SSD_N_GROUPS = 8
SSD_D_STATE = 128
SSD_CHUNK = 256
DSA_N_HEADS = 32
DSA_HEAD_DIM = 128
DSA_Q_RANK = 512
DSA_KV_RANK = 256
IDX_N_HEADS = 16
IDX_HEAD_DIM = 64
IDX_TOPK = 256

ATT_TQ = 256
ATT_STEP_ELEMS = 8 * 1024 * 1024
ATT_CHUNK_COLS = 2048
IDX_TQ = 512
LATENT_ROWS = 1024
BISECT_ITERS = 40
ONES_ROWS = 16
CONV_ROW_CHUNK = 256
LN_ROW_CHUNK = 128
SSD_GROUPS_PER_STEP = 2


def _cparams(n_axes):
    return pltpu.CompilerParams(dimension_semantics=("arbitrary",) * n_axes,
                                vmem_limit_bytes=VMEM_LIMIT_BYTES)


LOG2_E = math.log2(math.e)


def _sigmoid(v):
    return 1.0 / (1.0 + jnp.exp2(v * (-LOG2_E)))


def _proj_silu_kernel(x_ref, w_ref, o_ref, *, chunk):
    w = w_ref[...]
    for r0 in range(0, o_ref.shape[0], chunk):
        y = jnp.dot(x_ref[r0:r0 + chunk, :], w, preferred_element_type=F32)
        o_ref[r0:r0 + chunk, :] = (y * _sigmoid(y)).astype(o_ref.dtype)


def _proj_conv_silu_kernel(x_ref, w_ref, conv_w_ref, conv_b_ref, o_ref, y_scr, *, chunk):
    seq, tn = o_ref.shape
    k_taps = conv_w_ref.shape[0]
    head = y_scr.shape[0] - seq
    y_scr[:head, :] = jnp.zeros((head, tn), F32)
    w = w_ref[...]

    def matmul(r0):
        y_scr[head + r0:head + r0 + chunk, :] = jnp.dot(x_ref[r0:r0 + chunk, :], w, preferred_element_type=F32)

    def finish(r0):
        acc = conv_b_ref[...] + conv_w_ref[k_taps - 1:k_taps, :] * y_scr[head + r0:head + r0 + chunk, :]
        for back in range(1, k_taps):
            tap = conv_w_ref[k_taps - 1 - back:k_taps - back, :]
            acc = acc + tap * y_scr[head + r0 - back:head + r0 - back + chunk, :]
        o_ref[r0:r0 + chunk, :] = (acc * _sigmoid(acc)).astype(o_ref.dtype)

    for r0 in range(0, seq, chunk):
        matmul(r0)
        finish(r0)


def _proj(x, w, *, batch, seq, tn, col0=0, n_dim=None, conv=None, out_dtype=BF16):
    k_dim = w.shape[0]
    n_dim = w.shape[1] - col0 if n_dim is None else n_dim
    assert n_dim % tn == 0 and tn % LANES == 0 and col0 % tn == 0
    j0 = col0 // tn
    in_specs = [pl.BlockSpec((seq, k_dim), lambda b, j: (b, 0)),
                pl.BlockSpec((k_dim, tn), lambda b, j: (0, j0 + j))]
    if conv is None:
        body = functools.partial(_proj_silu_kernel, chunk=CONV_ROW_CHUNK)
        args, scratch, name = (x, w), [], "proj_silu"
    else:
        body = functools.partial(_proj_conv_silu_kernel, chunk=CONV_ROW_CHUNK)
        args = (x, w) + tuple(conv)
        in_specs += [pl.BlockSpec((c.shape[0], tn), lambda b, j: (0, j)) for c in conv]
        assert conv[0].shape[0] - 1 <= SUBLANES
        scratch, name = [pltpu.VMEM((SUBLANES + seq, tn), F32)], "proj_conv_silu"
    return pl.pallas_call(
        body,
        grid=(batch, n_dim // tn),
        in_specs=in_specs,
        out_specs=pl.BlockSpec((seq, tn), lambda b, j: (b, j)),
        out_shape=jax.ShapeDtypeStruct((batch * seq, n_dim), out_dtype),
        scratch_shapes=scratch,
        compiler_params=_cparams(2),
        name=name,
    )(*args)


def _dt_kernel(x_ref, w_ref, b_ref, spread_ref, o_ref, xbf_ref):
    x = x_ref[...].astype(BF16)
    xbf_ref[...] = x
    y = jnp.dot(x, w_ref[...], preferred_element_type=F32) + b_ref[...]
    y = jnp.maximum(y, 0.0) + jnp.log1p(jnp.exp(-jnp.abs(y)))
    spread = spread_ref[...]
    hi = y.astype(BF16)
    rest = y - hi.astype(F32)
    mid = rest.astype(BF16)
    lo = (rest - mid.astype(F32)).astype(BF16)
    o_ref[...] = (jnp.dot(hi, spread, preferred_element_type=F32)
                  + jnp.dot(mid, spread, preferred_element_type=F32)
                  + jnp.dot(lo, spread, preferred_element_type=F32))


def _dt_proj(x, w, bias, spread, *, tm):
    rows, k_dim = x.shape
    width = spread.shape[1]
    return pl.pallas_call(
        _dt_kernel,
        grid=(rows // tm,),
        in_specs=[pl.BlockSpec((tm, k_dim), lambda i: (i, 0)),
                  pl.BlockSpec((k_dim, LANES), lambda i: (0, 0)),
                  pl.BlockSpec((1, LANES), lambda i: (0, 0)),
                  pl.BlockSpec((LANES, width), lambda i: (0, 0))],
        out_specs=[pl.BlockSpec((tm, width), lambda i: (i, 0)),
                   pl.BlockSpec((tm, k_dim), lambda i: (i, 0))],
        out_shape=[jax.ShapeDtypeStruct((rows, width), F32),
                   jax.ShapeDtypeStruct((rows, k_dim), BF16)],
        compiler_params=_cparams(1),
        name="dt_proj",
    )(x, w, bias, spread)


def _cumsum_rows(v):
    n = v.shape[0]
    row = lax.broadcasted_iota(jnp.int32, v.shape, 0)
    shift = 1
    while shift < n:
        v = v + jnp.where(row >= shift, pltpu.roll(v, shift, 0), 0.0)
        shift *= 2
    return v


def _ssd_kernel(xs_ref, b_ref, c_ref, z_ref, dt_ref, alog_ref, dskip_ref, ng_ref, e_ref, o_ref,
                state_ref, *, groups, heads_per_group, head_dim):
    @pl.when(pl.program_id(2) == 0)
    def _():
        state_ref[...] = jnp.zeros_like(state_ref)

    gw = e_ref.shape[1]
    n = b_ref.shape[1] // groups
    for s in range(groups):
        g = pl.program_id(1) * groups + s
        lanes = slice(s * gw, (s + 1) * gw)
        _ssd_group(xs_ref.at[:, lanes], b_ref.at[:, s * n:(s + 1) * n], c_ref.at[:, s * n:(s + 1) * n],
                   z_ref.at[:, lanes], dt_ref.at[:, s * LANES:(s + 1) * LANES], alog_ref[pl.ds(g, 1), :],
                   dskip_ref[pl.ds(g, 1), :], ng_ref.at[:, lanes], e_ref, o_ref.at[:, lanes], state_ref.at[s],
                   heads_per_group=heads_per_group, head_dim=head_dim)


def _ssd_group(xs_ref, b_ref, c_ref, z_ref, dt_ref, alog, dskip, ng_ref, e_ref, o_ref, state_ref,
               *, heads_per_group, head_dim):
    q = xs_ref.shape[0]
    xs = xs_ref[...].astype(F32)
    bm = b_ref[...]
    cm = c_ref[...]
    dt = dt_ref[...]
    a = -jnp.exp(alog) * LOG2_E
    acs = _cumsum_rows(dt * a)
    acs_t = acs.T
    last = acs[q - 1:q, :]
    expand_mat = e_ref[...]

    def expand(v):
        return jnp.dot(v.astype(BF16), expand_mat, preferred_element_type=F32)

    def expand_hi_lo(v):
        hi = v.astype(BF16)
        lo = (v - hi.astype(F32)).astype(BF16)
        return (jnp.dot(hi, expand_mat, preferred_element_type=F32)
                + jnp.dot(lo, expand_mat, preferred_element_type=F32))

    decay_in_x = expand_hi_lo(jnp.exp2(acs))
    decay_out_dt_x = expand(jnp.exp2(last - acs) * dt)

    bc = lax.dot_general(bm, cm, (((1,), (1,)), ((), ())), preferred_element_type=F32)
    blk = LANES
    tri = (lax.broadcasted_iota(jnp.int32, (blk, blk), 1) >= lax.broadcasted_iota(jnp.int32, (blk, blk), 0))
    xs_t = xs.T
    dt_t = dt.T

    def masked_bc(r):
        rows = []
        for s0 in range(0, q, blk):
            parts = []
            if s0 > 0:
                parts.append(jnp.zeros((blk, s0), BF16))
            diff = acs_t[r:r + 1, s0:s0 + blk] - acs[s0:s0 + blk, r:r + 1]
            decay = jnp.exp2(jnp.where(tri, diff, -jnp.inf))
            parts.append((bc[s0:s0 + blk, s0:s0 + blk] * decay).astype(BF16))
            if s0 + blk < q:
                diff = acs_t[r:r + 1, s0 + blk:] - acs[s0:s0 + blk, r:r + 1]
                parts.append((bc[s0:s0 + blk, s0 + blk:] * jnp.exp2(diff)).astype(BF16))
            rows.append(jnp.concatenate(parts, axis=1) if len(parts) > 1 else parts[0])
        return jnp.concatenate(rows, axis=0) if len(rows) > 1 else rows[0]

    y_t = []
    for r in range(heads_per_group):
        x_head_t = (xs_t[r * head_dim:(r + 1) * head_dim, :] * dt_t[r:r + 1, :]).astype(BF16)
        y_t.append(jnp.dot(x_head_t, masked_bc(r), preferred_element_type=F32))
    y_diag = jnp.concatenate(y_t, axis=0).T

    s_prev = state_ref[...]
    y_off = jnp.dot(cm, s_prev.astype(BF16), preferred_element_type=F32) * decay_in_x
    y = y_diag + y_off + dskip * xs
    b_t = bm.astype(F32).T.astype(BF16)
    s_new = jnp.dot(b_t, (xs * decay_out_dt_x).astype(BF16), preferred_element_type=F32)
    state_ref[...] = s_prev * decay_in_x[q - 1:q, :] + s_new

    yg = y * z_ref[...].astype(F32)
    ms = jnp.mean(yg * yg, axis=-1, keepdims=True)
    o_ref[...] = (yg * lax.rsqrt(ms + RMS_EPS) * ng_ref[...]).astype(o_ref.dtype)


def _ssd_scan(xbc, z, dt, alog, dskip, norm_g, expand_mat, *, batch, seq, d_inner):
    n_groups, d_state, chunk = SSD_N_GROUPS, SSD_D_STATE, math.gcd(SSD_CHUNK, seq)
    gw = d_inner // n_groups
    heads_per_group = gw // SSD_HEAD_DIM
    n_chunks = seq // chunk
    gps = SSD_GROUPS_PER_STEP
    assert n_groups % gps == 0
    b_col0 = d_inner // (gps * d_state)
    c_col0 = b_col0 + n_groups // gps
    row = lambda b, g, c: b * n_chunks + c
    return pl.pallas_call(
        functools.partial(_ssd_kernel, groups=gps, heads_per_group=heads_per_group, head_dim=SSD_HEAD_DIM),
        grid=(batch, n_groups // gps, n_chunks),
        in_specs=[
            pl.BlockSpec((chunk, gps * gw), lambda b, g, c: (row(b, g, c), g)),
            pl.BlockSpec((chunk, gps * d_state), lambda b, g, c: (row(b, g, c), b_col0 + g)),
            pl.BlockSpec((chunk, gps * d_state), lambda b, g, c: (row(b, g, c), c_col0 + g)),
            pl.BlockSpec((chunk, gps * gw), lambda b, g, c: (row(b, g, c), g)),
            pl.BlockSpec((chunk, gps * LANES), lambda b, g, c: (row(b, g, c), g)),
            pl.BlockSpec((n_groups, LANES), lambda b, g, c: (0, 0)),
            pl.BlockSpec((n_groups, gw), lambda b, g, c: (0, 0)),
            pl.BlockSpec((1, gps * gw), lambda b, g, c: (0, g)),
            pl.BlockSpec((LANES, gw), lambda b, g, c: (0, 0)),
        ],
        out_specs=pl.BlockSpec((chunk, gps * gw), lambda b, g, c: (row(b, g, c), g)),
        out_shape=jax.ShapeDtypeStruct((batch * seq, d_inner), BF16),
        scratch_shapes=[pltpu.VMEM((gps, d_state, gw), F32)],
        compiler_params=_cparams(3),
        name="ssd_scan",
    )(xbc, xbc, xbc, z, dt, alog, dskip, norm_g, expand_mat)


def _outproj_ln_kernel(y_ref, w_ref, x_ref, g_ref, b_ref, o_ref, maybe_obf_ref=None, *, alpha, chunk):
    for r0 in range(0, y_ref.shape[0], chunk):
        rows = slice(r0, r0 + chunk)
        h = jnp.dot(y_ref[rows, :], w_ref[...], preferred_element_type=F32)
        v = alpha * x_ref[rows, :] + h
        mu = jnp.mean(v, axis=-1, keepdims=True)
        d = v - mu
        var = jnp.mean(d * d, axis=-1, keepdims=True)
        out = d * lax.rsqrt(var + LN_EPS) * g_ref[...] + b_ref[...]
        o_ref[rows, :] = out
        if maybe_obf_ref is not None:
            maybe_obf_ref[rows, :] = out.astype(BF16)


def _outproj_ln(y, w, x, ln_g, ln_b, *, alpha, tm, want_bf16):
    m, k_dim = y.shape
    d = w.shape[1]
    n_out = 2 if want_bf16 else 1
    outs = pl.pallas_call(
        functools.partial(_outproj_ln_kernel, alpha=alpha, chunk=LN_ROW_CHUNK),
        grid=(m // tm,),
        in_specs=[pl.BlockSpec((tm, k_dim), lambda i: (i, 0)),
                  pl.BlockSpec((k_dim, d), lambda i: (0, 0), pipeline_mode=pl.Buffered(1)),
                  pl.BlockSpec((tm, d), lambda i: (i, 0)),
                  pl.BlockSpec((1, d), lambda i: (0, 0)),
                  pl.BlockSpec((1, d), lambda i: (0, 0))],
        out_specs=[pl.BlockSpec((tm, d), lambda i: (i, 0))] * n_out,
        out_shape=[jax.ShapeDtypeStruct((m, d), F32), jax.ShapeDtypeStruct((m, d), BF16)][:n_out],
        compiler_params=_cparams(1),
        name="outproj_ln",
    )(y, w, x, ln_g, ln_b)
    return (outs[0], outs[1]) if want_bf16 else (outs[0], None)


def _dsa_latent_kernel(x_ref, w_ref, qg_ref, kvg_ref, cqt_ref, ckv_ref, ckvt_ref, slab_ref, slabt_ref):
    p = jnp.dot(x_ref[...], w_ref[...], preferred_element_type=F32)
    q_rank = cqt_ref.shape[1]
    kv_rank = ckv_ref.shape[1]
    cq = p[:, :q_rank]
    ckv = p[:, q_rank:q_rank + kv_rank]
    slab = p[:, q_rank + kv_rank:]
    cq = cq * lax.rsqrt(jnp.mean(cq * cq, axis=-1, keepdims=True) + RMS_EPS) * qg_ref[...]
    ckv = ckv * lax.rsqrt(jnp.mean(ckv * ckv, axis=-1, keepdims=True) + RMS_EPS) * kvg_ref[...]
    cqt_ref[0] = cq.T.astype(BF16)
    ckv_ref[...] = ckv.astype(BF16)
    ckvt_ref[0, :kv_rank, :] = ckv.T.astype(BF16)
    ckvt_ref[0, kv_rank:, :] = jnp.ones((ckvt_ref.shape[1] - kv_rank, ckvt_ref.shape[2]), BF16)
    slab_ref[...] = slab.astype(BF16)
    slabt_ref[0] = slab.T


def _dsa_latent(x, w, q_gain, kv_gain, *, batch, seq):
    tm = min(seq, LATENT_ROWS)
    n_t = seq // tm
    k_dim, n_dim = w.shape
    return pl.pallas_call(
        _dsa_latent_kernel,
        grid=(batch, n_t),
        in_specs=[pl.BlockSpec((tm, k_dim), lambda b, t: (b * n_t + t, 0)),
                  pl.BlockSpec((k_dim, n_dim), lambda b, t: (0, 0)),
                  pl.BlockSpec((1, DSA_Q_RANK), lambda b, t: (0, 0)),
                  pl.BlockSpec((1, DSA_KV_RANK), lambda b, t: (0, 0))],
        out_specs=[pl.BlockSpec((1, DSA_Q_RANK, tm), lambda b, t: (b, 0, t)),
                   pl.BlockSpec((tm, DSA_KV_RANK), lambda b, t: (b * n_t + t, 0)),
                   pl.BlockSpec((1, DSA_KV_RANK + ONES_ROWS, tm), lambda b, t: (b, 0, t)),
                   pl.BlockSpec((tm, LANES), lambda b, t: (b * n_t + t, 0)),
                   pl.BlockSpec((1, LANES, tm), lambda b, t: (b, 0, t))],
        out_shape=[jax.ShapeDtypeStruct((batch, DSA_Q_RANK, seq), BF16),
                   jax.ShapeDtypeStruct((batch * seq, DSA_KV_RANK), BF16),
                   jax.ShapeDtypeStruct((batch, DSA_KV_RANK + ONES_ROWS, seq), BF16),
                   jax.ShapeDtypeStruct((batch * seq, LANES), BF16),
                   jax.ShapeDtypeStruct((batch, LANES, seq), F32)],
        compiler_params=_cparams(2),
        name="dsa_latent",
    )(x, w, q_gain, kv_gain)


def _dsa_query_kernel(cqt_ref, wuqt_ref, wuk_ref, ql_ref, *, scale):
    hb, _, hd = wuk_ref.shape
    tq = ql_ref.shape[3] // hb
    q_t = jnp.dot(wuqt_ref[...], cqt_ref[0], preferred_element_type=F32).astype(BF16)
    for j in range(hb):
        ql_t = jnp.dot(wuk_ref[j], q_t[j * hd:(j + 1) * hd, :], preferred_element_type=F32)
        ql_t = (ql_t * scale).astype(BF16)
        for blk in range(ql_ref.shape[1]):
            ql_ref[0, blk, :, j * tq:(j + 1) * tq] = ql_t[:, blk * tq:(blk + 1) * tq]


def _dsa_query(cq_t, wuq_t, wuk, *, batch, seq, scale):
    tm = min(seq, 2048)
    hb = 8
    tq = ATT_TQ
    n_t = seq // tm
    n_heads, kv_rank, hd = wuk.shape
    return pl.pallas_call(
        functools.partial(_dsa_query_kernel, scale=scale),
        grid=(batch, n_t, n_heads // hb),
        in_specs=[pl.BlockSpec((1, DSA_Q_RANK, tm), lambda b, t, h: (b, 0, t)),
                  pl.BlockSpec((hb * hd, DSA_Q_RANK), lambda b, t, h: (h, 0)),
                  pl.BlockSpec((hb, kv_rank, hd), lambda b, t, h: (h, 0, 0))],
        out_specs=pl.BlockSpec((1, tm // tq, kv_rank, hb * tq), lambda b, t, h: (b, t, 0, h)),
        out_shape=jax.ShapeDtypeStruct((batch, seq // tq, kv_rank, n_heads * tq), BF16),
        compiler_params=_cparams(3),
        name="dsa_query",
    )(cq_t, wuq_t, wuk)


def _indexer_kernel(keys_ref, slabt_ref, cqt_ref, widx_ref, bias_ref, qt_scr, sc_scr, *, k_top, w_scale, n_iter):
    i = pl.program_id(1)
    n_t, kt_w, tq = sc_scr.shape
    cq_t = cqt_ref[0]
    n_h, hd_pad, q_rank = widx_ref.shape
    q_all = jnp.dot(widx_ref[...].reshape(n_h * hd_pad, q_rank), cq_t, preferred_element_type=F32)
    qt_scr[...] = q_all.astype(BF16).reshape(n_h, hd_pad, tq)
    w_t = slabt_ref[0][IDX_HEAD_DIM:IDX_HEAD_DIM + IDX_N_HEADS, :] * w_scale
    q_pos = i * tq + lax.broadcasted_iota(jnp.int32, (kt_w, tq), 1)

    for kt in range(n_t):
        @pl.when(kt <= i)
        def _(kt=kt):
            keys = keys_ref[kt * kt_w:(kt + 1) * kt_w, :]
            acc = jnp.zeros((kt_w, tq), F32)
            for h in range(IDX_N_HEADS):
                s = jnp.dot(keys, qt_scr[h], preferred_element_type=F32)
                acc = acc + jnp.maximum(s, 0.0) * w_t[h:h + 1, :]
            k_pos = kt * kt_w + lax.broadcasted_iota(jnp.int32, (kt_w, tq), 0)
            sc_scr[kt] = jnp.where(k_pos <= q_pos, acc, -jnp.inf)

    def bounds(kt, carry):
        lo, hi = carry
        s = sc_scr[kt]
        hi = jnp.maximum(hi, jnp.max(s, axis=0, keepdims=True))
        lo = jnp.minimum(lo, jnp.min(jnp.where(s == -jnp.inf, jnp.inf, s), axis=0, keepdims=True))
        return lo, hi

    lo, hi = lax.fori_loop(0, i + 1, bounds,
                           (jnp.full((1, tq), jnp.inf, F32), jnp.full((1, tq), -jnp.inf, F32)))

    def halve(carry):
        it, lo, hi, n_lo = carry
        mid = 0.5 * lo + 0.5 * hi

        ones = jnp.ones((2 * SUBLANES, kt_w), BF16)

        def count(kt, cnt):
            hit = jnp.where(sc_scr[kt] >= mid, 1.0, 0.0).astype(BF16)
            return cnt + jnp.dot(ones, hit, preferred_element_type=F32)

        cnt = lax.fori_loop(0, i + 1, count, jnp.zeros((2 * SUBLANES, tq), F32))[0:1]
        enough = cnt >= k_top
        return it + 1, jnp.where(enough, mid, lo), jnp.where(enough, hi, mid), jnp.where(enough, cnt, n_lo)

    def unresolved(carry):
        it, _, _, n_lo = carry
        return jnp.logical_and(it < n_iter, jnp.max(n_lo) > k_top)

    n_causal = (i * tq + lax.broadcasted_iota(jnp.int32, (1, tq), 1) + 1).astype(F32)
    _, lo, hi, _ = lax.while_loop(unresolved, lambda c: halve(halve(c)), (jnp.int32(0), lo, hi, n_causal))

    for kt in range(n_t):
        @pl.when(kt <= i)
        def _(kt=kt):
            bias_ref[0, kt * kt_w:(kt + 1) * kt_w, :] = jnp.where(sc_scr[kt] >= lo, 0.0, -jnp.inf)

        @pl.when(kt > i)
        def _(kt=kt):
            bias_ref[0, kt * kt_w:(kt + 1) * kt_w, :] = jnp.full((kt_w, tq), -jnp.inf, F32)


def _dsa_indexer(keys, slab_t, cq_t, widx_t, *, batch, seq, k_top, w_scale):
    tq = IDX_TQ
    n_t = seq // tq
    return pl.pallas_call(
        functools.partial(_indexer_kernel, k_top=k_top, w_scale=w_scale, n_iter=BISECT_ITERS),
        grid=(batch, n_t),
        in_specs=[pl.BlockSpec((seq, LANES), lambda b, i: (b, 0)),
                  pl.BlockSpec((1, LANES, tq), lambda b, i: (b, 0, i)),
                  pl.BlockSpec((1, DSA_Q_RANK, tq), lambda b, i: (b, 0, i)),
                  pl.BlockSpec(widx_t.shape, lambda b, i: (0, 0, 0))],
        out_specs=pl.BlockSpec((1, seq, tq), lambda b, i: (b, 0, i)),
        out_shape=jax.ShapeDtypeStruct((batch, seq, seq), F32),
        scratch_shapes=[pltpu.VMEM((IDX_N_HEADS, LANES, tq), BF16),
                        pltpu.VMEM((n_t, tq, tq), F32)],
        compiler_params=_cparams(2),
        name="dsa_indexer",
    )(keys, slab_t, cq_t, widx_t)


def _attn_kernel(ql_ref, bias_ref, k_ref, vt_ref, wuvt_ref, gate_ref, *rest, n_keys, hc):
    o_ref = rest[-1]
    tq = bias_ref.shape[2]
    hb, hd, kv_rank = wuvt_ref.shape
    cols = hc * tq
    n_chunks = hb // hc
    keys = k_ref[:n_keys, :]
    bias_t = bias_ref[0]
    bias_rep = jnp.concatenate([bias_t] * hc, axis=1) if hc > 1 else bias_t

    def scores(c):
        q_t = ql_ref[0, 0, :, c * cols:(c + 1) * cols]
        return jnp.dot(keys, q_t, preferred_element_type=F32) + bias_rep

    def finish(c, s):
        m = jnp.max(s, axis=0, keepdims=True)
        p = jnp.exp2(s - m).astype(BF16)
        o_aug = jnp.dot(vt_ref[0], p, preferred_element_type=F32)
        o_t = (o_aug[:kv_rank] / o_aug[kv_rank:kv_rank + 1]).astype(BF16)
        for j in range(hc):
            h = c * hc + j
            oh_t = jnp.dot(wuvt_ref[h], o_t[:, j * tq:(j + 1) * tq], preferred_element_type=F32)
            gate = gate_ref[:, h * hd:(h + 1) * hd].astype(F32)
            o_ref[:, h * hd:(h + 1) * hd] = (oh_t.T * gate).astype(o_ref.dtype)

    s_next = scores(0)
    for c in range(n_chunks):
        s_cur = s_next
        if c + 1 < n_chunks:
            s_next = scores(c + 1)
        finish(c, s_cur)


def _dsa_attention(ql, bias, ckv_t, ckv, wuv_t, gate, *, batch, seq):
    tq = ATT_TQ
    n_q = seq // tq
    n_heads, hd, kv_rank = wuv_t.shape
    width = n_heads * hd
    og = None
    for i in range(n_q):
        n_keys = (i + 1) * tq
        hb = n_heads
        while hb > 1 and hb * tq * n_keys > ATT_STEP_ELEMS:
            hb //= 2
        hc = max(1, min(hb, ATT_CHUNK_COLS // tq))
        in_specs = [pl.BlockSpec((1, 1, kv_rank, hb * tq), lambda b, g, i=i: (b, i, 0, g)),
                    pl.BlockSpec((1, n_keys, tq), lambda b, g, i=i: (b, 0, i)),
                    pl.BlockSpec((seq, kv_rank), lambda b, g: (b, 0)),
                    pl.BlockSpec((1, kv_rank + ONES_ROWS, n_keys), lambda b, g: (b, 0, 0)),
                    pl.BlockSpec((hb, hd, kv_rank), lambda b, g: (g, 0, 0)),
                    pl.BlockSpec((tq, hb * hd), lambda b, g, i=i: (b * n_q + i, g))]
        args = [ql, bias, ckv, ckv_t, wuv_t, gate]
        aliases = {}
        if og is not None:
            in_specs.append(pl.BlockSpec(memory_space=pl.ANY))
            args.append(og)
            aliases = {len(args) - 1: 0}
        og = pl.pallas_call(
            functools.partial(_attn_kernel, n_keys=n_keys, hc=hc),
            grid=(batch, n_heads // hb),
            in_specs=in_specs,
            out_specs=pl.BlockSpec((tq, hb * hd), lambda b, g, i=i: (b * n_q + i, g)),
            out_shape=jax.ShapeDtypeStruct((batch * seq, width), BF16),
            input_output_aliases=aliases,
            compiler_params=_cparams(2),
            name="dsa_attention_q%d" % i,
        )(*args)
    return og


def _ssd_layer(x, w_in, conv_w, conv_b, dt_bias, a_log, d_skip, norm_g, w_out, ln_g, ln_b,
               *, batch, seq, alpha, want_bf16):
    d_inner = w_out.shape[0]
    n_heads = a_log.shape[0]
    n_groups = SSD_N_GROUPS
    heads_per_group = n_heads // n_groups
    conv_dim = conv_w.shape[1]
    gw = d_inner // n_groups
    assert gw % LANES == 0 and heads_per_group <= LANES and SSD_D_STATE == LANES
    assert SSD_HEAD_DIM % SUBLANES == 0 and n_heads <= LANES and conv_w.shape[1] == d_inner + 2 * n_groups * SSD_D_STATE

    def per_group_lanes(v):
        lead = v.shape[:-1]
        v = v.reshape(lead + (n_groups, heads_per_group))
        v = jnp.pad(v, [(0, 0)] * len(lead) + [(0, 0), (0, LANES - heads_per_group)])
        return v.reshape(lead + (n_groups * LANES,))

    w_bf = w_in.astype(BF16)
    w_dt = jnp.pad(w_bf[:, d_inner + conv_dim:], ((0, 0), (0, LANES - n_heads)))
    dt_b = jnp.pad(dt_bias, (0, LANES - n_heads))[None, :]
    head_of_lane = jnp.arange(n_groups * LANES)
    head_of_lane = jnp.where(head_of_lane % LANES < heads_per_group,
                             (head_of_lane // LANES) * heads_per_group + head_of_lane % LANES, -1)
    spread = (jnp.arange(LANES)[:, None] == head_of_lane[None, :]).astype(BF16)
    alog = per_group_lanes(a_log).reshape(n_groups, LANES)
    dskip = jnp.repeat(d_skip, SSD_HEAD_DIM).reshape(n_groups, gw)
    lane_of = jnp.arange(gw) // SSD_HEAD_DIM
    expand_mat = (jnp.arange(LANES)[:, None] == lane_of[None, :]).astype(BF16)

    dt, x_bf = _dt_proj(x, w_dt, dt_b, spread, tm=min(seq, 1024))
    z = _proj(x_bf, w_bf, batch=batch, seq=seq, tn=1024, n_dim=d_inner)
    xbc = _proj(x_bf, w_bf, batch=batch, seq=seq, tn=1024, col0=d_inner, n_dim=conv_dim,
                conv=(conv_w, conv_b[None, :]))
    yn = _ssd_scan(xbc, z, dt, alog, dskip, norm_g[None, :], expand_mat, batch=batch, seq=seq, d_inner=d_inner)
    return _outproj_ln(yn, w_out.astype(BF16), x, ln_g[None, :], ln_b[None, :], alpha=alpha, tm=512,
                       want_bf16=want_bf16)


def _dsa_layer(x, x_bf, w_in, q_norm_g, kv_norm_g, w_uq, w_uk, w_uv, w_idx_q, w_out, ln_g, ln_b,
               *, batch, seq, alpha, want_bf16):
    assert seq % IDX_TQ == 0 and seq % ATT_TQ == 0 and IDX_HEAD_DIM + IDX_N_HEADS <= LANES
    small = DSA_Q_RANK + DSA_KV_RANK + IDX_HEAD_DIM + IDX_N_HEADS
    w_small = jnp.pad(w_in[:, :small], ((0, 0), (0, DSA_Q_RANK + DSA_KV_RANK + LANES - small))).astype(BF16)
    w_gate = w_in[:, small:].astype(BF16)
    wuq_t = w_uq.T.astype(BF16)
    wuv_t = w_uv.transpose(0, 2, 1).astype(BF16)
    widx_t = w_idx_q.reshape(DSA_Q_RANK, IDX_N_HEADS, IDX_HEAD_DIM).transpose(1, 2, 0)
    widx_t = jnp.pad(widx_t, ((0, 0), (0, LANES - IDX_HEAD_DIM), (0, 0))).astype(BF16)
    k_top = min(IDX_TOPK, seq // 4)

    cq_t, ckv, ckv_t, slab, slab_t = _dsa_latent(x_bf, w_small, q_norm_g[None, :], kv_norm_g[None, :],
                                                     batch=batch, seq=seq)
    gate = _proj(x_bf, w_gate, batch=batch, seq=seq, tn=1024)
    ql = _dsa_query(cq_t, wuq_t, w_uk.astype(BF16), batch=batch, seq=seq,
                    scale=DSA_HEAD_DIM ** -0.5 * math.log2(math.e))
    bias = _dsa_indexer(slab, slab_t, cq_t, widx_t, batch=batch, seq=seq, k_top=k_top,
                        w_scale=IDX_N_HEADS ** -0.5 * IDX_HEAD_DIM ** -0.5)
    og = _dsa_attention(ql, bias, ckv_t, ckv, wuv_t, gate, batch=batch, seq=seq)
    return _outproj_ln(og, w_out.astype(BF16), x, ln_g[None, :], ln_b[None, :], alpha=alpha, tm=512,
                       want_bf16=want_bf16)


def kernel(x, ssd_w_in, ssd_conv_w, ssd_conv_b, ssd_dt_bias, ssd_a_log, ssd_d_skip, ssd_norm_g, ssd_w_out,
           dsa_w_in, dsa_q_norm_g, dsa_kv_norm_g, dsa_w_uq, dsa_w_uk, dsa_w_uv, dsa_w_idx_q, dsa_w_out,
           ln_g, ln_b):
    batch, seq, d_model = x.shape
    depth = ln_g.shape[0]
    alpha = (2.0 * depth) ** 0.25
    xf = x.reshape(batch * seq, d_model)
    x_bf = None
    for i in range(depth):
        j = i // 2
        next_is_dsa = i + 1 < depth and (i + 1) % 2 == 1
        if i % 2 == 0:
            xf, x_bf = _ssd_layer(xf, ssd_w_in[j], ssd_conv_w[j], ssd_conv_b[j], ssd_dt_bias[j],
                                  ssd_a_log[j], ssd_d_skip[j], ssd_norm_g[j], ssd_w_out[j], ln_g[i], ln_b[i],
                                  batch=batch, seq=seq, alpha=alpha, want_bf16=next_is_dsa)
        else:
            x_bf = xf.astype(BF16) if x_bf is None else x_bf
            xf, x_bf = _dsa_layer(xf, x_bf, dsa_w_in[j], dsa_q_norm_g[j], dsa_kv_norm_g[j], dsa_w_uq[j],
                                  dsa_w_uk[j], dsa_w_uv[j], dsa_w_idx_q[j], dsa_w_out[j], ln_g[i], ln_b[i],
                                  batch=batch, seq=seq, alpha=alpha, want_bf16=next_is_dsa)
    return xf.reshape(batch, seq, d_model)
```

```python
import functools
import math

import jax
import jax.numpy as jnp
from jax import lax
from jax.experimental import pallas as pl
from jax.experimental.pallas import tpu as pltpu

F32 = jnp.float32
BF16 = jnp.bfloat16

V7X_VMEM_BYTES = 64 * 1024 * 1024
LANES = 128
SUBLANES = 8
VMEM_LIMIT_BYTES = V7X_VMEM_BYTES - 8 * 1024 * 1024

LN_EPS = 1e-5
RMS_EPS = 1e-6

SSD_HEAD_DIM = 64
SSD_N_GROUPS = 8
SSD_D_STATE = 128
SSD_CHUNK = 256
DSA_N_HEADS = 32
DSA_HEAD_DIM = 128
DSA_Q_RANK = 512
DSA_KV_RANK = 256
IDX_N_HEADS = 16
IDX_HEAD_DIM = 64
IDX_TOPK = 256

ATT_TQ = 256
ATT_STEP_ELEMS = 8 * 1024 * 1024
ATT_CHUNK_COLS = 2048
IDX_TQ = 512
LATENT_ROWS = 1024
BISECT_ITERS = 40
ONES_ROWS = 16
CONV_ROW_CHUNK = 256
LN_ROW_CHUNK = 128
SSD_GROUPS_PER_STEP = 2


def _cparams(n_axes):
    return pltpu.CompilerParams(dimension_semantics=("arbitrary",) * n_axes,
                                vmem_limit_bytes=VMEM_LIMIT_BYTES)


LOG2_E = math.log2(math.e)


def _sigmoid(v):
    return 1.0 / (1.0 + jnp.exp2(v * (-LOG2_E)))


def _proj_silu_kernel(x_ref, w_ref, o_ref, *, chunk):
    w = w_ref[...]
    for r0 in range(0, o_ref.shape[0], chunk):
        y = jnp.dot(x_ref[r0:r0 + chunk, :], w, preferred_element_type=F32)
        o_ref[r0:r0 + chunk, :] = (y * _sigmoid(y)).astype(o_ref.dtype)


def _proj_conv_silu_kernel(x_ref, w_ref, conv_w_ref, conv_b_ref, o_ref, y_scr, *, chunk):
    seq, tn = o_ref.shape
    k_taps = conv_w_ref.shape[0]
    head = y_scr.shape[0] - seq
    y_scr[:head, :] = jnp.zeros((head, tn), F32)
    w = w_ref[...]

    def matmul(r0):
        y_scr[head + r0:head + r0 + chunk, :] = jnp.dot(x_ref[r0:r0 + chunk, :], w, preferred_element_type=F32)

    def finish(r0):
        acc = conv_b_ref[...] + conv_w_ref[k_taps - 1:k_taps, :] * y_scr[head + r0:head + r0 + chunk, :]
        for back in range(1, k_taps):
            tap = conv_w_ref[k_taps - 1 - back:k_taps - back, :]
            acc = acc + tap * y_scr[head + r0 - back:head + r0 - back + chunk, :]
        o_ref[r0:r0 + chunk, :] = (acc * _sigmoid(acc)).astype(o_ref.dtype)

    for r0 in range(0, seq, chunk):
        matmul(r0)
        finish(r0)


def _proj(x, w, *, batch, seq, tn, col0=0, n_dim=None, conv=None, out_dtype=BF16):
    k_dim = w.shape[0]
    n_dim = w.shape[1] - col0 if n_dim is None else n_dim
    assert n_dim % tn == 0 and tn % LANES == 0 and col0 % tn == 0
    j0 = col0 // tn
    in_specs = [pl.BlockSpec((seq, k_dim), lambda b, j: (b, 0)),
                pl.BlockSpec((k_dim, tn), lambda b, j: (0, j0 + j))]
    if conv is None:
        body = functools.partial(_proj_silu_kernel, chunk=CONV_ROW_CHUNK)
        args, scratch, name = (x, w), [], "proj_silu"
    else:
        body = functools.partial(_proj_conv_silu_kernel, chunk=CONV_ROW_CHUNK)
        args = (x, w) + tuple(conv)
        in_specs += [pl.BlockSpec((c.shape[0], tn), lambda b, j: (0, j)) for c in conv]
        assert conv[0].shape[0] - 1 <= SUBLANES
        scratch, name = [pltpu.VMEM((SUBLANES + seq, tn), F32)], "proj_conv_silu"
    return pl.pallas_call(
        body,
        grid=(batch, n_dim // tn),
        in_specs=in_specs,
        out_specs=pl.BlockSpec((seq, tn), lambda b, j: (b, j)),
        out_shape=jax.ShapeDtypeStruct((batch * seq, n_dim), out_dtype),
        scratch_shapes=scratch,
        compiler_params=_cparams(2),
        name=name,
    )(*args)


def _dt_kernel(x_ref, w_ref, b_ref, spread_ref, o_ref, xbf_ref):
    x = x_ref[...].astype(BF16)
    xbf_ref[...] = x
    y = jnp.dot(x, w_ref[...], preferred_element_type=F32) + b_ref[...]
    y = jnp.maximum(y, 0.0) + jnp.log1p(jnp.exp(-jnp.abs(y)))
    spread = spread_ref[...]
    hi = y.astype(BF16)
    rest = y - hi.astype(F32)
    mid = rest.astype(BF16)
    lo = (rest - mid.astype(F32)).astype(BF16)
    o_ref[...] = (jnp.dot(hi, spread, preferred_element_type=F32)
                  + jnp.dot(mid, spread, preferred_element_type=F32)
                  + jnp.dot(lo, spread, preferred_element_type=F32))


def _dt_proj(x, w, bias, spread, *, tm):
    rows, k_dim = x.shape
    width = spread.shape[1]
    return pl.pallas_call(
        _dt_kernel,
        grid=(rows // tm,),
        in_specs=[pl.BlockSpec((tm, k_dim), lambda i: (i, 0)),
                  pl.BlockSpec((k_dim, LANES), lambda i: (0, 0)),
                  pl.BlockSpec((1, LANES), lambda i: (0, 0)),
                  pl.BlockSpec((LANES, width), lambda i: (0, 0))],
        out_specs=[pl.BlockSpec((tm, width), lambda i: (i, 0)),
                   pl.BlockSpec((tm, k_dim), lambda i: (i, 0))],
        out_shape=[jax.ShapeDtypeStruct((rows, width), F32),
                   jax.ShapeDtypeStruct((rows, k_dim), BF16)],
        compiler_params=_cparams(1),
        name="dt_proj",
    )(x, w, bias, spread)


def _cumsum_rows(v):
    n = v.shape[0]
    row = lax.broadcasted_iota(jnp.int32, v.shape, 0)
    shift = 1
    while shift < n:
        v = v + jnp.where(row >= shift, pltpu.roll(v, shift, 0), 0.0)
        shift *= 2
    return v


def _ssd_kernel(xs_ref, b_ref, c_ref, z_ref, dt_ref, alog_ref, dskip_ref, ng_ref, e_ref, o_ref,
                state_ref, *, groups, heads_per_group, head_dim):
    @pl.when(pl.program_id(2) == 0)
    def _():
        state_ref[...] = jnp.zeros_like(state_ref)

    gw = e_ref.shape[1]
    n = b_ref.shape[1] // groups
    for s in range(groups):
        g = pl.program_id(1) * groups + s
        lanes = slice(s * gw, (s + 1) * gw)
        _ssd_group(xs_ref.at[:, lanes], b_ref.at[:, s * n:(s + 1) * n], c_ref.at[:, s * n:(s + 1) * n],
                   z_ref.at[:, lanes], dt_ref.at[:, s * LANES:(s + 1) * LANES], alog_ref[pl.ds(g, 1), :],
                   dskip_ref[pl.ds(g, 1), :], ng_ref.at[:, lanes], e_ref, o_ref.at[:, lanes], state_ref.at[s],
                   heads_per_group=heads_per_group, head_dim=head_dim)


def _ssd_group(xs_ref, b_ref, c_ref, z_ref, dt_ref, alog, dskip, ng_ref, e_ref, o_ref, state_ref,
               *, heads_per_group, head_dim):
    q = xs_ref.shape[0]
    xs = xs_ref[...].astype(F32)
    bm = b_ref[...]
    cm = c_ref[...]
    dt = dt_ref[...]
    a = -jnp.exp(alog) * LOG2_E
    acs = _cumsum_rows(dt * a)
    acs_t = acs.T
    last = acs[q - 1:q, :]
    expand_mat = e_ref[...]

    def expand(v):
        return jnp.dot(v.astype(BF16), expand_mat, preferred_element_type=F32)

    def expand_hi_lo(v):
        hi = v.astype(BF16)
        lo = (v - hi.astype(F32)).astype(BF16)
        return (jnp.dot(hi, expand_mat, preferred_element_type=F32)
                + jnp.dot(lo, expand_mat, preferred_element_type=F32))

    decay_in_x = expand_hi_lo(jnp.exp2(acs))
    decay_out_dt_x = expand(jnp.exp2(last - acs) * dt)

    bc = lax.dot_general(bm, cm, (((1,), (1,)), ((), ())), preferred_element_type=F32)
    blk = LANES
    tri = (lax.broadcasted_iota(jnp.int32, (blk, blk), 1) >= lax.broadcasted_iota(jnp.int32, (blk, blk), 0))
    xs_t = xs.T
    dt_t = dt.T

    def masked_bc(r):
        rows = []
        for s0 in range(0, q, blk):
            parts = []
            if s0 > 0:
                parts.append(jnp.zeros((blk, s0), BF16))
            diff = acs_t[r:r + 1, s0:s0 + blk] - acs[s0:s0 + blk, r:r + 1]
            decay = jnp.exp2(jnp.where(tri, diff, -jnp.inf))
            parts.append((bc[s0:s0 + blk, s0:s0 + blk] * decay).astype(BF16))
            if s0 + blk < q:
                diff = acs_t[r:r + 1, s0 + blk:] - acs[s0:s0 + blk, r:r + 1]
                parts.append((bc[s0:s0 + blk, s0 + blk:] * jnp.exp2(diff)).astype(BF16))
            rows.append(jnp.concatenate(parts, axis=1) if len(parts) > 1 else parts[0])
        return jnp.concatenate(rows, axis=0) if len(rows) > 1 else rows[0]

    y_t = []
    for r in range(heads_per_group):
        x_head_t = (xs_t[r * head_dim:(r + 1) * head_dim, :] * dt_t[r:r + 1, :]).astype(BF16)
        y_t.append(jnp.dot(x_head_t, masked_bc(r), preferred_element_type=F32))
    y_diag = jnp.concatenate(y_t, axis=0).T

    s_prev = state_ref[...]
    y_off = jnp.dot(cm, s_prev.astype(BF16), preferred_element_type=F32) * decay_in_x
    y = y_diag + y_off + dskip * xs
    b_t = bm.astype(F32).T.astype(BF16)
    s_new = jnp.dot(b_t, (xs * decay_out_dt_x).astype(BF16), preferred_element_type=F32)
    state_ref[...] = s_prev * decay_in_x[q - 1:q, :] + s_new

    yg = y * z_ref[...].astype(F32)
    ms = jnp.mean(yg * yg, axis=-1, keepdims=True)
    o_ref[...] = (yg * lax.rsqrt(ms + RMS_EPS) * ng_ref[...]).astype(o_ref.dtype)


def _ssd_scan(xbc, z, dt, alog, dskip, norm_g, expand_mat, *, batch, seq, d_inner):
    n_groups, d_state, chunk = SSD_N_GROUPS, SSD_D_STATE, math.gcd(SSD_CHUNK, seq)
    gw = d_inner // n_groups
    heads_per_group = gw // SSD_HEAD_DIM
    n_chunks = seq // chunk
    gps = SSD_GROUPS_PER_STEP
    assert n_groups % gps == 0
    b_col0 = d_inner // (gps * d_state)
    c_col0 = b_col0 + n_groups // gps
    row = lambda b, g, c: b * n_chunks + c
    return pl.pallas_call(
        functools.partial(_ssd_kernel, groups=gps, heads_per_group=heads_per_group, head_dim=SSD_HEAD_DIM),
        grid=(batch, n_groups // gps, n_chunks),
        in_specs=[
            pl.BlockSpec((chunk, gps * gw), lambda b, g, c: (row(b, g, c), g)),
            pl.BlockSpec((chunk, gps * d_state), lambda b, g, c: (row(b, g, c), b_col0 + g)),
            pl.BlockSpec((chunk, gps * d_state), lambda b, g, c: (row(b, g, c), c_col0 + g)),
            pl.BlockSpec((chunk, gps * gw), lambda b, g, c: (row(b, g, c), g)),
            pl.BlockSpec((chunk, gps * LANES), lambda b, g, c: (row(b, g, c), g)),
            pl.BlockSpec((n_groups, LANES), lambda b, g, c: (0, 0)),
            pl.BlockSpec((n_groups, gw), lambda b, g, c: (0, 0)),
            pl.BlockSpec((1, gps * gw), lambda b, g, c: (0, g)),
            pl.BlockSpec((LANES, gw), lambda b, g, c: (0, 0)),
        ],
        out_specs=pl.BlockSpec((chunk, gps * gw), lambda b, g, c: (row(b, g, c), g)),
        out_shape=jax.ShapeDtypeStruct((batch * seq, d_inner), BF16),
        scratch_shapes=[pltpu.VMEM((gps, d_state, gw), F32)],
        compiler_params=_cparams(3),
        name="ssd_scan",
    )(xbc, xbc, xbc, z, dt, alog, dskip, norm_g, expand_mat)


def _outproj_ln_kernel(y_ref, w_ref, x_ref, g_ref, b_ref, o_ref, maybe_obf_ref=None, *, alpha, chunk):
    for r0 in range(0, y_ref.shape[0], chunk):
        rows = slice(r0, r0 + chunk)
        h = jnp.dot(y_ref[rows, :], w_ref[...], preferred_element_type=F32)
        v = alpha * x_ref[rows, :] + h
        mu = jnp.mean(v, axis=-1, keepdims=True)
        d = v - mu
        var = jnp.mean(d * d, axis=-1, keepdims=True)
        out = d * lax.rsqrt(var + LN_EPS) * g_ref[...] + b_ref[...]
        o_ref[rows, :] = out
        if maybe_obf_ref is not None:
            maybe_obf_ref[rows, :] = out.astype(BF16)


def _outproj_ln(y, w, x, ln_g, ln_b, *, alpha, tm, want_bf16):
    m, k_dim = y.shape
    d = w.shape[1]
    n_out = 2 if want_bf16 else 1
    outs = pl.pallas_call(
        functools.partial(_outproj_ln_kernel, alpha=alpha, chunk=LN_ROW_CHUNK),
        grid=(m // tm,),
        in_specs=[pl.BlockSpec((tm, k_dim), lambda i: (i, 0)),
                  pl.BlockSpec((k_dim, d), lambda i: (0, 0), pipeline_mode=pl.Buffered(1)),
                  pl.BlockSpec((tm, d), lambda i: (i, 0)),
                  pl.BlockSpec((1, d), lambda i: (0, 0)),
                  pl.BlockSpec((1, d), lambda i: (0, 0))],
        out_specs=[pl.BlockSpec((tm, d), lambda i: (i, 0))] * n_out,
        out_shape=[jax.ShapeDtypeStruct((m, d), F32), jax.ShapeDtypeStruct((m, d), BF16)][:n_out],
        compiler_params=_cparams(1),
        name="outproj_ln",
    )(y, w, x, ln_g, ln_b)
    return (outs[0], outs[1]) if want_bf16 else (outs[0], None)


def _dsa_latent_kernel(x_ref, w_ref, qg_ref, kvg_ref, cqt_ref, ckv_ref, ckvt_ref, slab_ref, slabt_ref):
    p = jnp.dot(x_ref[...], w_ref[...], preferred_element_type=F32)
    q_rank = cqt_ref.shape[1]
    kv_rank = ckv_ref.shape[1]
    cq = p[:, :q_rank]
    ckv = p[:, q_rank:q_rank + kv_rank]
    slab = p[:, q_rank + kv_rank:]
    cq = cq * lax.rsqrt(jnp.mean(cq * cq, axis=-1, keepdims=True) + RMS_EPS) * qg_ref[...]
    ckv = ckv * lax.rsqrt(jnp.mean(ckv * ckv, axis=-1, keepdims=True) + RMS_EPS) * kvg_ref[...]
    cqt_ref[0] = cq.T.astype(BF16)
    ckv_ref[...] = ckv.astype(BF16)
    ckvt_ref[0, :kv_rank, :] = ckv.T.astype(BF16)
    ckvt_ref[0, kv_rank:, :] = jnp.ones((ckvt_ref.shape[1] - kv_rank, ckvt_ref.shape[2]), BF16)
    slab_ref[...] = slab.astype(BF16)
    slabt_ref[0] = slab.T


def _dsa_latent(x, w, q_gain, kv_gain, *, batch, seq):
    tm = min(seq, LATENT_ROWS)
    n_t = seq // tm
    k_dim, n_dim = w.shape
    return pl.pallas_call(
        _dsa_latent_kernel,
        grid=(batch, n_t),
        in_specs=[pl.BlockSpec((tm, k_dim), lambda b, t: (b * n_t + t, 0)),
                  pl.BlockSpec((k_dim, n_dim), lambda b, t: (0, 0)),
                  pl.BlockSpec((1, DSA_Q_RANK), lambda b, t: (0, 0)),
                  pl.BlockSpec((1, DSA_KV_RANK), lambda b, t: (0, 0))],
        out_specs=[pl.BlockSpec((1, DSA_Q_RANK, tm), lambda b, t: (b, 0, t)),
                   pl.BlockSpec((tm, DSA_KV_RANK), lambda b, t: (b * n_t + t, 0)),
                   pl.BlockSpec((1, DSA_KV_RANK + ONES_ROWS, tm), lambda b, t: (b, 0, t)),
                   pl.BlockSpec((tm, LANES), lambda b, t: (b * n_t + t, 0)),
                   pl.BlockSpec((1, LANES, tm), lambda b, t: (b, 0, t))],
        out_shape=[jax.ShapeDtypeStruct((batch, DSA_Q_RANK, seq), BF16),
                   jax.ShapeDtypeStruct((batch * seq, DSA_KV_RANK), BF16),
                   jax.ShapeDtypeStruct((batch, DSA_KV_RANK + ONES_ROWS, seq), BF16),
                   jax.ShapeDtypeStruct((batch * seq, LANES), BF16),
                   jax.ShapeDtypeStruct((batch, LANES, seq), F32)],
        compiler_params=_cparams(2),
        name="dsa_latent",
    )(x, w, q_gain, kv_gain)


def _dsa_query_kernel(cqt_ref, wuqt_ref, wuk_ref, ql_ref, *, scale):
    hb, _, hd = wuk_ref.shape
    tq = ql_ref.shape[3] // hb
    q_t = jnp.dot(wuqt_ref[...], cqt_ref[0], preferred_element_type=F32).astype(BF16)
    for j in range(hb):
        ql_t = jnp.dot(wuk_ref[j], q_t[j * hd:(j + 1) * hd, :], preferred_element_type=F32)
        ql_t = (ql_t * scale).astype(BF16)
        for blk in range(ql_ref.shape[1]):
            ql_ref[0, blk, :, j * tq:(j + 1) * tq] = ql_t[:, blk * tq:(blk + 1) * tq]


def _dsa_query(cq_t, wuq_t, wuk, *, batch, seq, scale):
    tm = min(seq, 2048)
    hb = 8
    tq = ATT_TQ
    n_t = seq // tm
    n_heads, kv_rank, hd = wuk.shape
    return pl.pallas_call(
        functools.partial(_dsa_query_kernel, scale=scale),
        grid=(batch, n_t, n_heads // hb),
        in_specs=[pl.BlockSpec((1, DSA_Q_RANK, tm), lambda b, t, h: (b, 0, t)),
                  pl.BlockSpec((hb * hd, DSA_Q_RANK), lambda b, t, h: (h, 0)),
                  pl.BlockSpec((hb, kv_rank, hd), lambda b, t, h: (h, 0, 0))],
        out_specs=pl.BlockSpec((1, tm // tq, kv_rank, hb * tq), lambda b, t, h: (b, t, 0, h)),
        out_shape=jax.ShapeDtypeStruct((batch, seq // tq, kv_rank, n_heads * tq), BF16),
        compiler_params=_cparams(3),
        name="dsa_query",
    )(cq_t, wuq_t, wuk)


def _indexer_kernel(keys_ref, slabt_ref, cqt_ref, widx_ref, bias_ref, qt_scr, sc_scr, *, k_top, w_scale, n_iter):
    i = pl.program_id(1)
    n_t, kt_w, tq = sc_scr.shape
    cq_t = cqt_ref[0]
    n_h, hd_pad, q_rank = widx_ref.shape
    q_all = jnp.dot(widx_ref[...].reshape(n_h * hd_pad, q_rank), cq_t, preferred_element_type=F32)
    qt_scr[...] = q_all.astype(BF16).reshape(n_h, hd_pad, tq)
    w_t = slabt_ref[0][IDX_HEAD_DIM:IDX_HEAD_DIM + IDX_N_HEADS, :] * w_scale
    q_pos = i * tq + lax.broadcasted_iota(jnp.int32, (kt_w, tq), 1)

    for kt in range(n_t):
        @pl.when(kt <= i)
        def _(kt=kt):
            keys = keys_ref[kt * kt_w:(kt + 1) * kt_w, :]
            acc = jnp.zeros((kt_w, tq), F32)
            for h in range(IDX_N_HEADS):
                s = jnp.dot(keys, qt_scr[h], preferred_element_type=F32)
                acc = acc + jnp.maximum(s, 0.0) * w_t[h:h + 1, :]
            k_pos = kt * kt_w + lax.broadcasted_iota(jnp.int32, (kt_w, tq), 0)
            sc_scr[kt] = jnp.where(k_pos <= q_pos, acc, -jnp.inf)

    def bounds(kt, carry):
        lo, hi = carry
        s = sc_scr[kt]
        hi = jnp.maximum(hi, jnp.max(s, axis=0, keepdims=True))
        lo = jnp.minimum(lo, jnp.min(jnp.where(s == -jnp.inf, jnp.inf, s), axis=0, keepdims=True))
        return lo, hi

    lo, hi = lax.fori_loop(0, i + 1, bounds,
                           (jnp.full((1, tq), jnp.inf, F32), jnp.full((1, tq), -jnp.inf, F32)))

    def halve(carry):
        it, lo, hi, n_lo = carry
        mid = 0.5 * lo + 0.5 * hi

        mid_rows = jnp.broadcast_to(mid, (SUBLANES, tq))

        def count(kt, cnt):
            for r0 in range(0, kt_w, SUBLANES):
                cnt = jnp.where(sc_scr[kt, r0:r0 + SUBLANES, :] >= mid_rows, cnt + 1.0, cnt)
            return cnt

        cnt = lax.fori_loop(0, i, count, jnp.zeros((SUBLANES, tq), F32))
        half = kt_w // 2
        for r0 in range(0, half, SUBLANES):
            cnt = jnp.where(sc_scr[i, r0:r0 + SUBLANES, :] >= mid_rows, cnt + 1.0, cnt)
        late = cnt[:, half:]
        for r0 in range(half, kt_w, SUBLANES):
            late = jnp.where(sc_scr[i, r0:r0 + SUBLANES, half:] >= mid_rows[:, half:], late + 1.0, late)
        cnt = jnp.concatenate([cnt[:, :half], late], axis=1)
        cnt = jnp.sum(cnt, axis=0, keepdims=True)
        enough = cnt >= k_top
        return it + 1, jnp.where(enough, mid, lo), jnp.where(enough, hi, mid), jnp.where(enough, cnt, n_lo)

    def unresolved(carry):
        it, _, _, n_lo = carry
        return jnp.logical_and(it < n_iter, jnp.max(n_lo) > k_top)

    n_causal = (i * tq + lax.broadcasted_iota(jnp.int32, (1, tq), 1) + 1).astype(F32)
    _, lo, hi, _ = lax.while_loop(unresolved, lambda c: halve(halve(c)), (jnp.int32(0), lo, hi, n_causal))

    for kt in range(n_t):
        @pl.when(kt <= i)
        def _(kt=kt):
            bias_ref[0, kt * kt_w:(kt + 1) * kt_w, :] = jnp.where(sc_scr[kt] >= lo, 0.0, -jnp.inf)

        @pl.when(kt > i)
        def _(kt=kt):
            bias_ref[0, kt * kt_w:(kt + 1) * kt_w, :] = jnp.full((kt_w, tq), -jnp.inf, F32)


def _dsa_indexer(keys, slab_t, cq_t, widx_t, *, batch, seq, k_top, w_scale):
    tq = IDX_TQ
    n_t = seq // tq
    return pl.pallas_call(
        functools.partial(_indexer_kernel, k_top=k_top, w_scale=w_scale, n_iter=BISECT_ITERS),
        grid=(batch, n_t),
        in_specs=[pl.BlockSpec((seq, LANES), lambda b, i: (b, 0)),
                  pl.BlockSpec((1, LANES, tq), lambda b, i: (b, 0, i)),
                  pl.BlockSpec((1, DSA_Q_RANK, tq), lambda b, i: (b, 0, i)),
                  pl.BlockSpec(widx_t.shape, lambda b, i: (0, 0, 0))],
        out_specs=pl.BlockSpec((1, seq, tq), lambda b, i: (b, 0, i)),
        out_shape=jax.ShapeDtypeStruct((batch, seq, seq), F32),
        scratch_shapes=[pltpu.VMEM((IDX_N_HEADS, LANES, tq), BF16),
                        pltpu.VMEM((n_t, tq, tq), F32)],
        compiler_params=_cparams(2),
        name="dsa_indexer",
    )(keys, slab_t, cq_t, widx_t)


def _attn_kernel(ql_ref, bias_ref, k_ref, vt_ref, wuvt_ref, gate_ref, *rest, n_keys, hc):
    o_ref = rest[-1]
    tq = bias_ref.shape[2]
    hb, hd, kv_rank = wuvt_ref.shape
    cols = hc * tq
    n_chunks = hb // hc
    keys = k_ref[:n_keys, :]
    bias_t = bias_ref[0]
    bias_rep = jnp.concatenate([bias_t] * hc, axis=1) if hc > 1 else bias_t

    def scores(c):
        q_t = ql_ref[0, 0, :, c * cols:(c + 1) * cols]
        return jnp.dot(keys, q_t, preferred_element_type=F32) + bias_rep

    def finish(c, s):
        m = jnp.max(s, axis=0, keepdims=True)
        p = jnp.exp2(s - m).astype(BF16)
        o_aug = jnp.dot(vt_ref[0], p, preferred_element_type=F32)
        o_t = (o_aug[:kv_rank] / o_aug[kv_rank:kv_rank + 1]).astype(BF16)
        for j in range(hc):
            h = c * hc + j
            oh_t = jnp.dot(wuvt_ref[h], o_t[:, j * tq:(j + 1) * tq], preferred_element_type=F32)
            gate = gate_ref[:, h * hd:(h + 1) * hd].astype(F32)
            o_ref[:, h * hd:(h + 1) * hd] = (oh_t.T * gate).astype(o_ref.dtype)

    s_next = scores(0)
    for c in range(n_chunks):
        s_cur = s_next
        if c + 1 < n_chunks:
            s_next = scores(c + 1)
        finish(c, s_cur)


def _dsa_attention(ql, bias, ckv_t, ckv, wuv_t, gate, *, batch, seq):
    tq = ATT_TQ
    n_q = seq // tq
    n_heads, hd, kv_rank = wuv_t.shape
    width = n_heads * hd
    og = None
    for i in range(n_q):
        n_keys = (i + 1) * tq
        hb = n_heads
        while hb > 1 and hb * tq * n_keys > ATT_STEP_ELEMS:
            hb //= 2
        hc = max(1, min(hb, ATT_CHUNK_COLS // tq))
        in_specs = [pl.BlockSpec((1, 1, kv_rank, hb * tq), lambda b, g, i=i: (b, i, 0, g)),
                    pl.BlockSpec((1, n_keys, tq), lambda b, g, i=i: (b, 0, i)),
                    pl.BlockSpec((seq, kv_rank), lambda b, g: (b, 0)),
                    pl.BlockSpec((1, kv_rank + ONES_ROWS, n_keys), lambda b, g: (b, 0, 0)),
                    pl.BlockSpec((hb, hd, kv_rank), lambda b, g: (g, 0, 0)),
                    pl.BlockSpec((tq, hb * hd), lambda b, g, i=i: (b * n_q + i, g))]
        args = [ql, bias, ckv, ckv_t, wuv_t, gate]
        aliases = {}
        if og is not None:
            in_specs.append(pl.BlockSpec(memory_space=pl.ANY))
            args.append(og)
            aliases = {len(args) - 1: 0}
        og = pl.pallas_call(
            functools.partial(_attn_kernel, n_keys=n_keys, hc=hc),
            grid=(batch, n_heads // hb),
            in_specs=in_specs,
            out_specs=pl.BlockSpec((tq, hb * hd), lambda b, g, i=i: (b * n_q + i, g)),
            out_shape=jax.ShapeDtypeStruct((batch * seq, width), BF16),
            input_output_aliases=aliases,
            compiler_params=_cparams(2),
            name="dsa_attention_q%d" % i,
        )(*args)
    return og


def _ssd_layer(x, w_in, conv_w, conv_b, dt_bias, a_log, d_skip, norm_g, w_out, ln_g, ln_b,
               *, batch, seq, alpha, want_bf16):
    d_inner = w_out.shape[0]
    n_heads = a_log.shape[0]
    n_groups = SSD_N_GROUPS
    heads_per_group = n_heads // n_groups
    conv_dim = conv_w.shape[1]
    gw = d_inner // n_groups
    assert gw % LANES == 0 and heads_per_group <= LANES and SSD_D_STATE == LANES
    assert SSD_HEAD_DIM % SUBLANES == 0 and n_heads <= LANES and conv_w.shape[1] == d_inner + 2 * n_groups * SSD_D_STATE

    def per_group_lanes(v):
        lead = v.shape[:-1]
        v = v.reshape(lead + (n_groups, heads_per_group))
        v = jnp.pad(v, [(0, 0)] * len(lead) + [(0, 0), (0, LANES - heads_per_group)])
        return v.reshape(lead + (n_groups * LANES,))

    w_bf = w_in.astype(BF16)
    w_dt = jnp.pad(w_bf[:, d_inner + conv_dim:], ((0, 0), (0, LANES - n_heads)))
    dt_b = jnp.pad(dt_bias, (0, LANES - n_heads))[None, :]
    head_of_lane = jnp.arange(n_groups * LANES)
    head_of_lane = jnp.where(head_of_lane % LANES < heads_per_group,
                             (head_of_lane // LANES) * heads_per_group + head_of_lane % LANES, -1)
    spread = (jnp.arange(LANES)[:, None] == head_of_lane[None, :]).astype(BF16)
    alog = per_group_lanes(a_log).reshape(n_groups, LANES)
    dskip = jnp.repeat(d_skip, SSD_HEAD_DIM).reshape(n_groups, gw)
    lane_of = jnp.arange(gw) // SSD_HEAD_DIM
    expand_mat = (jnp.arange(LANES)[:, None] == lane_of[None, :]).astype(BF16)

    dt, x_bf = _dt_proj(x, w_dt, dt_b, spread, tm=min(seq, 1024))
    z = _proj(x_bf, w_bf, batch=batch, seq=seq, tn=1024, n_dim=d_inner)
    xbc = _proj(x_bf, w_bf, batch=batch, seq=seq, tn=1024, col0=d_inner, n_dim=conv_dim,
                conv=(conv_w, conv_b[None, :]))
    yn = _ssd_scan(xbc, z, dt, alog, dskip, norm_g[None, :], expand_mat, batch=batch, seq=seq, d_inner=d_inner)
    return _outproj_ln(yn, w_out.astype(BF16), x, ln_g[None, :], ln_b[None, :], alpha=alpha, tm=512,
                       want_bf16=want_bf16)


def _dsa_layer(x, x_bf, w_in, q_norm_g, kv_norm_g, w_uq, w_uk, w_uv, w_idx_q, w_out, ln_g, ln_b,
               *, batch, seq, alpha, want_bf16):
    assert seq % IDX_TQ == 0 and seq % ATT_TQ == 0 and IDX_HEAD_DIM + IDX_N_HEADS <= LANES
    small = DSA_Q_RANK + DSA_KV_RANK + IDX_HEAD_DIM + IDX_N_HEADS
    w_small = jnp.pad(w_in[:, :small], ((0, 0), (0, DSA_Q_RANK + DSA_KV_RANK + LANES - small))).astype(BF16)
    w_gate = w_in[:, small:].astype(BF16)
    wuq_t = w_uq.T.astype(BF16)
    wuv_t = w_uv.transpose(0, 2, 1).astype(BF16)
    widx_t = w_idx_q.reshape(DSA_Q_RANK, IDX_N_HEADS, IDX_HEAD_DIM).transpose(1, 2, 0)
    widx_t = jnp.pad(widx_t, ((0, 0), (0, LANES - IDX_HEAD_DIM), (0, 0))).astype(BF16)
    k_top = min(IDX_TOPK, seq // 4)

    cq_t, ckv, ckv_t, slab, slab_t = _dsa_latent(x_bf, w_small, q_norm_g[None, :], kv_norm_g[None, :],
                                                     batch=batch, seq=seq)
    gate = _proj(x_bf, w_gate, batch=batch, seq=seq, tn=1024)
    ql = _dsa_query(cq_t, wuq_t, w_uk.astype(BF16), batch=batch, seq=seq,
                    scale=DSA_HEAD_DIM ** -0.5 * math.log2(math.e))
    bias = _dsa_indexer(slab, slab_t, cq_t, widx_t, batch=batch, seq=seq, k_top=k_top,
                        w_scale=IDX_N_HEADS ** -0.5 * IDX_HEAD_DIM ** -0.5)
    og = _dsa_attention(ql, bias, ckv_t, ckv, wuv_t, gate, batch=batch, seq=seq)
    return _outproj_ln(og, w_out.astype(BF16), x, ln_g[None, :], ln_b[None, :], alpha=alpha, tm=512,
                       want_bf16=want_bf16)


def kernel(x, ssd_w_in, ssd_conv_w, ssd_conv_b, ssd_dt_bias, ssd_a_log, ssd_d_skip, ssd_norm_g, ssd_w_out,
           dsa_w_in, dsa_q_norm_g, dsa_kv_norm_g, dsa_w_uq, dsa_w_uk, dsa_w_uv, dsa_w_idx_q, dsa_w_out,
           ln_g, ln_b):
    batch, seq, d_model = x.shape
    depth = ln_g.shape[0]
    alpha = (2.0 * depth) ** 0.25
    xf = x.reshape(batch * seq, d_model)
    x_bf = None
    for i in range(depth):
        j = i // 2
        next_is_dsa = i + 1 < depth and (i + 1) % 2 == 1
        if i % 2 == 0:
            xf, x_bf = _ssd_layer(xf, ssd_w_in[j], ssd_conv_w[j], ssd_conv_b[j], ssd_dt_bias[j],
                                  ssd_a_log[j], ssd_d_skip[j], ssd_norm_g[j], ssd_w_out[j], ln_g[i], ln_b[i],
                                  batch=batch, seq=seq, alpha=alpha, want_bf16=next_is_dsa)
        else:
            x_bf = xf.astype(BF16) if x_bf is None else x_bf
            xf, x_bf = _dsa_layer(xf, x_bf, dsa_w_in[j], dsa_q_norm_g[j], dsa_kv_norm_g[j], dsa_w_uq[j],
                                  dsa_w_uk[j], dsa_w_uv[j], dsa_w_idx_q[j], dsa_w_out[j], ln_g[i], ln_b[i],
                                  batch=batch, seq=seq, alpha=alpha, want_bf16=next_is_dsa)
    return xf.reshape(batch, seq, d_model)
```
